```python
import math
import jax, jax.numpy as jnp
from jax import lax
import numpy as np

D_MODEL = 1024
BATCH = 8
SEQ = 2048
DEPTH = 4
DEC_BATCH = 128
DEC_SEQ = 4
PAST_LEN = 16384
PAGE_SIZE = 128

MIX_WIDTH = D_MODEL
A_HEADS = 4
A_WIDTH = MIX_WIDTH // 2
A_HEAD_DIM = A_WIDTH // A_HEADS
CHUNK_A = 128
B_HEADS = 4
B_WIDTH = MIX_WIDTH - A_WIDTH
B_HEAD_DIM = B_WIDTH // B_HEADS
DELTA_CHUNK = 64
CONV_W = 4
D_FF = 2816
PLE_DIM = 256
EPS = 1e-6
IN_WIDTH = 2 * A_WIDTH + 4 * B_WIDTH + 2 * B_HEADS

kernel_name = 'hybrid_gmlp_gdn_step'


def rmsnorm(x, gain):
    xf = x.astype(jnp.float32)
    y = xf * lax.rsqrt(jnp.mean(xf * xf, axis=-1, keepdims=True) + EPS)
    return (y * gain.astype(jnp.float32)).astype(x.dtype)


def l2norm(x):
    return x * lax.rsqrt(jnp.sum(x * x, axis=-1, keepdims=True) + EPS)


def swiglu(h, w_in, w_out):
    gate, up = jnp.split(h @ w_in, 2, axis=-1)
    return (jax.nn.silu(gate) * up) @ w_out


def chunk_gmlp(u, v, w_s, b_s):
    bsz, length = u.shape[:2]
    c = CHUNK_A if length % CHUNK_A == 0 else length
    n = length // c
    causal = jnp.tril(jnp.ones((c, c), dtype=bool))
    w = jnp.where(causal, w_s[:, :c, :c], 0.0).astype(v.dtype)
    vc = v.reshape(bsz, n, c, A_HEADS, A_HEAD_DIM)
    bias = jnp.transpose(b_s[:, :c])[:, :, None].astype(v.dtype)
    mixed = jnp.einsum('hts,bnshd->bnthd', w, vc) + bias
    return u * mixed.reshape(bsz, length, A_HEADS, A_HEAD_DIM)


def causal_conv_silu(x_pre, buf, w):
    length = x_pre.shape[1]
    xp = jnp.concatenate([buf.astype(x_pre.dtype), x_pre], axis=1)
    y = xp[:, 0:length] * w[0]
    for j in range(1, CONV_W):
        y = y + xp[:, j:j + length] * w[j]
    return jax.nn.silu(y), xp[:, xp.shape[1] - (CONV_W - 1):]


def gated_delta_rule(q, k, v, g, beta, s0):
    bsz, length, nh, dk = q.shape
    dv = v.shape[-1]
    c = DELTA_CHUNK if length % DELTA_CHUNK == 0 else length
    n = length // c

    def blocks(t):
        t = t.reshape((bsz, n, c, nh) + t.shape[3:])
        return jnp.moveaxis(t, 3, 1)

    q, k, v, g, beta = (blocks(t) for t in (q, k, v, g, beta))
    gam = jnp.cumsum(g, axis=-1)
    incl = jnp.tril(jnp.ones((c, c), dtype=bool))
    strict = jnp.tril(jnp.ones((c, c), dtype=bool), -1)
    diff = gam[..., :, None] - gam[..., None, :]
    decay = jnp.where(incl, jnp.exp(jnp.where(incl, diff, 0.0)), 0.0)
    kk = jnp.einsum('bhnci,bhnsi->bhncs', k, k)
    a_mat = jnp.eye(c, dtype=q.dtype) + jnp.where(strict, beta[..., :, None] * kk * decay, 0.0)

    def solve(rhs):
        return lax.linalg.triangular_solve(a_mat, rhs, left_side=True, lower=True, unit_diagonal=True)

    w_blk = solve(beta[..., None] * v)
    kb_blk = solve((beta * jnp.exp(gam))[..., None] * k)
    qk = jnp.einsum('bhnci,bhnsi->bhncs', q, k) * decay
    qb = q * jnp.exp(gam)[..., None]
    kend = k * jnp.exp(gam[..., -1:] - gam)[..., None]
    btot = jnp.exp(gam[..., -1])

    def step(s, blk):
        qb_n, qk_n, w_n, kb_n, kend_n, btot_n = blk
        u = w_n - jnp.einsum('bhci,bhij->bhcj', kb_n, s)
        o = jnp.einsum('bhci,bhij->bhcj', qb_n, s) + jnp.einsum('bhcs,bhsj->bhcj', qk_n, u)
        s = btot_n[..., None, None] * s + jnp.einsum('bhci,bhcj->bhij', kend_n, u)
        return s, o

    xs = tuple(jnp.moveaxis(t, 2, 0) for t in (qb, qk, w_blk, kb_blk, kend, btot))
    s_fin, o = lax.scan(step, s0, xs)
    o = jnp.transpose(o, (1, 0, 3, 2, 4)).reshape(bsz, length, nh, dv)
    return o, s_fin


def hybrid_layer(x, p, conv_buf, s0, lw):
    (n_f1, w_f1_in, w_f1_out, n_mix, w_in, a_v_gain, a_w_s, a_b_s, a_out_gain,
     b_conv, b_a_log, b_dt_bias, b_out_gain, w_out, n_f2, w_f2_in, w_f2_out,
     n_ple, w_ple_gate, w_ple_proj) = lw
    bsz, length, _ = x.shape
    f32 = jnp.float32
    x = x + 0.5 * swiglu(rmsnorm(x, n_f1), w_f1_in, w_f1_out)
    z = rmsnorm(x, n_mix) @ w_in
    uv = jax.nn.gelu(z[..., :2 * A_WIDTH])
    u = uv[..., :A_WIDTH].reshape(bsz, length, A_HEADS, A_HEAD_DIM)
    v = rmsnorm(uv[..., A_WIDTH:].reshape(bsz, length, A_HEADS, A_HEAD_DIM), a_v_gain)
    a_out = rmsnorm(chunk_gmlp(u, v, a_w_s, a_b_s), a_out_gain).reshape(bsz, length, A_WIDTH)
    o0 = 2 * A_WIDTH
    qkv, new_buf = causal_conv_silu(z[..., o0:o0 + 3 * B_WIDTH], conv_buf, b_conv)
    o1 = o0 + 3 * B_WIDTH
    a_logit = z[..., o1:o1 + B_HEADS]
    b_logit = z[..., o1 + B_HEADS:o1 + 2 * B_HEADS]
    gate = z[..., o1 + 2 * B_HEADS:]
    qkv = qkv.astype(f32).reshape(bsz, length, 3, B_HEADS, B_HEAD_DIM)
    q = l2norm(qkv[:, :, 0]) * (B_HEAD_DIM ** -0.5)
    k = l2norm(qkv[:, :, 1])
    vb = qkv[:, :, 2]
    g = -jnp.exp(b_a_log.astype(f32)) * jax.nn.softplus(a_logit.astype(f32) + b_dt_bias.astype(f32))
    beta = jax.nn.sigmoid(b_logit.astype(f32))
    o, s_new = gated_delta_rule(q, k, vb, g, beta, s0.astype(f32))
    o = rmsnorm(o, b_out_gain) * jax.nn.silu(gate.astype(f32).reshape(bsz, length, B_HEADS, B_HEAD_DIM))
    b_out = o.astype(x.dtype).reshape(bsz, length, B_WIDTH)
    x = x + jnp.concatenate([a_out, b_out], axis=-1) @ w_out
    x = x + 0.5 * swiglu(rmsnorm(x, n_f2), w_f2_in, w_f2_out)
    x = x + (p @ w_ple_proj) * jax.nn.sigmoid(rmsnorm(x, n_ple) @ w_ple_gate)
    return x, v, s_new.astype(s0.dtype), new_buf


def setup_inputs(seed: int = 0) -> dict:
    key = jax.random.key(seed)
    ks = jax.random.split(key, 32)

    def nrm(k, shape, scale):
        return jax.random.normal(k, shape, jnp.float32) * scale

    def gain(k, shape):
        return 1.0 + 0.01 * jax.random.normal(k, shape, jnp.float32)

    dt = jnp.exp(jax.random.uniform(ks[20], (DEPTH, B_HEADS), jnp.float32, math.log(1e-3), math.log(1e-1)))
    return {
        'x_prompt': nrm(ks[0], (BATCH, SEQ, D_MODEL), 1.0),
        'x_sample': nrm(ks[1], (DEC_BATCH, DEC_SEQ, D_MODEL), 1.0),
        'state_S': nrm(ks[2], (DEPTH, DEC_BATCH, B_HEADS, B_HEAD_DIM, B_HEAD_DIM), 0.05),
        'state_conv': nrm(ks[3], (DEPTH, DEC_BATCH, CONV_W - 1, 3 * B_WIDTH), 1.0),
        'p_prompt': nrm(ks[4], (DEPTH, BATCH, SEQ, PLE_DIM), 1.0),
        'p_sample': nrm(ks[5], (DEPTH, DEC_BATCH, DEC_SEQ, PLE_DIM), 1.0),
        'norm_ffn1': gain(ks[6], (DEPTH, D_MODEL)),
        'w_ffn1_in': nrm(ks[7], (DEPTH, D_MODEL, 2 * D_FF), D_MODEL ** -0.5),
        'w_ffn1_out': nrm(ks[8], (DEPTH, D_FF, D_MODEL), D_FF ** -0.5),
        'norm_mix': gain(ks[9], (DEPTH, D_MODEL)),
        'w_in': nrm(ks[10], (DEPTH, D_MODEL, IN_WIDTH), D_MODEL ** -0.5),
        'a_v_gain': gain(ks[11], (DEPTH, A_HEAD_DIM)),
        'a_spatial_w': nrm(ks[12], (DEPTH, A_HEADS, CHUNK_A, CHUNK_A), CHUNK_A ** -0.5),
        'a_spatial_b': 1.0 + nrm(ks[13], (DEPTH, A_HEADS, CHUNK_A), 0.1),
        'a_out_gain': gain(ks[14], (DEPTH, A_HEAD_DIM)),
        'b_conv_w': nrm(ks[15], (DEPTH, CONV_W, 3 * B_WIDTH), CONV_W ** -0.5),
        'b_a_log': jnp.log(jax.random.uniform(ks[16], (DEPTH, B_HEADS), jnp.float32, 1.0, 16.0)),
        'b_dt_bias': dt + jnp.log(-jnp.expm1(-dt)),
        'b_out_gain': gain(ks[17], (DEPTH, B_HEAD_DIM)),
        'w_out': nrm(ks[18], (DEPTH, MIX_WIDTH, D_MODEL), MIX_WIDTH ** -0.5),
        'norm_ffn2': gain(ks[19], (DEPTH, D_MODEL)),
        'w_ffn2_in': nrm(ks[21], (DEPTH, D_MODEL, 2 * D_FF), D_MODEL ** -0.5),
        'w_ffn2_out': nrm(ks[22], (DEPTH, D_FF, D_MODEL), D_FF ** -0.5),
        'norm_ple': gain(ks[23], (DEPTH, D_MODEL)),
        'w_ple_gate': nrm(ks[24], (DEPTH, D_MODEL, D_MODEL), D_MODEL ** -0.5),
        'w_ple_proj': nrm(ks[25], (DEPTH, PLE_DIM, D_MODEL), PLE_DIM ** -0.5),
        'final_norm': gain(ks[26], (D_MODEL,)),
    }


def reference(x_prompt, x_sample, state_S, state_conv, p_prompt, p_sample,
              norm_ffn1, w_ffn1_in, w_ffn1_out, norm_mix, w_in, a_v_gain, a_spatial_w,
              a_spatial_b, a_out_gain, b_conv_w, b_a_log, b_dt_bias, b_out_gain, w_out,
              norm_ffn2, w_ffn2_in, w_ffn2_out, norm_ple, w_ple_gate, w_ple_proj, final_norm):
    bsz = x_prompt.shape[0]
    zero_buf = jnp.zeros((bsz, CONV_W - 1, 3 * B_WIDTH), x_prompt.dtype)
    zero_s = jnp.zeros((bsz, B_HEADS, B_HEAD_DIM, B_HEAD_DIM), x_prompt.dtype)
    xp, xs = x_prompt, x_sample
    s_prompt, c_prompt, s_sample, c_sample, v_sample = [], [], [], [], []
    for i in range(DEPTH):
        lw = (norm_ffn1[i], w_ffn1_in[i], w_ffn1_out[i], norm_mix[i], w_in[i], a_v_gain[i],
              a_spatial_w[i], a_spatial_b[i], a_out_gain[i], b_conv_w[i], b_a_log[i], b_dt_bias[i],
              b_out_gain[i], w_out[i], norm_ffn2[i], w_ffn2_in[i], w_ffn2_out[i], norm_ple[i],
              w_ple_gate[i], w_ple_proj[i])
        xp, _, sp, cp = hybrid_layer(xp, p_prompt[i], zero_buf, zero_s, lw)
        xs, vs, ss, cs = hybrid_layer(xs, p_sample[i], state_conv[i], state_S[i], lw)
        s_prompt.append(sp)
        c_prompt.append(cp)
        s_sample.append(ss)
        c_sample.append(cs)
        v_sample.append(vs)
    y_prompt = rmsnorm(xp, final_norm)
    y_sample = rmsnorm(xs, final_norm)
    return (y_prompt, y_sample, jnp.stack(s_prompt), jnp.stack(c_prompt),
            jnp.stack(s_sample), jnp.stack(c_sample), jnp.stack(v_sample))
```

```python
import functools

import jax
import jax.numpy as jnp
from jax import lax
from jax.experimental import pallas as pl
from jax.experimental.pallas import tpu as pltpu

F32 = jnp.float32
BF16 = jnp.bfloat16
EPS = 1e-6
HIGHEST = lax.Precision.HIGHEST

D_MODEL = 1024
D_FF = 2816
DEPTH = 4
N_HEADS = 4
HEAD_DIM = 128
A_WIDTH = N_HEADS * HEAD_DIM
B_WIDTH = N_HEADS * HEAD_DIM
CHUNK_A = 128
CHUNK_D = 64
CONV_W = 4
PLE_DIM = 256
Z_MAIN = 2 * A_WIDTH + 4 * B_WIDTH
OFF_QKV = 2 * A_WIDTH
OFF_GATE = OFF_QKV + 3 * B_WIDTH

VMEM_LIMIT_BYTES = 52 * 1024 * 1024
MXU_N = 256
FFN_ROWS = 512
MIX_ROWS = 256
SAMPLE_GROUP = 8


def _rms(x, gain):
    return x * lax.rsqrt(jnp.mean(x * x, axis=-1, keepdims=True) + EPS) * gain


def _l2(x):
    return x * lax.rsqrt(jnp.sum(x * x, axis=-1, keepdims=True) + EPS)


def _silu(x):
    return x * jax.nn.sigmoid(x)


def _softplus(x):
    return jnp.maximum(x, 0.0) + jnp.log1p(jnp.exp(-jnp.abs(x)))


def _dot(a, b):
    return jnp.dot(a.astype(BF16), b.astype(BF16), preferred_element_type=F32)


def _dot_nt(a, b):
    return lax.dot_general(a.astype(BF16), b.astype(BF16), (((1,), (1,)), ((), ())),
                           preferred_element_type=F32)


def _dot_f32(a, b):
    return jnp.dot(a, b, precision=HIGHEST, preferred_element_type=F32)


def _const_spec(shape):
    zeros = (0,) * len(shape)
    return pl.BlockSpec(shape, lambda *_: zeros, pipeline_mode=pl.Buffered(1))


def _ffn_kernel(*refs, with_ple, with_final):
    x_ref, gain_ref, wg_ref, wu_ref, wo_ref = refs[:5]
    o_ref = refs[-1]
    x = x_ref[...]
    xn = _rms(x, gain_ref[...]).astype(BF16)
    acc = jnp.zeros_like(x)
    for c in range(D_FF // MXU_N):
        sl = slice(c * MXU_N, (c + 1) * MXU_N)
        gate = jnp.dot(xn, wg_ref[:, sl], preferred_element_type=F32)
        up = jnp.dot(xn, wu_ref[:, sl], preferred_element_type=F32)
        h = (_silu(gate) * up).astype(BF16)
        acc = acc + jnp.dot(h, wo_ref[sl, :], preferred_element_type=F32)
    x = x + 0.5 * acc
    if with_ple:
        p_ref, npl_ref, wpg_ref, wpp_ref = refs[5:9]
        emb = _dot(p_ref[...], wpp_ref[...])
        gate = _dot(_rms(x, npl_ref[...]), wpg_ref[...])
        x = x + emb * jax.nn.sigmoid(gate)
    if with_final:
        x = _rms(x, refs[9][...])
    o_ref[...] = x


def _ffn(x, gain, w_in, w_out, ple=None, final_gain=None):
    rows = x.shape[0]
    tm = min(FFN_ROWS, rows)
    row_spec = pl.BlockSpec((tm, D_MODEL), lambda i: (i, 0))
    in_specs = [row_spec, _const_spec((1, D_MODEL)),
                pl.BlockSpec((D_MODEL, D_FF), lambda i: (0, 0), pipeline_mode=pl.Buffered(1)),
                pl.BlockSpec((D_MODEL, D_FF), lambda i: (0, 1), pipeline_mode=pl.Buffered(1)),
                _const_spec((D_FF, D_MODEL))]
    args = [x, gain, w_in, w_in, w_out]
    if ple is not None:
        p, n_ple, w_gate, w_proj = ple
        in_specs += [pl.BlockSpec((tm, PLE_DIM), lambda i: (i, 0)), _const_spec((1, D_MODEL)),
                     _const_spec((D_MODEL, D_MODEL)), _const_spec((PLE_DIM, D_MODEL))]
        args += [p, n_ple, w_gate, w_proj]
    if final_gain is not None:
        in_specs.append(_const_spec((1, D_MODEL)))
        args.append(final_gain)
    return pl.pallas_call(
        functools.partial(_ffn_kernel, with_ple=ple is not None, with_final=final_gain is not None),
        grid=(rows // tm,),
        in_specs=in_specs,
        out_specs=row_spec,
        out_shape=jax.ShapeDtypeStruct(x.shape, F32),
        compiler_params=pltpu.CompilerParams(dimension_semantics=("arbitrary",),
                                             vmem_limit_bytes=VMEM_LIMIT_BYTES),
    )(*args)


def _unit_lower_inverse(strict_lower):
    c = strict_lower.shape[0]
    row = lax.broadcasted_iota(jnp.int32, (c, c), 0)
    col = lax.broadcasted_iota(jnp.int32, (c, c), 1)
    inv = jnp.where(row == col, 1.0, 0.0).astype(F32)
    bs = 1
    while bs < c:
        m = jnp.where((row // (2 * bs) == col // (2 * bs)) & ((row // bs) % 2 == 1) & ((col // bs) % 2 == 0),
                      strict_lower, 0.0)
        if bs == 1:
            inv = inv - m
        else:
            inv = inv - _dot_f32(inv, _dot_f32(m, inv))
        bs *= 2
    return inv


def _mix_prompt_kernel(x_ref, nmix_ref, wmain_ref, wab_ref, avg_ref, aog_ref, bog_ref, wsp_ref, bsp_ref,
                       cw_ref, abp_ref, wout_ref,
                       xo_ref, s_ref, ct_ref,
                       zext_ref, ob_ref):
    tl = x_ref.shape[0]
    step = pl.program_id(1)

    @pl.when(step == 0)
    def _():
        s_ref[...] = jnp.zeros_like(s_ref)
        zext_ref[0:8, :] = jnp.zeros((8, 3 * B_WIDTH), F32)

    x = x_ref[...]
    xn = _rms(x, nmix_ref[...]).astype(BF16)
    z = jnp.dot(xn, wmain_ref[...], preferred_element_type=F32)
    zab = jnp.dot(xn, wab_ref[...], preferred_element_type=F32)

    uv = jax.nn.gelu(z[:, :2 * A_WIDTH])
    row = lax.broadcasted_iota(jnp.int32, (CHUNK_A, CHUNK_A), 0)
    col = lax.broadcasted_iota(jnp.int32, (CHUNK_A, CHUNK_A), 1)
    causal = col <= row
    for h in range(N_HEADS):
        hs = slice(h * HEAD_DIM, (h + 1) * HEAD_DIM)
        u_h = uv[:, hs]
        v_h = _rms(uv[:, A_WIDTH + h * HEAD_DIM:A_WIDTH + (h + 1) * HEAD_DIM], avg_ref[...]).astype(BF16)
        w_h = jnp.where(causal, wsp_ref[h], 0.0).astype(BF16)
        bias_h = bsp_ref[:, h:h + 1]
        for c in range(tl // CHUNK_A):
            rs = slice(c * CHUNK_A, (c + 1) * CHUNK_A)
            mixed = jnp.dot(w_h, v_h[rs], preferred_element_type=F32) + bias_h
            ob_ref[rs, hs] = _rms(u_h[rs] * mixed, aog_ref[...]).astype(BF16)

    zext_ref[8:8 + tl, :] = z[:, OFF_QKV:OFF_GATE]
    cw = cw_ref[...]
    y = zext_ref[5:5 + tl, :] * cw[0:1]
    for j in range(1, CONV_W):
        y = y + zext_ref[5 + j:5 + j + tl, :] * cw[j:j + 1]
    tail = zext_ref[tl + 5:tl + 8, :]
    ct_ref[...] = tail
    zext_ref[5:8, :] = tail
    qkv = _silu(y)

    abp = abp_ref[...]
    g = -jnp.exp(abp[0:1]) * _softplus(zab[:, :128] + abp[1:2])
    beta = jax.nn.sigmoid(zab[:, 128:])

    r2 = lax.broadcasted_iota(jnp.int32, (tl, tl), 0)
    c2 = lax.broadcasted_iota(jnp.int32, (tl, tl), 1)
    same = (r2 // CHUNK_D) == (c2 // CHUNK_D)
    lower_ones = jnp.where(same & (c2 <= r2), 1.0, 0.0).astype(F32)
    upper_ones = jnp.where(same & (r2 <= c2), 1.0, 0.0).astype(F32)
    gam = _dot_f32(lower_ones, g)
    gam_t = _dot_f32(g.T, upper_ones)

    rb = lax.broadcasted_iota(jnp.int32, (CHUNK_D, CHUNK_D), 0)
    cb = lax.broadcasted_iota(jnp.int32, (CHUNK_D, CHUNK_D), 1)
    incl = cb <= rb
    strict = cb < rb

    for h in range(N_HEADS):
        hs = slice(h * HEAD_DIM, (h + 1) * HEAD_DIM)
        q_h = _l2(qkv[:, hs]) * (HEAD_DIM ** -0.5)
        k_h = _l2(qkv[:, B_WIDTH + h * HEAD_DIM:B_WIDTH + (h + 1) * HEAD_DIM])
        v_h = qkv[:, 2 * B_WIDTH + h * HEAD_DIM:2 * B_WIDTH + (h + 1) * HEAD_DIM]
        gate_h = z[:, OFF_GATE + h * HEAD_DIM:OFF_GATE + (h + 1) * HEAD_DIM]
        gc_h = gam[:, h:h + 1]
        gr_h = gam_t[h:h + 1, :]
        bc_h = beta[:, h:h + 1]
        for i in range(tl // CHUNK_D):
            rs = slice(i * CHUNK_D, (i + 1) * CHUNK_D)
            q, k, v = q_h[rs], k_h[rs], v_h[rs]
            gc, bc = gc_h[rs], bc_h[rs]
            diff = gc - gr_h[:, rs]
            decay = jnp.where(incl, jnp.exp(jnp.where(incl, diff, 0.0)), 0.0)
            kk = _dot_nt(k, k)
            inv = _unit_lower_inverse(jnp.where(strict, bc * kk * decay, 0.0))
            eg = jnp.exp(gc)
            sol = _dot_f32(inv, jnp.concatenate([bc * v, (bc * eg) * k], axis=1))
            w_blk, kb_blk = sol[:, :HEAD_DIM], sol[:, HEAD_DIM:]
            qk = _dot_nt(q, k) * decay
            g_last = gc[CHUNK_D - 1:CHUNK_D]
            kend = k * jnp.exp(g_last - gc)
            s_old = s_ref[h]
            u = w_blk - _dot(kb_blk, s_old)
            o = _dot(q * eg, s_old) + _dot(qk, u)
            s_ref[h] = jnp.exp(g_last) * s_old + _dot(kend.T, u)
            ob_ref[rs, A_WIDTH + h * HEAD_DIM:A_WIDTH + (h + 1) * HEAD_DIM] = (
                _rms(o, bog_ref[...]) * _silu(gate_h[rs])).astype(BF16)

    xo_ref[...] = x + jnp.dot(ob_ref[...], wout_ref[...], preferred_element_type=F32)


def _mix_prompt(x, lw):
    bsz, length, _ = x.shape
    tl = MIX_ROWS
    row_spec = pl.BlockSpec((None, tl, D_MODEL), lambda b, t: (b, t, 0))
    in_specs = [row_spec,
                _const_spec((1, D_MODEL)), _const_spec((D_MODEL, Z_MAIN)), _const_spec((D_MODEL, 256)),
                _const_spec((1, HEAD_DIM)), _const_spec((1, HEAD_DIM)), _const_spec((1, HEAD_DIM)),
                _const_spec((N_HEADS, CHUNK_A, CHUNK_A)), _const_spec((CHUNK_A, N_HEADS)),
                _const_spec((CONV_W, 3 * B_WIDTH)), _const_spec((2, 128)), _const_spec((D_MODEL, D_MODEL))]
    out_specs = [row_spec,
                 pl.BlockSpec((None, N_HEADS, HEAD_DIM, HEAD_DIM), lambda b, t: (b, 0, 0, 0)),
                 pl.BlockSpec((None, CONV_W - 1, 3 * B_WIDTH), lambda b, t: (b, 0, 0))]
    out_shape = [jax.ShapeDtypeStruct(x.shape, F32),
                 jax.ShapeDtypeStruct((bsz, N_HEADS, HEAD_DIM, HEAD_DIM), F32),
                 jax.ShapeDtypeStruct((bsz, CONV_W - 1, 3 * B_WIDTH), F32)]
    return pl.pallas_call(
        _mix_prompt_kernel,
        grid=(bsz, length // tl),
        in_specs=in_specs,
        out_specs=out_specs,
        out_shape=out_shape,
        scratch_shapes=[pltpu.VMEM((tl + 8, 3 * B_WIDTH), F32), pltpu.VMEM((tl, D_MODEL), BF16)],
        compiler_params=pltpu.CompilerParams(dimension_semantics=("arbitrary", "arbitrary"),
                                             vmem_limit_bytes=VMEM_LIMIT_BYTES),
    )(x, lw["n_mix"], lw["w_main"], lw["w_ab"], lw["a_v_gain"], lw["a_out_gain"], lw["b_out_gain"],
      lw["a_w_s"], lw["a_b_s_t"], lw["b_conv"], lw["ab_par"], lw["w_out"])


def _mix_sample_kernel(x_ref, s_ref, conv_ref, nmix_ref, wmain_ref, wab_ref, avg_ref, aog_ref, bog_ref,
                       wrow_ref, brow_ref, cw_ref, abp_ref, wout_ref,
                       xo_ref, so_ref, co_ref, vo_ref,
                       o_ref, ob_ref):
    n_tok, nb = x_ref.shape[0], x_ref.shape[1]
    x = x_ref[...].reshape(n_tok * nb, D_MODEL)
    xn = _rms(x, nmix_ref[...]).astype(BF16)
    z = jnp.dot(xn, wmain_ref[...], preferred_element_type=F32)
    zab = jnp.dot(xn, wab_ref[...], preferred_element_type=F32)

    def tok(a, t):
        return a[t * nb:(t + 1) * nb]

    def per_head(fn, a):
        return jnp.concatenate([fn(a[:, h * HEAD_DIM:(h + 1) * HEAD_DIM]) for h in range(N_HEADS)], axis=1)

    uv = jax.nn.gelu(z[:, :2 * A_WIDTH])
    vn = per_head(lambda a: _rms(a, avg_ref[...]), uv[:, A_WIDTH:])
    for t in range(n_tok):
        vo_ref[t] = tok(vn, t)
        mixed = brow_ref[t:t + 1, :]
        for s in range(t + 1):
            mixed = mixed + wrow_ref[t * n_tok + s:t * n_tok + s + 1, :] * tok(vn, s)
        a_out = per_head(lambda a: _rms(a, aog_ref[...]), tok(uv[:, :A_WIDTH], t) * mixed)
        ob_ref[t * nb:(t + 1) * nb, :A_WIDTH] = a_out

    zq = z[:, OFF_QKV:OFF_GATE]
    seq = [conv_ref[j] for j in range(CONV_W - 1)] + [tok(zq, t) for t in range(n_tok)]
    cw = cw_ref[...]
    for j in range(CONV_W - 1):
        co_ref[j] = seq[n_tok + j]
    abp = abp_ref[...]
    g = -jnp.exp(abp[0:1]) * _softplus(zab[:, :128] + abp[1:2])
    decay = jnp.exp(g)
    beta = jax.nn.sigmoid(zab[:, 128:])

    ys = []
    for t in range(n_tok):
        y = seq[t] * cw[0:1]
        for j in range(1, CONV_W):
            y = y + seq[t + j] * cw[j:j + 1]
        ys.append(y)
    qkv = _silu(jnp.concatenate(ys, axis=0))

    n_rows = n_tok * nb
    for h in range(N_HEADS):
        q = _l2(qkv[:, h * HEAD_DIM:(h + 1) * HEAD_DIM]) * (HEAD_DIM ** -0.5)
        k = _l2(qkv[:, B_WIDTH + h * HEAD_DIM:B_WIDTH + (h + 1) * HEAD_DIM])
        v = qkv[:, 2 * B_WIDTH + h * HEAD_DIM:2 * B_WIDTH + (h + 1) * HEAD_DIM]
        a_full = jnp.broadcast_to(decay[:, h:h + 1], (n_rows, HEAD_DIM))
        cols = jnp.concatenate([k, beta[:, h:h + 1] * k, q, a_full], axis=0).T
        for b in range(nb):
            s = s_ref[b, h]
            for t in range(n_tok):
                r = t * nb + b
                k_c, bk_c = cols[:, r:r + 1], cols[:, n_rows + r:n_rows + r + 1]
                q_c, a_c = cols[:, 2 * n_rows + r:2 * n_rows + r + 1], cols[:, 3 * n_rows + r:3 * n_rows + r + 1]
                ks = jnp.sum(k_c * s, axis=0, keepdims=True)
                s = a_c * s + bk_c * (v[r:r + 1] - a_full[r:r + 1] * ks)
                o_ref[r:r + 1, h * HEAD_DIM:(h + 1) * HEAD_DIM] = jnp.sum(q_c * s, axis=0, keepdims=True)
            so_ref[b, h] = s

    o = o_ref[...]
    gate = z[:, OFF_GATE:]
    ob_ref[:, A_WIDTH:] = per_head(lambda a: _rms(a, bog_ref[...]), o) * _silu(gate)
    out = x + _dot(ob_ref[...], wout_ref[...])
    xo_ref[...] = out.reshape(n_tok, nb, D_MODEL)


def _mix_sample(x, s0, conv0, lw):
    n_tok, bsz, _ = x.shape
    nb = SAMPLE_GROUP
    tok_spec = lambda width: pl.BlockSpec((n_tok, nb, width), lambda i: (0, i, 0))
    s_spec = pl.BlockSpec((nb, N_HEADS, HEAD_DIM, HEAD_DIM), lambda i: (i, 0, 0, 0))
    c_spec = pl.BlockSpec((CONV_W - 1, nb, 3 * B_WIDTH), lambda i: (0, i, 0))
    in_specs = [tok_spec(D_MODEL), s_spec, c_spec,
                _const_spec((1, D_MODEL)), _const_spec((D_MODEL, Z_MAIN)), _const_spec((D_MODEL, 256)),
                _const_spec((1, HEAD_DIM)), _const_spec((1, HEAD_DIM)), _const_spec((1, HEAD_DIM)),
                _const_spec((n_tok * n_tok, A_WIDTH)), _const_spec((n_tok, A_WIDTH)),
                _const_spec((CONV_W, 3 * B_WIDTH)), _const_spec((2, 128)), _const_spec((D_MODEL, D_MODEL))]
    out_specs = [tok_spec(D_MODEL), s_spec, c_spec, tok_spec(A_WIDTH)]
    out_shape = [jax.ShapeDtypeStruct(x.shape, F32), jax.ShapeDtypeStruct(s0.shape, F32),
                 jax.ShapeDtypeStruct(conv0.shape, F32), jax.ShapeDtypeStruct((n_tok, bsz, A_WIDTH), F32)]
    return pl.pallas_call(
        _mix_sample_kernel,
        grid=(bsz // nb,),
        in_specs=in_specs,
        out_specs=out_specs,
        out_shape=out_shape,
        scratch_shapes=[pltpu.VMEM((n_tok * nb, B_WIDTH), F32), pltpu.VMEM((n_tok * nb, D_MODEL), F32)],
        compiler_params=pltpu.CompilerParams(dimension_semantics=("arbitrary",),
                                             vmem_limit_bytes=VMEM_LIMIT_BYTES),
    )(x, s0, conv0, lw["n_mix"], lw["w_main"], lw["w_ab"], lw["a_v_gain"], lw["a_out_gain"], lw["b_out_gain"],
      lw["a_w_rows"], lw["a_b_rows"], lw["b_conv"], lw["ab_par"], lw["w_out"])


def _prep_layer(i, n_tok, norm_ffn1, w_ffn1_in, w_ffn1_out, norm_mix, w_in, a_v_gain, a_spatial_w, a_spatial_b,
                a_out_gain, b_conv_w, b_a_log, b_dt_bias, b_out_gain, w_out, norm_ffn2, w_ffn2_in, w_ffn2_out,
                norm_ple, w_ple_gate, w_ple_proj):
    o_ab = OFF_GATE
    w = w_in[i]
    w_main = jnp.concatenate([w[:, :o_ab], w[:, o_ab + 2 * N_HEADS:]], axis=1).astype(BF16)
    w_ab = jnp.zeros((D_MODEL, 256), F32)
    w_ab = w_ab.at[:, 0:N_HEADS].set(w[:, o_ab:o_ab + N_HEADS])
    w_ab = w_ab.at[:, 128:128 + N_HEADS].set(w[:, o_ab + N_HEADS:o_ab + 2 * N_HEADS]).astype(BF16)
    ab_par = jnp.zeros((2, 128), F32).at[0, :N_HEADS].set(b_a_log[i]).at[1, :N_HEADS].set(b_dt_bias[i])
    ws_small = a_spatial_w[i][:, :n_tok, :n_tok]
    a_w_rows = jnp.repeat(jnp.transpose(ws_small, (1, 2, 0)).reshape(n_tok * n_tok, N_HEADS), HEAD_DIM, axis=1)
    a_b_rows = jnp.repeat(jnp.transpose(a_spatial_b[i][:, :n_tok]), HEAD_DIM, axis=1)
    return dict(
        n_f1=norm_ffn1[i][None], w_f1_in=w_ffn1_in[i].astype(BF16), w_f1_out=w_ffn1_out[i].astype(BF16),
        n_mix=norm_mix[i][None], w_main=w_main, w_ab=w_ab, ab_par=ab_par,
        a_v_gain=a_v_gain[i][None], a_out_gain=a_out_gain[i][None], b_out_gain=b_out_gain[i][None],
        a_w_s=a_spatial_w[i], a_b_s_t=jnp.transpose(a_spatial_b[i]), a_w_rows=a_w_rows, a_b_rows=a_b_rows,
        b_conv=b_conv_w[i], w_out=w_out[i].astype(BF16),
        n_f2=norm_ffn2[i][None], w_f2_in=w_ffn2_in[i].astype(BF16), w_f2_out=w_ffn2_out[i].astype(BF16),
        n_ple=norm_ple[i][None], w_ple_gate=w_ple_gate[i].astype(BF16), w_ple_proj=w_ple_proj[i].astype(BF16),
    )


def kernel(x_prompt, x_sample, state_S, state_conv, p_prompt, p_sample, norm_ffn1, w_ffn1_in, w_ffn1_out, norm_mix, w_in, a_v_gain, a_spatial_w, a_spatial_b, a_out_gain, b_conv_w, b_a_log, b_dt_bias, b_out_gain, w_out, norm_ffn2, w_ffn2_in, w_ffn2_out, norm_ple, w_ple_gate, w_ple_proj, final_norm):
    bsz, length, _ = x_prompt.shape
    dec_bsz, n_tok, _ = x_sample.shape
    assert length % MIX_ROWS == 0 and MIX_ROWS % CHUNK_A == 0 and dec_bsz % SAMPLE_GROUP == 0
    assert n_tok % CHUNK_A != 0 and n_tok % CHUNK_D != 0

    xp = x_prompt.reshape(bsz * length, D_MODEL)
    xs = jnp.transpose(x_sample, (1, 0, 2)).reshape(n_tok * dec_bsz, D_MODEL)
    pp = p_prompt.reshape(DEPTH, bsz * length, PLE_DIM)
    ps = jnp.transpose(p_sample, (0, 2, 1, 3)).reshape(DEPTH, n_tok * dec_bsz, PLE_DIM)
    conv_s = jnp.transpose(state_conv, (0, 2, 1, 3))

    s_prompt, c_prompt, s_sample, c_sample, v_sample = [], [], [], [], []
    for i in range(DEPTH):
        lw = _prep_layer(i, n_tok, norm_ffn1, w_ffn1_in, w_ffn1_out, norm_mix, w_in, a_v_gain, a_spatial_w,
                         a_spatial_b, a_out_gain, b_conv_w, b_a_log, b_dt_bias, b_out_gain, w_out, norm_ffn2,
                         w_ffn2_in, w_ffn2_out, norm_ple, w_ple_gate, w_ple_proj)
        final = final_norm[None] if i == DEPTH - 1 else None

        xp = _ffn(xp, lw["n_f1"], lw["w_f1_in"], lw["w_f1_out"])
        xp, sp, cp = _mix_prompt(xp.reshape(bsz, length, D_MODEL), lw)
        xp = _ffn(xp.reshape(bsz * length, D_MODEL), lw["n_f2"], lw["w_f2_in"], lw["w_f2_out"],
                  ple=(pp[i], lw["n_ple"], lw["w_ple_gate"], lw["w_ple_proj"]), final_gain=final)

        xs = _ffn(xs, lw["n_f1"], lw["w_f1_in"], lw["w_f1_out"])
        xs, ss, cs, vs = _mix_sample(xs.reshape(n_tok, dec_bsz, D_MODEL), state_S[i], conv_s[i], lw)
        xs = _ffn(xs.reshape(n_tok * dec_bsz, D_MODEL), lw["n_f2"], lw["w_f2_in"], lw["w_f2_out"],
                  ple=(ps[i], lw["n_ple"], lw["w_ple_gate"], lw["w_ple_proj"]), final_gain=final)

        s_prompt.append(sp)
        c_prompt.append(cp)
        s_sample.append(ss)
        c_sample.append(jnp.transpose(cs, (1, 0, 2)))
        v_sample.append(jnp.transpose(vs, (1, 0, 2)).reshape(dec_bsz, n_tok, N_HEADS, HEAD_DIM))

    y_prompt = xp.reshape(bsz, length, D_MODEL)
    y_sample = jnp.transpose(xs.reshape(n_tok, dec_bsz, D_MODEL), (1, 0, 2))
    return (y_prompt, y_sample, jnp.stack(s_prompt), jnp.stack(c_prompt),
            jnp.stack(s_sample), jnp.stack(c_sample), jnp.stack(v_sample))
```

```python
import functools

import jax
import jax.numpy as jnp
from jax import lax
from jax.experimental import pallas as pl
from jax.experimental.pallas import tpu as pltpu

F32 = jnp.float32
BF16 = jnp.bfloat16
EPS = 1e-6
HIGHEST = lax.Precision.HIGHEST

D_MODEL = 1024
D_FF = 2816
DEPTH = 4
N_HEADS = 4
HEAD_DIM = 128
A_WIDTH = N_HEADS * HEAD_DIM
B_WIDTH = N_HEADS * HEAD_DIM
CHUNK_A = 128
CHUNK_D = 64
CONV_W = 4
PLE_DIM = 256
Z_MAIN = 2 * A_WIDTH + 4 * B_WIDTH
OFF_QKV = 2 * A_WIDTH
OFF_GATE = OFF_QKV + 3 * B_WIDTH

VMEM_LIMIT_BYTES = 52 * 1024 * 1024
MXU_N = 256
FFN_ROWS = 512
MIX_ROWS = 256
SAMPLE_GROUP = 8


def _rms(x, gain):
    return x * lax.rsqrt(jnp.mean(x * x, axis=-1, keepdims=True) + EPS) * gain


def _l2(x):
    return x * lax.rsqrt(jnp.sum(x * x, axis=-1, keepdims=True) + EPS)


def _silu(x):
    return x * jax.nn.sigmoid(x)


def _softplus(x):
    return jnp.maximum(x, 0.0) + jnp.log1p(jnp.exp(-jnp.abs(x)))


def _dot(a, b):
    return jnp.dot(a.astype(BF16), b.astype(BF16), preferred_element_type=F32)


def _dot_nt(a, b):
    return lax.dot_general(a.astype(BF16), b.astype(BF16), (((1,), (1,)), ((), ())),
                           preferred_element_type=F32)


def _dot_f32(a, b):
    return jnp.dot(a, b, precision=HIGHEST, preferred_element_type=F32)


def _const_spec(shape):
    zeros = (0,) * len(shape)
    return pl.BlockSpec(shape, lambda *_: zeros, pipeline_mode=pl.Buffered(1))


def _ffn_kernel(*refs, with_ple, with_final):
    x_ref, gain_ref, wg_ref, wu_ref, wo_ref = refs[:5]
    o_ref = refs[-1]
    x = x_ref[...]
    xn = _rms(x, gain_ref[...]).astype(BF16)
    acc = jnp.zeros_like(x)
    for c in range(D_FF // MXU_N):
        sl = slice(c * MXU_N, (c + 1) * MXU_N)
        gate = jnp.dot(xn, wg_ref[:, sl], preferred_element_type=F32)
        up = jnp.dot(xn, wu_ref[:, sl], preferred_element_type=F32)
        h = (_silu(gate) * up).astype(BF16)
        acc = acc + jnp.dot(h, wo_ref[sl, :], preferred_element_type=F32)
    x = x + 0.5 * acc
    if with_ple:
        p_ref, npl_ref, wpg_ref, wpp_ref = refs[5:9]
        emb = _dot(p_ref[...], wpp_ref[...])
        gate = _dot(_rms(x, npl_ref[...]), wpg_ref[...])
        x = x + emb * jax.nn.sigmoid(gate)
    if with_final:
        x = _rms(x, refs[9][...])
    o_ref[...] = x


def _ffn(x, gain, w_in, w_out, ple=None, final_gain=None):
    rows = x.shape[0]
    tm = min(FFN_ROWS, rows)
    row_spec = pl.BlockSpec((tm, D_MODEL), lambda i: (i, 0))
    in_specs = [row_spec, _const_spec((1, D_MODEL)),
                pl.BlockSpec((D_MODEL, D_FF), lambda i: (0, 0), pipeline_mode=pl.Buffered(1)),
                pl.BlockSpec((D_MODEL, D_FF), lambda i: (0, 1), pipeline_mode=pl.Buffered(1)),
                _const_spec((D_FF, D_MODEL))]
    args = [x, gain, w_in, w_in, w_out]
    if ple is not None:
        p, n_ple, w_gate, w_proj = ple
        in_specs += [pl.BlockSpec((tm, PLE_DIM), lambda i: (i, 0)), _const_spec((1, D_MODEL)),
                     _const_spec((D_MODEL, D_MODEL)), _const_spec((PLE_DIM, D_MODEL))]
        args += [p, n_ple, w_gate, w_proj]
    if final_gain is not None:
        in_specs.append(_const_spec((1, D_MODEL)))
        args.append(final_gain)
    return pl.pallas_call(
        functools.partial(_ffn_kernel, with_ple=ple is not None, with_final=final_gain is not None),
        grid=(rows // tm,),
        in_specs=in_specs,
        out_specs=row_spec,
        out_shape=jax.ShapeDtypeStruct(x.shape, F32),
        compiler_params=pltpu.CompilerParams(dimension_semantics=("arbitrary",),
                                             vmem_limit_bytes=VMEM_LIMIT_BYTES),
    )(*args)


def _split(a):
    hi = a.astype(BF16)
    lo = (a - hi.astype(F32)).astype(BF16)
    return hi, lo


def _dot3(a_hi, a_lo, b_hi, b_lo):
    m = a_hi.shape[0]
    both = jnp.dot(jnp.concatenate([a_hi, a_lo], axis=0), b_hi, preferred_element_type=F32)
    return both[:m] + both[m:] + jnp.dot(a_hi, b_lo, preferred_element_type=F32)


def _fold(block_diag):
    n = block_diag.shape[0] // CHUNK_D
    out = block_diag[0:CHUNK_D]
    for g in range(1, n):
        out = out + block_diag[g * CHUNK_D:(g + 1) * CHUNK_D]
    return out


def _expand(packed, diag_mask):
    n = packed.shape[1] // CHUNK_D
    return jnp.where(diag_mask, jnp.concatenate([packed] * n, axis=0), jnp.zeros((), packed.dtype))


def _unit_lower_inverse_packed(l_packed, diag_mask):
    c, width = l_packed.shape
    row = lax.broadcasted_iota(jnp.int32, (c, width), 0)
    col = lax.broadcasted_iota(jnp.int32, (c, width), 1) % c
    l_hi, l_lo = _split(l_packed)
    zero = jnp.zeros((), BF16)

    def lower_left(bs):
        return (row // (2 * bs) == col // (2 * bs)) & ((row // bs) % 2 == 1) & ((col // bs) % 2 == 0)

    x = jnp.where(row == col, 1.0, 0.0) - jnp.where(lower_left(1), l_packed, 0.0)
    bs = 2
    while bs < c:
        sel = lower_left(bs)
        x_hi, x_lo = _split(x)
        y = _dot3(jnp.where(sel, l_hi, zero), jnp.where(sel, l_lo, zero),
                  _expand(x_hi, diag_mask), _expand(x_lo, diag_mask))
        y_hi, y_lo = _split(y)
        x = x - _dot3(x_hi, x_lo, _expand(y_hi, diag_mask), _expand(y_lo, diag_mask))
        bs *= 2
    return x


def _mix_prompt_kernel(x_ref, nmix_ref, wmain_ref, wab_ref, avg_ref, aog_ref, bog_ref, wsp_ref, bsp_ref,
                       cw_ref, abp_ref, wout_ref,
                       xo_ref, s_ref, ct_ref,
                       zext_ref, ob_ref):
    tl = x_ref.shape[0]
    step = pl.program_id(1)

    @pl.when(step == 0)
    def _():
        s_ref[...] = jnp.zeros_like(s_ref)
        zext_ref[0:8, :] = jnp.zeros((8, 3 * B_WIDTH), F32)

    x = x_ref[...]
    xn = _rms(x, nmix_ref[...]).astype(BF16)
    z = jnp.dot(xn, wmain_ref[...], preferred_element_type=F32)
    zab = jnp.dot(xn, wab_ref[...], preferred_element_type=F32)

    uv = jax.nn.gelu(z[:, :2 * A_WIDTH])
    row = lax.broadcasted_iota(jnp.int32, (CHUNK_A, CHUNK_A), 0)
    col = lax.broadcasted_iota(jnp.int32, (CHUNK_A, CHUNK_A), 1)
    causal = col <= row
    for h in range(N_HEADS):
        hs = slice(h * HEAD_DIM, (h + 1) * HEAD_DIM)
        u_h = uv[:, hs]
        v_h = _rms(uv[:, A_WIDTH + h * HEAD_DIM:A_WIDTH + (h + 1) * HEAD_DIM], avg_ref[...]).astype(BF16)
        w_h = jnp.where(causal, wsp_ref[h], 0.0).astype(BF16)
        bias_h = bsp_ref[:, h:h + 1]
        for c in range(tl // CHUNK_A):
            rs = slice(c * CHUNK_A, (c + 1) * CHUNK_A)
            mixed = jnp.dot(w_h, v_h[rs], preferred_element_type=F32) + bias_h
            ob_ref[rs, hs] = _rms(u_h[rs] * mixed, aog_ref[...]).astype(BF16)

    zext_ref[8:8 + tl, :] = z[:, OFF_QKV:OFF_GATE]
    cw = cw_ref[...]
    y = zext_ref[5:5 + tl, :] * cw[0:1]
    for j in range(1, CONV_W):
        y = y + zext_ref[5 + j:5 + j + tl, :] * cw[j:j + 1]
    tail = zext_ref[tl + 5:tl + 8, :]
    ct_ref[...] = tail
    zext_ref[5:8, :] = tail
    qkv = _silu(y)

    abp = abp_ref[...]
    g = -jnp.exp(abp[0:1]) * _softplus(zab[:, :128] + abp[1:2])
    beta = jax.nn.sigmoid(zab[:, 128:])

    r2 = lax.broadcasted_iota(jnp.int32, (tl, tl), 0)
    c2 = lax.broadcasted_iota(jnp.int32, (tl, tl), 1)
    same = (r2 // CHUNK_D) == (c2 // CHUNK_D)
    lower_ones = jnp.where(same & (c2 <= r2), 1.0, 0.0).astype(F32)
    upper_ones = jnp.where(same & (r2 <= c2), 1.0, 0.0).astype(F32)
    gam = _dot_f32(lower_ones, g)
    gam_t = _dot_f32(g.T, upper_ones)

    rb = lax.broadcasted_iota(jnp.int32, (CHUNK_D, CHUNK_D), 0)
    cb = lax.broadcasted_iota(jnp.int32, (CHUNK_D, CHUNK_D), 1)
    incl = cb <= rb
    strict_bd = same & (c2 < r2)

    heads = []
    for h in range(N_HEADS):
        q_h = _l2(qkv[:, h * HEAD_DIM:(h + 1) * HEAD_DIM]) * (HEAD_DIM ** -0.5)
        k_h = _l2(qkv[:, B_WIDTH + h * HEAD_DIM:B_WIDTH + (h + 1) * HEAD_DIM])
        v_h = qkv[:, 2 * B_WIDTH + h * HEAD_DIM:2 * B_WIDTH + (h + 1) * HEAD_DIM]
        gc_h = gam[:, h:h + 1]
        gr_h = gam_t[h:h + 1, :]
        bc_h = beta[:, h:h + 1]
        eg_h = jnp.exp(gc_h)
        kk = _dot_nt(k_h, k_h)
        a_bd = jnp.where(strict_bd, bc_h * kk * jnp.exp(jnp.where(strict_bd, gc_h - gr_h, 0.0)), 0.0)
        inv = _unit_lower_inverse_packed(_fold(a_bd), same)
        inv_hi, inv_lo = _split(inv)
        rhs_hi, rhs_lo = _split(jnp.concatenate([bc_h * v_h, (bc_h * eg_h) * k_h], axis=1))
        sol = _dot3(_expand(inv_hi, same), _expand(inv_lo, same), rhs_hi, rhs_lo)
        heads.append((q_h, k_h, gc_h, gr_h, eg_h, sol))

    for i in range(tl // CHUNK_D):
        rs = slice(i * CHUNK_D, (i + 1) * CHUNK_D)
        for h in range(N_HEADS):
            q_h, k_h, gc_h, gr_h, eg_h, sol = heads[h]
            q, k, gc = q_h[rs], k_h[rs], gc_h[rs]
            decay = jnp.where(incl, jnp.exp(jnp.where(incl, gc - gr_h[:, rs], 0.0)), 0.0)
            qk = _dot_nt(q, k) * decay
            g_last = gc[CHUNK_D - 1:CHUNK_D]
            kend = k * jnp.exp(g_last - gc)
            s_old = s_ref[h]
            from_s = _dot(jnp.concatenate([sol[rs, HEAD_DIM:], q * eg_h[rs]], axis=0), s_old)
            u = sol[rs, :HEAD_DIM] - from_s[:CHUNK_D]
            from_u = _dot(jnp.concatenate([qk, kend.T], axis=0), u)
            o = from_s[CHUNK_D:] + from_u[:CHUNK_D]
            s_ref[h] = jnp.exp(g_last) * s_old + from_u[CHUNK_D:]
            gate = z[rs, OFF_GATE + h * HEAD_DIM:OFF_GATE + (h + 1) * HEAD_DIM]
            ob_ref[rs, A_WIDTH + h * HEAD_DIM:A_WIDTH + (h + 1) * HEAD_DIM] = (
                _rms(o, bog_ref[...]) * _silu(gate)).astype(BF16)

    xo_ref[...] = x + jnp.dot(ob_ref[...], wout_ref[...], preferred_element_type=F32)


def _mix_prompt(x, lw):
    bsz, length, _ = x.shape
    tl = MIX_ROWS
    row_spec = pl.BlockSpec((None, tl, D_MODEL), lambda b, t: (b, t, 0))
    in_specs = [row_spec,
                _const_spec((1, D_MODEL)), _const_spec((D_MODEL, Z_MAIN)), _const_spec((D_MODEL, 256)),
                _const_spec((1, HEAD_DIM)), _const_spec((1, HEAD_DIM)), _const_spec((1, HEAD_DIM)),
                _const_spec((N_HEADS, CHUNK_A, CHUNK_A)), _const_spec((CHUNK_A, N_HEADS)),
                _const_spec((CONV_W, 3 * B_WIDTH)), _const_spec((2, 128)), _const_spec((D_MODEL, D_MODEL))]
    out_specs = [row_spec,
                 pl.BlockSpec((None, N_HEADS, HEAD_DIM, HEAD_DIM), lambda b, t: (b, 0, 0, 0)),
                 pl.BlockSpec((None, CONV_W - 1, 3 * B_WIDTH), lambda b, t: (b, 0, 0))]
    out_shape = [jax.ShapeDtypeStruct(x.shape, F32),
                 jax.ShapeDtypeStruct((bsz, N_HEADS, HEAD_DIM, HEAD_DIM), F32),
                 jax.ShapeDtypeStruct((bsz, CONV_W - 1, 3 * B_WIDTH), F32)]
    return pl.pallas_call(
        _mix_prompt_kernel,
        grid=(bsz, length // tl),
        in_specs=in_specs,
        out_specs=out_specs,
        out_shape=out_shape,
        scratch_shapes=[pltpu.VMEM((tl + 8, 3 * B_WIDTH), F32), pltpu.VMEM((tl, D_MODEL), BF16)],
        compiler_params=pltpu.CompilerParams(dimension_semantics=("arbitrary", "arbitrary"),
                                             vmem_limit_bytes=VMEM_LIMIT_BYTES),
    )(x, lw["n_mix"], lw["w_main"], lw["w_ab"], lw["a_v_gain"], lw["a_out_gain"], lw["b_out_gain"],
      lw["a_w_s"], lw["a_b_s_t"], lw["b_conv"], lw["ab_par"], lw["w_out"])


def _mix_sample_kernel(x_ref, s_ref, conv_ref, nmix_ref, wmain_ref, wab_ref, avg_ref, aog_ref, bog_ref,
                       wrow_ref, brow_ref, cw_ref, abp_ref, wout_ref,
                       xo_ref, so_ref, co_ref, vo_ref,
                       o_ref, ob_ref):
    n_tok, nb = x_ref.shape[0], x_ref.shape[1]
    x = x_ref[...].reshape(n_tok * nb, D_MODEL)
    xn = _rms(x, nmix_ref[...]).astype(BF16)
    z = jnp.dot(xn, wmain_ref[...], preferred_element_type=F32)
    zab = jnp.dot(xn, wab_ref[...], preferred_element_type=F32)

    def tok(a, t):
        return a[t * nb:(t + 1) * nb]

    def per_head(fn, a):
        return jnp.concatenate([fn(a[:, h * HEAD_DIM:(h + 1) * HEAD_DIM]) for h in range(N_HEADS)], axis=1)

    uv = jax.nn.gelu(z[:, :2 * A_WIDTH])
    vn = per_head(lambda a: _rms(a, avg_ref[...]), uv[:, A_WIDTH:])
    for t in range(n_tok):
        vo_ref[t] = tok(vn, t)
        mixed = brow_ref[t:t + 1, :]
        for s in range(t + 1):
            mixed = mixed + wrow_ref[t * n_tok + s:t * n_tok + s + 1, :] * tok(vn, s)
        a_out = per_head(lambda a: _rms(a, aog_ref[...]), tok(uv[:, :A_WIDTH], t) * mixed)
        ob_ref[t * nb:(t + 1) * nb, :A_WIDTH] = a_out

    zq = z[:, OFF_QKV:OFF_GATE]
    seq = [conv_ref[j] for j in range(CONV_W - 1)] + [tok(zq, t) for t in range(n_tok)]
    cw = cw_ref[...]
    for j in range(CONV_W - 1):
        co_ref[j] = seq[n_tok + j]
    abp = abp_ref[...]
    g = -jnp.exp(abp[0:1]) * _softplus(zab[:, :128] + abp[1:2])
    decay = jnp.exp(g)
    beta = jax.nn.sigmoid(zab[:, 128:])

    ys = []
    for t in range(n_tok):
        y = seq[t] * cw[0:1]
        for j in range(1, CONV_W):
            y = y + seq[t + j] * cw[j:j + 1]
        ys.append(y)
    qkv = _silu(jnp.concatenate(ys, axis=0))

    n_rows = n_tok * nb
    for h in range(N_HEADS):
        q = _l2(qkv[:, h * HEAD_DIM:(h + 1) * HEAD_DIM]) * (HEAD_DIM ** -0.5)
        k = _l2(qkv[:, B_WIDTH + h * HEAD_DIM:B_WIDTH + (h + 1) * HEAD_DIM])
        v = qkv[:, 2 * B_WIDTH + h * HEAD_DIM:2 * B_WIDTH + (h + 1) * HEAD_DIM]
        a_full = jnp.broadcast_to(decay[:, h:h + 1], (n_rows, HEAD_DIM))
        cols = jnp.concatenate([k, beta[:, h:h + 1] * k, q, a_full], axis=0).T
        for b in range(nb):
            s = s_ref[b, h]
            for t in range(n_tok):
                r = t * nb + b
                k_c, bk_c = cols[:, r:r + 1], cols[:, n_rows + r:n_rows + r + 1]
                q_c, a_c = cols[:, 2 * n_rows + r:2 * n_rows + r + 1], cols[:, 3 * n_rows + r:3 * n_rows + r + 1]
                ks = jnp.sum(k_c * s, axis=0, keepdims=True)
                s = a_c * s + bk_c * (v[r:r + 1] - a_full[r:r + 1] * ks)
                o_ref[r:r + 1, h * HEAD_DIM:(h + 1) * HEAD_DIM] = jnp.sum(q_c * s, axis=0, keepdims=True)
            so_ref[b, h] = s

    o = o_ref[...]
    gate = z[:, OFF_GATE:]
    ob_ref[:, A_WIDTH:] = per_head(lambda a: _rms(a, bog_ref[...]), o) * _silu(gate)
    out = x + _dot(ob_ref[...], wout_ref[...])
    xo_ref[...] = out.reshape(n_tok, nb, D_MODEL)


def _mix_sample(x, s0, conv0, lw):
    n_tok, bsz, _ = x.shape
    nb = SAMPLE_GROUP
    tok_spec = lambda width: pl.BlockSpec((n_tok, nb, width), lambda i: (0, i, 0))
    s_spec = pl.BlockSpec((nb, N_HEADS, HEAD_DIM, HEAD_DIM), lambda i: (i, 0, 0, 0))
    c_spec = pl.BlockSpec((CONV_W - 1, nb, 3 * B_WIDTH), lambda i: (0, i, 0))
    in_specs = [tok_spec(D_MODEL), s_spec, c_spec,
                _const_spec((1, D_MODEL)), _const_spec((D_MODEL, Z_MAIN)), _const_spec((D_MODEL, 256)),
                _const_spec((1, HEAD_DIM)), _const_spec((1, HEAD_DIM)), _const_spec((1, HEAD_DIM)),
                _const_spec((n_tok * n_tok, A_WIDTH)), _const_spec((n_tok, A_WIDTH)),
                _const_spec((CONV_W, 3 * B_WIDTH)), _const_spec((2, 128)), _const_spec((D_MODEL, D_MODEL))]
    out_specs = [tok_spec(D_MODEL), s_spec, c_spec, tok_spec(A_WIDTH)]
    out_shape = [jax.ShapeDtypeStruct(x.shape, F32), jax.ShapeDtypeStruct(s0.shape, F32),
                 jax.ShapeDtypeStruct(conv0.shape, F32), jax.ShapeDtypeStruct((n_tok, bsz, A_WIDTH), F32)]
    return pl.pallas_call(
        _mix_sample_kernel,
        grid=(bsz // nb,),
        in_specs=in_specs,
        out_specs=out_specs,
        out_shape=out_shape,
        scratch_shapes=[pltpu.VMEM((n_tok * nb, B_WIDTH), F32), pltpu.VMEM((n_tok * nb, D_MODEL), F32)],
        compiler_params=pltpu.CompilerParams(dimension_semantics=("arbitrary",),
                                             vmem_limit_bytes=VMEM_LIMIT_BYTES),
    )(x, s0, conv0, lw["n_mix"], lw["w_main"], lw["w_ab"], lw["a_v_gain"], lw["a_out_gain"], lw["b_out_gain"],
      lw["a_w_rows"], lw["a_b_rows"], lw["b_conv"], lw["ab_par"], lw["w_out"])


def _prep_layer(i, n_tok, norm_ffn1, w_ffn1_in, w_ffn1_out, norm_mix, w_in, a_v_gain, a_spatial_w, a_spatial_b,
                a_out_gain, b_conv_w, b_a_log, b_dt_bias, b_out_gain, w_out, norm_ffn2, w_ffn2_in, w_ffn2_out,
                norm_ple, w_ple_gate, w_ple_proj):
    o_ab = OFF_GATE
    w = w_in[i]
    w_main = jnp.concatenate([w[:, :o_ab], w[:, o_ab + 2 * N_HEADS:]], axis=1).astype(BF16)
    w_ab = jnp.zeros((D_MODEL, 256), F32)
    w_ab = w_ab.at[:, 0:N_HEADS].set(w[:, o_ab:o_ab + N_HEADS])
    w_ab = w_ab.at[:, 128:128 + N_HEADS].set(w[:, o_ab + N_HEADS:o_ab + 2 * N_HEADS]).astype(BF16)
    ab_par = jnp.zeros((2, 128), F32).at[0, :N_HEADS].set(b_a_log[i]).at[1, :N_HEADS].set(b_dt_bias[i])
    ws_small = a_spatial_w[i][:, :n_tok, :n_tok]
    a_w_rows = jnp.repeat(jnp.transpose(ws_small, (1, 2, 0)).reshape(n_tok * n_tok, N_HEADS), HEAD_DIM, axis=1)
    a_b_rows = jnp.repeat(jnp.transpose(a_spatial_b[i][:, :n_tok]), HEAD_DIM, axis=1)
    return dict(
        n_f1=norm_ffn1[i][None], w_f1_in=w_ffn1_in[i].astype(BF16), w_f1_out=w_ffn1_out[i].astype(BF16),
        n_mix=norm_mix[i][None], w_main=w_main, w_ab=w_ab, ab_par=ab_par,
        a_v_gain=a_v_gain[i][None], a_out_gain=a_out_gain[i][None], b_out_gain=b_out_gain[i][None],
        a_w_s=a_spatial_w[i], a_b_s_t=jnp.transpose(a_spatial_b[i]), a_w_rows=a_w_rows, a_b_rows=a_b_rows,
        b_conv=b_conv_w[i], w_out=w_out[i].astype(BF16),
        n_f2=norm_ffn2[i][None], w_f2_in=w_ffn2_in[i].astype(BF16), w_f2_out=w_ffn2_out[i].astype(BF16),
        n_ple=norm_ple[i][None], w_ple_gate=w_ple_gate[i].astype(BF16), w_ple_proj=w_ple_proj[i].astype(BF16),
    )


def kernel(x_prompt, x_sample, state_S, state_conv, p_prompt, p_sample, norm_ffn1, w_ffn1_in, w_ffn1_out, norm_mix, w_in, a_v_gain, a_spatial_w, a_spatial_b, a_out_gain, b_conv_w, b_a_log, b_dt_bias, b_out_gain, w_out, norm_ffn2, w_ffn2_in, w_ffn2_out, norm_ple, w_ple_gate, w_ple_proj, final_norm):
    bsz, length, _ = x_prompt.shape
    dec_bsz, n_tok, _ = x_sample.shape
    assert length % MIX_ROWS == 0 and MIX_ROWS % CHUNK_A == 0 and dec_bsz % SAMPLE_GROUP == 0
    assert n_tok % CHUNK_A != 0 and n_tok % CHUNK_D != 0

    xp = x_prompt.reshape(bsz * length, D_MODEL)
    xs = jnp.transpose(x_sample, (1, 0, 2)).reshape(n_tok * dec_bsz, D_MODEL)
    pp = p_prompt.reshape(DEPTH, bsz * length, PLE_DIM)
    ps = jnp.transpose(p_sample, (0, 2, 1, 3)).reshape(DEPTH, n_tok * dec_bsz, PLE_DIM)
    conv_s = jnp.transpose(state_conv, (0, 2, 1, 3))

    s_prompt, c_prompt, s_sample, c_sample, v_sample = [], [], [], [], []
    for i in range(DEPTH):
        lw = _prep_layer(i, n_tok, norm_ffn1, w_ffn1_in, w_ffn1_out, norm_mix, w_in, a_v_gain, a_spatial_w,
                         a_spatial_b, a_out_gain, b_conv_w, b_a_log, b_dt_bias, b_out_gain, w_out, norm_ffn2,
                         w_ffn2_in, w_ffn2_out, norm_ple, w_ple_gate, w_ple_proj)
        final = final_norm[None] if i == DEPTH - 1 else None

        xp = _ffn(xp, lw["n_f1"], lw["w_f1_in"], lw["w_f1_out"])
        xp, sp, cp = _mix_prompt(xp.reshape(bsz, length, D_MODEL), lw)
        xp = _ffn(xp.reshape(bsz * length, D_MODEL), lw["n_f2"], lw["w_f2_in"], lw["w_f2_out"],
                  ple=(pp[i], lw["n_ple"], lw["w_ple_gate"], lw["w_ple_proj"]), final_gain=final)

        xs = _ffn(xs, lw["n_f1"], lw["w_f1_in"], lw["w_f1_out"])
        xs, ss, cs, vs = _mix_sample(xs.reshape(n_tok, dec_bsz, D_MODEL), state_S[i], conv_s[i], lw)
        xs = _ffn(xs.reshape(n_tok * dec_bsz, D_MODEL), lw["n_f2"], lw["w_f2_in"], lw["w_f2_out"],
                  ple=(ps[i], lw["n_ple"], lw["w_ple_gate"], lw["w_ple_proj"]), final_gain=final)

        s_prompt.append(sp)
        c_prompt.append(cp)
        s_sample.append(ss)
        c_sample.append(jnp.transpose(cs, (1, 0, 2)))
        v_sample.append(jnp.transpose(vs, (1, 0, 2)).reshape(dec_bsz, n_tok, N_HEADS, HEAD_DIM))

    y_prompt = xp.reshape(bsz, length, D_MODEL)
    y_sample = jnp.transpose(xs.reshape(n_tok, dec_bsz, D_MODEL), (1, 0, 2))
    return (y_prompt, y_sample, jnp.stack(s_prompt), jnp.stack(c_prompt),
            jnp.stack(s_sample), jnp.stack(c_sample), jnp.stack(v_sample))
```

```python
import functools

import jax
import jax.numpy as jnp
from jax import lax
from jax.experimental import pallas as pl
from jax.experimental.pallas import tpu as pltpu

F32 = jnp.float32
BF16 = jnp.bfloat16
EPS = 1e-6
HIGHEST = lax.Precision.HIGHEST

D_MODEL = 1024
D_FF = 2816
DEPTH = 4
N_HEADS = 4
HEAD_DIM = 128
A_WIDTH = N_HEADS * HEAD_DIM
B_WIDTH = N_HEADS * HEAD_DIM
CHUNK_A = 128
CHUNK_D = 64
CONV_W = 4
PLE_DIM = 256
Z_MAIN = 2 * A_WIDTH + 4 * B_WIDTH
OFF_QKV = 2 * A_WIDTH
OFF_GATE = OFF_QKV + 3 * B_WIDTH

VMEM_LIMIT_BYTES = 52 * 1024 * 1024
MXU_N = 256
FFN_ROWS = 512
MIX_ROWS = 256
SAMPLE_GROUP = 8


def _rms(x, gain):
    return x * lax.rsqrt(jnp.mean(x * x, axis=-1, keepdims=True) + EPS) * gain


def _l2(x):
    return x * lax.rsqrt(jnp.sum(x * x, axis=-1, keepdims=True) + EPS)


def _silu(x):
    return x * jax.nn.sigmoid(x)


def _softplus(x):
    return jnp.maximum(x, 0.0) + jnp.log1p(jnp.exp(-jnp.abs(x)))


def _dot(a, b):
    return jnp.dot(a.astype(BF16), b.astype(BF16), preferred_element_type=F32)


def _dot_nt(a, b):
    return lax.dot_general(a.astype(BF16), b.astype(BF16), (((1,), (1,)), ((), ())),
                           preferred_element_type=F32)


def _dot_f32(a, b):
    return jnp.dot(a, b, precision=HIGHEST, preferred_element_type=F32)


def _const_spec(shape):
    zeros = (0,) * len(shape)
    return pl.BlockSpec(shape, lambda *_: zeros, pipeline_mode=pl.Buffered(1))


def _layer_spec(shape, layer, block=None):
    index = (layer,) + (0,) * (len(shape) - 1) + (0 if block is None else block,)
    return pl.BlockSpec((None,) + tuple(shape), lambda *_: index, pipeline_mode=pl.Buffered(1))


def _ffn_kernel(*refs, with_ple, with_final):
    x_ref, gain_ref, wg_ref, wu_ref, wo_ref = refs[:5]
    o_ref = refs[-1]
    x = x_ref[...]
    xn = _rms(x, gain_ref[...]).astype(BF16)
    acc = jnp.zeros_like(x)
    for c in range(D_FF // MXU_N):
        sl = slice(c * MXU_N, (c + 1) * MXU_N)
        gate = jnp.dot(xn, wg_ref[:, sl], preferred_element_type=F32)
        up = jnp.dot(xn, wu_ref[:, sl], preferred_element_type=F32)
        h = (_silu(gate) * up).astype(BF16)
        acc = acc + jnp.dot(h, wo_ref[sl, :], preferred_element_type=F32)
    x = x + 0.5 * acc
    if with_ple:
        p_ref, npl_ref, wpg_ref, wpp_ref = refs[5:9]
        emb = _dot(p_ref[...], wpp_ref[...])
        gate = _dot(_rms(x, npl_ref[...]), wpg_ref[...])
        x = x + emb * jax.nn.sigmoid(gate)
    if with_final:
        x = _rms(x, refs[9][...])
    o_ref[...] = x


def _ffn(x, layer, gain, w_in, w_out, ple=None, final_gain=None):
    rows = x.shape[0]
    tm = min(FFN_ROWS, rows)
    row_spec = pl.BlockSpec((tm, D_MODEL), lambda i: (i, 0))
    in_specs = [row_spec, _layer_spec((1, D_MODEL), layer),
                _layer_spec((D_MODEL, D_FF), layer, block=0), _layer_spec((D_MODEL, D_FF), layer, block=1),
                _layer_spec((D_FF, D_MODEL), layer)]
    args = [x, gain, w_in, w_in, w_out]
    if ple is not None:
        p, n_ple, w_gate, w_proj = ple
        in_specs += [pl.BlockSpec((None, tm, PLE_DIM), lambda i: (layer, i, 0)), _layer_spec((1, D_MODEL), layer),
                     _layer_spec((D_MODEL, D_MODEL), layer), _layer_spec((PLE_DIM, D_MODEL), layer)]
        args += [p, n_ple, w_gate, w_proj]
    if final_gain is not None:
        in_specs.append(_layer_spec((1, D_MODEL), 0))
        args.append(final_gain)
    return pl.pallas_call(
        functools.partial(_ffn_kernel, with_ple=ple is not None, with_final=final_gain is not None),
        grid=(rows // tm,),
        in_specs=in_specs,
        out_specs=row_spec,
        out_shape=jax.ShapeDtypeStruct(x.shape, F32),
        compiler_params=pltpu.CompilerParams(dimension_semantics=("arbitrary",),
                                             vmem_limit_bytes=VMEM_LIMIT_BYTES),
    )(*args)


def _split(a):
    hi = a.astype(BF16)
    lo = (a - hi.astype(F32)).astype(BF16)
    return hi, lo


def _dot3(a_hi, a_lo, b_hi, b_lo):
    m = a_hi.shape[0]
    both = jnp.dot(jnp.concatenate([a_hi, a_lo], axis=0), b_hi, preferred_element_type=F32)
    return both[:m] + both[m:] + jnp.dot(a_hi, b_lo, preferred_element_type=F32)


def _fold(block_diag):
    n = block_diag.shape[0] // CHUNK_D
    out = block_diag[0:CHUNK_D]
    for g in range(1, n):
        out = out + block_diag[g * CHUNK_D:(g + 1) * CHUNK_D]
    return out


def _expand(packed, diag_mask):
    n = packed.shape[1] // CHUNK_D
    return jnp.where(diag_mask, jnp.concatenate([packed] * n, axis=0), jnp.zeros((), packed.dtype))


def _unit_lower_inverse_packed(l_packed, diag_mask):
    c, width = l_packed.shape
    row = lax.broadcasted_iota(jnp.int32, (c, width), 0)
    col = lax.broadcasted_iota(jnp.int32, (c, width), 1) % c
    l_hi, l_lo = _split(l_packed)
    zero = jnp.zeros((), BF16)

    def lower_left(bs):
        return (row // (2 * bs) == col // (2 * bs)) & ((row // bs) % 2 == 1) & ((col // bs) % 2 == 0)

    x = jnp.where(row == col, 1.0, 0.0) - jnp.where(lower_left(1), l_packed, 0.0)
    bs = 2
    while bs < c:
        sel = lower_left(bs)
        x_hi, x_lo = _split(x)
        y = _dot3(jnp.where(sel, l_hi, zero), jnp.where(sel, l_lo, zero),
                  _expand(x_hi, diag_mask), _expand(x_lo, diag_mask))
        y_hi, y_lo = _split(y)
        x = x - _dot3(x_hi, x_lo, _expand(y_hi, diag_mask), _expand(y_lo, diag_mask))
        bs *= 2
    return x


def _mix_prompt_kernel(x_ref, nmix_ref, wmain_ref, wab_ref, avg_ref, aog_ref, bog_ref, wsp_ref, bsp_ref,
                       cw_ref, abp_ref, wout_ref,
                       xo_ref, s_ref, ct_ref,
                       zext_ref, ob_ref):
    tl = x_ref.shape[0]
    step = pl.program_id(1)

    @pl.when(step == 0)
    def _():
        s_ref[...] = jnp.zeros_like(s_ref)
        zext_ref[0:8, :] = jnp.zeros((8, 3 * B_WIDTH), F32)

    x = x_ref[...]
    xn = _rms(x, nmix_ref[...]).astype(BF16)
    z = jnp.dot(xn, wmain_ref[...], preferred_element_type=F32)
    zab = jnp.dot(xn, wab_ref[...], preferred_element_type=F32)

    uv = jax.nn.gelu(z[:, :2 * A_WIDTH])
    row = lax.broadcasted_iota(jnp.int32, (CHUNK_A, CHUNK_A), 0)
    col = lax.broadcasted_iota(jnp.int32, (CHUNK_A, CHUNK_A), 1)
    causal = col <= row
    for h in range(N_HEADS):
        hs = slice(h * HEAD_DIM, (h + 1) * HEAD_DIM)
        u_h = uv[:, hs]
        v_h = _rms(uv[:, A_WIDTH + h * HEAD_DIM:A_WIDTH + (h + 1) * HEAD_DIM], avg_ref[...]).astype(BF16)
        w_h = jnp.where(causal, wsp_ref[h], 0.0).astype(BF16)
        bias_h = bsp_ref[:, h:h + 1]
        for c in range(tl // CHUNK_A):
            rs = slice(c * CHUNK_A, (c + 1) * CHUNK_A)
            mixed = jnp.dot(w_h, v_h[rs], preferred_element_type=F32) + bias_h
            ob_ref[rs, hs] = _rms(u_h[rs] * mixed, aog_ref[...]).astype(BF16)

    zext_ref[8:8 + tl, :] = z[:, OFF_QKV:OFF_GATE]
    cw = cw_ref[...]
    y = zext_ref[5:5 + tl, :] * cw[0:1]
    for j in range(1, CONV_W):
        y = y + zext_ref[5 + j:5 + j + tl, :] * cw[j:j + 1]
    tail = zext_ref[tl + 5:tl + 8, :]
    ct_ref[...] = tail
    zext_ref[5:8, :] = tail
    qkv = _silu(y)

    abp = abp_ref[...]
    g = -jnp.exp(abp[0:1]) * _softplus(zab[:, :128] + abp[1:2])
    beta = jax.nn.sigmoid(zab[:, 128:])

    r2 = lax.broadcasted_iota(jnp.int32, (tl, tl), 0)
    c2 = lax.broadcasted_iota(jnp.int32, (tl, tl), 1)
    same = (r2 // CHUNK_D) == (c2 // CHUNK_D)
    lower_ones = jnp.where(same & (c2 <= r2), 1.0, 0.0).astype(F32)
    upper_ones = jnp.where(same & (r2 <= c2), 1.0, 0.0).astype(F32)
    gam = _dot_f32(lower_ones, g)
    gam_t = _dot_f32(g.T, upper_ones)

    rb = lax.broadcasted_iota(jnp.int32, (CHUNK_D, CHUNK_D), 0)
    cb = lax.broadcasted_iota(jnp.int32, (CHUNK_D, CHUNK_D), 1)
    incl = cb <= rb
    strict_bd = same & (c2 < r2)

    heads = []
    for h in range(N_HEADS):
        q_h = _l2(qkv[:, h * HEAD_DIM:(h + 1) * HEAD_DIM]) * (HEAD_DIM ** -0.5)
        k_h = _l2(qkv[:, B_WIDTH + h * HEAD_DIM:B_WIDTH + (h + 1) * HEAD_DIM])
        v_h = qkv[:, 2 * B_WIDTH + h * HEAD_DIM:2 * B_WIDTH + (h + 1) * HEAD_DIM]
        gc_h = gam[:, h:h + 1]
        gr_h = gam_t[h:h + 1, :]
        bc_h = beta[:, h:h + 1]
        eg_h = jnp.exp(gc_h)
        kk = _dot_nt(k_h, k_h)
        a_bd = jnp.where(strict_bd, bc_h * kk * jnp.exp(jnp.where(strict_bd, gc_h - gr_h, 0.0)), 0.0)
        inv = _unit_lower_inverse_packed(_fold(a_bd), same)
        inv_hi, inv_lo = _split(inv)
        rhs_hi, rhs_lo = _split(jnp.concatenate([bc_h * v_h, (bc_h * eg_h) * k_h], axis=1))
        sol = _dot3(_expand(inv_hi, same), _expand(inv_lo, same), rhs_hi, rhs_lo)
        heads.append((q_h, k_h, gc_h, gr_h, eg_h, sol))

    for i in range(tl // CHUNK_D):
        rs = slice(i * CHUNK_D, (i + 1) * CHUNK_D)
        for h in range(N_HEADS):
            q_h, k_h, gc_h, gr_h, eg_h, sol = heads[h]
            q, k, gc = q_h[rs], k_h[rs], gc_h[rs]
            decay = jnp.where(incl, jnp.exp(jnp.where(incl, gc - gr_h[:, rs], 0.0)), 0.0)
            qk = _dot_nt(q, k) * decay
            g_last = gc[CHUNK_D - 1:CHUNK_D]
            kend = k * jnp.exp(g_last - gc)
            s_old = s_ref[h]
            from_s = _dot(jnp.concatenate([sol[rs, HEAD_DIM:], q * eg_h[rs]], axis=0), s_old)
            u = sol[rs, :HEAD_DIM] - from_s[:CHUNK_D]
            from_u = _dot(jnp.concatenate([qk, kend.T], axis=0), u)
            o = from_s[CHUNK_D:] + from_u[:CHUNK_D]
            s_ref[h] = jnp.exp(g_last) * s_old + from_u[CHUNK_D:]
            gate = z[rs, OFF_GATE + h * HEAD_DIM:OFF_GATE + (h + 1) * HEAD_DIM]
            ob_ref[rs, A_WIDTH + h * HEAD_DIM:A_WIDTH + (h + 1) * HEAD_DIM] = (
                _rms(o, bog_ref[...]) * _silu(gate)).astype(BF16)

    xo_ref[...] = x + jnp.dot(ob_ref[...], wout_ref[...], preferred_element_type=F32)


def _mixer_weight_specs(layer):
    return [_layer_spec((1, D_MODEL), layer), _layer_spec((D_MODEL, Z_MAIN), layer), _layer_spec((D_MODEL, 256), layer),
            _layer_spec((1, HEAD_DIM), layer), _layer_spec((1, HEAD_DIM), layer), _layer_spec((1, HEAD_DIM), layer)]


def _mixer_weights(w):
    return [w["n_mix"], w["w_main"], w["w_ab"], w["a_v_gain"], w["a_out_gain"], w["b_out_gain"]]


def _mix_prompt(x, layer, w):
    bsz, length, _ = x.shape
    tl = MIX_ROWS
    row_spec = pl.BlockSpec((None, tl, D_MODEL), lambda b, t: (b, t, 0))
    in_specs = [row_spec] + _mixer_weight_specs(layer) + [
        _layer_spec((N_HEADS, CHUNK_A, CHUNK_A), layer), _layer_spec((CHUNK_A, N_HEADS), layer),
        _layer_spec((CONV_W, 3 * B_WIDTH), layer), _layer_spec((2, 128), layer), _layer_spec((D_MODEL, D_MODEL), layer)]
    out_specs = [row_spec,
                 pl.BlockSpec((None, N_HEADS, HEAD_DIM, HEAD_DIM), lambda b, t: (b, 0, 0, 0)),
                 pl.BlockSpec((None, CONV_W - 1, 3 * B_WIDTH), lambda b, t: (b, 0, 0))]
    out_shape = [jax.ShapeDtypeStruct(x.shape, F32),
                 jax.ShapeDtypeStruct((bsz, N_HEADS, HEAD_DIM, HEAD_DIM), F32),
                 jax.ShapeDtypeStruct((bsz, CONV_W - 1, 3 * B_WIDTH), F32)]
    return pl.pallas_call(
        _mix_prompt_kernel,
        grid=(bsz, length // tl),
        in_specs=in_specs,
        out_specs=out_specs,
        out_shape=out_shape,
        scratch_shapes=[pltpu.VMEM((tl + 8, 3 * B_WIDTH), F32), pltpu.VMEM((tl, D_MODEL), BF16)],
        compiler_params=pltpu.CompilerParams(dimension_semantics=("arbitrary", "arbitrary"),
                                             vmem_limit_bytes=VMEM_LIMIT_BYTES),
    )(x, *_mixer_weights(w), w["a_w_s"], w["a_b_s_t"], w["b_conv"], w["ab_par"], w["w_out"])


def _mix_sample_kernel(x_ref, s_ref, cpad_ref, nmix_ref, wmain_ref, wab_ref, avg_ref, aog_ref, bog_ref,
                       coef_ref, bias_ref, cw_ref, abp_ref, wout_ref,
                       xo_ref, so_ref, zq_ref, vo_ref,
                       ob_ref, *, n_tok):
    rows = x_ref.shape[0]
    nb = rows // n_tok
    x = x_ref[...]
    xn = _rms(x, nmix_ref[...]).astype(BF16)
    z = jnp.dot(xn, wmain_ref[...], preferred_element_type=F32)
    zab = jnp.dot(xn, wab_ref[...], preferred_element_type=F32)
    tok = lax.broadcasted_iota(jnp.int32, (rows, 1), 0) % n_tok

    def prev(a, d):
        return pltpu.roll(a, d, axis=0)

    def prev_or_zero(a, d):
        return a if d == 0 else jnp.where(tok >= d, prev(a, d), 0.0)

    def per_head(fn, a):
        return jnp.concatenate([fn(a[:, h * HEAD_DIM:(h + 1) * HEAD_DIM]) for h in range(N_HEADS)], axis=1)

    uv = jax.nn.gelu(z[:, :2 * A_WIDTH])
    vn = per_head(lambda a: _rms(a, avg_ref[...]), uv[:, A_WIDTH:])
    vo_ref[...] = vn
    mixed = bias_ref[...]
    for d in range(n_tok):
        mixed = mixed + coef_ref[d] * prev_or_zero(vn, d)
    ob_ref[:, :A_WIDTH] = per_head(lambda a: _rms(a, aog_ref[...]), uv[:, :A_WIDTH] * mixed)

    zq = z[:, OFF_QKV:OFF_GATE]
    zq_ref[...] = zq
    cpad = cpad_ref[...]
    cw = cw_ref[...]
    y = zq * cw[CONV_W - 1:CONV_W]
    for d in range(1, CONV_W):
        carried = pltpu.roll(cpad, rows - (n_tok - d), axis=0)
        y = y + jnp.where(tok >= d, prev(zq, d), carried) * cw[CONV_W - 1 - d:CONV_W - d]
    qkv = _silu(y)

    abp = abp_ref[...]
    g_all = -jnp.exp(abp[0:1]) * _softplus(zab[:, :128] + abp[1:2])
    beta_all = jax.nn.sigmoid(zab[:, 128:])

    sub = lax.broadcasted_iota(jnp.int32, (8, 1), 0)
    first_half = sub < n_tok
    o_heads, kend_heads, u_heads, btot_heads = [], [], [], []
    for h in range(N_HEADS):
        q = _l2(qkv[:, h * HEAD_DIM:(h + 1) * HEAD_DIM]) * (HEAD_DIM ** -0.5)
        k = _l2(qkv[:, B_WIDTH + h * HEAD_DIM:B_WIDTH + (h + 1) * HEAD_DIM])
        v = qkv[:, 2 * B_WIDTH + h * HEAD_DIM:2 * B_WIDTH + (h + 1) * HEAD_DIM]
        g = jnp.broadcast_to(g_all[:, h:h + 1], (rows, HEAD_DIM))
        beta = jnp.broadcast_to(beta_all[:, h:h + 1], (rows, HEAD_DIM))
        gam = g
        for d in range(1, n_tok):
            gam = gam + prev_or_zero(g, d)
        gam_last = jnp.where(tok == n_tok - 1, gam, 0.0)
        for d in range(1, n_tok):
            gam_last = gam_last + jnp.where(tok == n_tok - 1 - d, pltpu.roll(gam, rows - d, axis=0), 0.0)
        eg = jnp.exp(gam)

        def decay_to(d, gam=gam):
            return jnp.exp(jnp.where(tok >= d, gam - prev(gam, d), 0.0))

        a_sub = [None] + [jnp.where(tok >= d, beta * jnp.sum(k * prev(k, d), axis=-1, keepdims=True) * decay_to(d),
                                    0.0) for d in range(1, n_tok)]
        def forward_substitute(rhs, a_sub=a_sub):
            sol = rhs
            for t in range(1, n_tok):
                acc = rhs
                for d in range(1, t + 1):
                    acc = acc - a_sub[d] * prev(sol, d)
                sol = jnp.where(tok == t, acc, sol)
            return sol

        w_blk = forward_substitute(beta * v)
        kb_blk = forward_substitute((beta * eg) * k)
        qb = q * eg

        kb_s, qb_s = [], []
        for p in range(rows // 8):
            kb_t, qb_t = kb_blk[8 * p:8 * p + 8], qb[8 * p:8 * p + 8]
            f0 = _dot(jnp.where(first_half, kb_t, pltpu.roll(qb_t, n_tok, axis=0)), s_ref[2 * p, h])
            f1 = _dot(jnp.where(first_half, pltpu.roll(kb_t, n_tok, axis=0), qb_t), s_ref[2 * p + 1, h])
            kb_s.append(jnp.where(first_half, f0, pltpu.roll(f1, n_tok, axis=0)))
            qb_s.append(jnp.where(first_half, pltpu.roll(f0, n_tok, axis=0), f1))
        u = w_blk - jnp.concatenate(kb_s, axis=0)
        o = jnp.concatenate(qb_s, axis=0)
        for d in range(n_tok):
            qk = jnp.where(tok >= d, jnp.sum(q * prev(k, d), axis=-1, keepdims=True) * decay_to(d), 0.0)
            o = o + qk * prev_or_zero(u, d)
        o_heads.append(o)
        kend_heads.append(k * jnp.exp(gam_last - gam))
        u_heads.append(u)
        btot_heads.append(jnp.broadcast_to(jnp.exp(gam_last), (rows, HEAD_DIM)))

    kend_t = jnp.concatenate(kend_heads, axis=0).T
    u_all = jnp.concatenate(u_heads, axis=0).astype(BF16)
    owner = lax.broadcasted_iota(jnp.int32, (1, N_HEADS * rows), 1) // n_tok
    for h in range(N_HEADS):
        for b in range(nb):
            mine = jnp.where(owner == h * nb + b, kend_t, 0.0).astype(BF16)
            last = b * n_tok + n_tok - 1
            so_ref[b, h] = btot_heads[h][last:last + 1] * s_ref[b, h] + jnp.dot(
                mine, u_all, preferred_element_type=F32)

    gate = z[:, OFF_GATE:]
    ob_ref[:, A_WIDTH:] = per_head(lambda a: _rms(a, bog_ref[...]), jnp.concatenate(o_heads, axis=1)) * _silu(gate)
    xo_ref[...] = x + _dot(ob_ref[...], wout_ref[...])


def _mix_sample(x, layer, state_s, cpad, w, n_tok):
    rows_total = x.shape[0]
    nb = SAMPLE_GROUP
    rows = nb * n_tok
    row_spec = lambda width: pl.BlockSpec((rows, width), lambda i: (i, 0))
    s_blk = (nb, N_HEADS, HEAD_DIM, HEAD_DIM)
    in_specs = [row_spec(D_MODEL),
                pl.BlockSpec((None,) + s_blk, lambda i: (layer, i, 0, 0, 0)),
                pl.BlockSpec((None, rows, 3 * B_WIDTH), lambda i: (layer, i, 0))] + _mixer_weight_specs(layer) + [
        _layer_spec((n_tok, rows, A_WIDTH), layer), _layer_spec((rows, A_WIDTH), layer),
        _layer_spec((CONV_W, 3 * B_WIDTH), layer), _layer_spec((2, 128), layer), _layer_spec((D_MODEL, D_MODEL), layer)]
    out_specs = [row_spec(D_MODEL), pl.BlockSpec(s_blk, lambda i: (i, 0, 0, 0)),
                 row_spec(3 * B_WIDTH), row_spec(A_WIDTH)]
    out_shape = [jax.ShapeDtypeStruct(x.shape, F32), jax.ShapeDtypeStruct(state_s.shape[1:], F32),
                 jax.ShapeDtypeStruct((rows_total, 3 * B_WIDTH), F32), jax.ShapeDtypeStruct((rows_total, A_WIDTH), F32)]
    return pl.pallas_call(
        functools.partial(_mix_sample_kernel, n_tok=n_tok),
        grid=(rows_total // rows,),
        in_specs=in_specs,
        out_specs=out_specs,
        out_shape=out_shape,
        scratch_shapes=[pltpu.VMEM((rows, D_MODEL), F32)],
        compiler_params=pltpu.CompilerParams(dimension_semantics=("arbitrary",),
                                             vmem_limit_bytes=VMEM_LIMIT_BYTES),
    )(x, state_s, cpad, *_mixer_weights(w), w["a_coef"], w["a_bias"], w["b_conv"], w["ab_par"], w["w_out"])


def _prep_weights(n_tok, norm_ffn1, w_ffn1_in, w_ffn1_out, norm_mix, w_in, a_v_gain, a_spatial_w, a_spatial_b,
                  a_out_gain, b_conv_w, b_a_log, b_dt_bias, b_out_gain, w_out, norm_ffn2, w_ffn2_in, w_ffn2_out,
                  norm_ple, w_ple_gate, w_ple_proj):
    o_ab = OFF_GATE
    w_main = jnp.concatenate([w_in[:, :, :o_ab], w_in[:, :, o_ab + 2 * N_HEADS:]], axis=2).astype(BF16)
    lane_pad = jnp.zeros((DEPTH, D_MODEL, 128 - N_HEADS), F32)
    w_ab = jnp.concatenate([w_in[:, :, o_ab:o_ab + N_HEADS], lane_pad,
                            w_in[:, :, o_ab + N_HEADS:o_ab + 2 * N_HEADS], lane_pad], axis=2).astype(BF16)
    par_pad = jnp.zeros((DEPTH, 128 - N_HEADS), F32)
    ab_par = jnp.stack([jnp.concatenate([b_a_log, par_pad], axis=1),
                        jnp.concatenate([b_dt_bias, par_pad], axis=1)], axis=1)

    def sample_rows(a):
        return jnp.tile(jnp.repeat(jnp.transpose(a, (0, 2, 1)), HEAD_DIM, axis=2), (1, SAMPLE_GROUP, 1))

    ws_small = a_spatial_w[:, :, :n_tok, :n_tok]
    a_coef = jnp.stack([sample_rows(jnp.pad(jnp.diagonal(ws_small, offset=-d, axis1=2, axis2=3),
                                            ((0, 0), (0, 0), (d, 0)))) for d in range(n_tok)], axis=1)
    return dict(
        n_f1=norm_ffn1[:, None], w_f1_in=w_ffn1_in.astype(BF16), w_f1_out=w_ffn1_out.astype(BF16),
        n_mix=norm_mix[:, None], w_main=w_main, w_ab=w_ab, ab_par=ab_par,
        a_v_gain=a_v_gain[:, None], a_out_gain=a_out_gain[:, None], b_out_gain=b_out_gain[:, None],
        a_w_s=a_spatial_w, a_b_s_t=jnp.transpose(a_spatial_b, (0, 2, 1)),
        a_coef=a_coef, a_bias=sample_rows(a_spatial_b[:, :, :n_tok]),
        b_conv=b_conv_w, w_out=w_out.astype(BF16),
        n_f2=norm_ffn2[:, None], w_f2_in=w_ffn2_in.astype(BF16), w_f2_out=w_ffn2_out.astype(BF16),
        n_ple=norm_ple[:, None], w_ple_gate=w_ple_gate.astype(BF16), w_ple_proj=w_ple_proj.astype(BF16),
    )


def kernel(x_prompt, x_sample, state_S, state_conv, p_prompt, p_sample, norm_ffn1, w_ffn1_in, w_ffn1_out, norm_mix, w_in, a_v_gain, a_spatial_w, a_spatial_b, a_out_gain, b_conv_w, b_a_log, b_dt_bias, b_out_gain, w_out, norm_ffn2, w_ffn2_in, w_ffn2_out, norm_ple, w_ple_gate, w_ple_proj, final_norm):
    bsz, length, _ = x_prompt.shape
    dec_bsz, n_tok, _ = x_sample.shape
    assert length % MIX_ROWS == 0 and MIX_ROWS % CHUNK_A == 0 and dec_bsz % SAMPLE_GROUP == 0
    assert n_tok % CHUNK_A != 0 and n_tok % CHUNK_D != 0
    assert 2 * n_tok == 8 and N_HEADS * SAMPLE_GROUP * n_tok == HEAD_DIM and n_tok >= CONV_W - 1

    w = _prep_weights(n_tok, norm_ffn1, w_ffn1_in, w_ffn1_out, norm_mix, w_in, a_v_gain, a_spatial_w, a_spatial_b,
                      a_out_gain, b_conv_w, b_a_log, b_dt_bias, b_out_gain, w_out, norm_ffn2, w_ffn2_in,
                      w_ffn2_out, norm_ple, w_ple_gate, w_ple_proj)
    final = final_norm[None, None]
    xp = x_prompt.reshape(bsz * length, D_MODEL)
    xs = x_sample.reshape(dec_bsz * n_tok, D_MODEL)
    pp = p_prompt.reshape(DEPTH, bsz * length, PLE_DIM)
    ps = p_sample.reshape(DEPTH, dec_bsz * n_tok, PLE_DIM)
    keep = CONV_W - 1
    cpad = jnp.pad(state_conv, ((0, 0), (0, 0), (n_tok - keep, 0), (0, 0))).reshape(DEPTH, dec_bsz * n_tok, 3 * B_WIDTH)

    s_prompt, c_prompt, s_sample, c_sample, v_sample = [], [], [], [], []
    for i in range(DEPTH):
        last = dict(final_gain=final) if i == DEPTH - 1 else {}
        ple = (w["n_ple"], w["w_ple_gate"], w["w_ple_proj"])

        xp = _ffn(xp, i, w["n_f1"], w["w_f1_in"], w["w_f1_out"])
        xp, sp, cp = _mix_prompt(xp.reshape(bsz, length, D_MODEL), i, w)
        xp = _ffn(xp.reshape(bsz * length, D_MODEL), i, w["n_f2"], w["w_f2_in"], w["w_f2_out"], ple=(pp,) + ple, **last)

        xs = _ffn(xs, i, w["n_f1"], w["w_f1_in"], w["w_f1_out"])
        xs, ss, zq, vs = _mix_sample(xs, i, state_S, cpad, w, n_tok)
        xs = _ffn(xs, i, w["n_f2"], w["w_f2_in"], w["w_f2_out"], ple=(ps,) + ple, **last)

        s_prompt.append(sp)
        c_prompt.append(cp)
        s_sample.append(ss)
        c_sample.append(zq.reshape(dec_bsz, n_tok, 3 * B_WIDTH)[:, n_tok - keep:])
        v_sample.append(vs.reshape(dec_bsz, n_tok, N_HEADS, HEAD_DIM))

    return (xp.reshape(bsz, length, D_MODEL), xs.reshape(dec_bsz, n_tok, D_MODEL), jnp.stack(s_prompt),
            jnp.stack(c_prompt), jnp.stack(s_sample), jnp.stack(c_sample), jnp.stack(v_sample))
```

```python
import functools

import jax
import jax.numpy as jnp
from jax import lax
from jax.experimental import pallas as pl
from jax.experimental.pallas import tpu as pltpu

F32 = jnp.float32
BF16 = jnp.bfloat16
EPS = 1e-6

D_MODEL = 1024
D_FF = 2816
DEPTH = 4
N_HEADS = 4
HEAD_DIM = 128
A_WIDTH = N_HEADS * HEAD_DIM
B_WIDTH = N_HEADS * HEAD_DIM
CHUNK_A = 128
CHUNK_D = 64
CONV_W = 4
PLE_DIM = 256
Z_MAIN = 2 * A_WIDTH + 4 * B_WIDTH
OFF_QKV = 2 * A_WIDTH
OFF_GATE = OFF_QKV + 3 * B_WIDTH

VMEM_LIMIT_BYTES = 52 * 1024 * 1024
MXU_N = 256
FFN_ROWS = 512
MIX_ROWS = 256
SAMPLE_GROUP = 8


def _rms(x, gain):
    return x * lax.rsqrt(jnp.mean(x * x, axis=-1, keepdims=True) + EPS) * gain


def _l2(x):
    return x * lax.rsqrt(jnp.sum(x * x, axis=-1, keepdims=True) + EPS)


def _silu(x):
    return x * jax.nn.sigmoid(x)


def _softplus(x):
    return jnp.maximum(x, 0.0) + jnp.log1p(jnp.exp(-jnp.abs(x)))


def _dot(a, b):
    return jnp.dot(a.astype(BF16), b.astype(BF16), preferred_element_type=F32)


def _dot_nt(a, b):
    return lax.dot_general(a.astype(BF16), b.astype(BF16), (((1,), (1,)), ((), ())),
                           preferred_element_type=F32)


def _split3(a):
    p1 = a.astype(BF16)
    r1 = a - p1.astype(F32)
    p2 = r1.astype(BF16)
    p3 = (r1 - p2.astype(F32)).astype(BF16)
    return p1, p2, p3


def _const_spec(shape):
    zeros = (0,) * len(shape)
    return pl.BlockSpec(shape, lambda *_: zeros, pipeline_mode=pl.Buffered(1))


def _layer_spec(shape, layer, block=None):
    index = (layer,) + (0,) * (len(shape) - 1) + (0 if block is None else block,)
    return pl.BlockSpec((None,) + tuple(shape), lambda *_: index, pipeline_mode=pl.Buffered(1))


def _ffn_kernel(*refs, with_ple, with_final):
    x_ref, gain_ref, wg_ref, wu_ref, wo_ref = refs[:5]
    o_ref = refs[-1]
    x = x_ref[...]
    xn = _rms(x, gain_ref[...]).astype(BF16)
    acc = jnp.zeros_like(x)
    for c in range(D_FF // MXU_N):
        sl = slice(c * MXU_N, (c + 1) * MXU_N)
        gate = jnp.dot(xn, wg_ref[:, sl], preferred_element_type=F32)
        up = jnp.dot(xn, wu_ref[:, sl], preferred_element_type=F32)
        h = (_silu(gate) * up).astype(BF16)
        acc = acc + jnp.dot(h, wo_ref[sl, :], preferred_element_type=F32)
    x = x + 0.5 * acc
    if with_ple:
        p_ref, npl_ref, wpg_ref, wpp_ref = refs[5:9]
        emb = _dot(p_ref[...], wpp_ref[...])
        gate = _dot(_rms(x, npl_ref[...]), wpg_ref[...])
        x = x + emb * jax.nn.sigmoid(gate)
    if with_final:
        x = _rms(x, refs[9][...])
    o_ref[...] = x


def _ffn(x, layer, gain, w_in, w_out, ple=None, final_gain=None):
    rows = x.shape[0]
    tm = min(FFN_ROWS, rows)
    row_spec = pl.BlockSpec((tm, D_MODEL), lambda i: (i, 0))
    in_specs = [row_spec, _layer_spec((1, D_MODEL), layer),
                _layer_spec((D_MODEL, D_FF), layer, block=0), _layer_spec((D_MODEL, D_FF), layer, block=1),
                _layer_spec((D_FF, D_MODEL), layer)]
    args = [x, gain, w_in, w_in, w_out]
    if ple is not None:
        p, n_ple, w_gate, w_proj = ple
        in_specs += [pl.BlockSpec((None, tm, PLE_DIM), lambda i: (layer, i, 0)), _layer_spec((1, D_MODEL), layer),
                     _layer_spec((D_MODEL, D_MODEL), layer), _layer_spec((PLE_DIM, D_MODEL), layer)]
        args += [p, n_ple, w_gate, w_proj]
    if final_gain is not None:
        in_specs.append(_layer_spec((1, D_MODEL), 0))
        args.append(final_gain)
    return pl.pallas_call(
        functools.partial(_ffn_kernel, with_ple=ple is not None, with_final=final_gain is not None),
        grid=(rows // tm,),
        in_specs=in_specs,
        out_specs=row_spec,
        out_shape=jax.ShapeDtypeStruct(x.shape, F32),
        compiler_params=pltpu.CompilerParams(dimension_semantics=("arbitrary",),
                                             vmem_limit_bytes=VMEM_LIMIT_BYTES),
    )(*args)


def _split(a):
    hi = a.astype(BF16)
    lo = (a - hi.astype(F32)).astype(BF16)
    return hi, lo


def _dot3(a_hi, a_lo, b_hi, b_lo):
    m = a_hi.shape[0]
    both = jnp.dot(jnp.concatenate([a_hi, a_lo], axis=0), b_hi, preferred_element_type=F32)
    return both[:m] + both[m:] + jnp.dot(a_hi, b_lo, preferred_element_type=F32)


def _fold(block_diag):
    n = block_diag.shape[0] // CHUNK_D
    out = block_diag[0:CHUNK_D]
    for g in range(1, n):
        out = out + block_diag[g * CHUNK_D:(g + 1) * CHUNK_D]
    return out


def _expand(packed, diag_mask):
    n = packed.shape[1] // CHUNK_D
    return jnp.where(diag_mask, jnp.concatenate([packed] * n, axis=0), jnp.zeros((), packed.dtype))


def _unit_lower_inverses_packed(l_packed_list, diag_mask):
    c, width = l_packed_list[0].shape
    row = lax.broadcasted_iota(jnp.int32, (c, width), 0)
    col = lax.broadcasted_iota(jnp.int32, (c, width), 1) % c
    zero = jnp.zeros((), BF16)

    def lower_left(bs):
        return (row // (2 * bs) == col // (2 * bs)) & ((row // bs) % 2 == 1) & ((col // bs) % 2 == 0)

    l_split = [_split(l) for l in l_packed_list]
    xs = [jnp.where(row == col, 1.0, 0.0) - jnp.where(lower_left(1), l, 0.0) for l in l_packed_list]
    bs = 2
    while bs < c:
        sel = lower_left(bs)
        x_split = [_split(x) for x in xs]
        ys = [_dot3(jnp.where(sel, l_hi, zero), jnp.where(sel, l_lo, zero),
                    _expand(x_hi, diag_mask), _expand(x_lo, diag_mask))
              for (l_hi, l_lo), (x_hi, x_lo) in zip(l_split, x_split)]
        y_split = [_split(y) for y in ys]
        xs = [x - _dot3(x_hi, x_lo, _expand(y_hi, diag_mask), _expand(y_lo, diag_mask))
              for x, (x_hi, x_lo), (y_hi, y_lo) in zip(xs, x_split, y_split)]
        bs *= 2
    return xs


def _mix_prompt_kernel(x_ref, nmix_ref, wmain_ref, wab_ref, avg_ref, aog_ref, bog_ref, wsp_ref, bsp_ref,
                       cw_ref, abp_ref, wout_ref,
                       xo_ref, s_ref, ct_ref,
                       zext_ref, ob_ref):
    tl = x_ref.shape[0]
    step = pl.program_id(1)

    @pl.when(step == 0)
    def _():
        s_ref[...] = jnp.zeros_like(s_ref)
        zext_ref[0:8, :] = jnp.zeros((8, 3 * B_WIDTH), F32)

    x = x_ref[...]
    xn = _rms(x, nmix_ref[...]).astype(BF16)
    z = jnp.dot(xn, wmain_ref[...], preferred_element_type=F32)
    zab = jnp.dot(xn, wab_ref[...], preferred_element_type=F32)

    uv = jax.nn.gelu(z[:, :2 * A_WIDTH])
    row = lax.broadcasted_iota(jnp.int32, (CHUNK_A, CHUNK_A), 0)
    col = lax.broadcasted_iota(jnp.int32, (CHUNK_A, CHUNK_A), 1)
    causal = col <= row
    for h in range(N_HEADS):
        hs = slice(h * HEAD_DIM, (h + 1) * HEAD_DIM)
        u_h = uv[:, hs]
        v_h = _rms(uv[:, A_WIDTH + h * HEAD_DIM:A_WIDTH + (h + 1) * HEAD_DIM], avg_ref[...]).astype(BF16)
        w_h = jnp.where(causal, wsp_ref[h], 0.0).astype(BF16)
        bias_h = bsp_ref[:, h:h + 1]
        for c in range(tl // CHUNK_A):
            rs = slice(c * CHUNK_A, (c + 1) * CHUNK_A)
            mixed = jnp.dot(w_h, v_h[rs], preferred_element_type=F32) + bias_h
            ob_ref[rs, hs] = _rms(u_h[rs] * mixed, aog_ref[...]).astype(BF16)

    zext_ref[8:8 + tl, :] = z[:, OFF_QKV:OFF_GATE]
    cw = cw_ref[...]
    y = zext_ref[5:5 + tl, :] * cw[0:1]
    for j in range(1, CONV_W):
        y = y + zext_ref[5 + j:5 + j + tl, :] * cw[j:j + 1]
    tail = zext_ref[tl + 5:tl + 8, :]
    ct_ref[...] = tail
    zext_ref[5:8, :] = tail
    qkv = _silu(y)

    abp = abp_ref[...]
    g = -jnp.exp(abp[0:1]) * _softplus(zab[:, :128] + abp[1:2])
    beta = jax.nn.sigmoid(zab[:, 128:])

    r2 = lax.broadcasted_iota(jnp.int32, (tl, tl), 0)
    c2 = lax.broadcasted_iota(jnp.int32, (tl, tl), 1)
    same = (r2 // CHUNK_D) == (c2 // CHUNK_D)
    lower_ones = jnp.where(same & (c2 <= r2), 1.0, 0.0).astype(BF16)
    upper_ones = jnp.where(same & (r2 <= c2), 1.0, 0.0).astype(BF16)
    by_col = jnp.dot(lower_ones, jnp.concatenate(_split3(g), axis=1), preferred_element_type=F32)
    gam = by_col[:, :128] + by_col[:, 128:256] + by_col[:, 256:]
    by_row = jnp.dot(jnp.concatenate(_split3(g.T), axis=0), upper_ones, preferred_element_type=F32)
    gam_t = by_row[:128] + by_row[128:256] + by_row[256:]

    rb = lax.broadcasted_iota(jnp.int32, (CHUNK_D, CHUNK_D), 0)
    cb = lax.broadcasted_iota(jnp.int32, (CHUNK_D, CHUNK_D), 1)
    incl = cb <= rb
    strict_bd = same & (c2 < r2)

    heads, a_packed, rhs = [], [], []
    for h in range(N_HEADS):
        q_h = _l2(qkv[:, h * HEAD_DIM:(h + 1) * HEAD_DIM]) * (HEAD_DIM ** -0.5)
        k_h = _l2(qkv[:, B_WIDTH + h * HEAD_DIM:B_WIDTH + (h + 1) * HEAD_DIM])
        v_h = qkv[:, 2 * B_WIDTH + h * HEAD_DIM:2 * B_WIDTH + (h + 1) * HEAD_DIM]
        gc_h = gam[:, h:h + 1]
        gr_h = gam_t[h:h + 1, :]
        bc_h = beta[:, h:h + 1]
        eg_h = jnp.exp(gc_h)
        kk = _dot_nt(k_h, k_h)
        a_bd = jnp.where(strict_bd, bc_h * kk * jnp.exp(jnp.where(strict_bd, gc_h - gr_h, 0.0)), 0.0)
        a_packed.append(_fold(a_bd))
        rhs.append(_split(jnp.concatenate([bc_h * v_h, (bc_h * eg_h) * k_h], axis=1)))
        heads.append([q_h, k_h, gc_h, gr_h, eg_h])
    inv_split = [_split(inv) for inv in _unit_lower_inverses_packed(a_packed, same)]
    for h in range(N_HEADS):
        heads[h].append(_dot3(_expand(inv_split[h][0], same), _expand(inv_split[h][1], same),
                              rhs[h][0], rhs[h][1]))

    for i in range(tl // CHUNK_D):
        rs = slice(i * CHUNK_D, (i + 1) * CHUNK_D)
        for h in range(N_HEADS):
            q_h, k_h, gc_h, gr_h, eg_h, sol = heads[h]
            q, k, gc = q_h[rs], k_h[rs], gc_h[rs]
            decay = jnp.where(incl, jnp.exp(jnp.where(incl, gc - gr_h[:, rs], 0.0)), 0.0)
            qk = _dot_nt(q, k) * decay
            g_last = gc[CHUNK_D - 1:CHUNK_D]
            kend = k * jnp.exp(g_last - gc)
            s_old = s_ref[h]
            from_s = _dot(jnp.concatenate([sol[rs, HEAD_DIM:], q * eg_h[rs]], axis=0), s_old)
            u = sol[rs, :HEAD_DIM] - from_s[:CHUNK_D]
            from_u = _dot(jnp.concatenate([qk, kend.T], axis=0), u)
            o = from_s[CHUNK_D:] + from_u[:CHUNK_D]
            s_ref[h] = jnp.exp(g_last) * s_old + from_u[CHUNK_D:]
            gate = z[rs, OFF_GATE + h * HEAD_DIM:OFF_GATE + (h + 1) * HEAD_DIM]
            ob_ref[rs, A_WIDTH + h * HEAD_DIM:A_WIDTH + (h + 1) * HEAD_DIM] = (
                _rms(o, bog_ref[...]) * _silu(gate)).astype(BF16)

    xo_ref[...] = x + jnp.dot(ob_ref[...], wout_ref[...], preferred_element_type=F32)


def _mixer_weight_specs(layer):
    return [_layer_spec((1, D_MODEL), layer), _layer_spec((D_MODEL, Z_MAIN), layer), _layer_spec((D_MODEL, 256), layer),
            _layer_spec((1, HEAD_DIM), layer), _layer_spec((1, HEAD_DIM), layer), _layer_spec((1, HEAD_DIM), layer)]


def _mixer_weights(w):
    return [w["n_mix"], w["w_main"], w["w_ab"], w["a_v_gain"], w["a_out_gain"], w["b_out_gain"]]


def _mix_prompt(x, layer, w):
    bsz, length, _ = x.shape
    tl = MIX_ROWS
    row_spec = pl.BlockSpec((None, tl, D_MODEL), lambda b, t: (b, t, 0))
    in_specs = [row_spec] + _mixer_weight_specs(layer) + [
        _layer_spec((N_HEADS, CHUNK_A, CHUNK_A), layer), _layer_spec((CHUNK_A, N_HEADS), layer),
        _layer_spec((CONV_W, 3 * B_WIDTH), layer), _layer_spec((2, 128), layer), _layer_spec((D_MODEL, D_MODEL), layer)]
    out_specs = [row_spec,
                 pl.BlockSpec((None, N_HEADS, HEAD_DIM, HEAD_DIM), lambda b, t: (b, 0, 0, 0)),
                 pl.BlockSpec((None, CONV_W - 1, 3 * B_WIDTH), lambda b, t: (b, 0, 0))]
    out_shape = [jax.ShapeDtypeStruct(x.shape, F32),
                 jax.ShapeDtypeStruct((bsz, N_HEADS, HEAD_DIM, HEAD_DIM), F32),
                 jax.ShapeDtypeStruct((bsz, CONV_W - 1, 3 * B_WIDTH), F32)]
    return pl.pallas_call(
        _mix_prompt_kernel,
        grid=(bsz, length // tl),
        in_specs=in_specs,
        out_specs=out_specs,
        out_shape=out_shape,
        scratch_shapes=[pltpu.VMEM((tl + 8, 3 * B_WIDTH), F32), pltpu.VMEM((tl, D_MODEL), BF16)],
        compiler_params=pltpu.CompilerParams(dimension_semantics=("arbitrary", "arbitrary"),
                                             vmem_limit_bytes=VMEM_LIMIT_BYTES),
    )(x, *_mixer_weights(w), w["a_w_s"], w["a_b_s_t"], w["b_conv"], w["ab_par"], w["w_out"])


def _mix_sample_kernel(x_ref, s_ref, cpad_ref, nmix_ref, wmain_ref, wab_ref, avg_ref, aog_ref, bog_ref,
                       coef_ref, bias_ref, cw_ref, abp_ref, wout_ref,
                       xo_ref, so_ref, zq_ref, vo_ref,
                       ob_ref, *, n_tok):
    rows = x_ref.shape[0]
    nb = rows // n_tok
    x = x_ref[...]
    xn = _rms(x, nmix_ref[...]).astype(BF16)
    z = jnp.dot(xn, wmain_ref[...], preferred_element_type=F32)
    zab = jnp.dot(xn, wab_ref[...], preferred_element_type=F32)
    tok = lax.broadcasted_iota(jnp.int32, (rows, 1), 0) % n_tok

    def prev(a, d):
        return pltpu.roll(a, d, axis=0)

    def prev_or_zero(a, d):
        return a if d == 0 else jnp.where(tok >= d, prev(a, d), 0.0)

    def per_head(fn, a):
        return jnp.concatenate([fn(a[:, h * HEAD_DIM:(h + 1) * HEAD_DIM]) for h in range(N_HEADS)], axis=1)

    uv = jax.nn.gelu(z[:, :2 * A_WIDTH])
    vn = per_head(lambda a: _rms(a, avg_ref[...]), uv[:, A_WIDTH:])
    vo_ref[...] = vn
    mixed = bias_ref[...]
    for d in range(n_tok):
        mixed = mixed + coef_ref[d] * prev_or_zero(vn, d)
    ob_ref[:, :A_WIDTH] = per_head(lambda a: _rms(a, aog_ref[...]), uv[:, :A_WIDTH] * mixed)

    zq = z[:, OFF_QKV:OFF_GATE]
    zq_ref[...] = zq
    cpad = cpad_ref[...]
    cw = cw_ref[...]
    y = zq * cw[CONV_W - 1:CONV_W]
    for d in range(1, CONV_W):
        carried = pltpu.roll(cpad, rows - (n_tok - d), axis=0)
        y = y + jnp.where(tok >= d, prev(zq, d), carried) * cw[CONV_W - 1 - d:CONV_W - d]
    qkv = _silu(y)

    abp = abp_ref[...]
    g_all = -jnp.exp(abp[0:1]) * _softplus(zab[:, :128] + abp[1:2])
    beta_all = jax.nn.sigmoid(zab[:, 128:])

    sub = lax.broadcasted_iota(jnp.int32, (8, 1), 0)
    first_half = sub < n_tok
    o_heads, kend_heads, u_heads, btot_heads = [], [], [], []
    for h in range(N_HEADS):
        q = _l2(qkv[:, h * HEAD_DIM:(h + 1) * HEAD_DIM]) * (HEAD_DIM ** -0.5)
        k = _l2(qkv[:, B_WIDTH + h * HEAD_DIM:B_WIDTH + (h + 1) * HEAD_DIM])
        v = qkv[:, 2 * B_WIDTH + h * HEAD_DIM:2 * B_WIDTH + (h + 1) * HEAD_DIM]
        g = jnp.broadcast_to(g_all[:, h:h + 1], (rows, HEAD_DIM))
        beta = jnp.broadcast_to(beta_all[:, h:h + 1], (rows, HEAD_DIM))
        gam = g
        for d in range(1, n_tok):
            gam = gam + prev_or_zero(g, d)
        gam_last = jnp.where(tok == n_tok - 1, gam, 0.0)
        for d in range(1, n_tok):
            gam_last = gam_last + jnp.where(tok == n_tok - 1 - d, pltpu.roll(gam, rows - d, axis=0), 0.0)
        eg = jnp.exp(gam)

        def decay_to(d, gam=gam):
            return jnp.exp(jnp.where(tok >= d, gam - prev(gam, d), 0.0))

        a_sub = [None] + [jnp.where(tok >= d, beta * jnp.sum(k * prev(k, d), axis=-1, keepdims=True) * decay_to(d),
                                    0.0) for d in range(1, n_tok)]
        def forward_substitute(rhs, a_sub=a_sub):
            sol = rhs
            for t in range(1, n_tok):
                acc = rhs
                for d in range(1, t + 1):
                    acc = acc - a_sub[d] * prev(sol, d)
                sol = jnp.where(tok == t, acc, sol)
            return sol

        w_blk = forward_substitute(beta * v)
        kb_blk = forward_substitute((beta * eg) * k)
        qb = q * eg

        kb_s, qb_s = [], []
        for p in range(rows // 8):
            kb_t, qb_t = kb_blk[8 * p:8 * p + 8], qb[8 * p:8 * p + 8]
            f0 = _dot(jnp.where(first_half, kb_t, pltpu.roll(qb_t, n_tok, axis=0)), s_ref[2 * p, h])
            f1 = _dot(jnp.where(first_half, pltpu.roll(kb_t, n_tok, axis=0), qb_t), s_ref[2 * p + 1, h])
            kb_s.append(jnp.where(first_half, f0, pltpu.roll(f1, n_tok, axis=0)))
            qb_s.append(jnp.where(first_half, pltpu.roll(f0, n_tok, axis=0), f1))
        u = w_blk - jnp.concatenate(kb_s, axis=0)
        o = jnp.concatenate(qb_s, axis=0)
        for d in range(n_tok):
            qk = jnp.where(tok >= d, jnp.sum(q * prev(k, d), axis=-1, keepdims=True) * decay_to(d), 0.0)
            o = o + qk * prev_or_zero(u, d)
        o_heads.append(o)
        kend_heads.append(k * jnp.exp(gam_last - gam))
        u_heads.append(u)
        btot_heads.append(jnp.broadcast_to(jnp.exp(gam_last), (rows, HEAD_DIM)))

    kend_t = jnp.concatenate(kend_heads, axis=0).T
    u_all = jnp.concatenate(u_heads, axis=0).astype(BF16)
    owner = lax.broadcasted_iota(jnp.int32, (1, N_HEADS * rows), 1) // n_tok
    for h in range(N_HEADS):
        for b in range(nb):
            mine = jnp.where(owner == h * nb + b, kend_t, 0.0).astype(BF16)
            last = b * n_tok + n_tok - 1
            so_ref[b, h] = btot_heads[h][last:last + 1] * s_ref[b, h] + jnp.dot(
                mine, u_all, preferred_element_type=F32)

    gate = z[:, OFF_GATE:]
    ob_ref[:, A_WIDTH:] = per_head(lambda a: _rms(a, bog_ref[...]), jnp.concatenate(o_heads, axis=1)) * _silu(gate)
    xo_ref[...] = x + _dot(ob_ref[...], wout_ref[...])


def _mix_sample(x, layer, state_s, cpad, w, n_tok):
    rows_total = x.shape[0]
    nb = SAMPLE_GROUP
    rows = nb * n_tok
    row_spec = lambda width: pl.BlockSpec((rows, width), lambda i: (i, 0))
    s_blk = (nb, N_HEADS, HEAD_DIM, HEAD_DIM)
    in_specs = [row_spec(D_MODEL),
                pl.BlockSpec((None,) + s_blk, lambda i: (layer, i, 0, 0, 0)),
                pl.BlockSpec((None, rows, 3 * B_WIDTH), lambda i: (layer, i, 0))] + _mixer_weight_specs(layer) + [
        _layer_spec((n_tok, rows, A_WIDTH), layer), _layer_spec((rows, A_WIDTH), layer),
        _layer_spec((CONV_W, 3 * B_WIDTH), layer), _layer_spec((2, 128), layer), _layer_spec((D_MODEL, D_MODEL), layer)]
    out_specs = [row_spec(D_MODEL), pl.BlockSpec(s_blk, lambda i: (i, 0, 0, 0)),
                 row_spec(3 * B_WIDTH), row_spec(A_WIDTH)]
    out_shape = [jax.ShapeDtypeStruct(x.shape, F32), jax.ShapeDtypeStruct(state_s.shape[1:], F32),
                 jax.ShapeDtypeStruct((rows_total, 3 * B_WIDTH), F32), jax.ShapeDtypeStruct((rows_total, A_WIDTH), F32)]
    return pl.pallas_call(
        functools.partial(_mix_sample_kernel, n_tok=n_tok),
        grid=(rows_total // rows,),
        in_specs=in_specs,
        out_specs=out_specs,
        out_shape=out_shape,
        scratch_shapes=[pltpu.VMEM((rows, D_MODEL), F32)],
        compiler_params=pltpu.CompilerParams(dimension_semantics=("arbitrary",),
                                             vmem_limit_bytes=VMEM_LIMIT_BYTES),
    )(x, state_s, cpad, *_mixer_weights(w), w["a_coef"], w["a_bias"], w["b_conv"], w["ab_par"], w["w_out"])


def _prep_weights(n_tok, norm_ffn1, w_ffn1_in, w_ffn1_out, norm_mix, w_in, a_v_gain, a_spatial_w, a_spatial_b,
                  a_out_gain, b_conv_w, b_a_log, b_dt_bias, b_out_gain, w_out, norm_ffn2, w_ffn2_in, w_ffn2_out,
                  norm_ple, w_ple_gate, w_ple_proj):
    o_ab = OFF_GATE
    w_main = jnp.concatenate([w_in[:, :, :o_ab], w_in[:, :, o_ab + 2 * N_HEADS:]], axis=2).astype(BF16)
    lane_pad = jnp.zeros((DEPTH, D_MODEL, 128 - N_HEADS), F32)
    w_ab = jnp.concatenate([w_in[:, :, o_ab:o_ab + N_HEADS], lane_pad,
                            w_in[:, :, o_ab + N_HEADS:o_ab + 2 * N_HEADS], lane_pad], axis=2).astype(BF16)
    par_pad = jnp.zeros((DEPTH, 128 - N_HEADS), F32)
    ab_par = jnp.stack([jnp.concatenate([b_a_log, par_pad], axis=1),
                        jnp.concatenate([b_dt_bias, par_pad], axis=1)], axis=1)

    def sample_rows(a):
        return jnp.tile(jnp.repeat(jnp.transpose(a, (0, 2, 1)), HEAD_DIM, axis=2), (1, SAMPLE_GROUP, 1))

    ws_small = a_spatial_w[:, :, :n_tok, :n_tok]
    a_coef = jnp.stack([sample_rows(jnp.pad(jnp.diagonal(ws_small, offset=-d, axis1=2, axis2=3),
                                            ((0, 0), (0, 0), (d, 0)))) for d in range(n_tok)], axis=1)
    return dict(
        n_f1=norm_ffn1[:, None], w_f1_in=w_ffn1_in.astype(BF16), w_f1_out=w_ffn1_out.astype(BF16),
        n_mix=norm_mix[:, None], w_main=w_main, w_ab=w_ab, ab_par=ab_par,
        a_v_gain=a_v_gain[:, None], a_out_gain=a_out_gain[:, None], b_out_gain=b_out_gain[:, None],
        a_w_s=a_spatial_w, a_b_s_t=jnp.transpose(a_spatial_b, (0, 2, 1)),
        a_coef=a_coef, a_bias=sample_rows(a_spatial_b[:, :, :n_tok]),
        b_conv=b_conv_w, w_out=w_out.astype(BF16),
        n_f2=norm_ffn2[:, None], w_f2_in=w_ffn2_in.astype(BF16), w_f2_out=w_ffn2_out.astype(BF16),
        n_ple=norm_ple[:, None], w_ple_gate=w_ple_gate.astype(BF16), w_ple_proj=w_ple_proj.astype(BF16),
    )


def kernel(x_prompt, x_sample, state_S, state_conv, p_prompt, p_sample, norm_ffn1, w_ffn1_in, w_ffn1_out, norm_mix, w_in, a_v_gain, a_spatial_w, a_spatial_b, a_out_gain, b_conv_w, b_a_log, b_dt_bias, b_out_gain, w_out, norm_ffn2, w_ffn2_in, w_ffn2_out, norm_ple, w_ple_gate, w_ple_proj, final_norm):
    bsz, length, _ = x_prompt.shape
    dec_bsz, n_tok, _ = x_sample.shape
    assert length % MIX_ROWS == 0 and MIX_ROWS % CHUNK_A == 0 and dec_bsz % SAMPLE_GROUP == 0
    assert n_tok % CHUNK_A != 0 and n_tok % CHUNK_D != 0
    assert 2 * n_tok == 8 and N_HEADS * SAMPLE_GROUP * n_tok == HEAD_DIM and n_tok >= CONV_W - 1

    w = _prep_weights(n_tok, norm_ffn1, w_ffn1_in, w_ffn1_out, norm_mix, w_in, a_v_gain, a_spatial_w, a_spatial_b,
                      a_out_gain, b_conv_w, b_a_log, b_dt_bias, b_out_gain, w_out, norm_ffn2, w_ffn2_in,
                      w_ffn2_out, norm_ple, w_ple_gate, w_ple_proj)
    final = final_norm[None, None]
    xp = x_prompt.reshape(bsz * length, D_MODEL)
    xs = x_sample.reshape(dec_bsz * n_tok, D_MODEL)
    pp = p_prompt.reshape(DEPTH, bsz * length, PLE_DIM)
    ps = p_sample.reshape(DEPTH, dec_bsz * n_tok, PLE_DIM)
    keep = CONV_W - 1
    cpad = jnp.pad(state_conv, ((0, 0), (0, 0), (n_tok - keep, 0), (0, 0))).reshape(DEPTH, dec_bsz * n_tok, 3 * B_WIDTH)

    s_prompt, c_prompt, s_sample, c_sample, v_sample = [], [], [], [], []
    for i in range(DEPTH):
        last = dict(final_gain=final) if i == DEPTH - 1 else {}
        ple = (w["n_ple"], w["w_ple_gate"], w["w_ple_proj"])

        xp = _ffn(xp, i, w["n_f1"], w["w_f1_in"], w["w_f1_out"])
        xp, sp, cp = _mix_prompt(xp.reshape(bsz, length, D_MODEL), i, w)
        xp = _ffn(xp.reshape(bsz * length, D_MODEL), i, w["n_f2"], w["w_f2_in"], w["w_f2_out"], ple=(pp,) + ple, **last)

        xs = _ffn(xs, i, w["n_f1"], w["w_f1_in"], w["w_f1_out"])
        xs, ss, zq, vs = _mix_sample(xs, i, state_S, cpad, w, n_tok)
        xs = _ffn(xs, i, w["n_f2"], w["w_f2_in"], w["w_f2_out"], ple=(ps,) + ple, **last)

        s_prompt.append(sp)
        c_prompt.append(cp)
        s_sample.append(ss)
        c_sample.append(zq.reshape(dec_bsz, n_tok, 3 * B_WIDTH)[:, n_tok - keep:])
        v_sample.append(vs.reshape(dec_bsz, n_tok, N_HEADS, HEAD_DIM))

    return (xp.reshape(bsz, length, D_MODEL), xs.reshape(dec_bsz, n_tok, D_MODEL), jnp.stack(s_prompt),
            jnp.stack(c_prompt), jnp.stack(s_sample), jnp.stack(c_sample), jnp.stack(v_sample))
```

```python
import functools

import jax
import jax.numpy as jnp
from jax import lax
from jax.experimental import pallas as pl
from jax.experimental.pallas import tpu as pltpu

F32 = jnp.float32
BF16 = jnp.bfloat16
EPS = 1e-6

D_MODEL = 1024
D_FF = 2816
DEPTH = 4
N_HEADS = 4
HEAD_DIM = 128
A_WIDTH = N_HEADS * HEAD_DIM
B_WIDTH = N_HEADS * HEAD_DIM
CHUNK_A = 128
CHUNK_D = 64
CONV_W = 4
PLE_DIM = 256
Z_MAIN = 2 * A_WIDTH + 4 * B_WIDTH
OFF_QKV = 2 * A_WIDTH
OFF_GATE = OFF_QKV + 3 * B_WIDTH

VMEM_LIMIT_BYTES = 52 * 1024 * 1024
MXU_N = 256
FFN_ROWS = 512
GROUP = 4 * CHUNK_D
MIX_ROWS = 512
SAMPLE_GROUP = 8


def _rms(x, gain):
    return x * lax.rsqrt(jnp.mean(x * x, axis=-1, keepdims=True) + EPS) * gain


def _l2(x):
    return x * lax.rsqrt(jnp.sum(x * x, axis=-1, keepdims=True) + EPS)


def _silu(x):
    return x * jax.nn.sigmoid(x)


def _softplus(x):
    return jnp.maximum(x, 0.0) + jnp.log1p(jnp.exp(-jnp.abs(x)))


def _dot(a, b):
    return jnp.dot(a.astype(BF16), b.astype(BF16), preferred_element_type=F32)


def _dot_nt(a, b):
    return lax.dot_general(a.astype(BF16), b.astype(BF16), (((1,), (1,)), ((), ())),
                           preferred_element_type=F32)


def _split3(a):
    p1 = a.astype(BF16)
    r1 = a - p1.astype(F32)
    p2 = r1.astype(BF16)
    p3 = (r1 - p2.astype(F32)).astype(BF16)
    return p1, p2, p3


def _const_spec(shape):
    zeros = (0,) * len(shape)
    return pl.BlockSpec(shape, lambda *_: zeros, pipeline_mode=pl.Buffered(1))


def _layer_spec(shape, layer, block=None):
    index = (layer,) + (0,) * (len(shape) - 1) + (0 if block is None else block,)
    return pl.BlockSpec((None,) + tuple(shape), lambda *_: index, pipeline_mode=pl.Buffered(1))


def _ffn_kernel(*refs, with_ple, with_final):
    x_ref, gain_ref, wg_ref, wu_ref, wo_ref = refs[:5]
    o_ref = refs[-1]
    x = x_ref[...]
    xn = _rms(x, gain_ref[...]).astype(BF16)
    acc = jnp.zeros_like(x)
    for c in range(D_FF // MXU_N):
        sl = slice(c * MXU_N, (c + 1) * MXU_N)
        gate = jnp.dot(xn, wg_ref[:, sl], preferred_element_type=F32)
        up = jnp.dot(xn, wu_ref[:, sl], preferred_element_type=F32)
        h = (_silu(gate) * up).astype(BF16)
        acc = acc + jnp.dot(h, wo_ref[sl, :], preferred_element_type=F32)
    x = x + 0.5 * acc
    if with_ple:
        p_ref, npl_ref, wpg_ref, wpp_ref = refs[5:9]
        emb = _dot(p_ref[...], wpp_ref[...])
        gate = _dot(_rms(x, npl_ref[...]), wpg_ref[...])
        x = x + emb * jax.nn.sigmoid(gate)
    if with_final:
        x = _rms(x, refs[9][...])
    o_ref[...] = x


def _ffn(x, layer, gain, w_in, w_out, ple=None, final_gain=None):
    rows = x.shape[0]
    tm = min(FFN_ROWS, rows)
    row_spec = pl.BlockSpec((tm, D_MODEL), lambda i: (i, 0))
    in_specs = [row_spec, _layer_spec((1, D_MODEL), layer),
                _layer_spec((D_MODEL, D_FF), layer, block=0), _layer_spec((D_MODEL, D_FF), layer, block=1),
                _layer_spec((D_FF, D_MODEL), layer)]
    args = [x, gain, w_in, w_in, w_out]
    if ple is not None:
        p, n_ple, w_gate, w_proj = ple
        in_specs += [pl.BlockSpec((None, tm, PLE_DIM), lambda i: (layer, i, 0)), _layer_spec((1, D_MODEL), layer),
                     _layer_spec((D_MODEL, D_MODEL), layer), _layer_spec((PLE_DIM, D_MODEL), layer)]
        args += [p, n_ple, w_gate, w_proj]
    if final_gain is not None:
        in_specs.append(_layer_spec((1, D_MODEL), 0))
        args.append(final_gain)
    return pl.pallas_call(
        functools.partial(_ffn_kernel, with_ple=ple is not None, with_final=final_gain is not None),
        grid=(rows // tm,),
        in_specs=in_specs,
        out_specs=row_spec,
        out_shape=jax.ShapeDtypeStruct(x.shape, F32),
        compiler_params=pltpu.CompilerParams(dimension_semantics=("arbitrary",),
                                             vmem_limit_bytes=VMEM_LIMIT_BYTES),
    )(*args)


def _split(a):
    hi = a.astype(BF16)
    lo = (a - hi.astype(F32)).astype(BF16)
    return hi, lo


def _dot3(a_hi, a_lo, b_hi, b_lo):
    m = a_hi.shape[0]
    both = jnp.dot(jnp.concatenate([a_hi, a_lo], axis=0), b_hi, preferred_element_type=F32)
    return both[:m] + both[m:] + jnp.dot(a_hi, b_lo, preferred_element_type=F32)


def _fold(block_diag):
    n = block_diag.shape[0] // CHUNK_D
    out = block_diag[0:CHUNK_D]
    for g in range(1, n):
        out = out + block_diag[g * CHUNK_D:(g + 1) * CHUNK_D]
    return out


def _expand(packed, diag_ones_ref):
    n = packed.shape[1] // CHUNK_D
    return jnp.concatenate([packed] * n, axis=0) * diag_ones_ref[...]


def _unit_lower_inverses_packed(l_packed_list, diag_ones_ref):
    c, width = l_packed_list[0].shape
    row = lax.broadcasted_iota(jnp.int32, (c, width), 0)
    col = lax.broadcasted_iota(jnp.int32, (c, width), 1) % c
    zero = jnp.zeros((), BF16)

    def lower_left(bs):
        return (row // (2 * bs) == col // (2 * bs)) & ((row // bs) % 2 == 1) & ((col // bs) % 2 == 0)

    l_split = [_split(l) for l in l_packed_list]
    xs = [jnp.where(row == col, 1.0, 0.0) - jnp.where(lower_left(1), l, 0.0) for l in l_packed_list]
    bs = 2
    while bs < c:
        sel = lower_left(bs)
        x_split = [_split(x) for x in xs]
        ys = [_dot3(jnp.where(sel, l_hi, zero), jnp.where(sel, l_lo, zero),
                    _expand(x_hi, diag_ones_ref), _expand(x_lo, diag_ones_ref))
              for (l_hi, l_lo), (x_hi, x_lo) in zip(l_split, x_split)]
        y_split = [_split(y) for y in ys]
        xs = [x - _dot3(x_hi, x_lo, _expand(y_hi, diag_ones_ref), _expand(y_lo, diag_ones_ref))
              for x, (x_hi, x_lo), (y_hi, y_lo) in zip(xs, x_split, y_split)]
        bs *= 2
    return xs


def _mix_prompt_kernel(x_ref, nmix_ref, wmain_ref, wab_ref, avg_ref, aog_ref, bog_ref, wsp_ref, bsp_ref,
                       cw_ref, abp_ref, wout_ref,
                       xo_ref, s_ref, ct_ref,
                       zext_ref, ob_ref, bd_ref):
    tl = x_ref.shape[0]
    step = pl.program_id(1)

    @pl.when(step == 0)
    def _():
        s_ref[...] = jnp.zeros_like(s_ref)
        zext_ref[0:8, :] = jnp.zeros((8, 3 * B_WIDTH), F32)

    x = x_ref[...]
    xn = _rms(x, nmix_ref[...]).astype(BF16)
    z = jnp.dot(xn, wmain_ref[...], preferred_element_type=F32)
    zab = jnp.dot(xn, wab_ref[...], preferred_element_type=F32)

    uv = jax.nn.gelu(z[:, :2 * A_WIDTH])
    row = lax.broadcasted_iota(jnp.int32, (CHUNK_A, CHUNK_A), 0)
    col = lax.broadcasted_iota(jnp.int32, (CHUNK_A, CHUNK_A), 1)
    causal = col <= row
    for h in range(N_HEADS):
        hs = slice(h * HEAD_DIM, (h + 1) * HEAD_DIM)
        u_h = uv[:, hs]
        v_h = _rms(uv[:, A_WIDTH + h * HEAD_DIM:A_WIDTH + (h + 1) * HEAD_DIM], avg_ref[...]).astype(BF16)
        w_h = jnp.where(causal, wsp_ref[h], 0.0).astype(BF16)
        bias_h = bsp_ref[:, h:h + 1]
        for c in range(tl // CHUNK_A):
            rs = slice(c * CHUNK_A, (c + 1) * CHUNK_A)
            mixed = jnp.dot(w_h, v_h[rs], preferred_element_type=F32) + bias_h
            ob_ref[rs, hs] = _rms(u_h[rs] * mixed, aog_ref[...]).astype(BF16)

    zext_ref[8:8 + tl, :] = z[:, OFF_QKV:OFF_GATE]
    cw = cw_ref[...]
    y = zext_ref[5:5 + tl, :] * cw[0:1]
    for j in range(1, CONV_W):
        y = y + zext_ref[5 + j:5 + j + tl, :] * cw[j:j + 1]
    tail = zext_ref[tl + 5:tl + 8, :]
    ct_ref[...] = tail
    zext_ref[5:8, :] = tail
    qkv = _silu(y)

    abp = abp_ref[...]
    g = -jnp.exp(abp[0:1]) * _softplus(zab[:, :128] + abp[1:2])
    beta = jax.nn.sigmoid(zab[:, 128:])

    r2 = lax.broadcasted_iota(jnp.int32, (GROUP, GROUP), 0)
    c2 = lax.broadcasted_iota(jnp.int32, (GROUP, GROUP), 1)
    same = (r2 // CHUNK_D) == (c2 // CHUNK_D)
    strict_bd = same & (c2 < r2)
    col_ones = jnp.concatenate([jnp.where(same & (c2 <= r2), 1.0, 0.0), jnp.where(same, 1.0, 0.0)],
                               axis=0).astype(BF16)
    upper_ones = jnp.where(same & (r2 <= c2), 1.0, 0.0).astype(BF16)
    bd_ref[...] = jnp.where(same, 1.0, 0.0).astype(BF16)
    n_grp = tl // GROUP
    gam_parts, glast_parts, gam_t = [], [], []
    for gi in range(n_grp):
        g_grp = g[gi * GROUP:(gi + 1) * GROUP]
        by_col = jnp.dot(col_ones, jnp.concatenate(_split3(g_grp), axis=1), preferred_element_type=F32)
        by_col = by_col[:, :128] + by_col[:, 128:256] + by_col[:, 256:]
        gam_parts.append(by_col[:GROUP])
        glast_parts.append(by_col[GROUP:])
        by_row = jnp.dot(jnp.concatenate(_split3(g_grp.T), axis=0), upper_ones, preferred_element_type=F32)
        gam_t.append(by_row[:128] + by_row[128:256] + by_row[256:])
    gam = jnp.concatenate(gam_parts, axis=0)
    glast = jnp.concatenate(glast_parts, axis=0)

    rb = lax.broadcasted_iota(jnp.int32, (CHUNK_D, CHUNK_D), 0)
    cb = lax.broadcasted_iota(jnp.int32, (CHUNK_D, CHUNK_D), 1)
    incl = cb <= rb

    heads, a_packed, rhs = [], [], []
    for h in range(N_HEADS):
        q_h = _l2(qkv[:, h * HEAD_DIM:(h + 1) * HEAD_DIM]) * (HEAD_DIM ** -0.5)
        k_h = _l2(qkv[:, B_WIDTH + h * HEAD_DIM:B_WIDTH + (h + 1) * HEAD_DIM])
        v_h = qkv[:, 2 * B_WIDTH + h * HEAD_DIM:2 * B_WIDTH + (h + 1) * HEAD_DIM]
        gc_h = gam[:, h:h + 1]
        gl_h = glast[:, h:h + 1]
        bc_h = beta[:, h:h + 1]
        eg_h = jnp.exp(gc_h)
        for gi in range(n_grp):
            gs = slice(gi * GROUP, (gi + 1) * GROUP)
            kk = _dot_nt(k_h[gs], k_h[gs])
            decay = jnp.exp(jnp.where(strict_bd, gc_h[gs] - gam_t[gi][h:h + 1, :], 0.0))
            a_packed.append(_fold(jnp.where(strict_bd, bc_h[gs] * kk * decay, 0.0)))
        rhs.append(_split(jnp.concatenate([bc_h * v_h, (bc_h * eg_h) * k_h], axis=1)))
        heads.append((q_h * eg_h, q_h, k_h, k_h * jnp.exp(gl_h - gc_h), gc_h, jnp.exp(gl_h)))
    inv_split = [_split(inv) for inv in _unit_lower_inverses_packed(a_packed, bd_ref)]
    sol = []
    for h in range(N_HEADS):
        sol.append([_dot3(_expand(inv_split[h * n_grp + gi][0], bd_ref), _expand(inv_split[h * n_grp + gi][1], bd_ref),
                          rhs[h][0][gi * GROUP:(gi + 1) * GROUP], rhs[h][1][gi * GROUP:(gi + 1) * GROUP])
                    for gi in range(n_grp)])

    for i in range(tl // CHUNK_D):
        rs = slice(i * CHUNK_D, (i + 1) * CHUNK_D)
        gi, j = divmod(i, GROUP // CHUNK_D)
        ls = slice(j * CHUNK_D, (j + 1) * CHUNK_D)
        for h in range(N_HEADS):
            qb_h, q_h, k_h, kend_h, gc_h, btot_h = heads[h]
            decay = jnp.where(incl, jnp.exp(jnp.where(incl, gc_h[rs] - gam_t[gi][h:h + 1, ls], 0.0)), 0.0)
            qk = _dot_nt(q_h[rs], k_h[rs]) * decay
            s_old = s_ref[h]
            from_s = _dot(jnp.concatenate([sol[h][gi][ls, HEAD_DIM:], qb_h[rs]], axis=0), s_old)
            u = sol[h][gi][ls, :HEAD_DIM] - from_s[:CHUNK_D]
            from_u = _dot(jnp.concatenate([qk, kend_h[rs].T], axis=0), u)
            o = from_s[CHUNK_D:] + from_u[:CHUNK_D]
            s_ref[h] = btot_h[i * CHUNK_D:i * CHUNK_D + 1] * s_old + from_u[CHUNK_D:]
            gate = z[rs, OFF_GATE + h * HEAD_DIM:OFF_GATE + (h + 1) * HEAD_DIM]
            ob_ref[rs, A_WIDTH + h * HEAD_DIM:A_WIDTH + (h + 1) * HEAD_DIM] = (
                _rms(o, bog_ref[...]) * _silu(gate)).astype(BF16)

    xo_ref[...] = x + jnp.dot(ob_ref[...], wout_ref[...], preferred_element_type=F32)


def _mixer_weight_specs(layer):
    return [_layer_spec((1, D_MODEL), layer), _layer_spec((D_MODEL, Z_MAIN), layer), _layer_spec((D_MODEL, 256), layer),
            _layer_spec((1, HEAD_DIM), layer), _layer_spec((1, HEAD_DIM), layer), _layer_spec((1, HEAD_DIM), layer)]


def _mixer_weights(w):
    return [w["n_mix"], w["w_main"], w["w_ab"], w["a_v_gain"], w["a_out_gain"], w["b_out_gain"]]


def _mix_prompt(x, layer, w):
    bsz, length, _ = x.shape
    tl = MIX_ROWS
    row_spec = pl.BlockSpec((None, tl, D_MODEL), lambda b, t: (b, t, 0))
    in_specs = [row_spec] + _mixer_weight_specs(layer) + [
        _layer_spec((N_HEADS, CHUNK_A, CHUNK_A), layer), _layer_spec((CHUNK_A, N_HEADS), layer),
        _layer_spec((CONV_W, 3 * B_WIDTH), layer), _layer_spec((2, 128), layer), _layer_spec((D_MODEL, D_MODEL), layer)]
    out_specs = [row_spec,
                 pl.BlockSpec((None, N_HEADS, HEAD_DIM, HEAD_DIM), lambda b, t: (b, 0, 0, 0)),
                 pl.BlockSpec((None, CONV_W - 1, 3 * B_WIDTH), lambda b, t: (b, 0, 0))]
    out_shape = [jax.ShapeDtypeStruct(x.shape, F32),
                 jax.ShapeDtypeStruct((bsz, N_HEADS, HEAD_DIM, HEAD_DIM), F32),
                 jax.ShapeDtypeStruct((bsz, CONV_W - 1, 3 * B_WIDTH), F32)]
    return pl.pallas_call(
        _mix_prompt_kernel,
        grid=(bsz, length // tl),
        in_specs=in_specs,
        out_specs=out_specs,
        out_shape=out_shape,
        scratch_shapes=[pltpu.VMEM((tl + 8, 3 * B_WIDTH), F32), pltpu.VMEM((tl, D_MODEL), BF16),
                        pltpu.VMEM((GROUP, GROUP), BF16)],
        compiler_params=pltpu.CompilerParams(dimension_semantics=("arbitrary", "arbitrary"),
                                             vmem_limit_bytes=VMEM_LIMIT_BYTES),
    )(x, *_mixer_weights(w), w["a_w_s"], w["a_b_s_t"], w["b_conv"], w["ab_par"], w["w_out"])


def _mix_sample_kernel(x_ref, s_ref, cpad_ref, nmix_ref, wmain_ref, wab_ref, avg_ref, aog_ref, bog_ref,
                       coef_ref, bias_ref, cw_ref, abp_ref, wout_ref,
                       xo_ref, so_ref, zq_ref, vo_ref,
                       ob_ref, *, n_tok):
    rows = x_ref.shape[0]
    nb = rows // n_tok
    x = x_ref[...]
    xn = _rms(x, nmix_ref[...]).astype(BF16)
    z = jnp.dot(xn, wmain_ref[...], preferred_element_type=F32)
    zab = jnp.dot(xn, wab_ref[...], preferred_element_type=F32)
    tok = lax.broadcasted_iota(jnp.int32, (rows, 1), 0) % n_tok

    def prev(a, d):
        return pltpu.roll(a, d, axis=0)

    def prev_or_zero(a, d):
        return a if d == 0 else jnp.where(tok >= d, prev(a, d), 0.0)

    def per_head(fn, a):
        return jnp.concatenate([fn(a[:, h * HEAD_DIM:(h + 1) * HEAD_DIM]) for h in range(N_HEADS)], axis=1)

    uv = jax.nn.gelu(z[:, :2 * A_WIDTH])
    vn = per_head(lambda a: _rms(a, avg_ref[...]), uv[:, A_WIDTH:])
    vo_ref[...] = vn
    mixed = bias_ref[...]
    for d in range(n_tok):
        mixed = mixed + coef_ref[d] * prev_or_zero(vn, d)
    ob_ref[:, :A_WIDTH] = per_head(lambda a: _rms(a, aog_ref[...]), uv[:, :A_WIDTH] * mixed)

    zq = z[:, OFF_QKV:OFF_GATE]
    zq_ref[...] = zq
    cpad = cpad_ref[...]
    cw = cw_ref[...]
    y = zq * cw[CONV_W - 1:CONV_W]
    for d in range(1, CONV_W):
        carried = pltpu.roll(cpad, rows - (n_tok - d), axis=0)
        y = y + jnp.where(tok >= d, prev(zq, d), carried) * cw[CONV_W - 1 - d:CONV_W - d]
    qkv = _silu(y)

    abp = abp_ref[...]
    g_all = -jnp.exp(abp[0:1]) * _softplus(zab[:, :128] + abp[1:2])
    beta_all = jax.nn.sigmoid(zab[:, 128:])

    sub = lax.broadcasted_iota(jnp.int32, (8, 1), 0)
    first_half = sub < n_tok
    o_heads, kend_heads, u_heads, btot_heads = [], [], [], []
    for h in range(N_HEADS):
        q = _l2(qkv[:, h * HEAD_DIM:(h + 1) * HEAD_DIM]) * (HEAD_DIM ** -0.5)
        k = _l2(qkv[:, B_WIDTH + h * HEAD_DIM:B_WIDTH + (h + 1) * HEAD_DIM])
        v = qkv[:, 2 * B_WIDTH + h * HEAD_DIM:2 * B_WIDTH + (h + 1) * HEAD_DIM]
        g = jnp.broadcast_to(g_all[:, h:h + 1], (rows, HEAD_DIM))
        beta = jnp.broadcast_to(beta_all[:, h:h + 1], (rows, HEAD_DIM))
        gam = g
        for d in range(1, n_tok):
            gam = gam + prev_or_zero(g, d)
        gam_last = jnp.where(tok == n_tok - 1, gam, 0.0)
        for d in range(1, n_tok):
            gam_last = gam_last + jnp.where(tok == n_tok - 1 - d, pltpu.roll(gam, rows - d, axis=0), 0.0)
        eg = jnp.exp(gam)

        def decay_to(d, gam=gam):
            return jnp.exp(jnp.where(tok >= d, gam - prev(gam, d), 0.0))

        a_sub = [None] + [jnp.where(tok >= d, beta * jnp.sum(k * prev(k, d), axis=-1, keepdims=True) * decay_to(d),
                                    0.0) for d in range(1, n_tok)]
        def forward_substitute(rhs, a_sub=a_sub):
            sol = rhs
            for t in range(1, n_tok):
                acc = rhs
                for d in range(1, t + 1):
                    acc = acc - a_sub[d] * prev(sol, d)
                sol = jnp.where(tok == t, acc, sol)
            return sol

        w_blk = forward_substitute(beta * v)
        kb_blk = forward_substitute((beta * eg) * k)
        qb = q * eg

        kb_s, qb_s = [], []
        for p in range(rows // 8):
            kb_t, qb_t = kb_blk[8 * p:8 * p + 8], qb[8 * p:8 * p + 8]
            f0 = _dot(jnp.where(first_half, kb_t, pltpu.roll(qb_t, n_tok, axis=0)), s_ref[2 * p, h])
            f1 = _dot(jnp.where(first_half, pltpu.roll(kb_t, n_tok, axis=0), qb_t), s_ref[2 * p + 1, h])
            kb_s.append(jnp.where(first_half, f0, pltpu.roll(f1, n_tok, axis=0)))
            qb_s.append(jnp.where(first_half, pltpu.roll(f0, n_tok, axis=0), f1))
        u = w_blk - jnp.concatenate(kb_s, axis=0)
        o = jnp.concatenate(qb_s, axis=0)
        for d in range(n_tok):
            qk = jnp.where(tok >= d, jnp.sum(q * prev(k, d), axis=-1, keepdims=True) * decay_to(d), 0.0)
            o = o + qk * prev_or_zero(u, d)
        o_heads.append(o)
        kend_heads.append(k * jnp.exp(gam_last - gam))
        u_heads.append(u)
        btot_heads.append(jnp.broadcast_to(jnp.exp(gam_last), (rows, HEAD_DIM)))

    kend_t = jnp.concatenate(kend_heads, axis=0).T
    u_all = jnp.concatenate(u_heads, axis=0).astype(BF16)
    owner = lax.broadcasted_iota(jnp.int32, (1, N_HEADS * rows), 1) // n_tok
    for h in range(N_HEADS):
        for b in range(nb):
            mine = jnp.where(owner == h * nb + b, kend_t, 0.0).astype(BF16)
            last = b * n_tok + n_tok - 1
            so_ref[b, h] = btot_heads[h][last:last + 1] * s_ref[b, h] + jnp.dot(
                mine, u_all, preferred_element_type=F32)

    gate = z[:, OFF_GATE:]
    ob_ref[:, A_WIDTH:] = per_head(lambda a: _rms(a, bog_ref[...]), jnp.concatenate(o_heads, axis=1)) * _silu(gate)
    xo_ref[...] = x + _dot(ob_ref[...], wout_ref[...])


def _mix_sample(x, layer, state_s, cpad, w, n_tok):
    rows_total = x.shape[0]
    nb = SAMPLE_GROUP
    rows = nb * n_tok
    row_spec = lambda width: pl.BlockSpec((rows, width), lambda i: (i, 0))
    s_blk = (nb, N_HEADS, HEAD_DIM, HEAD_DIM)
    in_specs = [row_spec(D_MODEL),
                pl.BlockSpec((None,) + s_blk, lambda i: (layer, i, 0, 0, 0)),
                pl.BlockSpec((None, rows, 3 * B_WIDTH), lambda i: (layer, i, 0))] + _mixer_weight_specs(layer) + [
        _layer_spec((n_tok, rows, A_WIDTH), layer), _layer_spec((rows, A_WIDTH), layer),
        _layer_spec((CONV_W, 3 * B_WIDTH), layer), _layer_spec((2, 128), layer), _layer_spec((D_MODEL, D_MODEL), layer)]
    out_specs = [row_spec(D_MODEL), pl.BlockSpec(s_blk, lambda i: (i, 0, 0, 0)),
                 row_spec(3 * B_WIDTH), row_spec(A_WIDTH)]
    out_shape = [jax.ShapeDtypeStruct(x.shape, F32), jax.ShapeDtypeStruct(state_s.shape[1:], F32),
                 jax.ShapeDtypeStruct((rows_total, 3 * B_WIDTH), F32), jax.ShapeDtypeStruct((rows_total, A_WIDTH), F32)]
    return pl.pallas_call(
        functools.partial(_mix_sample_kernel, n_tok=n_tok),
        grid=(rows_total // rows,),
        in_specs=in_specs,
        out_specs=out_specs,
        out_shape=out_shape,
        scratch_shapes=[pltpu.VMEM((rows, D_MODEL), F32)],
        compiler_params=pltpu.CompilerParams(dimension_semantics=("arbitrary",),
                                             vmem_limit_bytes=VMEM_LIMIT_BYTES),
    )(x, state_s, cpad, *_mixer_weights(w), w["a_coef"], w["a_bias"], w["b_conv"], w["ab_par"], w["w_out"])


def _prep_weights(n_tok, norm_ffn1, w_ffn1_in, w_ffn1_out, norm_mix, w_in, a_v_gain, a_spatial_w, a_spatial_b,
                  a_out_gain, b_conv_w, b_a_log, b_dt_bias, b_out_gain, w_out, norm_ffn2, w_ffn2_in, w_ffn2_out,
                  norm_ple, w_ple_gate, w_ple_proj):
    o_ab = OFF_GATE
    w_main = jnp.concatenate([w_in[:, :, :o_ab], w_in[:, :, o_ab + 2 * N_HEADS:]], axis=2).astype(BF16)
    lane_pad = jnp.zeros((DEPTH, D_MODEL, 128 - N_HEADS), F32)
    w_ab = jnp.concatenate([w_in[:, :, o_ab:o_ab + N_HEADS], lane_pad,
                            w_in[:, :, o_ab + N_HEADS:o_ab + 2 * N_HEADS], lane_pad], axis=2).astype(BF16)
    par_pad = jnp.zeros((DEPTH, 128 - N_HEADS), F32)
    ab_par = jnp.stack([jnp.concatenate([b_a_log, par_pad], axis=1),
                        jnp.concatenate([b_dt_bias, par_pad], axis=1)], axis=1)

    def sample_rows(a):
        return jnp.tile(jnp.repeat(jnp.transpose(a, (0, 2, 1)), HEAD_DIM, axis=2), (1, SAMPLE_GROUP, 1))

    ws_small = a_spatial_w[:, :, :n_tok, :n_tok]
    a_coef = jnp.stack([sample_rows(jnp.pad(jnp.diagonal(ws_small, offset=-d, axis1=2, axis2=3),
                                            ((0, 0), (0, 0), (d, 0)))) for d in range(n_tok)], axis=1)
    return dict(
        n_f1=norm_ffn1[:, None], w_f1_in=w_ffn1_in.astype(BF16), w_f1_out=w_ffn1_out.astype(BF16),
        n_mix=norm_mix[:, None], w_main=w_main, w_ab=w_ab, ab_par=ab_par,
        a_v_gain=a_v_gain[:, None], a_out_gain=a_out_gain[:, None], b_out_gain=b_out_gain[:, None],
        a_w_s=a_spatial_w, a_b_s_t=jnp.transpose(a_spatial_b, (0, 2, 1)),
        a_coef=a_coef, a_bias=sample_rows(a_spatial_b[:, :, :n_tok]),
        b_conv=b_conv_w, w_out=w_out.astype(BF16),
        n_f2=norm_ffn2[:, None], w_f2_in=w_ffn2_in.astype(BF16), w_f2_out=w_ffn2_out.astype(BF16),
        n_ple=norm_ple[:, None], w_ple_gate=w_ple_gate.astype(BF16), w_ple_proj=w_ple_proj.astype(BF16),
    )


def kernel(x_prompt, x_sample, state_S, state_conv, p_prompt, p_sample, norm_ffn1, w_ffn1_in, w_ffn1_out, norm_mix, w_in, a_v_gain, a_spatial_w, a_spatial_b, a_out_gain, b_conv_w, b_a_log, b_dt_bias, b_out_gain, w_out, norm_ffn2, w_ffn2_in, w_ffn2_out, norm_ple, w_ple_gate, w_ple_proj, final_norm):
    bsz, length, _ = x_prompt.shape
    dec_bsz, n_tok, _ = x_sample.shape
    assert length % MIX_ROWS == 0 and MIX_ROWS % CHUNK_A == 0 and MIX_ROWS % GROUP == 0
    assert dec_bsz % SAMPLE_GROUP == 0
    assert n_tok % CHUNK_A != 0 and n_tok % CHUNK_D != 0
    assert 2 * n_tok == 8 and N_HEADS * SAMPLE_GROUP * n_tok == HEAD_DIM and n_tok >= CONV_W - 1

    w = _prep_weights(n_tok, norm_ffn1, w_ffn1_in, w_ffn1_out, norm_mix, w_in, a_v_gain, a_spatial_w, a_spatial_b,
                      a_out_gain, b_conv_w, b_a_log, b_dt_bias, b_out_gain, w_out, norm_ffn2, w_ffn2_in,
                      w_ffn2_out, norm_ple, w_ple_gate, w_ple_proj)
    final = final_norm[None, None]
    xp = x_prompt.reshape(bsz * length, D_MODEL)
    xs = x_sample.reshape(dec_bsz * n_tok, D_MODEL)
    pp = p_prompt.reshape(DEPTH, bsz * length, PLE_DIM)
    ps = p_sample.reshape(DEPTH, dec_bsz * n_tok, PLE_DIM)
    keep = CONV_W - 1
    cpad = jnp.pad(state_conv, ((0, 0), (0, 0), (n_tok - keep, 0), (0, 0))).reshape(DEPTH, dec_bsz * n_tok, 3 * B_WIDTH)

    s_prompt, c_prompt, s_sample, c_sample, v_sample = [], [], [], [], []
    for i in range(DEPTH):
        last = dict(final_gain=final) if i == DEPTH - 1 else {}
        ple = (w["n_ple"], w["w_ple_gate"], w["w_ple_proj"])

        xp = _ffn(xp, i, w["n_f1"], w["w_f1_in"], w["w_f1_out"])
        xp, sp, cp = _mix_prompt(xp.reshape(bsz, length, D_MODEL), i, w)
        xp = _ffn(xp.reshape(bsz * length, D_MODEL), i, w["n_f2"], w["w_f2_in"], w["w_f2_out"], ple=(pp,) + ple, **last)

        xs = _ffn(xs, i, w["n_f1"], w["w_f1_in"], w["w_f1_out"])
        xs, ss, zq, vs = _mix_sample(xs, i, state_S, cpad, w, n_tok)
        xs = _ffn(xs, i, w["n_f2"], w["w_f2_in"], w["w_f2_out"], ple=(ps,) + ple, **last)

        s_prompt.append(sp)
        c_prompt.append(cp)
        s_sample.append(ss)
        c_sample.append(zq.reshape(dec_bsz, n_tok, 3 * B_WIDTH)[:, n_tok - keep:])
        v_sample.append(vs.reshape(dec_bsz, n_tok, N_HEADS, HEAD_DIM))

    return (xp.reshape(bsz, length, D_MODEL), xs.reshape(dec_bsz, n_tok, D_MODEL), jnp.stack(s_prompt),
            jnp.stack(c_prompt), jnp.stack(s_sample), jnp.stack(c_sample), jnp.stack(v_sample))
```

```python
import functools

import jax
import jax.numpy as jnp
from jax import lax
from jax.experimental import pallas as pl
from jax.experimental.pallas import tpu as pltpu

F32 = jnp.float32
BF16 = jnp.bfloat16
EPS = 1e-6

D_MODEL = 1024
D_FF = 2816
DEPTH = 4
N_HEADS = 4
HEAD_DIM = 128
A_WIDTH = N_HEADS * HEAD_DIM
B_WIDTH = N_HEADS * HEAD_DIM
CHUNK_A = 128
CHUNK_D = 64
CONV_W = 4
PLE_DIM = 256
Z_MAIN = 2 * A_WIDTH + 4 * B_WIDTH
OFF_QKV = 2 * A_WIDTH
OFF_GATE = OFF_QKV + 3 * B_WIDTH

VMEM_LIMIT_BYTES = 52 * 1024 * 1024
MXU_N = 256
FFN_ROWS = 512
GROUP = 4 * CHUNK_D
MIX_ROWS = 512
SAMPLE_GROUP = 8


def _rms(x, gain):
    return x * lax.rsqrt(jnp.mean(x * x, axis=-1, keepdims=True) + EPS) * gain


def _l2(x):
    return x * lax.rsqrt(jnp.sum(x * x, axis=-1, keepdims=True) + EPS)


def _silu(x):
    return x * jax.nn.sigmoid(x)


def _softplus(x):
    return jnp.maximum(x, 0.0) + jnp.log1p(jnp.exp(-jnp.abs(x)))


def _dot(a, b):
    return jnp.dot(a.astype(BF16), b.astype(BF16), preferred_element_type=F32)


def _dot_nt(a, b):
    return lax.dot_general(a.astype(BF16), b.astype(BF16), (((1,), (1,)), ((), ())),
                           preferred_element_type=F32)


def _split3(a):
    p1 = a.astype(BF16)
    r1 = a - p1.astype(F32)
    p2 = r1.astype(BF16)
    p3 = (r1 - p2.astype(F32)).astype(BF16)
    return p1, p2, p3


def _const_spec(shape):
    zeros = (0,) * len(shape)
    return pl.BlockSpec(shape, lambda *_: zeros, pipeline_mode=pl.Buffered(1))


def _layer_spec(shape, layer, block=None):
    index = (layer,) + (0,) * (len(shape) - 1) + (0 if block is None else block,)
    return pl.BlockSpec((None,) + tuple(shape), lambda *_: index, pipeline_mode=pl.Buffered(1))


def _ffn_kernel(*refs, with_ple, with_final):
    x_ref, gain_ref, wg_ref, wu_ref, wo_ref = refs[:5]
    o_ref = refs[-1]
    x = x_ref[...]
    xn = _rms(x, gain_ref[...]).astype(BF16)
    acc = jnp.zeros_like(x)
    for c in range(D_FF // MXU_N):
        sl = slice(c * MXU_N, (c + 1) * MXU_N)
        gate = jnp.dot(xn, wg_ref[:, sl], preferred_element_type=F32)
        up = jnp.dot(xn, wu_ref[:, sl], preferred_element_type=F32)
        h = (_silu(gate) * up).astype(BF16)
        acc = acc + jnp.dot(h, wo_ref[sl, :], preferred_element_type=F32)
    x = x + 0.5 * acc
    if with_ple:
        p_ref, npl_ref, wpg_ref, wpp_ref = refs[5:9]
        emb = _dot(p_ref[...], wpp_ref[...])
        gate = _dot(_rms(x, npl_ref[...]), wpg_ref[...])
        x = x + emb * jax.nn.sigmoid(gate)
    if with_final:
        x = _rms(x, refs[9][...])
    o_ref[...] = x


def _ffn(x, layer, gain, w_in, w_out, ple=None, final_gain=None):
    rows = x.shape[0]
    tm = min(FFN_ROWS, rows)
    row_spec = pl.BlockSpec((tm, D_MODEL), lambda i: (i, 0))
    in_specs = [row_spec, _layer_spec((1, D_MODEL), layer),
                _layer_spec((D_MODEL, D_FF), layer, block=0), _layer_spec((D_MODEL, D_FF), layer, block=1),
                _layer_spec((D_FF, D_MODEL), layer)]
    args = [x, gain, w_in, w_in, w_out]
    if ple is not None:
        p, n_ple, w_gate, w_proj = ple
        in_specs += [pl.BlockSpec((None, tm, PLE_DIM), lambda i: (layer, i, 0)), _layer_spec((1, D_MODEL), layer),
                     _layer_spec((D_MODEL, D_MODEL), layer), _layer_spec((PLE_DIM, D_MODEL), layer)]
        args += [p, n_ple, w_gate, w_proj]
    if final_gain is not None:
        in_specs.append(_layer_spec((1, D_MODEL), 0))
        args.append(final_gain)
    return pl.pallas_call(
        functools.partial(_ffn_kernel, with_ple=ple is not None, with_final=final_gain is not None),
        grid=(rows // tm,),
        in_specs=in_specs,
        out_specs=row_spec,
        out_shape=jax.ShapeDtypeStruct(x.shape, F32),
        compiler_params=pltpu.CompilerParams(dimension_semantics=("arbitrary",),
                                             vmem_limit_bytes=VMEM_LIMIT_BYTES),
    )(*args)


def _split(a):
    hi = a.astype(BF16)
    lo = (a - hi.astype(F32)).astype(BF16)
    return hi, lo


def _dot3(a_hi, a_lo, b_hi, b_lo):
    m = a_hi.shape[0]
    both = jnp.dot(jnp.concatenate([a_hi, a_lo], axis=0), b_hi, preferred_element_type=F32)
    return both[:m] + both[m:] + jnp.dot(a_hi, b_lo, preferred_element_type=F32)


def _fold(block_diag):
    n = block_diag.shape[0] // CHUNK_D
    out = block_diag[0:CHUNK_D]
    for g in range(1, n):
        out = out + block_diag[g * CHUNK_D:(g + 1) * CHUNK_D]
    return out


def _expand(packed, diag_ones_ref):
    n = packed.shape[1] // CHUNK_D
    return jnp.concatenate([packed] * n, axis=0) * diag_ones_ref[...]


def _unit_lower_inverses_packed(l_packed_list, diag_ones_ref):
    c, width = l_packed_list[0].shape
    row = lax.broadcasted_iota(jnp.int32, (c, width), 0)
    col = lax.broadcasted_iota(jnp.int32, (c, width), 1) % c
    zero = jnp.zeros((), BF16)

    def lower_left(bs):
        return (row // (2 * bs) == col // (2 * bs)) & ((row // bs) % 2 == 1) & ((col // bs) % 2 == 0)

    l_split = [_split(l) for l in l_packed_list]
    xs = [jnp.where(row == col, 1.0, 0.0) - jnp.where(lower_left(1), l, 0.0) for l in l_packed_list]
    bs = 2
    while bs < c:
        sel = lower_left(bs)
        x_split = [_split(x) for x in xs]
        ys = [_dot3(jnp.where(sel, l_hi, zero), jnp.where(sel, l_lo, zero),
                    _expand(x_hi, diag_ones_ref), _expand(x_lo, diag_ones_ref))
              for (l_hi, l_lo), (x_hi, x_lo) in zip(l_split, x_split)]
        y_split = [_split(y) for y in ys]
        xs = [x - _dot3(x_hi, x_lo, _expand(y_hi, diag_ones_ref), _expand(y_lo, diag_ones_ref))
              for x, (x_hi, x_lo), (y_hi, y_lo) in zip(xs, x_split, y_split)]
        bs *= 2
    return xs


def _mix_prompt_kernel(x_ref, nmix_ref, wmain_ref, wab_ref, avg_ref, aog_ref, bog_ref, wsp_ref, bsp_ref,
                       cw_ref, abp_ref, wout_ref,
                       xo_ref, s_ref, ct_ref,
                       zext_ref, ob_ref, bd_ref):
    tl = x_ref.shape[0]
    step = pl.program_id(1)

    @pl.when(step == 0)
    def _():
        s_ref[...] = jnp.zeros_like(s_ref)
        zext_ref[0:8, :] = jnp.zeros((8, 3 * B_WIDTH), F32)

    x = x_ref[...]
    xn = _rms(x, nmix_ref[...]).astype(BF16)
    def in_proj(lo, hi):
        return jnp.dot(xn, wmain_ref[:, lo:hi], preferred_element_type=F32)

    zext_ref[8:8 + tl, :] = in_proj(OFF_QKV, OFF_GATE)
    zab = jnp.dot(xn, wab_ref[...], preferred_element_type=F32)
    cw = cw_ref[...]
    y = zext_ref[5:5 + tl, :] * cw[0:1]
    for j in range(1, CONV_W):
        y = y + zext_ref[5 + j:5 + j + tl, :] * cw[j:j + 1]
    tail = zext_ref[tl + 5:tl + 8, :]
    ct_ref[...] = tail
    zext_ref[5:8, :] = tail
    qkv = _silu(y)

    uv = jax.nn.gelu(in_proj(0, OFF_QKV))
    row = lax.broadcasted_iota(jnp.int32, (CHUNK_A, CHUNK_A), 0)
    col = lax.broadcasted_iota(jnp.int32, (CHUNK_A, CHUNK_A), 1)
    causal = col <= row
    for h in range(N_HEADS):
        hs = slice(h * HEAD_DIM, (h + 1) * HEAD_DIM)
        u_h = uv[:, hs]
        v_h = _rms(uv[:, A_WIDTH + h * HEAD_DIM:A_WIDTH + (h + 1) * HEAD_DIM], avg_ref[...]).astype(BF16)
        w_h = jnp.where(causal, wsp_ref[h], 0.0).astype(BF16)
        bias_h = bsp_ref[:, h:h + 1]
        for c in range(tl // CHUNK_A):
            rs = slice(c * CHUNK_A, (c + 1) * CHUNK_A)
            mixed = jnp.dot(w_h, v_h[rs], preferred_element_type=F32) + bias_h
            ob_ref[rs, hs] = _rms(u_h[rs] * mixed, aog_ref[...]).astype(BF16)

    z_gate = in_proj(OFF_GATE, Z_MAIN)
    abp = abp_ref[...]
    g = -jnp.exp(abp[0:1]) * _softplus(zab[:, :128] + abp[1:2])
    beta = jax.nn.sigmoid(zab[:, 128:])

    r2 = lax.broadcasted_iota(jnp.int32, (GROUP, GROUP), 0)
    c2 = lax.broadcasted_iota(jnp.int32, (GROUP, GROUP), 1)
    same = (r2 // CHUNK_D) == (c2 // CHUNK_D)
    strict_bd = same & (c2 < r2)
    col_ones = jnp.concatenate([jnp.where(same & (c2 <= r2), 1.0, 0.0), jnp.where(same, 1.0, 0.0)],
                               axis=0).astype(BF16)
    upper_ones = jnp.where(same & (r2 <= c2), 1.0, 0.0).astype(BF16)
    bd_ref[...] = jnp.where(same, 1.0, 0.0).astype(BF16)
    n_grp = tl // GROUP
    gam_parts, glast_parts, gam_t = [], [], []
    for gi in range(n_grp):
        g_grp = g[gi * GROUP:(gi + 1) * GROUP]
        by_col = jnp.dot(col_ones, jnp.concatenate(_split3(g_grp), axis=1), preferred_element_type=F32)
        by_col = by_col[:, :128] + by_col[:, 128:256] + by_col[:, 256:]
        gam_parts.append(by_col[:GROUP])
        glast_parts.append(by_col[GROUP:])
        by_row = jnp.dot(jnp.concatenate(_split3(g_grp.T), axis=0), upper_ones, preferred_element_type=F32)
        gam_t.append(by_row[:128] + by_row[128:256] + by_row[256:])
    gam = jnp.concatenate(gam_parts, axis=0)
    glast = jnp.concatenate(glast_parts, axis=0)

    rb = lax.broadcasted_iota(jnp.int32, (CHUNK_D, CHUNK_D), 0)
    cb = lax.broadcasted_iota(jnp.int32, (CHUNK_D, CHUNK_D), 1)
    incl = cb <= rb

    heads, a_packed, rhs = [], [], []
    for h in range(N_HEADS):
        q_h = _l2(qkv[:, h * HEAD_DIM:(h + 1) * HEAD_DIM]) * (HEAD_DIM ** -0.5)
        k_h = _l2(qkv[:, B_WIDTH + h * HEAD_DIM:B_WIDTH + (h + 1) * HEAD_DIM])
        v_h = qkv[:, 2 * B_WIDTH + h * HEAD_DIM:2 * B_WIDTH + (h + 1) * HEAD_DIM]
        gc_h = gam[:, h:h + 1]
        gl_h = glast[:, h:h + 1]
        bc_h = beta[:, h:h + 1]
        eg_h = jnp.exp(gc_h)
        for gi in range(n_grp):
            gs = slice(gi * GROUP, (gi + 1) * GROUP)
            kk = _dot_nt(k_h[gs], k_h[gs])
            decay = jnp.exp(jnp.where(strict_bd, gc_h[gs] - gam_t[gi][h:h + 1, :], 0.0))
            a_packed.append(_fold(jnp.where(strict_bd, bc_h[gs] * kk * decay, 0.0)))
        rhs.append(_split(jnp.concatenate([bc_h * v_h, (bc_h * eg_h) * k_h], axis=1)))
        heads.append((q_h * eg_h, q_h, k_h, k_h * jnp.exp(gl_h - gc_h), gc_h, jnp.exp(gl_h)))
    inv_split = [_split(inv) for inv in _unit_lower_inverses_packed(a_packed, bd_ref)]
    sol = []
    for h in range(N_HEADS):
        sol.append([_dot3(_expand(inv_split[h * n_grp + gi][0], bd_ref), _expand(inv_split[h * n_grp + gi][1], bd_ref),
                          rhs[h][0][gi * GROUP:(gi + 1) * GROUP], rhs[h][1][gi * GROUP:(gi + 1) * GROUP])
                    for gi in range(n_grp)])

    for i in range(tl // CHUNK_D):
        rs = slice(i * CHUNK_D, (i + 1) * CHUNK_D)
        gi, j = divmod(i, GROUP // CHUNK_D)
        ls = slice(j * CHUNK_D, (j + 1) * CHUNK_D)
        for h in range(N_HEADS):
            qb_h, q_h, k_h, kend_h, gc_h, btot_h = heads[h]
            decay = jnp.where(incl, jnp.exp(jnp.where(incl, gc_h[rs] - gam_t[gi][h:h + 1, ls], 0.0)), 0.0)
            qk = _dot_nt(q_h[rs], k_h[rs]) * decay
            s_old = s_ref[h]
            from_s = _dot(jnp.concatenate([sol[h][gi][ls, HEAD_DIM:], qb_h[rs]], axis=0), s_old)
            u = sol[h][gi][ls, :HEAD_DIM] - from_s[:CHUNK_D]
            from_u = _dot(jnp.concatenate([qk, kend_h[rs].T], axis=0), u)
            o = from_s[CHUNK_D:] + from_u[:CHUNK_D]
            s_ref[h] = btot_h[i * CHUNK_D:i * CHUNK_D + 1] * s_old + from_u[CHUNK_D:]
            gate = z_gate[rs, h * HEAD_DIM:(h + 1) * HEAD_DIM]
            ob_ref[rs, A_WIDTH + h * HEAD_DIM:A_WIDTH + (h + 1) * HEAD_DIM] = (
                _rms(o, bog_ref[...]) * _silu(gate)).astype(BF16)

    xo_ref[...] = x + jnp.dot(ob_ref[...], wout_ref[...], preferred_element_type=F32)


def _mixer_weight_specs(layer):
    return [_layer_spec((1, D_MODEL), layer), _layer_spec((D_MODEL, Z_MAIN), layer), _layer_spec((D_MODEL, 256), layer),
            _layer_spec((1, HEAD_DIM), layer), _layer_spec((1, HEAD_DIM), layer), _layer_spec((1, HEAD_DIM), layer)]


def _mixer_weights(w):
    return [w["n_mix"], w["w_main"], w["w_ab"], w["a_v_gain"], w["a_out_gain"], w["b_out_gain"]]


def _mix_prompt(x, layer, w):
    bsz, length, _ = x.shape
    tl = MIX_ROWS
    row_spec = pl.BlockSpec((None, tl, D_MODEL), lambda b, t: (b, t, 0))
    in_specs = [row_spec] + _mixer_weight_specs(layer) + [
        _layer_spec((N_HEADS, CHUNK_A, CHUNK_A), layer), _layer_spec((CHUNK_A, N_HEADS), layer),
        _layer_spec((CONV_W, 3 * B_WIDTH), layer), _layer_spec((2, 128), layer), _layer_spec((D_MODEL, D_MODEL), layer)]
    out_specs = [row_spec,
                 pl.BlockSpec((None, N_HEADS, HEAD_DIM, HEAD_DIM), lambda b, t: (b, 0, 0, 0)),
                 pl.BlockSpec((None, CONV_W - 1, 3 * B_WIDTH), lambda b, t: (b, 0, 0))]
    out_shape = [jax.ShapeDtypeStruct(x.shape, F32),
                 jax.ShapeDtypeStruct((bsz, N_HEADS, HEAD_DIM, HEAD_DIM), F32),
                 jax.ShapeDtypeStruct((bsz, CONV_W - 1, 3 * B_WIDTH), F32)]
    return pl.pallas_call(
        _mix_prompt_kernel,
        grid=(bsz, length // tl),
        in_specs=in_specs,
        out_specs=out_specs,
        out_shape=out_shape,
        scratch_shapes=[pltpu.VMEM((tl + 8, 3 * B_WIDTH), F32), pltpu.VMEM((tl, D_MODEL), BF16),
                        pltpu.VMEM((GROUP, GROUP), BF16)],
        compiler_params=pltpu.CompilerParams(dimension_semantics=("arbitrary", "arbitrary"),
                                             vmem_limit_bytes=VMEM_LIMIT_BYTES),
    )(x, *_mixer_weights(w), w["a_w_s"], w["a_b_s_t"], w["b_conv"], w["ab_par"], w["w_out"])


def _mix_sample_kernel(x_ref, s_ref, cpad_ref, nmix_ref, wmain_ref, wab_ref, avg_ref, aog_ref, bog_ref,
                       coef_ref, bias_ref, cw_ref, abp_ref, wout_ref,
                       xo_ref, so_ref, zq_ref, vo_ref,
                       ob_ref, *, n_tok):
    rows = x_ref.shape[0]
    nb = rows // n_tok
    x = x_ref[...]
    xn = _rms(x, nmix_ref[...]).astype(BF16)
    z = jnp.dot(xn, wmain_ref[...], preferred_element_type=F32)
    zab = jnp.dot(xn, wab_ref[...], preferred_element_type=F32)
    tok = lax.broadcasted_iota(jnp.int32, (rows, 1), 0) % n_tok

    def prev(a, d):
        return pltpu.roll(a, d, axis=0)

    def prev_or_zero(a, d):
        return a if d == 0 else jnp.where(tok >= d, prev(a, d), 0.0)

    def per_head(fn, a):
        return jnp.concatenate([fn(a[:, h * HEAD_DIM:(h + 1) * HEAD_DIM]) for h in range(N_HEADS)], axis=1)

    uv = jax.nn.gelu(z[:, :2 * A_WIDTH])
    vn = per_head(lambda a: _rms(a, avg_ref[...]), uv[:, A_WIDTH:])
    vo_ref[...] = vn
    mixed = bias_ref[...]
    for d in range(n_tok):
        mixed = mixed + coef_ref[d] * prev_or_zero(vn, d)
    ob_ref[:, :A_WIDTH] = per_head(lambda a: _rms(a, aog_ref[...]), uv[:, :A_WIDTH] * mixed)

    zq = z[:, OFF_QKV:OFF_GATE]
    zq_ref[...] = zq
    cpad = cpad_ref[...]
    cw = cw_ref[...]
    y = zq * cw[CONV_W - 1:CONV_W]
    for d in range(1, CONV_W):
        carried = pltpu.roll(cpad, rows - (n_tok - d), axis=0)
        y = y + jnp.where(tok >= d, prev(zq, d), carried) * cw[CONV_W - 1 - d:CONV_W - d]
    qkv = _silu(y)

    abp = abp_ref[...]
    g_all = -jnp.exp(abp[0:1]) * _softplus(zab[:, :128] + abp[1:2])
    beta_all = jax.nn.sigmoid(zab[:, 128:])

    sub = lax.broadcasted_iota(jnp.int32, (8, 1), 0)
    first_half = sub < n_tok
    o_heads, kend_heads, u_heads, btot_heads = [], [], [], []
    for h in range(N_HEADS):
        q = _l2(qkv[:, h * HEAD_DIM:(h + 1) * HEAD_DIM]) * (HEAD_DIM ** -0.5)
        k = _l2(qkv[:, B_WIDTH + h * HEAD_DIM:B_WIDTH + (h + 1) * HEAD_DIM])
        v = qkv[:, 2 * B_WIDTH + h * HEAD_DIM:2 * B_WIDTH + (h + 1) * HEAD_DIM]
        g = jnp.broadcast_to(g_all[:, h:h + 1], (rows, HEAD_DIM))
        beta = jnp.broadcast_to(beta_all[:, h:h + 1], (rows, HEAD_DIM))
        gam = g
        for d in range(1, n_tok):
            gam = gam + prev_or_zero(g, d)
        gam_last = jnp.where(tok == n_tok - 1, gam, 0.0)
        for d in range(1, n_tok):
            gam_last = gam_last + jnp.where(tok == n_tok - 1 - d, pltpu.roll(gam, rows - d, axis=0), 0.0)
        eg = jnp.exp(gam)

        def decay_to(d, gam=gam):
            return jnp.exp(jnp.where(tok >= d, gam - prev(gam, d), 0.0))

        a_sub = [None] + [jnp.where(tok >= d, beta * jnp.sum(k * prev(k, d), axis=-1, keepdims=True) * decay_to(d),
                                    0.0) for d in range(1, n_tok)]
        def forward_substitute(rhs, a_sub=a_sub):
            sol = rhs
            for t in range(1, n_tok):
                acc = rhs
                for d in range(1, t + 1):
                    acc = acc - a_sub[d] * prev(sol, d)
                sol = jnp.where(tok == t, acc, sol)
            return sol

        w_blk = forward_substitute(beta * v)
        kb_blk = forward_substitute((beta * eg) * k)
        qb = q * eg

        kb_s, qb_s = [], []
        for p in range(rows // 8):
            kb_t, qb_t = kb_blk[8 * p:8 * p + 8], qb[8 * p:8 * p + 8]
            f0 = _dot(jnp.where(first_half, kb_t, pltpu.roll(qb_t, n_tok, axis=0)), s_ref[2 * p, h])
            f1 = _dot(jnp.where(first_half, pltpu.roll(kb_t, n_tok, axis=0), qb_t), s_ref[2 * p + 1, h])
            kb_s.append(jnp.where(first_half, f0, pltpu.roll(f1, n_tok, axis=0)))
            qb_s.append(jnp.where(first_half, pltpu.roll(f0, n_tok, axis=0), f1))
        u = w_blk - jnp.concatenate(kb_s, axis=0)
        o = jnp.concatenate(qb_s, axis=0)
        for d in range(n_tok):
            qk = jnp.where(tok >= d, jnp.sum(q * prev(k, d), axis=-1, keepdims=True) * decay_to(d), 0.0)
            o = o + qk * prev_or_zero(u, d)
        o_heads.append(o)
        kend_heads.append(k * jnp.exp(gam_last - gam))
        u_heads.append(u)
        btot_heads.append(jnp.broadcast_to(jnp.exp(gam_last), (rows, HEAD_DIM)))

    kend_t = jnp.concatenate(kend_heads, axis=0).T
    u_all = jnp.concatenate(u_heads, axis=0).astype(BF16)
    owner = lax.broadcasted_iota(jnp.int32, (1, N_HEADS * rows), 1) // n_tok
    for h in range(N_HEADS):
        for b in range(nb):
            mine = jnp.where(owner == h * nb + b, kend_t, 0.0).astype(BF16)
            last = b * n_tok + n_tok - 1
            so_ref[b, h] = btot_heads[h][last:last + 1] * s_ref[b, h] + jnp.dot(
                mine, u_all, preferred_element_type=F32)

    gate = z[:, OFF_GATE:]
    ob_ref[:, A_WIDTH:] = per_head(lambda a: _rms(a, bog_ref[...]), jnp.concatenate(o_heads, axis=1)) * _silu(gate)
    xo_ref[...] = x + _dot(ob_ref[...], wout_ref[...])


def _mix_sample(x, layer, state_s, cpad, w, n_tok):
    rows_total = x.shape[0]
    nb = SAMPLE_GROUP
    rows = nb * n_tok
    row_spec = lambda width: pl.BlockSpec((rows, width), lambda i: (i, 0))
    s_blk = (nb, N_HEADS, HEAD_DIM, HEAD_DIM)
    in_specs = [row_spec(D_MODEL),
                pl.BlockSpec((None,) + s_blk, lambda i: (layer, i, 0, 0, 0)),
                pl.BlockSpec((None, rows, 3 * B_WIDTH), lambda i: (layer, i, 0))] + _mixer_weight_specs(layer) + [
        _layer_spec((n_tok, rows, A_WIDTH), layer), _layer_spec((rows, A_WIDTH), layer),
        _layer_spec((CONV_W, 3 * B_WIDTH), layer), _layer_spec((2, 128), layer), _layer_spec((D_MODEL, D_MODEL), layer)]
    out_specs = [row_spec(D_MODEL), pl.BlockSpec(s_blk, lambda i: (i, 0, 0, 0)),
                 row_spec(3 * B_WIDTH), row_spec(A_WIDTH)]
    out_shape = [jax.ShapeDtypeStruct(x.shape, F32), jax.ShapeDtypeStruct(state_s.shape[1:], F32),
                 jax.ShapeDtypeStruct((rows_total, 3 * B_WIDTH), F32), jax.ShapeDtypeStruct((rows_total, A_WIDTH), F32)]
    return pl.pallas_call(
        functools.partial(_mix_sample_kernel, n_tok=n_tok),
        grid=(rows_total // rows,),
        in_specs=in_specs,
        out_specs=out_specs,
        out_shape=out_shape,
        scratch_shapes=[pltpu.VMEM((rows, D_MODEL), F32)],
        compiler_params=pltpu.CompilerParams(dimension_semantics=("arbitrary",),
                                             vmem_limit_bytes=VMEM_LIMIT_BYTES),
    )(x, state_s, cpad, *_mixer_weights(w), w["a_coef"], w["a_bias"], w["b_conv"], w["ab_par"], w["w_out"])


def _prep_weights(n_tok, norm_ffn1, w_ffn1_in, w_ffn1_out, norm_mix, w_in, a_v_gain, a_spatial_w, a_spatial_b,
                  a_out_gain, b_conv_w, b_a_log, b_dt_bias, b_out_gain, w_out, norm_ffn2, w_ffn2_in, w_ffn2_out,
                  norm_ple, w_ple_gate, w_ple_proj):
    o_ab = OFF_GATE
    w_main = jnp.concatenate([w_in[:, :, :o_ab], w_in[:, :, o_ab + 2 * N_HEADS:]], axis=2).astype(BF16)
    lane_pad = jnp.zeros((DEPTH, D_MODEL, 128 - N_HEADS), F32)
    w_ab = jnp.concatenate([w_in[:, :, o_ab:o_ab + N_HEADS], lane_pad,
                            w_in[:, :, o_ab + N_HEADS:o_ab + 2 * N_HEADS], lane_pad], axis=2).astype(BF16)
    par_pad = jnp.zeros((DEPTH, 128 - N_HEADS), F32)
    ab_par = jnp.stack([jnp.concatenate([b_a_log, par_pad], axis=1),
                        jnp.concatenate([b_dt_bias, par_pad], axis=1)], axis=1)

    def sample_rows(a):
        return jnp.tile(jnp.repeat(jnp.transpose(a, (0, 2, 1)), HEAD_DIM, axis=2), (1, SAMPLE_GROUP, 1))

    ws_small = a_spatial_w[:, :, :n_tok, :n_tok]
    a_coef = jnp.stack([sample_rows(jnp.pad(jnp.diagonal(ws_small, offset=-d, axis1=2, axis2=3),
                                            ((0, 0), (0, 0), (d, 0)))) for d in range(n_tok)], axis=1)
    return dict(
        n_f1=norm_ffn1[:, None], w_f1_in=w_ffn1_in.astype(BF16), w_f1_out=w_ffn1_out.astype(BF16),
        n_mix=norm_mix[:, None], w_main=w_main, w_ab=w_ab, ab_par=ab_par,
        a_v_gain=a_v_gain[:, None], a_out_gain=a_out_gain[:, None], b_out_gain=b_out_gain[:, None],
        a_w_s=a_spatial_w, a_b_s_t=jnp.transpose(a_spatial_b, (0, 2, 1)),
        a_coef=a_coef, a_bias=sample_rows(a_spatial_b[:, :, :n_tok]),
        b_conv=b_conv_w, w_out=w_out.astype(BF16),
        n_f2=norm_ffn2[:, None], w_f2_in=w_ffn2_in.astype(BF16), w_f2_out=w_ffn2_out.astype(BF16),
        n_ple=norm_ple[:, None], w_ple_gate=w_ple_gate.astype(BF16), w_ple_proj=w_ple_proj.astype(BF16),
    )


def kernel(x_prompt, x_sample, state_S, state_conv, p_prompt, p_sample, norm_ffn1, w_ffn1_in, w_ffn1_out, norm_mix, w_in, a_v_gain, a_spatial_w, a_spatial_b, a_out_gain, b_conv_w, b_a_log, b_dt_bias, b_out_gain, w_out, norm_ffn2, w_ffn2_in, w_ffn2_out, norm_ple, w_ple_gate, w_ple_proj, final_norm):
    bsz, length, _ = x_prompt.shape
    dec_bsz, n_tok, _ = x_sample.shape
    assert length % MIX_ROWS == 0 and MIX_ROWS % CHUNK_A == 0 and MIX_ROWS % GROUP == 0
    assert dec_bsz % SAMPLE_GROUP == 0
    assert n_tok % CHUNK_A != 0 and n_tok % CHUNK_D != 0
    assert 2 * n_tok == 8 and N_HEADS * SAMPLE_GROUP * n_tok == HEAD_DIM and n_tok >= CONV_W - 1

    w = _prep_weights(n_tok, norm_ffn1, w_ffn1_in, w_ffn1_out, norm_mix, w_in, a_v_gain, a_spatial_w, a_spatial_b,
                      a_out_gain, b_conv_w, b_a_log, b_dt_bias, b_out_gain, w_out, norm_ffn2, w_ffn2_in,
                      w_ffn2_out, norm_ple, w_ple_gate, w_ple_proj)
    final = final_norm[None, None]
    xp = x_prompt.reshape(bsz * length, D_MODEL)
    xs = x_sample.reshape(dec_bsz * n_tok, D_MODEL)
    pp = p_prompt.reshape(DEPTH, bsz * length, PLE_DIM)
    ps = p_sample.reshape(DEPTH, dec_bsz * n_tok, PLE_DIM)
    keep = CONV_W - 1
    cpad = jnp.pad(state_conv, ((0, 0), (0, 0), (n_tok - keep, 0), (0, 0))).reshape(DEPTH, dec_bsz * n_tok, 3 * B_WIDTH)

    s_prompt, c_prompt, s_sample, c_sample, v_sample = [], [], [], [], []
    for i in range(DEPTH):
        last = dict(final_gain=final) if i == DEPTH - 1 else {}
        ple = (w["n_ple"], w["w_ple_gate"], w["w_ple_proj"])

        xp = _ffn(xp, i, w["n_f1"], w["w_f1_in"], w["w_f1_out"])
        xp, sp, cp = _mix_prompt(xp.reshape(bsz, length, D_MODEL), i, w)
        xp = _ffn(xp.reshape(bsz * length, D_MODEL), i, w["n_f2"], w["w_f2_in"], w["w_f2_out"], ple=(pp,) + ple, **last)

        xs = _ffn(xs, i, w["n_f1"], w["w_f1_in"], w["w_f1_out"])
        xs, ss, zq, vs = _mix_sample(xs, i, state_S, cpad, w, n_tok)
        xs = _ffn(xs, i, w["n_f2"], w["w_f2_in"], w["w_f2_out"], ple=(ps,) + ple, **last)

        s_prompt.append(sp)
        c_prompt.append(cp)
        s_sample.append(ss)
        c_sample.append(zq.reshape(dec_bsz, n_tok, 3 * B_WIDTH)[:, n_tok - keep:])
        v_sample.append(vs.reshape(dec_bsz, n_tok, N_HEADS, HEAD_DIM))

    return (xp.reshape(bsz, length, D_MODEL), xs.reshape(dec_bsz, n_tok, D_MODEL), jnp.stack(s_prompt),
            jnp.stack(c_prompt), jnp.stack(s_sample), jnp.stack(c_sample), jnp.stack(v_sample))
```

```python
import functools

import jax
import jax.numpy as jnp
from jax import lax
from jax.experimental import pallas as pl
from jax.experimental.pallas import tpu as pltpu

F32 = jnp.float32
BF16 = jnp.bfloat16
EPS = 1e-6

D_MODEL = 1024
D_FF = 2816
DEPTH = 4
N_HEADS = 4
HEAD_DIM = 128
A_WIDTH = N_HEADS * HEAD_DIM
B_WIDTH = N_HEADS * HEAD_DIM
CHUNK_A = 128
CHUNK_D = 64
CONV_W = 4
PLE_DIM = 256
Z_MAIN = 2 * A_WIDTH + 4 * B_WIDTH
OFF_QKV = 2 * A_WIDTH
OFF_GATE = OFF_QKV + 3 * B_WIDTH

VMEM_LIMIT_BYTES = 52 * 1024 * 1024
MXU_N = 256
FFN_ROWS = 512
GROUP = 4 * CHUNK_D
MIX_ROWS = 512
SAMPLE_GROUP = 8


def _rms(x, gain):
    return x * lax.rsqrt(jnp.mean(x * x, axis=-1, keepdims=True) + EPS) * gain


def _l2(x):
    return x * lax.rsqrt(jnp.sum(x * x, axis=-1, keepdims=True) + EPS)


def _silu(x):
    return x * jax.nn.sigmoid(x)


def _softplus(x):
    return jnp.maximum(x, 0.0) + jnp.log1p(jnp.exp(-jnp.abs(x)))


def _dot(a, b):
    return jnp.dot(a.astype(BF16), b.astype(BF16), preferred_element_type=F32)


def _dot_nt(a, b):
    return lax.dot_general(a.astype(BF16), b.astype(BF16), (((1,), (1,)), ((), ())),
                           preferred_element_type=F32)


def _split3(a):
    p1 = a.astype(BF16)
    r1 = a - p1.astype(F32)
    p2 = r1.astype(BF16)
    p3 = (r1 - p2.astype(F32)).astype(BF16)
    return p1, p2, p3


def _const_spec(shape):
    zeros = (0,) * len(shape)
    return pl.BlockSpec(shape, lambda *_: zeros, pipeline_mode=pl.Buffered(1))


def _layer_spec(shape, layer, block=None):
    index = (layer,) + (0,) * (len(shape) - 1) + (0 if block is None else block,)
    return pl.BlockSpec((None,) + tuple(shape), lambda *_: index, pipeline_mode=pl.Buffered(1))


def _ffn_kernel(*refs, with_ple, with_final):
    x_ref, gain_ref, wg_ref, wu_ref, wo_ref = refs[:5]
    o_ref = refs[-1]
    x = x_ref[...]
    xn = _rms(x, gain_ref[...]).astype(BF16)
    acc = jnp.zeros_like(x)
    for c in range(D_FF // MXU_N):
        sl = slice(c * MXU_N, (c + 1) * MXU_N)
        gate = jnp.dot(xn, wg_ref[:, sl], preferred_element_type=F32)
        up = jnp.dot(xn, wu_ref[:, sl], preferred_element_type=F32)
        h = (_silu(gate) * up).astype(BF16)
        acc = acc + jnp.dot(h, wo_ref[sl, :], preferred_element_type=F32)
    x = x + 0.5 * acc
    if with_ple:
        p_ref, npl_ref, wpg_ref, wpp_ref = refs[5:9]
        emb = _dot(p_ref[...], wpp_ref[...])
        gate = _dot(_rms(x, npl_ref[...]), wpg_ref[...])
        x = x + emb * jax.nn.sigmoid(gate)
    if with_final:
        x = _rms(x, refs[9][...])
    o_ref[...] = x


def _ffn(x, layer, gain, w_in, w_out, ple=None, final_gain=None):
    rows = x.shape[0]
    tm = min(FFN_ROWS, rows)
    row_spec = pl.BlockSpec((tm, D_MODEL), lambda i: (i, 0))
    in_specs = [row_spec, _layer_spec((1, D_MODEL), layer),
                _layer_spec((D_MODEL, D_FF), layer, block=0), _layer_spec((D_MODEL, D_FF), layer, block=1),
                _layer_spec((D_FF, D_MODEL), layer)]
    args = [x, gain, w_in, w_in, w_out]
    if ple is not None:
        p, n_ple, w_gate, w_proj = ple
        in_specs += [pl.BlockSpec((None, tm, PLE_DIM), lambda i: (layer, i, 0)), _layer_spec((1, D_MODEL), layer),
                     _layer_spec((D_MODEL, D_MODEL), layer), _layer_spec((PLE_DIM, D_MODEL), layer)]
        args += [p, n_ple, w_gate, w_proj]
    if final_gain is not None:
        in_specs.append(_layer_spec((1, D_MODEL), 0))
        args.append(final_gain)
    return pl.pallas_call(
        functools.partial(_ffn_kernel, with_ple=ple is not None, with_final=final_gain is not None),
        grid=(rows // tm,),
        in_specs=in_specs,
        out_specs=row_spec,
        out_shape=jax.ShapeDtypeStruct(x.shape, F32),
        compiler_params=pltpu.CompilerParams(dimension_semantics=("arbitrary",),
                                             vmem_limit_bytes=VMEM_LIMIT_BYTES),
    )(*args)


def _split(a):
    hi = a.astype(BF16)
    lo = (a - hi.astype(F32)).astype(BF16)
    return hi, lo


def _dot3(a_hi, a_lo, b_hi, b_lo):
    m = a_hi.shape[0]
    both = jnp.dot(jnp.concatenate([a_hi, a_lo], axis=0), b_hi, preferred_element_type=F32)
    return both[:m] + both[m:] + jnp.dot(a_hi, b_lo, preferred_element_type=F32)


def _fold(block_diag):
    n = block_diag.shape[0] // CHUNK_D
    out = block_diag[0:CHUNK_D]
    for g in range(1, n):
        out = out + block_diag[g * CHUNK_D:(g + 1) * CHUNK_D]
    return out


def _expand(packed, diag_ones_ref):
    n = packed.shape[1] // CHUNK_D
    return jnp.concatenate([packed] * n, axis=0) * diag_ones_ref[...]


def _unit_lower_inverses_packed(l_packed_list, diag_ones_ref):
    c, width = l_packed_list[0].shape
    row = lax.broadcasted_iota(jnp.int32, (c, width), 0)
    col = lax.broadcasted_iota(jnp.int32, (c, width), 1) % c
    zero = jnp.zeros((), BF16)

    def lower_left(bs):
        return (row // (2 * bs) == col // (2 * bs)) & ((row // bs) % 2 == 1) & ((col // bs) % 2 == 0)

    l_bf = [l.astype(BF16) for l in l_packed_list]
    xs = [jnp.where(row == col, 1.0, 0.0) - jnp.where(lower_left(1), l, 0.0) for l in l_packed_list]
    bs = 2
    while bs < c:
        sel = lower_left(bs)
        x_bf = [x.astype(BF16) for x in xs]
        ys = [jnp.dot(jnp.where(sel, l, zero), _expand(x, diag_ones_ref), preferred_element_type=F32)
              for l, x in zip(l_bf, x_bf)]
        xs = [x - jnp.dot(xb, _expand(y.astype(BF16), diag_ones_ref), preferred_element_type=F32)
              for x, xb, y in zip(xs, x_bf, ys)]
        bs *= 2
    return xs


def _mix_prompt_kernel(x_ref, nmix_ref, wmain_ref, wab_ref, avg_ref, aog_ref, bog_ref, wsp_ref, bsp_ref,
                       cw_ref, abp_ref, wout_ref,
                       xo_ref, s_ref, ct_ref,
                       zext_ref, ob_ref, bd_ref):
    tl = x_ref.shape[0]
    step = pl.program_id(1)

    @pl.when(step == 0)
    def _():
        s_ref[...] = jnp.zeros_like(s_ref)
        zext_ref[0:8, :] = jnp.zeros((8, 3 * B_WIDTH), F32)

    x = x_ref[...]
    xn = _rms(x, nmix_ref[...]).astype(BF16)
    def in_proj(lo, hi):
        return jnp.dot(xn, wmain_ref[:, lo:hi], preferred_element_type=F32)

    zext_ref[8:8 + tl, :] = in_proj(OFF_QKV, OFF_GATE)
    zab = jnp.dot(xn, wab_ref[...], preferred_element_type=F32)
    cw = cw_ref[...]
    y = zext_ref[5:5 + tl, :] * cw[0:1]
    for j in range(1, CONV_W):
        y = y + zext_ref[5 + j:5 + j + tl, :] * cw[j:j + 1]
    tail = zext_ref[tl + 5:tl + 8, :]
    ct_ref[...] = tail
    zext_ref[5:8, :] = tail
    qkv = _silu(y)

    uv = jax.nn.gelu(in_proj(0, OFF_QKV))
    row = lax.broadcasted_iota(jnp.int32, (CHUNK_A, CHUNK_A), 0)
    col = lax.broadcasted_iota(jnp.int32, (CHUNK_A, CHUNK_A), 1)
    causal = col <= row
    for h in range(N_HEADS):
        hs = slice(h * HEAD_DIM, (h + 1) * HEAD_DIM)
        u_h = uv[:, hs]
        v_h = _rms(uv[:, A_WIDTH + h * HEAD_DIM:A_WIDTH + (h + 1) * HEAD_DIM], avg_ref[...]).astype(BF16)
        w_h = jnp.where(causal, wsp_ref[h], 0.0).astype(BF16)
        bias_h = bsp_ref[:, h:h + 1]
        for c in range(tl // CHUNK_A):
            rs = slice(c * CHUNK_A, (c + 1) * CHUNK_A)
            mixed = jnp.dot(w_h, v_h[rs], preferred_element_type=F32) + bias_h
            ob_ref[rs, hs] = _rms(u_h[rs] * mixed, aog_ref[...]).astype(BF16)

    z_gate = in_proj(OFF_GATE, Z_MAIN)
    abp = abp_ref[...]
    g = -jnp.exp(abp[0:1]) * _softplus(zab[:, :128] + abp[1:2])
    beta = jax.nn.sigmoid(zab[:, 128:])

    r2 = lax.broadcasted_iota(jnp.int32, (GROUP, GROUP), 0)
    c2 = lax.broadcasted_iota(jnp.int32, (GROUP, GROUP), 1)
    same = (r2 // CHUNK_D) == (c2 // CHUNK_D)
    strict_bd = same & (c2 < r2)
    col_ones = jnp.concatenate([jnp.where(same & (c2 <= r2), 1.0, 0.0), jnp.where(same, 1.0, 0.0)],
                               axis=0).astype(BF16)
    upper_ones = jnp.where(same & (r2 <= c2), 1.0, 0.0).astype(BF16)
    bd_ref[...] = jnp.where(same, 1.0, 0.0).astype(BF16)
    n_grp = tl // GROUP
    gam_parts, glast_parts, gam_t = [], [], []
    for gi in range(n_grp):
        g_grp = g[gi * GROUP:(gi + 1) * GROUP]
        by_col = jnp.dot(col_ones, jnp.concatenate(_split3(g_grp), axis=1), preferred_element_type=F32)
        by_col = by_col[:, :128] + by_col[:, 128:256] + by_col[:, 256:]
        gam_parts.append(by_col[:GROUP])
        glast_parts.append(by_col[GROUP:])
        by_row = jnp.dot(jnp.concatenate(_split3(g_grp.T), axis=0), upper_ones, preferred_element_type=F32)
        gam_t.append(by_row[:128] + by_row[128:256] + by_row[256:])
    gam = jnp.concatenate(gam_parts, axis=0)
    glast = jnp.concatenate(glast_parts, axis=0)

    rb = lax.broadcasted_iota(jnp.int32, (CHUNK_D, CHUNK_D), 0)
    cb = lax.broadcasted_iota(jnp.int32, (CHUNK_D, CHUNK_D), 1)
    incl = cb <= rb

    heads, a_packed, rhs = [], [], []
    for h in range(N_HEADS):
        q_h = _l2(qkv[:, h * HEAD_DIM:(h + 1) * HEAD_DIM]) * (HEAD_DIM ** -0.5)
        k_h = _l2(qkv[:, B_WIDTH + h * HEAD_DIM:B_WIDTH + (h + 1) * HEAD_DIM])
        v_h = qkv[:, 2 * B_WIDTH + h * HEAD_DIM:2 * B_WIDTH + (h + 1) * HEAD_DIM]
        gc_h = gam[:, h:h + 1]
        gl_h = glast[:, h:h + 1]
        bc_h = beta[:, h:h + 1]
        eg_h = jnp.exp(gc_h)
        for gi in range(n_grp):
            gs = slice(gi * GROUP, (gi + 1) * GROUP)
            kk = _dot_nt(k_h[gs], k_h[gs])
            decay = jnp.exp(jnp.where(strict_bd, gc_h[gs] - gam_t[gi][h:h + 1, :], 0.0))
            a_packed.append(_fold(jnp.where(strict_bd, bc_h[gs] * kk * decay, 0.0)))
        rhs.append(_split(jnp.concatenate([bc_h * v_h, (bc_h * eg_h) * k_h], axis=1)))
        heads.append((q_h * eg_h, q_h, k_h, k_h * jnp.exp(gl_h - gc_h), gc_h, jnp.exp(gl_h)))
    inv_split = [_split(inv) for inv in _unit_lower_inverses_packed(a_packed, bd_ref)]
    sol = []
    for h in range(N_HEADS):
        sol.append([_dot3(_expand(inv_split[h * n_grp + gi][0], bd_ref), _expand(inv_split[h * n_grp + gi][1], bd_ref),
                          rhs[h][0][gi * GROUP:(gi + 1) * GROUP], rhs[h][1][gi * GROUP:(gi + 1) * GROUP])
                    for gi in range(n_grp)])

    for i in range(tl // CHUNK_D):
        rs = slice(i * CHUNK_D, (i + 1) * CHUNK_D)
        gi, j = divmod(i, GROUP // CHUNK_D)
        ls = slice(j * CHUNK_D, (j + 1) * CHUNK_D)
        for h in range(N_HEADS):
            qb_h, q_h, k_h, kend_h, gc_h, btot_h = heads[h]
            decay = jnp.where(incl, jnp.exp(jnp.where(incl, gc_h[rs] - gam_t[gi][h:h + 1, ls], 0.0)), 0.0)
            qk = _dot_nt(q_h[rs], k_h[rs]) * decay
            s_old = s_ref[h]
            from_s = _dot(jnp.concatenate([sol[h][gi][ls, HEAD_DIM:], qb_h[rs]], axis=0), s_old)
            u = sol[h][gi][ls, :HEAD_DIM] - from_s[:CHUNK_D]
            from_u = _dot(jnp.concatenate([qk, kend_h[rs].T], axis=0), u)
            o = from_s[CHUNK_D:] + from_u[:CHUNK_D]
            s_ref[h] = btot_h[i * CHUNK_D:i * CHUNK_D + 1] * s_old + from_u[CHUNK_D:]
            gate = z_gate[rs, h * HEAD_DIM:(h + 1) * HEAD_DIM]
            ob_ref[rs, A_WIDTH + h * HEAD_DIM:A_WIDTH + (h + 1) * HEAD_DIM] = (
                _rms(o, bog_ref[...]) * _silu(gate)).astype(BF16)

    xo_ref[...] = x + jnp.dot(ob_ref[...], wout_ref[...], preferred_element_type=F32)


def _mixer_weight_specs(layer):
    return [_layer_spec((1, D_MODEL), layer), _layer_spec((D_MODEL, Z_MAIN), layer), _layer_spec((D_MODEL, 256), layer),
            _layer_spec((1, HEAD_DIM), layer), _layer_spec((1, HEAD_DIM), layer), _layer_spec((1, HEAD_DIM), layer)]


def _mixer_weights(w):
    return [w["n_mix"], w["w_main"], w["w_ab"], w["a_v_gain"], w["a_out_gain"], w["b_out_gain"]]


def _mix_prompt(x, layer, w):
    bsz, length, _ = x.shape
    tl = MIX_ROWS
    row_spec = pl.BlockSpec((None, tl, D_MODEL), lambda b, t: (b, t, 0))
    in_specs = [row_spec] + _mixer_weight_specs(layer) + [
        _layer_spec((N_HEADS, CHUNK_A, CHUNK_A), layer), _layer_spec((CHUNK_A, N_HEADS), layer),
        _layer_spec((CONV_W, 3 * B_WIDTH), layer), _layer_spec((2, 128), layer), _layer_spec((D_MODEL, D_MODEL), layer)]
    out_specs = [row_spec,
                 pl.BlockSpec((None, N_HEADS, HEAD_DIM, HEAD_DIM), lambda b, t: (b, 0, 0, 0)),
                 pl.BlockSpec((None, CONV_W - 1, 3 * B_WIDTH), lambda b, t: (b, 0, 0))]
    out_shape = [jax.ShapeDtypeStruct(x.shape, F32),
                 jax.ShapeDtypeStruct((bsz, N_HEADS, HEAD_DIM, HEAD_DIM), F32),
                 jax.ShapeDtypeStruct((bsz, CONV_W - 1, 3 * B_WIDTH), F32)]
    return pl.pallas_call(
        _mix_prompt_kernel,
        grid=(bsz, length // tl),
        in_specs=in_specs,
        out_specs=out_specs,
        out_shape=out_shape,
        scratch_shapes=[pltpu.VMEM((tl + 8, 3 * B_WIDTH), F32), pltpu.VMEM((tl, D_MODEL), BF16),
                        pltpu.VMEM((GROUP, GROUP), BF16)],
        compiler_params=pltpu.CompilerParams(dimension_semantics=("arbitrary", "arbitrary"),
                                             vmem_limit_bytes=VMEM_LIMIT_BYTES),
    )(x, *_mixer_weights(w), w["a_w_s"], w["a_b_s_t"], w["b_conv"], w["ab_par"], w["w_out"])


def _mix_sample_kernel(x_ref, s_ref, cpad_ref, nmix_ref, wmain_ref, wab_ref, avg_ref, aog_ref, bog_ref,
                       coef_ref, bias_ref, cw_ref, abp_ref, wout_ref,
                       xo_ref, so_ref, zq_ref, vo_ref,
                       ob_ref, *, n_tok):
    rows = x_ref.shape[0]
    nb = rows // n_tok
    x = x_ref[...]
    xn = _rms(x, nmix_ref[...]).astype(BF16)
    z = jnp.dot(xn, wmain_ref[...], preferred_element_type=F32)
    zab = jnp.dot(xn, wab_ref[...], preferred_element_type=F32)
    tok = lax.broadcasted_iota(jnp.int32, (rows, 1), 0) % n_tok

    def prev(a, d):
        return pltpu.roll(a, d, axis=0)

    def prev_or_zero(a, d):
        return a if d == 0 else jnp.where(tok >= d, prev(a, d), 0.0)

    def per_head(fn, a):
        return jnp.concatenate([fn(a[:, h * HEAD_DIM:(h + 1) * HEAD_DIM]) for h in range(N_HEADS)], axis=1)

    uv = jax.nn.gelu(z[:, :2 * A_WIDTH])
    vn = per_head(lambda a: _rms(a, avg_ref[...]), uv[:, A_WIDTH:])
    vo_ref[...] = vn
    mixed = bias_ref[...]
    for d in range(n_tok):
        mixed = mixed + coef_ref[d] * prev_or_zero(vn, d)
    ob_ref[:, :A_WIDTH] = per_head(lambda a: _rms(a, aog_ref[...]), uv[:, :A_WIDTH] * mixed)

    zq = z[:, OFF_QKV:OFF_GATE]
    zq_ref[...] = zq
    cpad = cpad_ref[...]
    cw = cw_ref[...]
    y = zq * cw[CONV_W - 1:CONV_W]
    for d in range(1, CONV_W):
        carried = pltpu.roll(cpad, rows - (n_tok - d), axis=0)
        y = y + jnp.where(tok >= d, prev(zq, d), carried) * cw[CONV_W - 1 - d:CONV_W - d]
    qkv = _silu(y)

    abp = abp_ref[...]
    g_all = -jnp.exp(abp[0:1]) * _softplus(zab[:, :128] + abp[1:2])
    beta_all = jax.nn.sigmoid(zab[:, 128:])

    sub = lax.broadcasted_iota(jnp.int32, (8, 1), 0)
    first_half = sub < n_tok
    o_heads, kend_heads, u_heads, btot_heads = [], [], [], []
    for h in range(N_HEADS):
        q = _l2(qkv[:, h * HEAD_DIM:(h + 1) * HEAD_DIM]) * (HEAD_DIM ** -0.5)
        k = _l2(qkv[:, B_WIDTH + h * HEAD_DIM:B_WIDTH + (h + 1) * HEAD_DIM])
        v = qkv[:, 2 * B_WIDTH + h * HEAD_DIM:2 * B_WIDTH + (h + 1) * HEAD_DIM]
        g = jnp.broadcast_to(g_all[:, h:h + 1], (rows, HEAD_DIM))
        beta = jnp.broadcast_to(beta_all[:, h:h + 1], (rows, HEAD_DIM))
        gam = g
        for d in range(1, n_tok):
            gam = gam + prev_or_zero(g, d)
        gam_last = jnp.where(tok == n_tok - 1, gam, 0.0)
        for d in range(1, n_tok):
            gam_last = gam_last + jnp.where(tok == n_tok - 1 - d, pltpu.roll(gam, rows - d, axis=0), 0.0)
        eg = jnp.exp(gam)

        def decay_to(d, gam=gam):
            return jnp.exp(jnp.where(tok >= d, gam - prev(gam, d), 0.0))

        a_sub = [None] + [jnp.where(tok >= d, beta * jnp.sum(k * prev(k, d), axis=-1, keepdims=True) * decay_to(d),
                                    0.0) for d in range(1, n_tok)]
        def forward_substitute(rhs, a_sub=a_sub):
            sol = rhs
            for t in range(1, n_tok):
                acc = rhs
                for d in range(1, t + 1):
                    acc = acc - a_sub[d] * prev(sol, d)
                sol = jnp.where(tok == t, acc, sol)
            return sol

        w_blk = forward_substitute(beta * v)
        kb_blk = forward_substitute((beta * eg) * k)
        qb = q * eg

        kb_s, qb_s = [], []
        for p in range(rows // 8):
            kb_t, qb_t = kb_blk[8 * p:8 * p + 8], qb[8 * p:8 * p + 8]
            f0 = _dot(jnp.where(first_half, kb_t, pltpu.roll(qb_t, n_tok, axis=0)), s_ref[2 * p, h])
            f1 = _dot(jnp.where(first_half, pltpu.roll(kb_t, n_tok, axis=0), qb_t), s_ref[2 * p + 1, h])
            kb_s.append(jnp.where(first_half, f0, pltpu.roll(f1, n_tok, axis=0)))
            qb_s.append(jnp.where(first_half, pltpu.roll(f0, n_tok, axis=0), f1))
        u = w_blk - jnp.concatenate(kb_s, axis=0)
        o = jnp.concatenate(qb_s, axis=0)
        for d in range(n_tok):
            qk = jnp.where(tok >= d, jnp.sum(q * prev(k, d), axis=-1, keepdims=True) * decay_to(d), 0.0)
            o = o + qk * prev_or_zero(u, d)
        o_heads.append(o)
        kend_heads.append(k * jnp.exp(gam_last - gam))
        u_heads.append(u)
        btot_heads.append(jnp.broadcast_to(jnp.exp(gam_last), (rows, HEAD_DIM)))

    kend_t = jnp.concatenate(kend_heads, axis=0).T
    u_all = jnp.concatenate(u_heads, axis=0).astype(BF16)
    owner = lax.broadcasted_iota(jnp.int32, (1, N_HEADS * rows), 1) // n_tok
    for h in range(N_HEADS):
        for b in range(nb):
            mine = jnp.where(owner == h * nb + b, kend_t, 0.0).astype(BF16)
            last = b * n_tok + n_tok - 1
            so_ref[b, h] = btot_heads[h][last:last + 1] * s_ref[b, h] + jnp.dot(
                mine, u_all, preferred_element_type=F32)

    gate = z[:, OFF_GATE:]
    ob_ref[:, A_WIDTH:] = per_head(lambda a: _rms(a, bog_ref[...]), jnp.concatenate(o_heads, axis=1)) * _silu(gate)
    xo_ref[...] = x + _dot(ob_ref[...], wout_ref[...])


def _mix_sample(x, layer, state_s, cpad, w, n_tok):
    rows_total = x.shape[0]
    nb = SAMPLE_GROUP
    rows = nb * n_tok
    row_spec = lambda width: pl.BlockSpec((rows, width), lambda i: (i, 0))
    s_blk = (nb, N_HEADS, HEAD_DIM, HEAD_DIM)
    in_specs = [row_spec(D_MODEL),
                pl.BlockSpec((None,) + s_blk, lambda i: (layer, i, 0, 0, 0)),
                pl.BlockSpec((None, rows, 3 * B_WIDTH), lambda i: (layer, i, 0))] + _mixer_weight_specs(layer) + [
        _layer_spec((n_tok, rows, A_WIDTH), layer), _layer_spec((rows, A_WIDTH), layer),
        _layer_spec((CONV_W, 3 * B_WIDTH), layer), _layer_spec((2, 128), layer), _layer_spec((D_MODEL, D_MODEL), layer)]
    out_specs = [row_spec(D_MODEL), pl.BlockSpec(s_blk, lambda i: (i, 0, 0, 0)),
                 row_spec(3 * B_WIDTH), row_spec(A_WIDTH)]
    out_shape = [jax.ShapeDtypeStruct(x.shape, F32), jax.ShapeDtypeStruct(state_s.shape[1:], F32),
                 jax.ShapeDtypeStruct((rows_total, 3 * B_WIDTH), F32), jax.ShapeDtypeStruct((rows_total, A_WIDTH), F32)]
    return pl.pallas_call(
        functools.partial(_mix_sample_kernel, n_tok=n_tok),
        grid=(rows_total // rows,),
        in_specs=in_specs,
        out_specs=out_specs,
        out_shape=out_shape,
        scratch_shapes=[pltpu.VMEM((rows, D_MODEL), F32)],
        compiler_params=pltpu.CompilerParams(dimension_semantics=("arbitrary",),
                                             vmem_limit_bytes=VMEM_LIMIT_BYTES),
    )(x, state_s, cpad, *_mixer_weights(w), w["a_coef"], w["a_bias"], w["b_conv"], w["ab_par"], w["w_out"])


def _prep_weights(n_tok, norm_ffn1, w_ffn1_in, w_ffn1_out, norm_mix, w_in, a_v_gain, a_spatial_w, a_spatial_b,
                  a_out_gain, b_conv_w, b_a_log, b_dt_bias, b_out_gain, w_out, norm_ffn2, w_ffn2_in, w_ffn2_out,
                  norm_ple, w_ple_gate, w_ple_proj):
    o_ab = OFF_GATE
    w_main = jnp.concatenate([w_in[:, :, :o_ab], w_in[:, :, o_ab + 2 * N_HEADS:]], axis=2).astype(BF16)
    lane_pad = jnp.zeros((DEPTH, D_MODEL, 128 - N_HEADS), F32)
    w_ab = jnp.concatenate([w_in[:, :, o_ab:o_ab + N_HEADS], lane_pad,
                            w_in[:, :, o_ab + N_HEADS:o_ab + 2 * N_HEADS], lane_pad], axis=2).astype(BF16)
    par_pad = jnp.zeros((DEPTH, 128 - N_HEADS), F32)
    ab_par = jnp.stack([jnp.concatenate([b_a_log, par_pad], axis=1),
                        jnp.concatenate([b_dt_bias, par_pad], axis=1)], axis=1)

    def sample_rows(a):
        return jnp.tile(jnp.repeat(jnp.transpose(a, (0, 2, 1)), HEAD_DIM, axis=2), (1, SAMPLE_GROUP, 1))

    ws_small = a_spatial_w[:, :, :n_tok, :n_tok]
    a_coef = jnp.stack([sample_rows(jnp.pad(jnp.diagonal(ws_small, offset=-d, axis1=2, axis2=3),
                                            ((0, 0), (0, 0), (d, 0)))) for d in range(n_tok)], axis=1)
    return dict(
        n_f1=norm_ffn1[:, None], w_f1_in=w_ffn1_in.astype(BF16), w_f1_out=w_ffn1_out.astype(BF16),
        n_mix=norm_mix[:, None], w_main=w_main, w_ab=w_ab, ab_par=ab_par,
        a_v_gain=a_v_gain[:, None], a_out_gain=a_out_gain[:, None], b_out_gain=b_out_gain[:, None],
        a_w_s=a_spatial_w, a_b_s_t=jnp.transpose(a_spatial_b, (0, 2, 1)),
        a_coef=a_coef, a_bias=sample_rows(a_spatial_b[:, :, :n_tok]),
        b_conv=b_conv_w, w_out=w_out.astype(BF16),
        n_f2=norm_ffn2[:, None], w_f2_in=w_ffn2_in.astype(BF16), w_f2_out=w_ffn2_out.astype(BF16),
        n_ple=norm_ple[:, None], w_ple_gate=w_ple_gate.astype(BF16), w_ple_proj=w_ple_proj.astype(BF16),
    )


def kernel(x_prompt, x_sample, state_S, state_conv, p_prompt, p_sample, norm_ffn1, w_ffn1_in, w_ffn1_out, norm_mix, w_in, a_v_gain, a_spatial_w, a_spatial_b, a_out_gain, b_conv_w, b_a_log, b_dt_bias, b_out_gain, w_out, norm_ffn2, w_ffn2_in, w_ffn2_out, norm_ple, w_ple_gate, w_ple_proj, final_norm):
    bsz, length, _ = x_prompt.shape
    dec_bsz, n_tok, _ = x_sample.shape
    assert length % MIX_ROWS == 0 and MIX_ROWS % CHUNK_A == 0 and MIX_ROWS % GROUP == 0
    assert dec_bsz % SAMPLE_GROUP == 0
    assert n_tok % CHUNK_A != 0 and n_tok % CHUNK_D != 0
    assert 2 * n_tok == 8 and N_HEADS * SAMPLE_GROUP * n_tok == HEAD_DIM and n_tok >= CONV_W - 1

    w = _prep_weights(n_tok, norm_ffn1, w_ffn1_in, w_ffn1_out, norm_mix, w_in, a_v_gain, a_spatial_w, a_spatial_b,
                      a_out_gain, b_conv_w, b_a_log, b_dt_bias, b_out_gain, w_out, norm_ffn2, w_ffn2_in,
                      w_ffn2_out, norm_ple, w_ple_gate, w_ple_proj)
    final = final_norm[None, None]
    xp = x_prompt.reshape(bsz * length, D_MODEL)
    xs = x_sample.reshape(dec_bsz * n_tok, D_MODEL)
    pp = p_prompt.reshape(DEPTH, bsz * length, PLE_DIM)
    ps = p_sample.reshape(DEPTH, dec_bsz * n_tok, PLE_DIM)
    keep = CONV_W - 1
    cpad = jnp.pad(state_conv, ((0, 0), (0, 0), (n_tok - keep, 0), (0, 0))).reshape(DEPTH, dec_bsz * n_tok, 3 * B_WIDTH)

    s_prompt, c_prompt, s_sample, c_sample, v_sample = [], [], [], [], []
    for i in range(DEPTH):
        last = dict(final_gain=final) if i == DEPTH - 1 else {}
        ple = (w["n_ple"], w["w_ple_gate"], w["w_ple_proj"])

        xp = _ffn(xp, i, w["n_f1"], w["w_f1_in"], w["w_f1_out"])
        xp, sp, cp = _mix_prompt(xp.reshape(bsz, length, D_MODEL), i, w)
        xp = _ffn(xp.reshape(bsz * length, D_MODEL), i, w["n_f2"], w["w_f2_in"], w["w_f2_out"], ple=(pp,) + ple, **last)

        xs = _ffn(xs, i, w["n_f1"], w["w_f1_in"], w["w_f1_out"])
        xs, ss, zq, vs = _mix_sample(xs, i, state_S, cpad, w, n_tok)
        xs = _ffn(xs, i, w["n_f2"], w["w_f2_in"], w["w_f2_out"], ple=(ps,) + ple, **last)

        s_prompt.append(sp)
        c_prompt.append(cp)
        s_sample.append(ss)
        c_sample.append(zq.reshape(dec_bsz, n_tok, 3 * B_WIDTH)[:, n_tok - keep:])
        v_sample.append(vs.reshape(dec_bsz, n_tok, N_HEADS, HEAD_DIM))

    return (xp.reshape(bsz, length, D_MODEL), xs.reshape(dec_bsz, n_tok, D_MODEL), jnp.stack(s_prompt),
            jnp.stack(c_prompt), jnp.stack(s_sample), jnp.stack(c_sample), jnp.stack(v_sample))
```

```python
import functools

import jax
import jax.numpy as jnp
from jax import lax
from jax.experimental import pallas as pl
from jax.experimental.pallas import tpu as pltpu

F32 = jnp.float32
BF16 = jnp.bfloat16
EPS = 1e-6

D_MODEL = 1024
D_FF = 2816
DEPTH = 4
N_HEADS = 4
HEAD_DIM = 128
A_WIDTH = N_HEADS * HEAD_DIM
B_WIDTH = N_HEADS * HEAD_DIM
CHUNK_A = 128
CHUNK_D = 64
CONV_W = 4
PLE_DIM = 256
Z_MAIN = 2 * A_WIDTH + 4 * B_WIDTH
OFF_QKV = 2 * A_WIDTH
OFF_GATE = OFF_QKV + 3 * B_WIDTH

VMEM_LIMIT_BYTES = 52 * 1024 * 1024
MXU_N = 256
FFN_ROWS = 512
GROUP = 4 * CHUNK_D
MIX_ROWS = 512
SAMPLE_GROUP = 8


def _rms(x, gain):
    return x * lax.rsqrt(jnp.mean(x * x, axis=-1, keepdims=True) + EPS) * gain


def _l2(x):
    return x * lax.rsqrt(jnp.sum(x * x, axis=-1, keepdims=True) + EPS)


def _silu(x):
    return x * jax.nn.sigmoid(x)


def _softplus(x):
    return jnp.maximum(x, 0.0) + jnp.log1p(jnp.exp(-jnp.abs(x)))


def _dot(a, b):
    return jnp.dot(a.astype(BF16), b.astype(BF16), preferred_element_type=F32)


def _dot_nt(a, b):
    return lax.dot_general(a.astype(BF16), b.astype(BF16), (((1,), (1,)), ((), ())),
                           preferred_element_type=F32)


def _split3(a):
    p1 = a.astype(BF16)
    r1 = a - p1.astype(F32)
    p2 = r1.astype(BF16)
    p3 = (r1 - p2.astype(F32)).astype(BF16)
    return p1, p2, p3


def _const_spec(shape):
    zeros = (0,) * len(shape)
    return pl.BlockSpec(shape, lambda *_: zeros, pipeline_mode=pl.Buffered(1))


def _layer_spec(shape, layer, block=None):
    index = (layer,) + (0,) * (len(shape) - 1) + (0 if block is None else block,)
    return pl.BlockSpec((None,) + tuple(shape), lambda *_: index, pipeline_mode=pl.Buffered(1))


def _ffn_kernel(*refs, with_ple, with_final):
    x_ref, gain_ref, wg_ref, wu_ref, wo_ref = refs[:5]
    o_ref = refs[-1]
    x = x_ref[...]
    xn = _rms(x, gain_ref[...]).astype(BF16)
    acc = jnp.zeros_like(x)
    for c in range(D_FF // MXU_N):
        sl = slice(c * MXU_N, (c + 1) * MXU_N)
        gate = jnp.dot(xn, wg_ref[:, sl], preferred_element_type=F32)
        up = jnp.dot(xn, wu_ref[:, sl], preferred_element_type=F32)
        h = (_silu(gate) * up).astype(BF16)
        acc = acc + jnp.dot(h, wo_ref[sl, :], preferred_element_type=F32)
    x = x + 0.5 * acc
    if with_ple:
        p_ref, npl_ref, wpg_ref, wpp_ref = refs[5:9]
        emb = _dot(p_ref[...], wpp_ref[...])
        gate = _dot(_rms(x, npl_ref[...]), wpg_ref[...])
        x = x + emb * jax.nn.sigmoid(gate)
    if with_final:
        x = _rms(x, refs[9][...])
    o_ref[...] = x


def _ffn(x, layer, gain, w_in, w_out, ple=None, final_gain=None):
    rows = x.shape[0]
    tm = min(FFN_ROWS, rows)
    row_spec = pl.BlockSpec((tm, D_MODEL), lambda i: (i, 0))
    in_specs = [row_spec, _layer_spec((1, D_MODEL), layer),
                _layer_spec((D_MODEL, D_FF), layer, block=0), _layer_spec((D_MODEL, D_FF), layer, block=1),
                _layer_spec((D_FF, D_MODEL), layer)]
    args = [x, gain, w_in, w_in, w_out]
    if ple is not None:
        p, n_ple, w_gate, w_proj = ple
        in_specs += [pl.BlockSpec((None, tm, PLE_DIM), lambda i: (layer, i, 0)), _layer_spec((1, D_MODEL), layer),
                     _layer_spec((D_MODEL, D_MODEL), layer), _layer_spec((PLE_DIM, D_MODEL), layer)]
        args += [p, n_ple, w_gate, w_proj]
    if final_gain is not None:
        in_specs.append(_layer_spec((1, D_MODEL), 0))
        args.append(final_gain)
    return pl.pallas_call(
        functools.partial(_ffn_kernel, with_ple=ple is not None, with_final=final_gain is not None),
        grid=(rows // tm,),
        in_specs=in_specs,
        out_specs=row_spec,
        out_shape=jax.ShapeDtypeStruct(x.shape, F32),
        compiler_params=pltpu.CompilerParams(dimension_semantics=("arbitrary",),
                                             vmem_limit_bytes=VMEM_LIMIT_BYTES),
    )(*args)


def _split(a):
    hi = a.astype(BF16)
    lo = (a - hi.astype(F32)).astype(BF16)
    return hi, lo


def _dot3(a_hi, a_lo, b_hi, b_lo):
    m = a_hi.shape[0]
    both = jnp.dot(jnp.concatenate([a_hi, a_lo], axis=0), b_hi, preferred_element_type=F32)
    return both[:m] + both[m:] + jnp.dot(a_hi, b_lo, preferred_element_type=F32)


def _fold(block_diag):
    n = block_diag.shape[0] // CHUNK_D
    out = block_diag[0:CHUNK_D]
    for g in range(1, n):
        out = out + block_diag[g * CHUNK_D:(g + 1) * CHUNK_D]
    return out


def _expand(packed, diag_ones_ref):
    n = packed.shape[1] // CHUNK_D
    return jnp.concatenate([packed] * n, axis=0) * diag_ones_ref[...]


def _unit_lower_inverses_packed(l_packed_list, diag_ones_ref):
    c, width = l_packed_list[0].shape
    row = lax.broadcasted_iota(jnp.int32, (c, width), 0)
    col = lax.broadcasted_iota(jnp.int32, (c, width), 1) % c
    zero = jnp.zeros((), BF16)

    def lower_left(bs):
        return (row // (2 * bs) == col // (2 * bs)) & ((row // bs) % 2 == 1) & ((col // bs) % 2 == 0)

    l_bf = [l.astype(BF16) for l in l_packed_list]
    xs = [jnp.where(row == col, 1.0, 0.0) - jnp.where(lower_left(1), l, 0.0) for l in l_packed_list]
    bs = 2
    while bs < c:
        sel = lower_left(bs)
        x_bf = [x.astype(BF16) for x in xs]
        ys = [jnp.dot(jnp.where(sel, l, zero), _expand(x, diag_ones_ref), preferred_element_type=F32)
              for l, x in zip(l_bf, x_bf)]
        xs = [x - jnp.dot(xb, _expand(y.astype(BF16), diag_ones_ref), preferred_element_type=F32)
              for x, xb, y in zip(xs, x_bf, ys)]
        bs *= 2
    return xs


def _mix_prompt_kernel(x_ref, nmix_ref, wmain_ref, wab_ref, avg_ref, aog_ref, bog_ref, wsp_ref, bsp_ref,
                       cw_ref, abp_ref, wout_ref,
                       xo_ref, s_ref, ct_ref,
                       zext_ref, ob_ref, bd_ref):
    tl = x_ref.shape[0]
    step = pl.program_id(1)

    @pl.when(step == 0)
    def _():
        s_ref[...] = jnp.zeros_like(s_ref)
        zext_ref[0:8, :] = jnp.zeros((8, 3 * B_WIDTH), F32)

    x = x_ref[...]
    xn = _rms(x, nmix_ref[...]).astype(BF16)
    def in_proj(lo, hi):
        return jnp.dot(xn, wmain_ref[:, lo:hi], preferred_element_type=F32)

    zext_ref[8:8 + tl, :] = in_proj(OFF_QKV, OFF_GATE)
    zab = jnp.dot(xn, wab_ref[...], preferred_element_type=F32)
    cw = cw_ref[...]
    y = zext_ref[5:5 + tl, :] * cw[0:1]
    for j in range(1, CONV_W):
        y = y + zext_ref[5 + j:5 + j + tl, :] * cw[j:j + 1]
    tail = zext_ref[tl + 5:tl + 8, :]
    ct_ref[...] = tail
    zext_ref[5:8, :] = tail
    qkv = _silu(y)

    uv = jax.nn.gelu(in_proj(0, OFF_QKV))
    row = lax.broadcasted_iota(jnp.int32, (CHUNK_A, CHUNK_A), 0)
    col = lax.broadcasted_iota(jnp.int32, (CHUNK_A, CHUNK_A), 1)
    causal = col <= row
    for h in range(N_HEADS):
        hs = slice(h * HEAD_DIM, (h + 1) * HEAD_DIM)
        u_h = uv[:, hs]
        v_h = _rms(uv[:, A_WIDTH + h * HEAD_DIM:A_WIDTH + (h + 1) * HEAD_DIM], avg_ref[...]).astype(BF16)
        w_h = jnp.where(causal, wsp_ref[h], 0.0).astype(BF16)
        bias_h = bsp_ref[:, h:h + 1]
        for c in range(tl // CHUNK_A):
            rs = slice(c * CHUNK_A, (c + 1) * CHUNK_A)
            mixed = jnp.dot(w_h, v_h[rs], preferred_element_type=F32) + bias_h
            ob_ref[rs, hs] = _rms(u_h[rs] * mixed, aog_ref[...]).astype(BF16)

    z_gate = in_proj(OFF_GATE, Z_MAIN)
    abp = abp_ref[...]
    g = -jnp.exp(abp[0:1]) * _softplus(zab[:, :128] + abp[1:2])
    beta = jax.nn.sigmoid(zab[:, 128:])

    r2 = lax.broadcasted_iota(jnp.int32, (GROUP, GROUP), 0)
    c2 = lax.broadcasted_iota(jnp.int32, (GROUP, GROUP), 1)
    same = (r2 // CHUNK_D) == (c2 // CHUNK_D)
    strict_bd = same & (c2 < r2)
    col_ones = jnp.concatenate([jnp.where(same & (c2 <= r2), 1.0, 0.0), jnp.where(same, 1.0, 0.0)],
                               axis=0).astype(BF16)
    upper_ones = jnp.where(same & (r2 <= c2), 1.0, 0.0).astype(BF16)
    bd_ref[...] = jnp.where(same, 1.0, 0.0).astype(BF16)
    n_grp = tl // GROUP
    gam_parts, glast_parts, gam_t = [], [], []
    for gi in range(n_grp):
        g_grp = g[gi * GROUP:(gi + 1) * GROUP]
        by_col = jnp.dot(col_ones, jnp.concatenate(_split3(g_grp), axis=1), preferred_element_type=F32)
        by_col = by_col[:, :128] + by_col[:, 128:256] + by_col[:, 256:]
        gam_parts.append(by_col[:GROUP])
        glast_parts.append(by_col[GROUP:])
        by_row = jnp.dot(jnp.concatenate(_split3(g_grp.T), axis=0), upper_ones, preferred_element_type=F32)
        gam_t.append(by_row[:128] + by_row[128:256] + by_row[256:])
    gam = jnp.concatenate(gam_parts, axis=0)
    glast = jnp.concatenate(glast_parts, axis=0)

    rb = lax.broadcasted_iota(jnp.int32, (CHUNK_D, CHUNK_D), 0)
    cb = lax.broadcasted_iota(jnp.int32, (CHUNK_D, CHUNK_D), 1)
    incl = cb <= rb

    heads, a_packed, rhs = [], [], []
    for h in range(N_HEADS):
        q_h = _l2(qkv[:, h * HEAD_DIM:(h + 1) * HEAD_DIM]) * (HEAD_DIM ** -0.5)
        k_h = _l2(qkv[:, B_WIDTH + h * HEAD_DIM:B_WIDTH + (h + 1) * HEAD_DIM])
        v_h = qkv[:, 2 * B_WIDTH + h * HEAD_DIM:2 * B_WIDTH + (h + 1) * HEAD_DIM]
        gc_h = gam[:, h:h + 1]
        gl_h = glast[:, h:h + 1]
        bc_h = beta[:, h:h + 1]
        eg_h = jnp.exp(gc_h)
        for gi in range(n_grp):
            gs = slice(gi * GROUP, (gi + 1) * GROUP)
            kk = _dot_nt(k_h[gs], k_h[gs])
            decay = jnp.exp(jnp.where(strict_bd, gc_h[gs] - gam_t[gi][h:h + 1, :], 0.0))
            a_packed.append(_fold(jnp.where(strict_bd, bc_h[gs] * kk * decay, 0.0)))
        rhs.append(_split(jnp.concatenate([bc_h * v_h, (bc_h * eg_h) * k_h], axis=1)))
        heads.append((q_h * eg_h, q_h, k_h, k_h * jnp.exp(gl_h - gc_h), gc_h, jnp.exp(gl_h)))
    inv_split = [_split(inv) for inv in _unit_lower_inverses_packed(a_packed, bd_ref)]
    sol = []
    for h in range(N_HEADS):
        sol.append([_dot3(_expand(inv_split[h * n_grp + gi][0], bd_ref), _expand(inv_split[h * n_grp + gi][1], bd_ref),
                          rhs[h][0][gi * GROUP:(gi + 1) * GROUP], rhs[h][1][gi * GROUP:(gi + 1) * GROUP])
                    for gi in range(n_grp)])

    for i in range(tl // CHUNK_D):
        rs = slice(i * CHUNK_D, (i + 1) * CHUNK_D)
        gi, j = divmod(i, GROUP // CHUNK_D)
        ls = slice(j * CHUNK_D, (j + 1) * CHUNK_D)
        for h in range(N_HEADS):
            qb_h, q_h, k_h, kend_h, gc_h, btot_h = heads[h]
            decay = jnp.where(incl, jnp.exp(jnp.where(incl, gc_h[rs] - gam_t[gi][h:h + 1, ls], 0.0)), 0.0)
            qk = _dot_nt(q_h[rs], k_h[rs]) * decay
            s_old = s_ref[h]
            from_s = _dot(jnp.concatenate([sol[h][gi][ls, HEAD_DIM:], qb_h[rs]], axis=0), s_old)
            u = sol[h][gi][ls, :HEAD_DIM] - from_s[:CHUNK_D]
            from_u = _dot(jnp.concatenate([qk, kend_h[rs].T], axis=0), u)
            o = from_s[CHUNK_D:] + from_u[:CHUNK_D]
            s_ref[h] = btot_h[i * CHUNK_D:i * CHUNK_D + 1] * s_old + from_u[CHUNK_D:]
            gate = z_gate[rs, h * HEAD_DIM:(h + 1) * HEAD_DIM]
            ob_ref[rs, A_WIDTH + h * HEAD_DIM:A_WIDTH + (h + 1) * HEAD_DIM] = (
                _rms(o, bog_ref[...]) * _silu(gate)).astype(BF16)

    xo_ref[...] = x + jnp.dot(ob_ref[...], wout_ref[...], preferred_element_type=F32)


def _mixer_weight_specs(layer):
    return [_layer_spec((1, D_MODEL), layer), _layer_spec((D_MODEL, Z_MAIN), layer), _layer_spec((D_MODEL, 256), layer),
            _layer_spec((1, HEAD_DIM), layer), _layer_spec((1, HEAD_DIM), layer), _layer_spec((1, HEAD_DIM), layer)]


def _mixer_weights(w):
    return [w["n_mix"], w["w_main"], w["w_ab"], w["a_v_gain"], w["a_out_gain"], w["b_out_gain"]]


def _mix_prompt(x, layer, w):
    bsz, length, _ = x.shape
    tl = MIX_ROWS
    row_spec = pl.BlockSpec((None, tl, D_MODEL), lambda b, t: (b, t, 0))
    in_specs = [row_spec] + _mixer_weight_specs(layer) + [
        _layer_spec((N_HEADS, CHUNK_A, CHUNK_A), layer), _layer_spec((CHUNK_A, N_HEADS), layer),
        _layer_spec((CONV_W, 3 * B_WIDTH), layer), _layer_spec((2, 128), layer), _layer_spec((D_MODEL, D_MODEL), layer)]
    out_specs = [row_spec,
                 pl.BlockSpec((None, N_HEADS, HEAD_DIM, HEAD_DIM), lambda b, t: (b, 0, 0, 0)),
                 pl.BlockSpec((None, CONV_W - 1, 3 * B_WIDTH), lambda b, t: (b, 0, 0))]
    out_shape = [jax.ShapeDtypeStruct(x.shape, F32),
                 jax.ShapeDtypeStruct((bsz, N_HEADS, HEAD_DIM, HEAD_DIM), F32),
                 jax.ShapeDtypeStruct((bsz, CONV_W - 1, 3 * B_WIDTH), F32)]
    return pl.pallas_call(
        _mix_prompt_kernel,
        grid=(bsz, length // tl),
        in_specs=in_specs,
        out_specs=out_specs,
        out_shape=out_shape,
        scratch_shapes=[pltpu.VMEM((tl + 8, 3 * B_WIDTH), F32), pltpu.VMEM((tl, D_MODEL), BF16),
                        pltpu.VMEM((GROUP, GROUP), BF16)],
        compiler_params=pltpu.CompilerParams(dimension_semantics=("arbitrary", "arbitrary"),
                                             vmem_limit_bytes=VMEM_LIMIT_BYTES),
    )(x, *_mixer_weights(w), w["a_w_s"], w["a_b_s_t"], w["b_conv"], w["ab_par"], w["w_out"])


def _mix_sample_kernel(x_ref, s_ref, cpad_ref, s_all_ref, nmix_ref, wmain_ref, wab_ref, avg_ref, aog_ref, bog_ref,
                       coef_ref, bias_ref, cw_ref, abp_ref, wout_ref,
                       xo_ref, so_ref, zq_ref, vo_ref,
                       z_ref, zab_ref, ob_ref, *, n_tok):
    del s_all_ref
    rows = zq_ref.shape[0]
    nb = rows // n_tok
    step = pl.program_id(0)

    @pl.when(step == 0)
    def _():
        xn = _rms(x_ref[...], nmix_ref[...]).astype(BF16)
        z_ref[...] = jnp.dot(xn, wmain_ref[...], preferred_element_type=F32)
        zab_ref[...] = jnp.dot(xn, wab_ref[...], preferred_element_type=F32)

    here = pl.ds(pl.multiple_of(step * rows, rows), rows)
    z = z_ref[here, :]
    zab = zab_ref[here, :]
    tok = lax.broadcasted_iota(jnp.int32, (rows, 1), 0) % n_tok

    def prev(a, d):
        return pltpu.roll(a, d, axis=0)

    def prev_or_zero(a, d):
        return a if d == 0 else jnp.where(tok >= d, prev(a, d), 0.0)

    def per_head(fn, a):
        return jnp.concatenate([fn(a[:, h * HEAD_DIM:(h + 1) * HEAD_DIM]) for h in range(N_HEADS)], axis=1)

    uv = jax.nn.gelu(z[:, :2 * A_WIDTH])
    vn = per_head(lambda a: _rms(a, avg_ref[...]), uv[:, A_WIDTH:])
    vo_ref[...] = vn
    mixed = bias_ref[...]
    for d in range(n_tok):
        mixed = mixed + coef_ref[d] * prev_or_zero(vn, d)
    ob_ref[here, :A_WIDTH] = per_head(lambda a: _rms(a, aog_ref[...]), uv[:, :A_WIDTH] * mixed)

    zq = z[:, OFF_QKV:OFF_GATE]
    zq_ref[...] = zq
    cpad = cpad_ref[...]
    cw = cw_ref[...]
    y = zq * cw[CONV_W - 1:CONV_W]
    for d in range(1, CONV_W):
        carried = pltpu.roll(cpad, rows - (n_tok - d), axis=0)
        y = y + jnp.where(tok >= d, prev(zq, d), carried) * cw[CONV_W - 1 - d:CONV_W - d]
    qkv = _silu(y)

    abp = abp_ref[...]
    g_all = -jnp.exp(abp[0:1]) * _softplus(zab[:, :128] + abp[1:2])
    beta_all = jax.nn.sigmoid(zab[:, 128:])

    sub = lax.broadcasted_iota(jnp.int32, (8, 1), 0)
    first_half = sub < n_tok
    o_heads, kend_heads, u_heads, btot_heads = [], [], [], []
    for h in range(N_HEADS):
        q = _l2(qkv[:, h * HEAD_DIM:(h + 1) * HEAD_DIM]) * (HEAD_DIM ** -0.5)
        k = _l2(qkv[:, B_WIDTH + h * HEAD_DIM:B_WIDTH + (h + 1) * HEAD_DIM])
        v = qkv[:, 2 * B_WIDTH + h * HEAD_DIM:2 * B_WIDTH + (h + 1) * HEAD_DIM]
        g = jnp.broadcast_to(g_all[:, h:h + 1], (rows, HEAD_DIM))
        beta = jnp.broadcast_to(beta_all[:, h:h + 1], (rows, HEAD_DIM))
        gam = g
        for d in range(1, n_tok):
            gam = gam + prev_or_zero(g, d)
        gam_last = jnp.where(tok == n_tok - 1, gam, 0.0)
        for d in range(1, n_tok):
            gam_last = gam_last + jnp.where(tok == n_tok - 1 - d, pltpu.roll(gam, rows - d, axis=0), 0.0)
        eg = jnp.exp(gam)

        def decay_to(d, gam=gam):
            return jnp.exp(jnp.where(tok >= d, gam - prev(gam, d), 0.0))

        a_sub = [None] + [jnp.where(tok >= d, beta * jnp.sum(k * prev(k, d), axis=-1, keepdims=True) * decay_to(d),
                                    0.0) for d in range(1, n_tok)]
        def forward_substitute(rhs, a_sub=a_sub):
            sol = rhs
            for t in range(1, n_tok):
                acc = rhs
                for d in range(1, t + 1):
                    acc = acc - a_sub[d] * prev(sol, d)
                sol = jnp.where(tok == t, acc, sol)
            return sol

        w_blk = forward_substitute(beta * v)
        kb_blk = forward_substitute((beta * eg) * k)
        qb = q * eg

        kb_s, qb_s = [], []
        for p in range(rows // 8):
            kb_t, qb_t = kb_blk[8 * p:8 * p + 8], qb[8 * p:8 * p + 8]
            f0 = _dot(jnp.where(first_half, kb_t, pltpu.roll(qb_t, n_tok, axis=0)), s_ref[2 * p, h])
            f1 = _dot(jnp.where(first_half, pltpu.roll(kb_t, n_tok, axis=0), qb_t), s_ref[2 * p + 1, h])
            kb_s.append(jnp.where(first_half, f0, pltpu.roll(f1, n_tok, axis=0)))
            qb_s.append(jnp.where(first_half, pltpu.roll(f0, n_tok, axis=0), f1))
        u = w_blk - jnp.concatenate(kb_s, axis=0)
        o = jnp.concatenate(qb_s, axis=0)
        for d in range(n_tok):
            qk = jnp.where(tok >= d, jnp.sum(q * prev(k, d), axis=-1, keepdims=True) * decay_to(d), 0.0)
            o = o + qk * prev_or_zero(u, d)
        o_heads.append(o)
        kend_heads.append(k * jnp.exp(gam_last - gam))
        u_heads.append(u)
        btot_heads.append(jnp.broadcast_to(jnp.exp(gam_last), (rows, HEAD_DIM)))

    kend_t = jnp.concatenate(kend_heads, axis=0).T
    u_all = jnp.concatenate(u_heads, axis=0).astype(BF16)
    owner = lax.broadcasted_iota(jnp.int32, (1, N_HEADS * rows), 1) // n_tok
    for h in range(N_HEADS):
        for b in range(nb):
            mine = jnp.where(owner == h * nb + b, kend_t, 0.0).astype(BF16)
            last = b * n_tok + n_tok - 1
            so_ref[b, h] = btot_heads[h][last:last + 1] * s_ref[b, h] + jnp.dot(
                mine, u_all, preferred_element_type=F32)

    gate = z[:, OFF_GATE:]
    ob_ref[here, A_WIDTH:] = per_head(lambda a: _rms(a, bog_ref[...]), jnp.concatenate(o_heads, axis=1)) * _silu(gate)

    @pl.when(step == pl.num_programs(0) - 1)
    def _():
        xo_ref[...] = x_ref[...] + _dot(ob_ref[...], wout_ref[...])


def _mix_sample(x, layer, state_s, cpad, s_all, w, n_tok):
    rows_total = x.shape[0]
    nb = SAMPLE_GROUP
    rows = nb * n_tok
    all_rows = lambda width: pl.BlockSpec((rows_total, width), lambda i: (0, 0))
    row_spec = lambda width: pl.BlockSpec((rows, width), lambda i: (i, 0))
    s_spec = pl.BlockSpec((None, nb, N_HEADS, HEAD_DIM, HEAD_DIM), lambda i: (layer, i, 0, 0, 0))
    in_specs = [all_rows(D_MODEL), s_spec, pl.BlockSpec((None, rows, 3 * B_WIDTH), lambda i: (layer, i, 0)),
                pl.BlockSpec(memory_space=pl.ANY)] + _mixer_weight_specs(layer) + [
        _layer_spec((n_tok, rows, A_WIDTH), layer), _layer_spec((rows, A_WIDTH), layer),
        _layer_spec((CONV_W, 3 * B_WIDTH), layer), _layer_spec((2, 128), layer), _layer_spec((D_MODEL, D_MODEL), layer)]
    out_specs = [all_rows(D_MODEL), s_spec, row_spec(3 * B_WIDTH), row_spec(A_WIDTH)]
    out_shape = [jax.ShapeDtypeStruct(x.shape, F32), jax.ShapeDtypeStruct(state_s.shape, F32),
                 jax.ShapeDtypeStruct((rows_total, 3 * B_WIDTH), F32), jax.ShapeDtypeStruct((rows_total, A_WIDTH), F32)]
    return pl.pallas_call(
        functools.partial(_mix_sample_kernel, n_tok=n_tok),
        grid=(rows_total // rows,),
        in_specs=in_specs,
        out_specs=out_specs,
        out_shape=out_shape,
        input_output_aliases={3: 1},
        scratch_shapes=[pltpu.VMEM((rows_total, Z_MAIN), F32), pltpu.VMEM((rows_total, 256), F32),
                        pltpu.VMEM((rows_total, D_MODEL), F32)],
        compiler_params=pltpu.CompilerParams(dimension_semantics=("arbitrary",),
                                             vmem_limit_bytes=VMEM_LIMIT_BYTES),
    )(x, state_s, cpad, s_all, *_mixer_weights(w), w["a_coef"], w["a_bias"], w["b_conv"], w["ab_par"], w["w_out"])


def _prep_weights(n_tok, norm_ffn1, w_ffn1_in, w_ffn1_out, norm_mix, w_in, a_v_gain, a_spatial_w, a_spatial_b,
                  a_out_gain, b_conv_w, b_a_log, b_dt_bias, b_out_gain, w_out, norm_ffn2, w_ffn2_in, w_ffn2_out,
                  norm_ple, w_ple_gate, w_ple_proj):
    o_ab = OFF_GATE
    w_main = jnp.concatenate([w_in[:, :, :o_ab], w_in[:, :, o_ab + 2 * N_HEADS:]], axis=2).astype(BF16)
    lane_pad = jnp.zeros((DEPTH, D_MODEL, 128 - N_HEADS), F32)
    w_ab = jnp.concatenate([w_in[:, :, o_ab:o_ab + N_HEADS], lane_pad,
                            w_in[:, :, o_ab + N_HEADS:o_ab + 2 * N_HEADS], lane_pad], axis=2).astype(BF16)
    par_pad = jnp.zeros((DEPTH, 128 - N_HEADS), F32)
    ab_par = jnp.stack([jnp.concatenate([b_a_log, par_pad], axis=1),
                        jnp.concatenate([b_dt_bias, par_pad], axis=1)], axis=1)

    def sample_rows(a):
        return jnp.tile(jnp.repeat(jnp.transpose(a, (0, 2, 1)), HEAD_DIM, axis=2), (1, SAMPLE_GROUP, 1))

    ws_small = a_spatial_w[:, :, :n_tok, :n_tok]
    a_coef = jnp.stack([sample_rows(jnp.pad(jnp.diagonal(ws_small, offset=-d, axis1=2, axis2=3),
                                            ((0, 0), (0, 0), (d, 0)))) for d in range(n_tok)], axis=1)
    return dict(
        n_f1=norm_ffn1[:, None], w_f1_in=w_ffn1_in.astype(BF16), w_f1_out=w_ffn1_out.astype(BF16),
        n_mix=norm_mix[:, None], w_main=w_main, w_ab=w_ab, ab_par=ab_par,
        a_v_gain=a_v_gain[:, None], a_out_gain=a_out_gain[:, None], b_out_gain=b_out_gain[:, None],
        a_w_s=a_spatial_w, a_b_s_t=jnp.transpose(a_spatial_b, (0, 2, 1)),
        a_coef=a_coef, a_bias=sample_rows(a_spatial_b[:, :, :n_tok]),
        b_conv=b_conv_w, w_out=w_out.astype(BF16),
        n_f2=norm_ffn2[:, None], w_f2_in=w_ffn2_in.astype(BF16), w_f2_out=w_ffn2_out.astype(BF16),
        n_ple=norm_ple[:, None], w_ple_gate=w_ple_gate.astype(BF16), w_ple_proj=w_ple_proj.astype(BF16),
    )


def kernel(x_prompt, x_sample, state_S, state_conv, p_prompt, p_sample, norm_ffn1, w_ffn1_in, w_ffn1_out, norm_mix, w_in, a_v_gain, a_spatial_w, a_spatial_b, a_out_gain, b_conv_w, b_a_log, b_dt_bias, b_out_gain, w_out, norm_ffn2, w_ffn2_in, w_ffn2_out, norm_ple, w_ple_gate, w_ple_proj, final_norm):
    bsz, length, _ = x_prompt.shape
    dec_bsz, n_tok, _ = x_sample.shape
    assert length % MIX_ROWS == 0 and MIX_ROWS % CHUNK_A == 0 and MIX_ROWS % GROUP == 0
    assert dec_bsz % SAMPLE_GROUP == 0
    assert n_tok % CHUNK_A != 0 and n_tok % CHUNK_D != 0
    assert 2 * n_tok == 8 and N_HEADS * SAMPLE_GROUP * n_tok == HEAD_DIM and n_tok >= CONV_W - 1

    w = _prep_weights(n_tok, norm_ffn1, w_ffn1_in, w_ffn1_out, norm_mix, w_in, a_v_gain, a_spatial_w, a_spatial_b,
                      a_out_gain, b_conv_w, b_a_log, b_dt_bias, b_out_gain, w_out, norm_ffn2, w_ffn2_in,
                      w_ffn2_out, norm_ple, w_ple_gate, w_ple_proj)
    final = final_norm[None, None]
    xp = x_prompt.reshape(bsz * length, D_MODEL)
    xs = x_sample.reshape(dec_bsz * n_tok, D_MODEL)
    pp = p_prompt.reshape(DEPTH, bsz * length, PLE_DIM)
    ps = p_sample.reshape(DEPTH, dec_bsz * n_tok, PLE_DIM)
    keep = CONV_W - 1
    cpad = jnp.pad(state_conv, ((0, 0), (0, 0), (n_tok - keep, 0), (0, 0))).reshape(DEPTH, dec_bsz * n_tok, 3 * B_WIDTH)

    s_prompt, c_prompt, c_sample, v_sample = [], [], [], []
    s_sample = jnp.zeros(state_S.shape, F32)
    for i in range(DEPTH):
        last = dict(final_gain=final) if i == DEPTH - 1 else {}
        ple = (w["n_ple"], w["w_ple_gate"], w["w_ple_proj"])

        xp = _ffn(xp, i, w["n_f1"], w["w_f1_in"], w["w_f1_out"])
        xp, sp, cp = _mix_prompt(xp.reshape(bsz, length, D_MODEL), i, w)
        xp = _ffn(xp.reshape(bsz * length, D_MODEL), i, w["n_f2"], w["w_f2_in"], w["w_f2_out"], ple=(pp,) + ple, **last)

        xs = _ffn(xs, i, w["n_f1"], w["w_f1_in"], w["w_f1_out"])
        xs, s_sample, zq, vs = _mix_sample(xs, i, state_S, cpad, s_sample, w, n_tok)
        xs = _ffn(xs, i, w["n_f2"], w["w_f2_in"], w["w_f2_out"], ple=(ps,) + ple, **last)

        s_prompt.append(sp)
        c_prompt.append(cp)
        c_sample.append(zq.reshape(dec_bsz, n_tok, 3 * B_WIDTH)[:, n_tok - keep:])
        v_sample.append(vs.reshape(dec_bsz, n_tok, N_HEADS, HEAD_DIM))

    return (xp.reshape(bsz, length, D_MODEL), xs.reshape(dec_bsz, n_tok, D_MODEL), jnp.stack(s_prompt),
            jnp.stack(c_prompt), s_sample, jnp.stack(c_sample), jnp.stack(v_sample))
```

```python
import functools

import jax
import jax.numpy as jnp
from jax import lax
from jax.experimental import pallas as pl
from jax.experimental.pallas import tpu as pltpu

F32 = jnp.float32
BF16 = jnp.bfloat16
EPS = 1e-6

D_MODEL = 1024
D_FF = 2816
DEPTH = 4
N_HEADS = 4
HEAD_DIM = 128
A_WIDTH = N_HEADS * HEAD_DIM
B_WIDTH = N_HEADS * HEAD_DIM
CHUNK_A = 128
CHUNK_D = 64
CONV_W = 4
PLE_DIM = 256
Z_MAIN = 2 * A_WIDTH + 4 * B_WIDTH
OFF_QKV = 2 * A_WIDTH
OFF_GATE = OFF_QKV + 3 * B_WIDTH

VMEM_LIMIT_BYTES = 52 * 1024 * 1024
MXU_N = 256
FFN_ROWS = 512
GROUP = 4 * CHUNK_D
MIX_ROWS = 512
SAMPLE_GROUP = 8


def _rms(x, gain):
    return x * lax.rsqrt(jnp.mean(x * x, axis=-1, keepdims=True) + EPS) * gain


def _l2(x):
    return x * lax.rsqrt(jnp.sum(x * x, axis=-1, keepdims=True) + EPS)


def _silu(x):
    return x * jax.nn.sigmoid(x)


def _softplus(x):
    return jnp.maximum(x, 0.0) + jnp.log1p(jnp.exp(-jnp.abs(x)))


def _dot(a, b):
    return jnp.dot(a.astype(BF16), b.astype(BF16), preferred_element_type=F32)


def _dot_nt(a, b):
    return lax.dot_general(a.astype(BF16), b.astype(BF16), (((1,), (1,)), ((), ())),
                           preferred_element_type=F32)


def _split3(a):
    p1 = a.astype(BF16)
    r1 = a - p1.astype(F32)
    p2 = r1.astype(BF16)
    p3 = (r1 - p2.astype(F32)).astype(BF16)
    return p1, p2, p3


def _const_spec(shape):
    zeros = (0,) * len(shape)
    return pl.BlockSpec(shape, lambda *_: zeros, pipeline_mode=pl.Buffered(1))


def _layer_spec(shape, layer, block=None):
    index = (layer,) + (0,) * (len(shape) - 1) + (0 if block is None else block,)
    return pl.BlockSpec((None,) + tuple(shape), lambda *_: index, pipeline_mode=pl.Buffered(1))


def _ffn_kernel(*refs, with_ple, with_final):
    x_ref, gain_ref, wg_ref, wu_ref, wo_ref = refs[:5]
    o_ref = refs[-1]
    x = x_ref[...]
    xn = _rms(x, gain_ref[...]).astype(BF16)
    acc = jnp.zeros_like(x)
    for c in range(D_FF // MXU_N):
        sl = slice(c * MXU_N, (c + 1) * MXU_N)
        gate = jnp.dot(xn, wg_ref[:, sl], preferred_element_type=F32)
        up = jnp.dot(xn, wu_ref[:, sl], preferred_element_type=F32)
        h = (_silu(gate) * up).astype(BF16)
        acc = acc + jnp.dot(h, wo_ref[sl, :], preferred_element_type=F32)
    x = x + 0.5 * acc
    if with_ple:
        p_ref, npl_ref, wpg_ref, wpp_ref = refs[5:9]
        emb = _dot(p_ref[...], wpp_ref[...])
        gate = _dot(_rms(x, npl_ref[...]), wpg_ref[...])
        x = x + emb * jax.nn.sigmoid(gate)
    if with_final:
        x = _rms(x, refs[9][...])
    o_ref[...] = x


def _ffn(x, layer, gain, w_in, w_out, ple=None, final_gain=None):
    rows = x.shape[0]
    tm = min(FFN_ROWS, rows)
    row_spec = pl.BlockSpec((tm, D_MODEL), lambda i: (i, 0))
    in_specs = [row_spec, _layer_spec((1, D_MODEL), layer),
                _layer_spec((D_MODEL, D_FF), layer, block=0), _layer_spec((D_MODEL, D_FF), layer, block=1),
                _layer_spec((D_FF, D_MODEL), layer)]
    args = [x, gain, w_in, w_in, w_out]
    if ple is not None:
        p, n_ple, w_gate, w_proj = ple
        in_specs += [pl.BlockSpec((None, tm, PLE_DIM), lambda i: (layer, i, 0)), _layer_spec((1, D_MODEL), layer),
                     _layer_spec((D_MODEL, D_MODEL), layer), _layer_spec((PLE_DIM, D_MODEL), layer)]
        args += [p, n_ple, w_gate, w_proj]
    if final_gain is not None:
        in_specs.append(_layer_spec((1, D_MODEL), 0))
        args.append(final_gain)
    return pl.pallas_call(
        functools.partial(_ffn_kernel, with_ple=ple is not None, with_final=final_gain is not None),
        grid=(rows // tm,),
        in_specs=in_specs,
        out_specs=row_spec,
        out_shape=jax.ShapeDtypeStruct(x.shape, F32),
        compiler_params=pltpu.CompilerParams(dimension_semantics=("arbitrary",),
                                             vmem_limit_bytes=VMEM_LIMIT_BYTES),
    )(*args)


def _split(a):
    hi = a.astype(BF16)
    lo = (a - hi.astype(F32)).astype(BF16)
    return hi, lo


def _dot3(a_hi, a_lo, b_hi, b_lo):
    m = a_hi.shape[0]
    both = jnp.dot(jnp.concatenate([a_hi, a_lo], axis=0), b_hi, preferred_element_type=F32)
    return both[:m] + both[m:] + jnp.dot(a_hi, b_lo, preferred_element_type=F32)


def _fold(block_diag):
    n = block_diag.shape[0] // CHUNK_D
    out = block_diag[0:CHUNK_D]
    for g in range(1, n):
        out = out + block_diag[g * CHUNK_D:(g + 1) * CHUNK_D]
    return out


def _expand(packed, diag_ones_ref):
    n = packed.shape[1] // CHUNK_D
    return jnp.concatenate([packed] * n, axis=0) * diag_ones_ref[...]


def _unit_lower_inverses_packed(l_packed_list, diag_ones_ref):
    c, width = l_packed_list[0].shape
    row = lax.broadcasted_iota(jnp.int32, (c, width), 0)
    col = lax.broadcasted_iota(jnp.int32, (c, width), 1) % c
    zero = jnp.zeros((), BF16)

    def lower_left(bs):
        return (row // (2 * bs) == col // (2 * bs)) & ((row // bs) % 2 == 1) & ((col // bs) % 2 == 0)

    l_bf = [l.astype(BF16) for l in l_packed_list]
    xs = [jnp.where(row == col, 1.0, 0.0) - jnp.where(lower_left(1), l, 0.0) for l in l_packed_list]
    bs = 2
    while bs < c:
        sel = lower_left(bs)
        x_bf = [x.astype(BF16) for x in xs]
        ys = [jnp.dot(jnp.where(sel, l, zero), _expand(x, diag_ones_ref), preferred_element_type=F32)
              for l, x in zip(l_bf, x_bf)]
        xs = [x - jnp.dot(xb, _expand(y.astype(BF16), diag_ones_ref), preferred_element_type=F32)
              for x, xb, y in zip(xs, x_bf, ys)]
        bs *= 2
    return xs


def _mix_prompt_kernel(x_ref, nmix_ref, wmain_ref, wab_ref, avg_ref, aog_ref, bog_ref, wsp_ref, bsp_ref,
                       cw_ref, abp_ref, wout_ref,
                       xo_ref, s_ref, ct_ref,
                       zlast_ref, ob_ref, bd_ref):
    tl = x_ref.shape[0]
    step = pl.program_id(1)

    @pl.when(step == 0)
    def _():
        s_ref[...] = jnp.zeros_like(s_ref)
        zlast_ref[...] = jnp.zeros_like(zlast_ref)

    x = x_ref[...]
    xn = _rms(x, nmix_ref[...]).astype(BF16)
    def in_proj(lo, hi):
        return jnp.dot(xn, wmain_ref[:, lo:hi], preferred_element_type=F32)

    zab = jnp.dot(xn, wab_ref[...], preferred_element_type=F32)
    first_tile = lax.broadcasted_iota(jnp.int32, (8, 1), 0)
    qkv_parts = []
    for part in range(3):
        cols = slice(part * B_WIDTH, (part + 1) * B_WIDTH)
        zc = in_proj(OFF_QKV + part * B_WIDTH, OFF_QKV + (part + 1) * B_WIDTH)
        cw = cw_ref[:, cols]
        carried = zlast_ref[:, cols]
        y = zc * cw[CONV_W - 1:CONV_W]
        for d in range(1, CONV_W):
            rolled = pltpu.roll(zc, d, axis=0)
            top = jnp.where(first_tile < d, pltpu.roll(carried, d, axis=0), rolled[0:8])
            y = y + jnp.concatenate([top, rolled[8:]], axis=0) * cw[CONV_W - 1 - d:CONV_W - d]
        zlast_ref[:, cols] = zc[tl - 8:tl]
        ct_ref[:, cols] = zc[tl - (CONV_W - 1):tl]
        qkv_parts.append(_silu(y))

    uv = jax.nn.gelu(in_proj(0, OFF_QKV))
    row = lax.broadcasted_iota(jnp.int32, (CHUNK_A, CHUNK_A), 0)
    col = lax.broadcasted_iota(jnp.int32, (CHUNK_A, CHUNK_A), 1)
    causal = col <= row
    for h in range(N_HEADS):
        hs = slice(h * HEAD_DIM, (h + 1) * HEAD_DIM)
        u_h = uv[:, hs]
        v_h = _rms(uv[:, A_WIDTH + h * HEAD_DIM:A_WIDTH + (h + 1) * HEAD_DIM], avg_ref[...]).astype(BF16)
        w_h = jnp.where(causal, wsp_ref[h], 0.0).astype(BF16)
        bias_h = bsp_ref[:, h:h + 1]
        for c in range(tl // CHUNK_A):
            rs = slice(c * CHUNK_A, (c + 1) * CHUNK_A)
            mixed = jnp.dot(w_h, v_h[rs], preferred_element_type=F32) + bias_h
            ob_ref[rs, hs] = _rms(u_h[rs] * mixed, aog_ref[...]).astype(BF16)

    z_gate = in_proj(OFF_GATE, Z_MAIN)
    abp = abp_ref[...]
    g = -jnp.exp(abp[0:1]) * _softplus(zab[:, :128] + abp[1:2])
    beta = jax.nn.sigmoid(zab[:, 128:])

    r2 = lax.broadcasted_iota(jnp.int32, (GROUP, GROUP), 0)
    c2 = lax.broadcasted_iota(jnp.int32, (GROUP, GROUP), 1)
    same = (r2 // CHUNK_D) == (c2 // CHUNK_D)
    strict_bd = same & (c2 < r2)
    col_ones = jnp.concatenate([jnp.where(same & (c2 <= r2), 1.0, 0.0), jnp.where(same, 1.0, 0.0)],
                               axis=0).astype(BF16)
    upper_ones = jnp.where(same & (r2 <= c2), 1.0, 0.0).astype(BF16)
    bd_ref[...] = jnp.where(same, 1.0, 0.0).astype(BF16)
    n_grp = tl // GROUP
    gam_parts, glast_parts, gam_t = [], [], []
    for gi in range(n_grp):
        g_grp = g[gi * GROUP:(gi + 1) * GROUP]
        by_col = jnp.dot(col_ones, jnp.concatenate(_split3(g_grp), axis=1), preferred_element_type=F32)
        by_col = by_col[:, :128] + by_col[:, 128:256] + by_col[:, 256:]
        gam_parts.append(by_col[:GROUP])
        glast_parts.append(by_col[GROUP:])
        by_row = jnp.dot(jnp.concatenate(_split3(g_grp.T), axis=0), upper_ones, preferred_element_type=F32)
        gam_t.append(by_row[:128] + by_row[128:256] + by_row[256:])
    gam = jnp.concatenate(gam_parts, axis=0)
    glast = jnp.concatenate(glast_parts, axis=0)

    rb = lax.broadcasted_iota(jnp.int32, (CHUNK_D, CHUNK_D), 0)
    cb = lax.broadcasted_iota(jnp.int32, (CHUNK_D, CHUNK_D), 1)
    incl = cb <= rb

    heads, a_packed, rhs = [], [], []
    for h in range(N_HEADS):
        q_h = _l2(qkv_parts[0][:, h * HEAD_DIM:(h + 1) * HEAD_DIM]) * (HEAD_DIM ** -0.5)
        k_h = _l2(qkv_parts[1][:, h * HEAD_DIM:(h + 1) * HEAD_DIM])
        v_h = qkv_parts[2][:, h * HEAD_DIM:(h + 1) * HEAD_DIM]
        gc_h = gam[:, h:h + 1]
        gl_h = glast[:, h:h + 1]
        bc_h = beta[:, h:h + 1]
        eg_h = jnp.exp(gc_h)
        for gi in range(n_grp):
            gs = slice(gi * GROUP, (gi + 1) * GROUP)
            kk = _dot_nt(k_h[gs], k_h[gs])
            decay = jnp.exp(jnp.where(strict_bd, gc_h[gs] - gam_t[gi][h:h + 1, :], 0.0))
            a_packed.append(_fold(jnp.where(strict_bd, bc_h[gs] * kk * decay, 0.0)))
        rhs.append(_split(jnp.concatenate([bc_h * v_h, (bc_h * eg_h) * k_h], axis=1)))
        heads.append((q_h * eg_h, q_h, k_h, k_h * jnp.exp(gl_h - gc_h), gc_h, jnp.exp(gl_h)))
    inv_split = [_split(inv) for inv in _unit_lower_inverses_packed(a_packed, bd_ref)]
    sol = []
    for h in range(N_HEADS):
        sol.append([_dot3(_expand(inv_split[h * n_grp + gi][0], bd_ref), _expand(inv_split[h * n_grp + gi][1], bd_ref),
                          rhs[h][0][gi * GROUP:(gi + 1) * GROUP], rhs[h][1][gi * GROUP:(gi + 1) * GROUP])
                    for gi in range(n_grp)])

    for i in range(tl // CHUNK_D):
        rs = slice(i * CHUNK_D, (i + 1) * CHUNK_D)
        gi, j = divmod(i, GROUP // CHUNK_D)
        ls = slice(j * CHUNK_D, (j + 1) * CHUNK_D)
        for h in range(N_HEADS):
            qb_h, q_h, k_h, kend_h, gc_h, btot_h = heads[h]
            decay = jnp.where(incl, jnp.exp(jnp.where(incl, gc_h[rs] - gam_t[gi][h:h + 1, ls], 0.0)), 0.0)
            qk = _dot_nt(q_h[rs], k_h[rs]) * decay
            s_old = s_ref[h]
            from_s = _dot(jnp.concatenate([sol[h][gi][ls, HEAD_DIM:], qb_h[rs]], axis=0), s_old)
            u = sol[h][gi][ls, :HEAD_DIM] - from_s[:CHUNK_D]
            from_u = _dot(jnp.concatenate([qk, kend_h[rs].T], axis=0), u)
            o = from_s[CHUNK_D:] + from_u[:CHUNK_D]
            s_ref[h] = btot_h[i * CHUNK_D:i * CHUNK_D + 1] * s_old + from_u[CHUNK_D:]
            gate = z_gate[rs, h * HEAD_DIM:(h + 1) * HEAD_DIM]
            ob_ref[rs, A_WIDTH + h * HEAD_DIM:A_WIDTH + (h + 1) * HEAD_DIM] = (
                _rms(o, bog_ref[...]) * _silu(gate)).astype(BF16)

    xo_ref[...] = x + jnp.dot(ob_ref[...], wout_ref[...], preferred_element_type=F32)


def _mixer_weight_specs(layer):
    return [_layer_spec((1, D_MODEL), layer), _layer_spec((D_MODEL, Z_MAIN), layer), _layer_spec((D_MODEL, 256), layer),
            _layer_spec((1, HEAD_DIM), layer), _layer_spec((1, HEAD_DIM), layer), _layer_spec((1, HEAD_DIM), layer)]


def _mixer_weights(w):
    return [w["n_mix"], w["w_main"], w["w_ab"], w["a_v_gain"], w["a_out_gain"], w["b_out_gain"]]


def _mix_prompt(x, layer, w):
    bsz, length, _ = x.shape
    tl = MIX_ROWS
    row_spec = pl.BlockSpec((None, tl, D_MODEL), lambda b, t: (b, t, 0))
    in_specs = [row_spec] + _mixer_weight_specs(layer) + [
        _layer_spec((N_HEADS, CHUNK_A, CHUNK_A), layer), _layer_spec((CHUNK_A, N_HEADS), layer),
        _layer_spec((CONV_W, 3 * B_WIDTH), layer), _layer_spec((2, 128), layer), _layer_spec((D_MODEL, D_MODEL), layer)]
    out_specs = [row_spec,
                 pl.BlockSpec((None, N_HEADS, HEAD_DIM, HEAD_DIM), lambda b, t: (b, 0, 0, 0)),
                 pl.BlockSpec((None, CONV_W - 1, 3 * B_WIDTH), lambda b, t: (b, 0, 0))]
    out_shape = [jax.ShapeDtypeStruct(x.shape, F32),
                 jax.ShapeDtypeStruct((bsz, N_HEADS, HEAD_DIM, HEAD_DIM), F32),
                 jax.ShapeDtypeStruct((bsz, CONV_W - 1, 3 * B_WIDTH), F32)]
    return pl.pallas_call(
        _mix_prompt_kernel,
        grid=(bsz, length // tl),
        in_specs=in_specs,
        out_specs=out_specs,
        out_shape=out_shape,
        scratch_shapes=[pltpu.VMEM((8, 3 * B_WIDTH), F32), pltpu.VMEM((tl, D_MODEL), BF16),
                        pltpu.VMEM((GROUP, GROUP), BF16)],
        compiler_params=pltpu.CompilerParams(dimension_semantics=("arbitrary", "arbitrary"),
                                             vmem_limit_bytes=VMEM_LIMIT_BYTES),
    )(x, *_mixer_weights(w), w["a_w_s"], w["a_b_s_t"], w["b_conv"], w["ab_par"], w["w_out"])


def _mix_sample_kernel(x_ref, s_ref, cpad_ref, s_all_ref, nmix_ref, wmain_ref, wab_ref, avg_ref, aog_ref, bog_ref,
                       coef_ref, bias_ref, cw_ref, abp_ref, wout_ref,
                       xo_ref, so_ref, zq_ref, vo_ref,
                       z_ref, zab_ref, ob_ref, *, n_tok):
    del s_all_ref
    rows = zq_ref.shape[0]
    nb = rows // n_tok
    step = pl.program_id(0)

    @pl.when(step == 0)
    def _():
        xn = _rms(x_ref[...], nmix_ref[...]).astype(BF16)
        z_ref[...] = jnp.dot(xn, wmain_ref[...], preferred_element_type=F32)
        zab_ref[...] = jnp.dot(xn, wab_ref[...], preferred_element_type=F32)

    here = pl.ds(pl.multiple_of(step * rows, rows), rows)
    z = z_ref[here, :]
    zab = zab_ref[here, :]
    tok = lax.broadcasted_iota(jnp.int32, (rows, 1), 0) % n_tok

    def prev(a, d):
        return pltpu.roll(a, d, axis=0)

    def prev_or_zero(a, d):
        return a if d == 0 else jnp.where(tok >= d, prev(a, d), 0.0)

    def per_head(fn, a):
        return jnp.concatenate([fn(a[:, h * HEAD_DIM:(h + 1) * HEAD_DIM]) for h in range(N_HEADS)], axis=1)

    uv = jax.nn.gelu(z[:, :2 * A_WIDTH])
    vn = per_head(lambda a: _rms(a, avg_ref[...]), uv[:, A_WIDTH:])
    vo_ref[...] = vn
    mixed = bias_ref[...]
    for d in range(n_tok):
        mixed = mixed + coef_ref[d] * prev_or_zero(vn, d)
    ob_ref[here, :A_WIDTH] = per_head(lambda a: _rms(a, aog_ref[...]), uv[:, :A_WIDTH] * mixed)

    zq = z[:, OFF_QKV:OFF_GATE]
    zq_ref[...] = zq
    cpad = cpad_ref[...]
    cw = cw_ref[...]
    y = zq * cw[CONV_W - 1:CONV_W]
    for d in range(1, CONV_W):
        carried = pltpu.roll(cpad, rows - (n_tok - d), axis=0)
        y = y + jnp.where(tok >= d, prev(zq, d), carried) * cw[CONV_W - 1 - d:CONV_W - d]
    qkv = _silu(y)

    abp = abp_ref[...]
    g_all = -jnp.exp(abp[0:1]) * _softplus(zab[:, :128] + abp[1:2])
    beta_all = jax.nn.sigmoid(zab[:, 128:])

    sub = lax.broadcasted_iota(jnp.int32, (8, 1), 0)
    first_half = sub < n_tok
    o_heads, kend_heads, u_heads, btot_heads = [], [], [], []
    for h in range(N_HEADS):
        q = _l2(qkv[:, h * HEAD_DIM:(h + 1) * HEAD_DIM]) * (HEAD_DIM ** -0.5)
        k = _l2(qkv[:, B_WIDTH + h * HEAD_DIM:B_WIDTH + (h + 1) * HEAD_DIM])
        v = qkv[:, 2 * B_WIDTH + h * HEAD_DIM:2 * B_WIDTH + (h + 1) * HEAD_DIM]
        g = jnp.broadcast_to(g_all[:, h:h + 1], (rows, HEAD_DIM))
        beta = jnp.broadcast_to(beta_all[:, h:h + 1], (rows, HEAD_DIM))
        gam = g
        for d in range(1, n_tok):
            gam = gam + prev_or_zero(g, d)
        gam_last = jnp.where(tok == n_tok - 1, gam, 0.0)
        for d in range(1, n_tok):
            gam_last = gam_last + jnp.where(tok == n_tok - 1 - d, pltpu.roll(gam, rows - d, axis=0), 0.0)
        eg = jnp.exp(gam)

        def decay_to(d, gam=gam):
            return jnp.exp(jnp.where(tok >= d, gam - prev(gam, d), 0.0))

        a_sub = [None] + [jnp.where(tok >= d, beta * jnp.sum(k * prev(k, d), axis=-1, keepdims=True) * decay_to(d),
                                    0.0) for d in range(1, n_tok)]
        def forward_substitute(rhs, a_sub=a_sub):
            sol = rhs
            for t in range(1, n_tok):
                acc = rhs
                for d in range(1, t + 1):
                    acc = acc - a_sub[d] * prev(sol, d)
                sol = jnp.where(tok == t, acc, sol)
            return sol

        w_blk = forward_substitute(beta * v)
        kb_blk = forward_substitute((beta * eg) * k)
        qb = q * eg

        kb_s, qb_s = [], []
        for p in range(rows // 8):
            kb_t, qb_t = kb_blk[8 * p:8 * p + 8], qb[8 * p:8 * p + 8]
            f0 = _dot(jnp.where(first_half, kb_t, pltpu.roll(qb_t, n_tok, axis=0)), s_ref[2 * p, h])
            f1 = _dot(jnp.where(first_half, pltpu.roll(kb_t, n_tok, axis=0), qb_t), s_ref[2 * p + 1, h])
            kb_s.append(jnp.where(first_half, f0, pltpu.roll(f1, n_tok, axis=0)))
            qb_s.append(jnp.where(first_half, pltpu.roll(f0, n_tok, axis=0), f1))
        u = w_blk - jnp.concatenate(kb_s, axis=0)
        o = jnp.concatenate(qb_s, axis=0)
        for d in range(n_tok):
            qk = jnp.where(tok >= d, jnp.sum(q * prev(k, d), axis=-1, keepdims=True) * decay_to(d), 0.0)
            o = o + qk * prev_or_zero(u, d)
        o_heads.append(o)
        kend_heads.append(k * jnp.exp(gam_last - gam))
        u_heads.append(u)
        btot_heads.append(jnp.broadcast_to(jnp.exp(gam_last), (rows, HEAD_DIM)))

    kend_t = jnp.concatenate(kend_heads, axis=0).T
    u_all = jnp.concatenate(u_heads, axis=0).astype(BF16)
    owner = lax.broadcasted_iota(jnp.int32, (1, N_HEADS * rows), 1) // n_tok
    for h in range(N_HEADS):
        for b in range(nb):
            mine = jnp.where(owner == h * nb + b, kend_t, 0.0).astype(BF16)
            last = b * n_tok + n_tok - 1
            so_ref[b, h] = btot_heads[h][last:last + 1] * s_ref[b, h] + jnp.dot(
                mine, u_all, preferred_element_type=F32)

    gate = z[:, OFF_GATE:]
    ob_ref[here, A_WIDTH:] = per_head(lambda a: _rms(a, bog_ref[...]), jnp.concatenate(o_heads, axis=1)) * _silu(gate)

    @pl.when(step == pl.num_programs(0) - 1)
    def _():
        xo_ref[...] = x_ref[...] + _dot(ob_ref[...], wout_ref[...])


def _mix_sample(x, layer, state_s, cpad, s_all, w, n_tok):
    rows_total = x.shape[0]
    nb = SAMPLE_GROUP
    rows = nb * n_tok
    all_rows = lambda width: pl.BlockSpec((rows_total, width), lambda i: (0, 0))
    row_spec = lambda width: pl.BlockSpec((rows, width), lambda i: (i, 0))
    s_spec = pl.BlockSpec((None, nb, N_HEADS, HEAD_DIM, HEAD_DIM), lambda i: (layer, i, 0, 0, 0))
    in_specs = [all_rows(D_MODEL), s_spec, pl.BlockSpec((None, rows, 3 * B_WIDTH), lambda i: (layer, i, 0)),
                pl.BlockSpec(memory_space=pl.ANY)] + _mixer_weight_specs(layer) + [
        _layer_spec((n_tok, rows, A_WIDTH), layer), _layer_spec((rows, A_WIDTH), layer),
        _layer_spec((CONV_W, 3 * B_WIDTH), layer), _layer_spec((2, 128), layer), _layer_spec((D_MODEL, D_MODEL), layer)]
    out_specs = [all_rows(D_MODEL), s_spec, row_spec(3 * B_WIDTH), row_spec(A_WIDTH)]
    out_shape = [jax.ShapeDtypeStruct(x.shape, F32), jax.ShapeDtypeStruct(state_s.shape, F32),
                 jax.ShapeDtypeStruct((rows_total, 3 * B_WIDTH), F32), jax.ShapeDtypeStruct((rows_total, A_WIDTH), F32)]
    return pl.pallas_call(
        functools.partial(_mix_sample_kernel, n_tok=n_tok),
        grid=(rows_total // rows,),
        in_specs=in_specs,
        out_specs=out_specs,
        out_shape=out_shape,
        input_output_aliases={3: 1},
        scratch_shapes=[pltpu.VMEM((rows_total, Z_MAIN), F32), pltpu.VMEM((rows_total, 256), F32),
                        pltpu.VMEM((rows_total, D_MODEL), F32)],
        compiler_params=pltpu.CompilerParams(dimension_semantics=("arbitrary",),
                                             vmem_limit_bytes=VMEM_LIMIT_BYTES),
    )(x, state_s, cpad, s_all, *_mixer_weights(w), w["a_coef"], w["a_bias"], w["b_conv"], w["ab_par"], w["w_out"])


def _prep_weights(n_tok, norm_ffn1, w_ffn1_in, w_ffn1_out, norm_mix, w_in, a_v_gain, a_spatial_w, a_spatial_b,
                  a_out_gain, b_conv_w, b_a_log, b_dt_bias, b_out_gain, w_out, norm_ffn2, w_ffn2_in, w_ffn2_out,
                  norm_ple, w_ple_gate, w_ple_proj):
    o_ab = OFF_GATE
    w_main = jnp.concatenate([w_in[:, :, :o_ab], w_in[:, :, o_ab + 2 * N_HEADS:]], axis=2).astype(BF16)
    lane_pad = jnp.zeros((DEPTH, D_MODEL, 128 - N_HEADS), F32)
    w_ab = jnp.concatenate([w_in[:, :, o_ab:o_ab + N_HEADS], lane_pad,
                            w_in[:, :, o_ab + N_HEADS:o_ab + 2 * N_HEADS], lane_pad], axis=2).astype(BF16)
    par_pad = jnp.zeros((DEPTH, 128 - N_HEADS), F32)
    ab_par = jnp.stack([jnp.concatenate([b_a_log, par_pad], axis=1),
                        jnp.concatenate([b_dt_bias, par_pad], axis=1)], axis=1)

    def sample_rows(a):
        return jnp.tile(jnp.repeat(jnp.transpose(a, (0, 2, 1)), HEAD_DIM, axis=2), (1, SAMPLE_GROUP, 1))

    ws_small = a_spatial_w[:, :, :n_tok, :n_tok]
    a_coef = jnp.stack([sample_rows(jnp.pad(jnp.diagonal(ws_small, offset=-d, axis1=2, axis2=3),
                                            ((0, 0), (0, 0), (d, 0)))) for d in range(n_tok)], axis=1)
    return dict(
        n_f1=norm_ffn1[:, None], w_f1_in=w_ffn1_in.astype(BF16), w_f1_out=w_ffn1_out.astype(BF16),
        n_mix=norm_mix[:, None], w_main=w_main, w_ab=w_ab, ab_par=ab_par,
        a_v_gain=a_v_gain[:, None], a_out_gain=a_out_gain[:, None], b_out_gain=b_out_gain[:, None],
        a_w_s=a_spatial_w, a_b_s_t=jnp.transpose(a_spatial_b, (0, 2, 1)),
        a_coef=a_coef, a_bias=sample_rows(a_spatial_b[:, :, :n_tok]),
        b_conv=b_conv_w, w_out=w_out.astype(BF16),
        n_f2=norm_ffn2[:, None], w_f2_in=w_ffn2_in.astype(BF16), w_f2_out=w_ffn2_out.astype(BF16),
        n_ple=norm_ple[:, None], w_ple_gate=w_ple_gate.astype(BF16), w_ple_proj=w_ple_proj.astype(BF16),
    )


def kernel(x_prompt, x_sample, state_S, state_conv, p_prompt, p_sample, norm_ffn1, w_ffn1_in, w_ffn1_out, norm_mix, w_in, a_v_gain, a_spatial_w, a_spatial_b, a_out_gain, b_conv_w, b_a_log, b_dt_bias, b_out_gain, w_out, norm_ffn2, w_ffn2_in, w_ffn2_out, norm_ple, w_ple_gate, w_ple_proj, final_norm):
    bsz, length, _ = x_prompt.shape
    dec_bsz, n_tok, _ = x_sample.shape
    assert length % MIX_ROWS == 0 and MIX_ROWS % CHUNK_A == 0 and MIX_ROWS % GROUP == 0
    assert dec_bsz % SAMPLE_GROUP == 0
    assert n_tok % CHUNK_A != 0 and n_tok % CHUNK_D != 0
    assert 2 * n_tok == 8 and N_HEADS * SAMPLE_GROUP * n_tok == HEAD_DIM and n_tok >= CONV_W - 1

    w = _prep_weights(n_tok, norm_ffn1, w_ffn1_in, w_ffn1_out, norm_mix, w_in, a_v_gain, a_spatial_w, a_spatial_b,
                      a_out_gain, b_conv_w, b_a_log, b_dt_bias, b_out_gain, w_out, norm_ffn2, w_ffn2_in,
                      w_ffn2_out, norm_ple, w_ple_gate, w_ple_proj)
    final = final_norm[None, None]
    xp = x_prompt.reshape(bsz * length, D_MODEL)
    xs = x_sample.reshape(dec_bsz * n_tok, D_MODEL)
    pp = p_prompt.reshape(DEPTH, bsz * length, PLE_DIM)
    ps = p_sample.reshape(DEPTH, dec_bsz * n_tok, PLE_DIM)
    keep = CONV_W - 1
    cpad = jnp.pad(state_conv, ((0, 0), (0, 0), (n_tok - keep, 0), (0, 0))).reshape(DEPTH, dec_bsz * n_tok, 3 * B_WIDTH)

    s_prompt, c_prompt, c_sample, v_sample = [], [], [], []
    s_sample = jnp.zeros(state_S.shape, F32)
    for i in range(DEPTH):
        last = dict(final_gain=final) if i == DEPTH - 1 else {}
        ple = (w["n_ple"], w["w_ple_gate"], w["w_ple_proj"])

        xp = _ffn(xp, i, w["n_f1"], w["w_f1_in"], w["w_f1_out"])
        xp, sp, cp = _mix_prompt(xp.reshape(bsz, length, D_MODEL), i, w)
        xp = _ffn(xp.reshape(bsz * length, D_MODEL), i, w["n_f2"], w["w_f2_in"], w["w_f2_out"], ple=(pp,) + ple, **last)

        xs = _ffn(xs, i, w["n_f1"], w["w_f1_in"], w["w_f1_out"])
        xs, s_sample, zq, vs = _mix_sample(xs, i, state_S, cpad, s_sample, w, n_tok)
        xs = _ffn(xs, i, w["n_f2"], w["w_f2_in"], w["w_f2_out"], ple=(ps,) + ple, **last)

        s_prompt.append(sp)
        c_prompt.append(cp)
        c_sample.append(zq.reshape(dec_bsz, n_tok, 3 * B_WIDTH)[:, n_tok - keep:])
        v_sample.append(vs.reshape(dec_bsz, n_tok, N_HEADS, HEAD_DIM))

    return (xp.reshape(bsz, length, D_MODEL), xs.reshape(dec_bsz, n_tok, D_MODEL), jnp.stack(s_prompt),
            jnp.stack(c_prompt), s_sample, jnp.stack(c_sample), jnp.stack(v_sample))
```

```python
import functools

import jax
import jax.numpy as jnp
from jax import lax
from jax.experimental import pallas as pl
from jax.experimental.pallas import tpu as pltpu

F32 = jnp.float32
BF16 = jnp.bfloat16
EPS = 1e-6

D_MODEL = 1024
D_FF = 2816
DEPTH = 4
N_HEADS = 4
HEAD_DIM = 128
A_WIDTH = N_HEADS * HEAD_DIM
B_WIDTH = N_HEADS * HEAD_DIM
CHUNK_A = 128
CHUNK_D = 64
CONV_W = 4
PLE_DIM = 256
Z_MAIN = 2 * A_WIDTH + 4 * B_WIDTH
OFF_QKV = 2 * A_WIDTH
OFF_GATE = OFF_QKV + 3 * B_WIDTH

VMEM_LIMIT_BYTES = 52 * 1024 * 1024
LANES = 128
BF16_SUBLANES = 16
AB_WIDTH = 2 * LANES
MXU_N = 256
FFN_ROWS = 512
GROUP = 4 * CHUNK_D
MIX_ROWS = 512
SAMPLE_GROUP = 8


def _rms(x, gain):
    return x * lax.rsqrt(jnp.mean(x * x, axis=-1, keepdims=True) + EPS) * gain


def _l2(x):
    return x * lax.rsqrt(jnp.sum(x * x, axis=-1, keepdims=True) + EPS)


def _silu(x):
    return x * jax.nn.sigmoid(x)


def _softplus(x):
    return jnp.maximum(x, 0.0) + jnp.log1p(jnp.exp(-jnp.abs(x)))


def _dot(a, b):
    return jnp.dot(a.astype(BF16), b.astype(BF16), preferred_element_type=F32)


def _dot_nt(a, b):
    return lax.dot_general(a.astype(BF16), b.astype(BF16), (((1,), (1,)), ((), ())),
                           preferred_element_type=F32)


def _split3(a):
    p1 = a.astype(BF16)
    r1 = a - p1.astype(F32)
    p2 = r1.astype(BF16)
    p3 = (r1 - p2.astype(F32)).astype(BF16)
    return p1, p2, p3


def _layer_spec(shape, layer, block=None):
    index = (layer,) + (0,) * (len(shape) - 1) + (0 if block is None else block,)
    return pl.BlockSpec((None,) + tuple(shape), lambda *_: index, pipeline_mode=pl.Buffered(1))


def _ffn_kernel(*refs, with_ple, with_final, with_cast):
    x_ref, gain_ref, wg_ref, wu_ref, wo_ref = refs[:5]
    o_ref = refs[-3] if with_cast else refs[-1]
    if with_cast:
        refs[-2][...] = refs[-5][...].astype(BF16)
        refs[-1][...] = refs[-4][...].astype(BF16)
    x = x_ref[...]
    xn = _rms(x, gain_ref[...]).astype(BF16)
    acc = jnp.zeros_like(x)
    for c in range(D_FF // MXU_N):
        sl = slice(c * MXU_N, (c + 1) * MXU_N)
        gate = jnp.dot(xn, wg_ref[:, sl], preferred_element_type=F32)
        up = jnp.dot(xn, wu_ref[:, sl], preferred_element_type=F32)
        h = (_silu(gate) * up).astype(BF16)
        acc = acc + jnp.dot(h, wo_ref[sl, :], preferred_element_type=F32)
    x = x + 0.5 * acc
    if with_ple:
        p_ref, npl_ref, wpg_ref, wpp_ref = refs[5:9]
        emb = _dot(p_ref[...], wpp_ref[...])
        gate = _dot(_rms(x, npl_ref[...]), wpg_ref[...])
        x = x + emb * jax.nn.sigmoid(gate)
    if with_final:
        x = _rms(x, refs[9][...])
    o_ref[...] = x


def _ffn(x, layer, gain, w_in, w_out, ple=None, final_gain=None, cast=None):
    rows = x.shape[0]
    tm = min(FFN_ROWS, rows)
    n_steps = rows // tm
    row_spec = pl.BlockSpec((tm, D_MODEL), lambda i: (i, 0))
    in_specs = [row_spec, _layer_spec((1, D_MODEL), layer),
                _layer_spec((D_MODEL, D_FF), 0, block=0), _layer_spec((D_MODEL, D_FF), 0, block=1),
                _layer_spec((D_FF, D_MODEL), 0)]
    args = [x, gain, w_in, w_in, w_out]
    out_specs = [row_spec]
    out_shape = [jax.ShapeDtypeStruct(x.shape, F32)]
    if ple is not None:
        p, n_ple, w_gate, w_proj = ple
        in_specs += [pl.BlockSpec((None, tm, PLE_DIM), lambda i: (layer, i, 0)), _layer_spec((1, D_MODEL), layer),
                     _layer_spec((D_MODEL, D_MODEL), layer), _layer_spec((PLE_DIM, D_MODEL), layer)]
        args += [p, n_ple, w_gate, w_proj]
    if final_gain is not None:
        in_specs.append(_layer_spec((1, D_MODEL), 0))
        args.append(final_gain)
    if cast is not None:
        w_in_f32, w_out_f32, layer_c = cast
        in_rows = D_MODEL // n_steps
        out_rows = 2 * D_FF // n_steps
        assert D_MODEL % n_steps == 0 and in_rows % BF16_SUBLANES == 0
        assert (2 * D_FF) % n_steps == 0 and out_rows % BF16_SUBLANES == 0
        in_specs += [pl.BlockSpec((None, in_rows, 2 * D_FF), lambda i: (layer_c, i, 0)),
                     pl.BlockSpec((None, out_rows, D_MODEL), lambda i: (layer_c, i // 2, 0))]
        args += [w_in_f32, w_out_f32]
        out_specs += [pl.BlockSpec((None, in_rows, 2 * D_FF), lambda i: (0, i, 0)),
                      pl.BlockSpec((None, out_rows, D_MODEL), lambda i: (0, i // 2, 0))]
        out_shape += [jax.ShapeDtypeStruct((1, D_MODEL, 2 * D_FF), BF16), jax.ShapeDtypeStruct((1, D_FF, D_MODEL), BF16)]
    out = pl.pallas_call(
        functools.partial(_ffn_kernel, with_ple=ple is not None, with_final=final_gain is not None,
                          with_cast=cast is not None),
        grid=(n_steps,),
        in_specs=in_specs,
        out_specs=out_specs,
        out_shape=out_shape,
        compiler_params=pltpu.CompilerParams(dimension_semantics=("arbitrary",),
                                             vmem_limit_bytes=VMEM_LIMIT_BYTES),
    )(*args)
    return out if cast is not None else out[0]


def _split(a):
    hi = a.astype(BF16)
    lo = (a - hi.astype(F32)).astype(BF16)
    return hi, lo


def _dot3(a_hi, a_lo, b_hi, b_lo):
    m = a_hi.shape[0]
    both = jnp.dot(jnp.concatenate([a_hi, a_lo], axis=0), b_hi, preferred_element_type=F32)
    return both[:m] + both[m:] + jnp.dot(a_hi, b_lo, preferred_element_type=F32)


def _fold(block_diag):
    n = block_diag.shape[0] // CHUNK_D
    out = block_diag[0:CHUNK_D]
    for g in range(1, n):
        out = out + block_diag[g * CHUNK_D:(g + 1) * CHUNK_D]
    return out


def _expand(packed, diag_ones_ref):
    n = packed.shape[1] // CHUNK_D
    return jnp.concatenate([packed] * n, axis=0) * diag_ones_ref[...]


def _unit_lower_inverses_packed(l_packed_list, diag_ones_ref):
    c, width = l_packed_list[0].shape
    row = lax.broadcasted_iota(jnp.int32, (c, width), 0)
    col = lax.broadcasted_iota(jnp.int32, (c, width), 1) % c
    zero = jnp.zeros((), BF16)

    def lower_left(bs):
        return (row // (2 * bs) == col // (2 * bs)) & ((row // bs) % 2 == 1) & ((col // bs) % 2 == 0)

    l_bf = [l.astype(BF16) for l in l_packed_list]
    xs = [jnp.where(row == col, 1.0, 0.0) - jnp.where(lower_left(1), l, 0.0) for l in l_packed_list]
    bs = 2
    while bs < c:
        sel = lower_left(bs)
        x_bf = [x.astype(BF16) for x in xs]
        ys = [jnp.dot(jnp.where(sel, l, zero), _expand(x, diag_ones_ref), preferred_element_type=F32)
              for l, x in zip(l_bf, x_bf)]
        xs = [x - jnp.dot(xb, _expand(y.astype(BF16), diag_ones_ref), preferred_element_type=F32)
              for x, xb, y in zip(xs, x_bf, ys)]
        bs *= 2
    return xs


def _mix_prompt_kernel(x_ref, nmix_ref, wmain_ref, wab_ref, avg_ref, aog_ref, bog_ref, wsp_ref, bsp_ref,
                       cw_ref, abp_ref, wout_ref,
                       xo_ref, s_ref, ct_ref,
                       zlast_ref, ob_ref, bd_ref):
    tl = x_ref.shape[0]
    step = pl.program_id(1)

    @pl.when(step == 0)
    def _():
        s_ref[...] = jnp.zeros_like(s_ref)
        zlast_ref[...] = jnp.zeros_like(zlast_ref)

    x = x_ref[...]
    xn = _rms(x, nmix_ref[...]).astype(BF16)
    def in_proj(lo, hi):
        return jnp.dot(xn, wmain_ref[:, lo:hi], preferred_element_type=F32)

    zab = jnp.dot(xn, wab_ref[...], preferred_element_type=F32)
    first_tile = lax.broadcasted_iota(jnp.int32, (8, 1), 0)
    qkv_parts = []
    for part in range(3):
        cols = slice(part * B_WIDTH, (part + 1) * B_WIDTH)
        zc = in_proj(OFF_QKV + part * B_WIDTH, OFF_QKV + (part + 1) * B_WIDTH)
        cw = cw_ref[:, cols]
        carried = zlast_ref[:, cols]
        y = zc * cw[CONV_W - 1:CONV_W]
        for d in range(1, CONV_W):
            rolled = pltpu.roll(zc, d, axis=0)
            top = jnp.where(first_tile < d, pltpu.roll(carried, d, axis=0), rolled[0:8])
            y = y + jnp.concatenate([top, rolled[8:]], axis=0) * cw[CONV_W - 1 - d:CONV_W - d]
        zlast_ref[:, cols] = zc[tl - 8:tl]
        ct_ref[:, cols] = zc[tl - (CONV_W - 1):tl]
        qkv_parts.append(_silu(y))

    uv = jax.nn.gelu(in_proj(0, OFF_QKV))
    row = lax.broadcasted_iota(jnp.int32, (CHUNK_A, CHUNK_A), 0)
    col = lax.broadcasted_iota(jnp.int32, (CHUNK_A, CHUNK_A), 1)
    causal = col <= row
    for h in range(N_HEADS):
        hs = slice(h * HEAD_DIM, (h + 1) * HEAD_DIM)
        u_h = uv[:, hs]
        v_h = _rms(uv[:, A_WIDTH + h * HEAD_DIM:A_WIDTH + (h + 1) * HEAD_DIM], avg_ref[...]).astype(BF16)
        w_h = jnp.where(causal, wsp_ref[h], 0.0).astype(BF16)
        bias_h = bsp_ref[:, h:h + 1]
        for c in range(tl // CHUNK_A):
            rs = slice(c * CHUNK_A, (c + 1) * CHUNK_A)
            mixed = jnp.dot(w_h, v_h[rs], preferred_element_type=F32) + bias_h
            ob_ref[rs, hs] = _rms(u_h[rs] * mixed, aog_ref[...]).astype(BF16)

    z_gate = in_proj(OFF_GATE, Z_MAIN)
    abp = abp_ref[...]
    g = -jnp.exp(abp[0:1]) * _softplus(zab[:, :LANES] + abp[1:2])
    beta = jax.nn.sigmoid(zab[:, LANES:])

    r2 = lax.broadcasted_iota(jnp.int32, (GROUP, GROUP), 0)
    c2 = lax.broadcasted_iota(jnp.int32, (GROUP, GROUP), 1)
    same = (r2 // CHUNK_D) == (c2 // CHUNK_D)
    strict_bd = same & (c2 < r2)
    col_ones = jnp.concatenate([jnp.where(same & (c2 <= r2), 1.0, 0.0), jnp.where(same, 1.0, 0.0)],
                               axis=0).astype(BF16)
    upper_ones = jnp.where(same & (r2 <= c2), 1.0, 0.0).astype(BF16)
    bd_ref[...] = jnp.where(same, 1.0, 0.0).astype(BF16)
    n_grp = tl // GROUP
    gam_parts, glast_parts, gam_t = [], [], []
    for gi in range(n_grp):
        g_grp = g[gi * GROUP:(gi + 1) * GROUP]
        by_col = jnp.dot(col_ones, jnp.concatenate(_split3(g_grp), axis=1), preferred_element_type=F32)
        by_col = by_col[:, :LANES] + by_col[:, LANES:2 * LANES] + by_col[:, 2 * LANES:]
        gam_parts.append(by_col[:GROUP])
        glast_parts.append(by_col[GROUP:])
        by_row = jnp.dot(jnp.concatenate(_split3(g_grp.T), axis=0), upper_ones, preferred_element_type=F32)
        gam_t.append(by_row[:LANES] + by_row[LANES:2 * LANES] + by_row[2 * LANES:])
    gam = jnp.concatenate(gam_parts, axis=0)
    glast = jnp.concatenate(glast_parts, axis=0)

    rb = lax.broadcasted_iota(jnp.int32, (CHUNK_D, CHUNK_D), 0)
    cb = lax.broadcasted_iota(jnp.int32, (CHUNK_D, CHUNK_D), 1)
    incl = cb <= rb

    heads, a_packed, rhs = [], [], []
    for h in range(N_HEADS):
        q_h = _l2(qkv_parts[0][:, h * HEAD_DIM:(h + 1) * HEAD_DIM]) * (HEAD_DIM ** -0.5)
        k_h = _l2(qkv_parts[1][:, h * HEAD_DIM:(h + 1) * HEAD_DIM])
        v_h = qkv_parts[2][:, h * HEAD_DIM:(h + 1) * HEAD_DIM]
        gc_h = gam[:, h:h + 1]
        gl_h = glast[:, h:h + 1]
        bc_h = beta[:, h:h + 1]
        eg_h = jnp.exp(gc_h)
        for gi in range(n_grp):
            gs = slice(gi * GROUP, (gi + 1) * GROUP)
            kk = _dot_nt(k_h[gs], k_h[gs])
            decay = jnp.exp(jnp.where(strict_bd, gc_h[gs] - gam_t[gi][h:h + 1, :], 0.0))
            a_packed.append(_fold(jnp.where(strict_bd, bc_h[gs] * kk * decay, 0.0)))
        rhs.append(_split(jnp.concatenate([bc_h * v_h, (bc_h * eg_h) * k_h], axis=1)))
        heads.append((q_h * eg_h, q_h, k_h, k_h * jnp.exp(gl_h - gc_h), gc_h, jnp.exp(gl_h)))
    inv_split = [_split(inv) for inv in _unit_lower_inverses_packed(a_packed, bd_ref)]
    sol = []
    for h in range(N_HEADS):
        sol.append([_dot3(_expand(inv_split[h * n_grp + gi][0], bd_ref), _expand(inv_split[h * n_grp + gi][1], bd_ref),
                          rhs[h][0][gi * GROUP:(gi + 1) * GROUP], rhs[h][1][gi * GROUP:(gi + 1) * GROUP])
                    for gi in range(n_grp)])

    for i in range(tl // CHUNK_D):
        rs = slice(i * CHUNK_D, (i + 1) * CHUNK_D)
        gi, j = divmod(i, GROUP // CHUNK_D)
        ls = slice(j * CHUNK_D, (j + 1) * CHUNK_D)
        for h in range(N_HEADS):
            qb_h, q_h, k_h, kend_h, gc_h, btot_h = heads[h]
            decay = jnp.where(incl, jnp.exp(jnp.where(incl, gc_h[rs] - gam_t[gi][h:h + 1, ls], 0.0)), 0.0)
            qk = _dot_nt(q_h[rs], k_h[rs]) * decay
            s_old = s_ref[h]
            from_s = _dot(jnp.concatenate([sol[h][gi][ls, HEAD_DIM:], qb_h[rs]], axis=0), s_old)
            u = sol[h][gi][ls, :HEAD_DIM] - from_s[:CHUNK_D]
            from_u = _dot(jnp.concatenate([qk, kend_h[rs].T], axis=0), u)
            o = from_s[CHUNK_D:] + from_u[:CHUNK_D]
            s_ref[h] = btot_h[i * CHUNK_D:i * CHUNK_D + 1] * s_old + from_u[CHUNK_D:]
            gate = z_gate[rs, h * HEAD_DIM:(h + 1) * HEAD_DIM]
            ob_ref[rs, A_WIDTH + h * HEAD_DIM:A_WIDTH + (h + 1) * HEAD_DIM] = (
                _rms(o, bog_ref[...]) * _silu(gate)).astype(BF16)

    xo_ref[...] = x + jnp.dot(ob_ref[...], wout_ref[...], preferred_element_type=F32)


def _mixer_weight_specs(layer):
    return [_layer_spec((1, D_MODEL), layer), _layer_spec((D_MODEL, Z_MAIN), layer), _layer_spec((D_MODEL, AB_WIDTH), layer),
            _layer_spec((1, HEAD_DIM), layer), _layer_spec((1, HEAD_DIM), layer), _layer_spec((1, HEAD_DIM), layer)]


def _mixer_weights(w):
    return [w["n_mix"], w["w_main"], w["w_ab"], w["a_v_gain"], w["a_out_gain"], w["b_out_gain"]]


def _mix_prompt(x, layer, w):
    bsz, length, _ = x.shape
    tl = MIX_ROWS
    row_spec = pl.BlockSpec((None, tl, D_MODEL), lambda b, t: (b, t, 0))
    in_specs = [row_spec] + _mixer_weight_specs(layer) + [
        _layer_spec((N_HEADS, CHUNK_A, CHUNK_A), layer), _layer_spec((CHUNK_A, N_HEADS), layer),
        _layer_spec((CONV_W, 3 * B_WIDTH), layer), _layer_spec((2, LANES), layer), _layer_spec((D_MODEL, D_MODEL), layer)]
    out_specs = [row_spec,
                 pl.BlockSpec((None, N_HEADS, HEAD_DIM, HEAD_DIM), lambda b, t: (b, 0, 0, 0)),
                 pl.BlockSpec((None, CONV_W - 1, 3 * B_WIDTH), lambda b, t: (b, 0, 0))]
    out_shape = [jax.ShapeDtypeStruct(x.shape, F32),
                 jax.ShapeDtypeStruct((bsz, N_HEADS, HEAD_DIM, HEAD_DIM), F32),
                 jax.ShapeDtypeStruct((bsz, CONV_W - 1, 3 * B_WIDTH), F32)]
    return pl.pallas_call(
        _mix_prompt_kernel,
        grid=(bsz, length // tl),
        in_specs=in_specs,
        out_specs=out_specs,
        out_shape=out_shape,
        scratch_shapes=[pltpu.VMEM((8, 3 * B_WIDTH), F32), pltpu.VMEM((tl, D_MODEL), BF16),
                        pltpu.VMEM((GROUP, GROUP), BF16)],
        compiler_params=pltpu.CompilerParams(dimension_semantics=("arbitrary", "arbitrary"),
                                             vmem_limit_bytes=VMEM_LIMIT_BYTES),
    )(x, *_mixer_weights(w), w["a_w_s"], w["a_b_s_t"], w["b_conv"], w["ab_par"], w["w_out"])


def _mix_sample_kernel(x_ref, s_ref, cpad_ref, s_all_ref, nmix_ref, wmain_ref, wab_ref, avg_ref, aog_ref, bog_ref,
                       coef_ref, bias_ref, cw_ref, abp_ref, wout_ref,
                       xo_ref, so_ref, zq_ref, vo_ref,
                       z_ref, zab_ref, ob_ref, *, n_tok):
    del s_all_ref
    rows = zq_ref.shape[0]
    nb = rows // n_tok
    step = pl.program_id(0)

    @pl.when(step == 0)
    def _():
        xn = _rms(x_ref[...], nmix_ref[...]).astype(BF16)
        z_ref[...] = jnp.dot(xn, wmain_ref[...], preferred_element_type=F32)
        zab_ref[...] = jnp.dot(xn, wab_ref[...], preferred_element_type=F32)

    here = pl.ds(pl.multiple_of(step * rows, rows), rows)
    z = z_ref[here, :]
    zab = zab_ref[here, :]
    tok = lax.broadcasted_iota(jnp.int32, (rows, 1), 0) % n_tok

    def prev(a, d):
        return pltpu.roll(a, d, axis=0)

    def prev_or_zero(a, d):
        return a if d == 0 else jnp.where(tok >= d, prev(a, d), 0.0)

    def per_head(fn, a):
        return jnp.concatenate([fn(a[:, h * HEAD_DIM:(h + 1) * HEAD_DIM]) for h in range(N_HEADS)], axis=1)

    uv = jax.nn.gelu(z[:, :2 * A_WIDTH])
    vn = per_head(lambda a: _rms(a, avg_ref[...]), uv[:, A_WIDTH:])
    vo_ref[...] = vn
    mixed = bias_ref[...]
    for d in range(n_tok):
        mixed = mixed + coef_ref[d] * prev_or_zero(vn, d)
    ob_ref[here, :A_WIDTH] = per_head(lambda a: _rms(a, aog_ref[...]), uv[:, :A_WIDTH] * mixed)

    zq = z[:, OFF_QKV:OFF_GATE]
    zq_ref[...] = zq
    cpad = cpad_ref[...]
    cw = cw_ref[...]
    y = zq * cw[CONV_W - 1:CONV_W]
    for d in range(1, CONV_W):
        carried = pltpu.roll(cpad, rows - (n_tok - d), axis=0)
        y = y + jnp.where(tok >= d, prev(zq, d), carried) * cw[CONV_W - 1 - d:CONV_W - d]
    qkv = _silu(y)

    abp = abp_ref[...]
    g_all = -jnp.exp(abp[0:1]) * _softplus(zab[:, :LANES] + abp[1:2])
    beta_all = jax.nn.sigmoid(zab[:, LANES:])

    sub = lax.broadcasted_iota(jnp.int32, (8, 1), 0)
    first_half = sub < n_tok
    o_heads, kend_heads, u_heads, btot_heads = [], [], [], []
    for h in range(N_HEADS):
        q = _l2(qkv[:, h * HEAD_DIM:(h + 1) * HEAD_DIM]) * (HEAD_DIM ** -0.5)
        k = _l2(qkv[:, B_WIDTH + h * HEAD_DIM:B_WIDTH + (h + 1) * HEAD_DIM])
        v = qkv[:, 2 * B_WIDTH + h * HEAD_DIM:2 * B_WIDTH + (h + 1) * HEAD_DIM]
        g = jnp.broadcast_to(g_all[:, h:h + 1], (rows, HEAD_DIM))
        beta = jnp.broadcast_to(beta_all[:, h:h + 1], (rows, HEAD_DIM))
        gam = g
        for d in range(1, n_tok):
            gam = gam + prev_or_zero(g, d)
        gam_last = jnp.where(tok == n_tok - 1, gam, 0.0)
        for d in range(1, n_tok):
            gam_last = gam_last + jnp.where(tok == n_tok - 1 - d, pltpu.roll(gam, rows - d, axis=0), 0.0)
        eg = jnp.exp(gam)

        def decay_to(d, gam=gam):
            return jnp.exp(jnp.where(tok >= d, gam - prev(gam, d), 0.0))

        a_sub = [None] + [jnp.where(tok >= d, beta * jnp.sum(k * prev(k, d), axis=-1, keepdims=True) * decay_to(d),
                                    0.0) for d in range(1, n_tok)]
        def forward_substitute(rhs, a_sub=a_sub):
            sol = rhs
            for t in range(1, n_tok):
                acc = rhs
                for d in range(1, t + 1):
                    acc = acc - a_sub[d] * prev(sol, d)
                sol = jnp.where(tok == t, acc, sol)
            return sol

        w_blk = forward_substitute(beta * v)
        kb_blk = forward_substitute((beta * eg) * k)
        qb = q * eg

        kb_s, qb_s = [], []
        for p in range(rows // 8):
            kb_t, qb_t = kb_blk[8 * p:8 * p + 8], qb[8 * p:8 * p + 8]
            f0 = _dot(jnp.where(first_half, kb_t, pltpu.roll(qb_t, n_tok, axis=0)), s_ref[2 * p, h])
            f1 = _dot(jnp.where(first_half, pltpu.roll(kb_t, n_tok, axis=0), qb_t), s_ref[2 * p + 1, h])
            kb_s.append(jnp.where(first_half, f0, pltpu.roll(f1, n_tok, axis=0)))
            qb_s.append(jnp.where(first_half, pltpu.roll(f0, n_tok, axis=0), f1))
        u = w_blk - jnp.concatenate(kb_s, axis=0)
        o = jnp.concatenate(qb_s, axis=0)
        for d in range(n_tok):
            qk = jnp.where(tok >= d, jnp.sum(q * prev(k, d), axis=-1, keepdims=True) * decay_to(d), 0.0)
            o = o + qk * prev_or_zero(u, d)
        o_heads.append(o)
        kend_heads.append(k * jnp.exp(gam_last - gam))
        u_heads.append(u)
        btot_heads.append(jnp.broadcast_to(jnp.exp(gam_last), (rows, HEAD_DIM)))

    kend_t = jnp.concatenate(kend_heads, axis=0).T
    u_all = jnp.concatenate(u_heads, axis=0).astype(BF16)
    owner = lax.broadcasted_iota(jnp.int32, (1, N_HEADS * rows), 1) // n_tok
    for h in range(N_HEADS):
        for b in range(nb):
            mine = jnp.where(owner == h * nb + b, kend_t, 0.0).astype(BF16)
            last = b * n_tok + n_tok - 1
            so_ref[b, h] = btot_heads[h][last:last + 1] * s_ref[b, h] + jnp.dot(
                mine, u_all, preferred_element_type=F32)

    gate = z[:, OFF_GATE:]
    ob_ref[here, A_WIDTH:] = per_head(lambda a: _rms(a, bog_ref[...]), jnp.concatenate(o_heads, axis=1)) * _silu(gate)

    @pl.when(step == pl.num_programs(0) - 1)
    def _():
        xo_ref[...] = x_ref[...] + _dot(ob_ref[...], wout_ref[...])


def _mix_sample(x, layer, state_s, cpad, s_all, w, n_tok):
    rows_total = x.shape[0]
    nb = SAMPLE_GROUP
    rows = nb * n_tok
    all_rows = lambda width: pl.BlockSpec((rows_total, width), lambda i: (0, 0))
    row_spec = lambda width: pl.BlockSpec((rows, width), lambda i: (i, 0))
    s_spec = pl.BlockSpec((None, nb, N_HEADS, HEAD_DIM, HEAD_DIM), lambda i: (layer, i, 0, 0, 0))
    in_specs = [all_rows(D_MODEL), s_spec, pl.BlockSpec((None, rows, 3 * B_WIDTH), lambda i: (layer, i, 0)),
                pl.BlockSpec(memory_space=pl.ANY)] + _mixer_weight_specs(layer) + [
        _layer_spec((n_tok, rows, A_WIDTH), layer), _layer_spec((rows, A_WIDTH), layer),
        _layer_spec((CONV_W, 3 * B_WIDTH), layer), _layer_spec((2, LANES), layer), _layer_spec((D_MODEL, D_MODEL), layer)]
    out_specs = [all_rows(D_MODEL), s_spec, row_spec(3 * B_WIDTH), row_spec(A_WIDTH)]
    out_shape = [jax.ShapeDtypeStruct(x.shape, F32), jax.ShapeDtypeStruct(state_s.shape, F32),
                 jax.ShapeDtypeStruct((rows_total, 3 * B_WIDTH), F32), jax.ShapeDtypeStruct((rows_total, A_WIDTH), F32)]
    return pl.pallas_call(
        functools.partial(_mix_sample_kernel, n_tok=n_tok),
        grid=(rows_total // rows,),
        in_specs=in_specs,
        out_specs=out_specs,
        out_shape=out_shape,
        input_output_aliases={3: 1},
        scratch_shapes=[pltpu.VMEM((rows_total, Z_MAIN), F32), pltpu.VMEM((rows_total, AB_WIDTH), F32),
                        pltpu.VMEM((rows_total, D_MODEL), F32)],
        compiler_params=pltpu.CompilerParams(dimension_semantics=("arbitrary",),
                                             vmem_limit_bytes=VMEM_LIMIT_BYTES),
    )(x, state_s, cpad, s_all, *_mixer_weights(w), w["a_coef"], w["a_bias"], w["b_conv"], w["ab_par"], w["w_out"])


def _prep_weights(n_tok, norm_ffn1, norm_mix, w_in, a_v_gain, a_spatial_w, a_spatial_b, a_out_gain, b_conv_w,
                  b_a_log, b_dt_bias, b_out_gain, w_out, norm_ffn2, norm_ple, w_ple_gate, w_ple_proj):
    o_ab = OFF_GATE
    w_main = jnp.concatenate([w_in[:, :, :o_ab], w_in[:, :, o_ab + 2 * N_HEADS:]], axis=2).astype(BF16)
    lane_pad = jnp.zeros((DEPTH, D_MODEL, LANES - N_HEADS), F32)
    w_ab = jnp.concatenate([w_in[:, :, o_ab:o_ab + N_HEADS], lane_pad,
                            w_in[:, :, o_ab + N_HEADS:o_ab + 2 * N_HEADS], lane_pad], axis=2).astype(BF16)
    par_pad = jnp.zeros((DEPTH, LANES - N_HEADS), F32)
    ab_par = jnp.stack([jnp.concatenate([b_a_log, par_pad], axis=1),
                        jnp.concatenate([b_dt_bias, par_pad], axis=1)], axis=1)

    def sample_rows(a):
        return jnp.tile(jnp.repeat(jnp.transpose(a, (0, 2, 1)), HEAD_DIM, axis=2), (1, SAMPLE_GROUP, 1))

    ws_small = a_spatial_w[:, :, :n_tok, :n_tok]
    a_coef = jnp.stack([sample_rows(jnp.pad(jnp.diagonal(ws_small, offset=-d, axis1=2, axis2=3),
                                            ((0, 0), (0, 0), (d, 0)))) for d in range(n_tok)], axis=1)
    return dict(
        n_f1=norm_ffn1[:, None],
        n_mix=norm_mix[:, None], w_main=w_main, w_ab=w_ab, ab_par=ab_par,
        a_v_gain=a_v_gain[:, None], a_out_gain=a_out_gain[:, None], b_out_gain=b_out_gain[:, None],
        a_w_s=a_spatial_w, a_b_s_t=jnp.transpose(a_spatial_b, (0, 2, 1)),
        a_coef=a_coef, a_bias=sample_rows(a_spatial_b[:, :, :n_tok]),
        b_conv=b_conv_w, w_out=w_out.astype(BF16),
        n_f2=norm_ffn2[:, None],
        n_ple=norm_ple[:, None], w_ple_gate=w_ple_gate.astype(BF16), w_ple_proj=w_ple_proj.astype(BF16),
    )


def kernel(x_prompt, x_sample, state_S, state_conv, p_prompt, p_sample, norm_ffn1, w_ffn1_in, w_ffn1_out, norm_mix, w_in, a_v_gain, a_spatial_w, a_spatial_b, a_out_gain, b_conv_w, b_a_log, b_dt_bias, b_out_gain, w_out, norm_ffn2, w_ffn2_in, w_ffn2_out, norm_ple, w_ple_gate, w_ple_proj, final_norm):
    bsz, length, _ = x_prompt.shape
    dec_bsz, n_tok, _ = x_sample.shape
    assert length % MIX_ROWS == 0 and MIX_ROWS % CHUNK_A == 0 and MIX_ROWS % GROUP == 0
    assert dec_bsz % SAMPLE_GROUP == 0
    assert n_tok % CHUNK_A != 0 and n_tok % CHUNK_D != 0
    assert 2 * n_tok == 8 and N_HEADS * SAMPLE_GROUP * n_tok == HEAD_DIM and n_tok >= CONV_W - 1

    w = _prep_weights(n_tok, norm_ffn1, norm_mix, w_in, a_v_gain, a_spatial_w, a_spatial_b, a_out_gain, b_conv_w,
                      b_a_log, b_dt_bias, b_out_gain, w_out, norm_ffn2, norm_ple, w_ple_gate, w_ple_proj)
    ffn1_w = (w_ffn1_in[0:1].astype(BF16), w_ffn1_out[0:1].astype(BF16))
    final = final_norm[None, None]
    xp = x_prompt.reshape(bsz * length, D_MODEL)
    xs = x_sample.reshape(dec_bsz * n_tok, D_MODEL)
    pp = p_prompt.reshape(DEPTH, bsz * length, PLE_DIM)
    ps = p_sample.reshape(DEPTH, dec_bsz * n_tok, PLE_DIM)
    keep = CONV_W - 1
    cpad = jnp.pad(state_conv, ((0, 0), (0, 0), (n_tok - keep, 0), (0, 0))).reshape(DEPTH, dec_bsz * n_tok, 3 * B_WIDTH)

    s_prompt, c_prompt, c_sample, v_sample = [], [], [], []
    s_sample = jnp.zeros(state_S.shape, F32)
    for i in range(DEPTH):
        last = dict(final_gain=final) if i == DEPTH - 1 else {}
        ple = (w["n_ple"], w["w_ple_gate"], w["w_ple_proj"])

        xp, *ffn2_w = _ffn(xp, i, w["n_f1"], *ffn1_w, cast=(w_ffn2_in, w_ffn2_out, i))
        xs = _ffn(xs, i, w["n_f1"], *ffn1_w)
        xp, sp, cp = _mix_prompt(xp.reshape(bsz, length, D_MODEL), i, w)
        xs, s_sample, zq, vs = _mix_sample(xs, i, state_S, cpad, s_sample, w, n_tok)
        xp = xp.reshape(bsz * length, D_MODEL)
        if i < DEPTH - 1:
            xp, *ffn1_w = _ffn(xp, i, w["n_f2"], *ffn2_w, ple=(pp,) + ple, cast=(w_ffn1_in, w_ffn1_out, i + 1))
        else:
            xp = _ffn(xp, i, w["n_f2"], *ffn2_w, ple=(pp,) + ple, **last)
        xs = _ffn(xs, i, w["n_f2"], *ffn2_w, ple=(ps,) + ple, **last)

        s_prompt.append(sp)
        c_prompt.append(cp)
        c_sample.append(zq.reshape(dec_bsz, n_tok, 3 * B_WIDTH)[:, n_tok - keep:])
        v_sample.append(vs.reshape(dec_bsz, n_tok, N_HEADS, HEAD_DIM))

    return (xp.reshape(bsz, length, D_MODEL), xs.reshape(dec_bsz, n_tok, D_MODEL), jnp.stack(s_prompt),
            jnp.stack(c_prompt), s_sample, jnp.stack(c_sample), jnp.stack(v_sample))
```

```python
import functools

import jax
import jax.numpy as jnp
from jax import lax
from jax.experimental import pallas as pl
from jax.experimental.pallas import tpu as pltpu

F32 = jnp.float32
BF16 = jnp.bfloat16
EPS = 1e-6

D_MODEL = 1024
D_FF = 2816
DEPTH = 4
N_HEADS = 4
HEAD_DIM = 128
A_WIDTH = N_HEADS * HEAD_DIM
B_WIDTH = N_HEADS * HEAD_DIM
CHUNK_A = 128
CHUNK_D = 64
CONV_W = 4
PLE_DIM = 256
Z_FRONT = 2 * A_WIDTH + 3 * B_WIDTH
Z_TAIL = 2 * N_HEADS + B_WIDTH
OFF_QKV = 2 * A_WIDTH
OFF_GATE = OFF_QKV + 3 * B_WIDTH

VMEM_LIMIT_BYTES = 52 * 1024 * 1024
LANES = 128
BF16_SUBLANES = 16
MXU_N = 256
FFN_ROWS = 512
GROUP = 4 * CHUNK_D
MIX_ROWS = 512
SAMPLE_GROUP = 8


def _rms(x, gain):
    return x * lax.rsqrt(jnp.mean(x * x, axis=-1, keepdims=True) + EPS) * gain


def _l2(x):
    return x * lax.rsqrt(jnp.sum(x * x, axis=-1, keepdims=True) + EPS)


def _silu(x):
    return x * jax.nn.sigmoid(x)


def _softplus(x):
    return jnp.maximum(x, 0.0) + jnp.log1p(jnp.exp(-jnp.abs(x)))


def _dot(a, b):
    return jnp.dot(a.astype(BF16), b.astype(BF16), preferred_element_type=F32)


def _dot_nt(a, b):
    return lax.dot_general(a.astype(BF16), b.astype(BF16), (((1,), (1,)), ((), ())),
                           preferred_element_type=F32)


def _split3(a):
    p1 = a.astype(BF16)
    r1 = a - p1.astype(F32)
    p2 = r1.astype(BF16)
    p3 = (r1 - p2.astype(F32)).astype(BF16)
    return p1, p2, p3


def _layer_spec(shape, layer, block=None):
    index = (layer,) + (0,) * (len(shape) - 1) + (0 if block is None else block,)
    return pl.BlockSpec((None,) + tuple(shape), lambda *_: index, pipeline_mode=pl.Buffered(1))


def _ffn_kernel(*refs, with_ple, with_final, with_cast):
    x_ref, gain_ref, wg_ref, wu_ref, wo_ref = refs[:5]
    o_ref = refs[-3] if with_cast else refs[-1]
    if with_cast:
        refs[-2][...] = refs[-5][...].astype(BF16)
        refs[-1][...] = refs[-4][...].astype(BF16)
    x = x_ref[...]
    xn = _rms(x, gain_ref[...]).astype(BF16)
    acc = jnp.zeros_like(x)
    for c in range(D_FF // MXU_N):
        sl = slice(c * MXU_N, (c + 1) * MXU_N)
        gate = jnp.dot(xn, wg_ref[:, sl], preferred_element_type=F32)
        up = jnp.dot(xn, wu_ref[:, sl], preferred_element_type=F32)
        h = (_silu(gate) * up).astype(BF16)
        acc = acc + jnp.dot(h, wo_ref[sl, :], preferred_element_type=F32)
    x = x + 0.5 * acc
    if with_ple:
        p_ref, npl_ref, wpg_ref, wpp_ref = refs[5:9]
        emb = _dot(p_ref[...], wpp_ref[...])
        gate = _dot(_rms(x, npl_ref[...]), wpg_ref[...])
        x = x + emb * jax.nn.sigmoid(gate)
    if with_final:
        x = _rms(x, refs[9][...])
    o_ref[...] = x


def _ffn(x, layer, gain, w_in, w_out, ple=None, final_gain=None, cast=None):
    rows = x.shape[0]
    tm = min(FFN_ROWS, rows)
    n_steps = rows // tm
    row_spec = pl.BlockSpec((tm, D_MODEL), lambda i: (i, 0))
    in_specs = [row_spec, _layer_spec((1, D_MODEL), layer),
                _layer_spec((D_MODEL, D_FF), 0, block=0), _layer_spec((D_MODEL, D_FF), 0, block=1),
                _layer_spec((D_FF, D_MODEL), 0)]
    args = [x, gain, w_in, w_in, w_out]
    out_specs = [row_spec]
    out_shape = [jax.ShapeDtypeStruct(x.shape, F32)]
    if ple is not None:
        p, n_ple, w_gate, w_proj = ple
        in_specs += [pl.BlockSpec((None, tm, PLE_DIM), lambda i: (layer, i, 0)), _layer_spec((1, D_MODEL), layer),
                     _layer_spec((D_MODEL, D_MODEL), layer), _layer_spec((PLE_DIM, D_MODEL), layer)]
        args += [p, n_ple, w_gate, w_proj]
    if final_gain is not None:
        in_specs.append(_layer_spec((1, D_MODEL), 0))
        args.append(final_gain)
    if cast is not None:
        w_in_f32, w_out_f32, layer_c = cast
        in_rows = D_MODEL // n_steps
        out_rows = 2 * D_FF // n_steps
        assert D_MODEL % n_steps == 0 and in_rows % BF16_SUBLANES == 0
        assert (2 * D_FF) % n_steps == 0 and out_rows % BF16_SUBLANES == 0
        in_specs += [pl.BlockSpec((None, in_rows, 2 * D_FF), lambda i: (layer_c, i, 0)),
                     pl.BlockSpec((None, out_rows, D_MODEL), lambda i: (layer_c, i // 2, 0))]
        args += [w_in_f32, w_out_f32]
        out_specs += [pl.BlockSpec((None, in_rows, 2 * D_FF), lambda i: (0, i, 0)),
                      pl.BlockSpec((None, out_rows, D_MODEL), lambda i: (0, i // 2, 0))]
        out_shape += [jax.ShapeDtypeStruct((1, D_MODEL, 2 * D_FF), BF16), jax.ShapeDtypeStruct((1, D_FF, D_MODEL), BF16)]
    out = pl.pallas_call(
        functools.partial(_ffn_kernel, with_ple=ple is not None, with_final=final_gain is not None,
                          with_cast=cast is not None),
        grid=(n_steps,),
        in_specs=in_specs,
        out_specs=out_specs,
        out_shape=out_shape,
        compiler_params=pltpu.CompilerParams(dimension_semantics=("arbitrary",),
                                             vmem_limit_bytes=VMEM_LIMIT_BYTES),
    )(*args)
    return out if cast is not None else out[0]


def _split(a):
    hi = a.astype(BF16)
    lo = (a - hi.astype(F32)).astype(BF16)
    return hi, lo


def _dot3(a_hi, a_lo, b_hi, b_lo):
    m = a_hi.shape[0]
    both = jnp.dot(jnp.concatenate([a_hi, a_lo], axis=0), b_hi, preferred_element_type=F32)
    return both[:m] + both[m:] + jnp.dot(a_hi, b_lo, preferred_element_type=F32)


def _fold(block_diag):
    n = block_diag.shape[0] // CHUNK_D
    out = block_diag[0:CHUNK_D]
    for g in range(1, n):
        out = out + block_diag[g * CHUNK_D:(g + 1) * CHUNK_D]
    return out


def _expand(packed, diag_ones_ref):
    n = packed.shape[1] // CHUNK_D
    return jnp.concatenate([packed] * n, axis=0) * diag_ones_ref[...]


def _unit_lower_inverses_packed(l_packed_list, diag_ones_ref):
    c, width = l_packed_list[0].shape
    row = lax.broadcasted_iota(jnp.int32, (c, width), 0)
    col = lax.broadcasted_iota(jnp.int32, (c, width), 1) % c
    zero = jnp.zeros((), BF16)

    def lower_left(bs):
        return (row // (2 * bs) == col // (2 * bs)) & ((row // bs) % 2 == 1) & ((col // bs) % 2 == 0)

    l_bf = [l.astype(BF16) for l in l_packed_list]
    xs = [jnp.where(row == col, 1.0, 0.0) - jnp.where(lower_left(1), l, 0.0) for l in l_packed_list]
    bs = 2
    while bs < c:
        sel = lower_left(bs)
        x_bf = [x.astype(BF16) for x in xs]
        ys = [jnp.dot(jnp.where(sel, l, zero), _expand(x, diag_ones_ref), preferred_element_type=F32)
              for l, x in zip(l_bf, x_bf)]
        xs = [x - jnp.dot(xb, _expand(y.astype(BF16), diag_ones_ref), preferred_element_type=F32)
              for x, xb, y in zip(xs, x_bf, ys)]
        bs *= 2
    return xs


def _mix_prompt_kernel(x_ref, nmix_ref, wmain_ref, wtail_ref, avg_ref, aog_ref, bog_ref, wsp_ref, bsp_ref,
                       cw_ref, abp_ref, wout_ref,
                       xo_ref, s_ref, ct_ref,
                       zlast_ref, ob_ref, bd_ref, wgate_ref):
    tl = x_ref.shape[0]
    step = pl.program_id(1)

    @pl.when((pl.program_id(0) == 0) & (step == 0))
    def _():
        wgate_ref[...] = wtail_ref[:, 2 * N_HEADS:]

    @pl.when(step == 0)
    def _():
        s_ref[...] = jnp.zeros_like(s_ref)
        zlast_ref[...] = jnp.zeros_like(zlast_ref)

    x = x_ref[...]
    xn = _rms(x, nmix_ref[...]).astype(BF16)
    def in_proj(lo, hi):
        return jnp.dot(xn, wmain_ref[:, lo:hi], preferred_element_type=F32)

    zab = jnp.dot(xn, wtail_ref[:, :LANES], preferred_element_type=F32)
    first_tile = lax.broadcasted_iota(jnp.int32, (8, 1), 0)
    qkv_parts = []
    for part in range(3):
        cols = slice(part * B_WIDTH, (part + 1) * B_WIDTH)
        zc = in_proj(OFF_QKV + part * B_WIDTH, OFF_QKV + (part + 1) * B_WIDTH)
        cw = cw_ref[:, cols]
        carried = zlast_ref[:, cols]
        y = zc * cw[CONV_W - 1:CONV_W]
        for d in range(1, CONV_W):
            rolled = pltpu.roll(zc, d, axis=0)
            top = jnp.where(first_tile < d, pltpu.roll(carried, d, axis=0), rolled[0:8])
            y = y + jnp.concatenate([top, rolled[8:]], axis=0) * cw[CONV_W - 1 - d:CONV_W - d]
        zlast_ref[:, cols] = zc[tl - 8:tl]
        ct_ref[:, cols] = zc[tl - (CONV_W - 1):tl]
        qkv_parts.append(_silu(y))

    uv = jax.nn.gelu(in_proj(0, OFF_QKV))
    row = lax.broadcasted_iota(jnp.int32, (CHUNK_A, CHUNK_A), 0)
    col = lax.broadcasted_iota(jnp.int32, (CHUNK_A, CHUNK_A), 1)
    causal = col <= row
    for h in range(N_HEADS):
        hs = slice(h * HEAD_DIM, (h + 1) * HEAD_DIM)
        u_h = uv[:, hs]
        v_h = _rms(uv[:, A_WIDTH + h * HEAD_DIM:A_WIDTH + (h + 1) * HEAD_DIM], avg_ref[...]).astype(BF16)
        w_h = jnp.where(causal, wsp_ref[h], 0.0).astype(BF16)
        bias_h = bsp_ref[:, h:h + 1]
        for c in range(tl // CHUNK_A):
            rs = slice(c * CHUNK_A, (c + 1) * CHUNK_A)
            mixed = jnp.dot(w_h, v_h[rs], preferred_element_type=F32) + bias_h
            ob_ref[rs, hs] = _rms(u_h[rs] * mixed, aog_ref[...]).astype(BF16)

    z_gate = jnp.dot(xn, wgate_ref[...], preferred_element_type=F32)
    abp = abp_ref[...]
    g = -jnp.exp(abp[0:1]) * _softplus(zab + abp[1:2])
    beta = jax.nn.sigmoid(zab)

    r2 = lax.broadcasted_iota(jnp.int32, (GROUP, GROUP), 0)
    c2 = lax.broadcasted_iota(jnp.int32, (GROUP, GROUP), 1)
    same = (r2 // CHUNK_D) == (c2 // CHUNK_D)
    strict_bd = same & (c2 < r2)
    col_ones = jnp.concatenate([jnp.where(same & (c2 <= r2), 1.0, 0.0), jnp.where(same, 1.0, 0.0)],
                               axis=0).astype(BF16)
    upper_ones = jnp.where(same & (r2 <= c2), 1.0, 0.0).astype(BF16)
    bd_ref[...] = jnp.where(same, 1.0, 0.0).astype(BF16)
    n_grp = tl // GROUP
    gam_parts, glast_parts, gam_t = [], [], []
    for gi in range(n_grp):
        g_grp = g[gi * GROUP:(gi + 1) * GROUP]
        by_col = jnp.dot(col_ones, jnp.concatenate(_split3(g_grp), axis=1), preferred_element_type=F32)
        by_col = by_col[:, :LANES] + by_col[:, LANES:2 * LANES] + by_col[:, 2 * LANES:]
        gam_parts.append(by_col[:GROUP])
        glast_parts.append(by_col[GROUP:])
        by_row = jnp.dot(jnp.concatenate(_split3(g_grp.T), axis=0), upper_ones, preferred_element_type=F32)
        gam_t.append(by_row[:LANES] + by_row[LANES:2 * LANES] + by_row[2 * LANES:])
    gam = jnp.concatenate(gam_parts, axis=0)
    glast = jnp.concatenate(glast_parts, axis=0)

    rb = lax.broadcasted_iota(jnp.int32, (CHUNK_D, CHUNK_D), 0)
    cb = lax.broadcasted_iota(jnp.int32, (CHUNK_D, CHUNK_D), 1)
    incl = cb <= rb

    heads, a_packed, rhs = [], [], []
    for h in range(N_HEADS):
        q_h = _l2(qkv_parts[0][:, h * HEAD_DIM:(h + 1) * HEAD_DIM]) * (HEAD_DIM ** -0.5)
        k_h = _l2(qkv_parts[1][:, h * HEAD_DIM:(h + 1) * HEAD_DIM])
        v_h = qkv_parts[2][:, h * HEAD_DIM:(h + 1) * HEAD_DIM]
        gc_h = gam[:, h:h + 1]
        gl_h = glast[:, h:h + 1]
        bc_h = beta[:, N_HEADS + h:N_HEADS + h + 1]
        eg_h = jnp.exp(gc_h)
        for gi in range(n_grp):
            gs = slice(gi * GROUP, (gi + 1) * GROUP)
            kk = _dot_nt(k_h[gs], k_h[gs])
            decay = jnp.exp(jnp.where(strict_bd, gc_h[gs] - gam_t[gi][h:h + 1, :], 0.0))
            a_packed.append(_fold(jnp.where(strict_bd, bc_h[gs] * kk * decay, 0.0)))
        rhs.append(_split(jnp.concatenate([bc_h * v_h, (bc_h * eg_h) * k_h], axis=1)))
        heads.append((q_h * eg_h, q_h, k_h, k_h * jnp.exp(gl_h - gc_h), gc_h, jnp.exp(gl_h)))
    inv_split = [_split(inv) for inv in _unit_lower_inverses_packed(a_packed, bd_ref)]
    sol = []
    for h in range(N_HEADS):
        sol.append([_dot3(_expand(inv_split[h * n_grp + gi][0], bd_ref), _expand(inv_split[h * n_grp + gi][1], bd_ref),
                          rhs[h][0][gi * GROUP:(gi + 1) * GROUP], rhs[h][1][gi * GROUP:(gi + 1) * GROUP])
                    for gi in range(n_grp)])

    for i in range(tl // CHUNK_D):
        rs = slice(i * CHUNK_D, (i + 1) * CHUNK_D)
        gi, j = divmod(i, GROUP // CHUNK_D)
        ls = slice(j * CHUNK_D, (j + 1) * CHUNK_D)
        for h in range(N_HEADS):
            qb_h, q_h, k_h, kend_h, gc_h, btot_h = heads[h]
            decay = jnp.where(incl, jnp.exp(jnp.where(incl, gc_h[rs] - gam_t[gi][h:h + 1, ls], 0.0)), 0.0)
            qk = _dot_nt(q_h[rs], k_h[rs]) * decay
            s_old = s_ref[h]
            from_s = _dot(jnp.concatenate([sol[h][gi][ls, HEAD_DIM:], qb_h[rs]], axis=0), s_old)
            u = sol[h][gi][ls, :HEAD_DIM] - from_s[:CHUNK_D]
            from_u = _dot(jnp.concatenate([qk, kend_h[rs].T], axis=0), u)
            o = from_s[CHUNK_D:] + from_u[:CHUNK_D]
            s_ref[h] = btot_h[i * CHUNK_D:i * CHUNK_D + 1] * s_old + from_u[CHUNK_D:]
            gate = z_gate[rs, h * HEAD_DIM:(h + 1) * HEAD_DIM]
            ob_ref[rs, A_WIDTH + h * HEAD_DIM:A_WIDTH + (h + 1) * HEAD_DIM] = (
                _rms(o, bog_ref[...]) * _silu(gate)).astype(BF16)

    xo_ref[...] = x + jnp.dot(ob_ref[...], wout_ref[...], preferred_element_type=F32)


def _mixer_weight_specs(layer):
    return [_layer_spec((1, D_MODEL), layer), _layer_spec((D_MODEL, Z_FRONT), layer), _layer_spec((D_MODEL, Z_TAIL), layer),
            _layer_spec((1, HEAD_DIM), layer), _layer_spec((1, HEAD_DIM), layer), _layer_spec((1, HEAD_DIM), layer)]


def _mixer_weights(w):
    return [w["n_mix"], w["w_front"], w["w_tail"], w["a_v_gain"], w["a_out_gain"], w["b_out_gain"]]


def _mix_prompt(x, layer, w):
    bsz, length, _ = x.shape
    tl = MIX_ROWS
    row_spec = pl.BlockSpec((None, tl, D_MODEL), lambda b, t: (b, t, 0))
    in_specs = [row_spec] + _mixer_weight_specs(layer) + [
        _layer_spec((N_HEADS, CHUNK_A, CHUNK_A), layer), _layer_spec((CHUNK_A, N_HEADS), layer),
        _layer_spec((CONV_W, 3 * B_WIDTH), layer), _layer_spec((2, LANES), layer), _layer_spec((D_MODEL, D_MODEL), layer)]
    out_specs = [row_spec,
                 pl.BlockSpec((None, N_HEADS, HEAD_DIM, HEAD_DIM), lambda b, t: (b, 0, 0, 0)),
                 pl.BlockSpec((None, CONV_W - 1, 3 * B_WIDTH), lambda b, t: (b, 0, 0))]
    out_shape = [jax.ShapeDtypeStruct(x.shape, F32),
                 jax.ShapeDtypeStruct((bsz, N_HEADS, HEAD_DIM, HEAD_DIM), F32),
                 jax.ShapeDtypeStruct((bsz, CONV_W - 1, 3 * B_WIDTH), F32)]
    return pl.pallas_call(
        _mix_prompt_kernel,
        grid=(bsz, length // tl),
        in_specs=in_specs,
        out_specs=out_specs,
        out_shape=out_shape,
        scratch_shapes=[pltpu.VMEM((8, 3 * B_WIDTH), F32), pltpu.VMEM((tl, D_MODEL), BF16),
                        pltpu.VMEM((GROUP, GROUP), BF16), pltpu.VMEM((D_MODEL, B_WIDTH), BF16)],
        compiler_params=pltpu.CompilerParams(dimension_semantics=("arbitrary", "arbitrary"),
                                             vmem_limit_bytes=VMEM_LIMIT_BYTES),
    )(x, *_mixer_weights(w), w["a_w_s"], w["a_b_s_t"], w["b_conv"], w["ab_par"], w["w_out"])


def _mix_sample_kernel(x_ref, s_ref, cpad_ref, s_all_ref, nmix_ref, wmain_ref, wtail_ref, avg_ref, aog_ref, bog_ref,
                       coef_ref, bias_ref, cw_ref, abp_ref, wout_ref,
                       xo_ref, so_ref, zq_ref, vo_ref,
                       z_ref, ztail_ref, ob_ref, *, n_tok):
    del s_all_ref
    rows = zq_ref.shape[0]
    nb = rows // n_tok
    step = pl.program_id(0)

    @pl.when(step == 0)
    def _():
        xn = _rms(x_ref[...], nmix_ref[...]).astype(BF16)
        z_ref[...] = jnp.dot(xn, wmain_ref[...], preferred_element_type=F32)
        ztail_ref[...] = jnp.dot(xn, wtail_ref[...], preferred_element_type=F32)

    here = pl.ds(pl.multiple_of(step * rows, rows), rows)
    z = z_ref[here, :]
    z_tail = ztail_ref[here, :]
    zab = z_tail[:, :LANES]
    tok = lax.broadcasted_iota(jnp.int32, (rows, 1), 0) % n_tok

    def prev(a, d):
        return pltpu.roll(a, d, axis=0)

    def prev_or_zero(a, d):
        return a if d == 0 else jnp.where(tok >= d, prev(a, d), 0.0)

    def per_head(fn, a):
        return jnp.concatenate([fn(a[:, h * HEAD_DIM:(h + 1) * HEAD_DIM]) for h in range(N_HEADS)], axis=1)

    uv = jax.nn.gelu(z[:, :2 * A_WIDTH])
    vn = per_head(lambda a: _rms(a, avg_ref[...]), uv[:, A_WIDTH:])
    vo_ref[...] = vn
    mixed = bias_ref[...]
    for d in range(n_tok):
        mixed = mixed + coef_ref[d] * prev_or_zero(vn, d)
    ob_ref[here, :A_WIDTH] = per_head(lambda a: _rms(a, aog_ref[...]), uv[:, :A_WIDTH] * mixed)

    zq = z[:, OFF_QKV:OFF_GATE]
    zq_ref[...] = zq
    cpad = cpad_ref[...]
    cw = cw_ref[...]
    y = zq * cw[CONV_W - 1:CONV_W]
    for d in range(1, CONV_W):
        carried = pltpu.roll(cpad, rows - (n_tok - d), axis=0)
        y = y + jnp.where(tok >= d, prev(zq, d), carried) * cw[CONV_W - 1 - d:CONV_W - d]
    qkv = _silu(y)

    abp = abp_ref[...]
    g_all = -jnp.exp(abp[0:1]) * _softplus(zab + abp[1:2])
    beta_all = jax.nn.sigmoid(zab)

    sub = lax.broadcasted_iota(jnp.int32, (8, 1), 0)
    first_half = sub < n_tok
    o_heads, kend_heads, u_heads, btot_heads = [], [], [], []
    for h in range(N_HEADS):
        q = _l2(qkv[:, h * HEAD_DIM:(h + 1) * HEAD_DIM]) * (HEAD_DIM ** -0.5)
        k = _l2(qkv[:, B_WIDTH + h * HEAD_DIM:B_WIDTH + (h + 1) * HEAD_DIM])
        v = qkv[:, 2 * B_WIDTH + h * HEAD_DIM:2 * B_WIDTH + (h + 1) * HEAD_DIM]
        g = jnp.broadcast_to(g_all[:, h:h + 1], (rows, HEAD_DIM))
        beta = jnp.broadcast_to(beta_all[:, N_HEADS + h:N_HEADS + h + 1], (rows, HEAD_DIM))
        gam = g
        for d in range(1, n_tok):
            gam = gam + prev_or_zero(g, d)
        gam_last = jnp.where(tok == n_tok - 1, gam, 0.0)
        for d in range(1, n_tok):
            gam_last = gam_last + jnp.where(tok == n_tok - 1 - d, pltpu.roll(gam, rows - d, axis=0), 0.0)
        eg = jnp.exp(gam)

        def decay_to(d, gam=gam):
            return jnp.exp(jnp.where(tok >= d, gam - prev(gam, d), 0.0))

        a_sub = [None] + [jnp.where(tok >= d, beta * jnp.sum(k * prev(k, d), axis=-1, keepdims=True) * decay_to(d),
                                    0.0) for d in range(1, n_tok)]
        def forward_substitute(rhs, a_sub=a_sub):
            sol = rhs
            for t in range(1, n_tok):
                acc = rhs
                for d in range(1, t + 1):
                    acc = acc - a_sub[d] * prev(sol, d)
                sol = jnp.where(tok == t, acc, sol)
            return sol

        w_blk = forward_substitute(beta * v)
        kb_blk = forward_substitute((beta * eg) * k)
        qb = q * eg

        kb_s, qb_s = [], []
        for p in range(rows // 8):
            kb_t, qb_t = kb_blk[8 * p:8 * p + 8], qb[8 * p:8 * p + 8]
            f0 = _dot(jnp.where(first_half, kb_t, pltpu.roll(qb_t, n_tok, axis=0)), s_ref[2 * p, h])
            f1 = _dot(jnp.where(first_half, pltpu.roll(kb_t, n_tok, axis=0), qb_t), s_ref[2 * p + 1, h])
            kb_s.append(jnp.where(first_half, f0, pltpu.roll(f1, n_tok, axis=0)))
            qb_s.append(jnp.where(first_half, pltpu.roll(f0, n_tok, axis=0), f1))
        u = w_blk - jnp.concatenate(kb_s, axis=0)
        o = jnp.concatenate(qb_s, axis=0)
        for d in range(n_tok):
            qk = jnp.where(tok >= d, jnp.sum(q * prev(k, d), axis=-1, keepdims=True) * decay_to(d), 0.0)
            o = o + qk * prev_or_zero(u, d)
        o_heads.append(o)
        kend_heads.append(k * jnp.exp(gam_last - gam))
        u_heads.append(u)
        btot_heads.append(jnp.broadcast_to(jnp.exp(gam_last), (rows, HEAD_DIM)))

    kend_t = jnp.concatenate(kend_heads, axis=0).T
    u_all = jnp.concatenate(u_heads, axis=0).astype(BF16)
    owner = lax.broadcasted_iota(jnp.int32, (1, N_HEADS * rows), 1) // n_tok
    for h in range(N_HEADS):
        for b in range(nb):
            mine = jnp.where(owner == h * nb + b, kend_t, 0.0).astype(BF16)
            last = b * n_tok + n_tok - 1
            so_ref[b, h] = btot_heads[h][last:last + 1] * s_ref[b, h] + jnp.dot(
                mine, u_all, preferred_element_type=F32)

    gate = z_tail[:, 2 * N_HEADS:]
    ob_ref[here, A_WIDTH:] = per_head(lambda a: _rms(a, bog_ref[...]), jnp.concatenate(o_heads, axis=1)) * _silu(gate)

    @pl.when(step == pl.num_programs(0) - 1)
    def _():
        xo_ref[...] = x_ref[...] + _dot(ob_ref[...], wout_ref[...])


def _mix_sample(x, layer, state_s, cpad, s_all, w, n_tok):
    rows_total = x.shape[0]
    nb = SAMPLE_GROUP
    rows = nb * n_tok
    all_rows = lambda width: pl.BlockSpec((rows_total, width), lambda i: (0, 0))
    row_spec = lambda width: pl.BlockSpec((rows, width), lambda i: (i, 0))
    s_spec = pl.BlockSpec((None, nb, N_HEADS, HEAD_DIM, HEAD_DIM), lambda i: (layer, i, 0, 0, 0))
    in_specs = [all_rows(D_MODEL), s_spec, pl.BlockSpec((None, rows, 3 * B_WIDTH), lambda i: (layer, i, 0)),
                pl.BlockSpec(memory_space=pl.ANY)] + _mixer_weight_specs(layer) + [
        _layer_spec((n_tok, rows, A_WIDTH), layer), _layer_spec((rows, A_WIDTH), layer),
        _layer_spec((CONV_W, 3 * B_WIDTH), layer), _layer_spec((2, LANES), layer), _layer_spec((D_MODEL, D_MODEL), layer)]
    out_specs = [all_rows(D_MODEL), s_spec, row_spec(3 * B_WIDTH), row_spec(A_WIDTH)]
    out_shape = [jax.ShapeDtypeStruct(x.shape, F32), jax.ShapeDtypeStruct(state_s.shape, F32),
                 jax.ShapeDtypeStruct((rows_total, 3 * B_WIDTH), F32), jax.ShapeDtypeStruct((rows_total, A_WIDTH), F32)]
    return pl.pallas_call(
        functools.partial(_mix_sample_kernel, n_tok=n_tok),
        grid=(rows_total // rows,),
        in_specs=in_specs,
        out_specs=out_specs,
        out_shape=out_shape,
        input_output_aliases={3: 1},
        scratch_shapes=[pltpu.VMEM((rows_total, Z_FRONT), F32), pltpu.VMEM((rows_total, Z_TAIL), F32),
                        pltpu.VMEM((rows_total, D_MODEL), F32)],
        compiler_params=pltpu.CompilerParams(dimension_semantics=("arbitrary",),
                                             vmem_limit_bytes=VMEM_LIMIT_BYTES),
    )(x, state_s, cpad, s_all, *_mixer_weights(w), w["a_coef"], w["a_bias"], w["b_conv"], w["ab_par"], w["w_out"])


def _prep_weights(n_tok, norm_ffn1, norm_mix, w_in, a_v_gain, a_spatial_w, a_spatial_b, a_out_gain, b_conv_w,
                  b_a_log, b_dt_bias, b_out_gain, w_out, norm_ffn2, norm_ple, w_ple_gate, w_ple_proj):
    w_front = w_in[:, :, :Z_FRONT].astype(BF16)
    w_tail = w_in[:, :, Z_FRONT:].astype(BF16)
    par_pad = jnp.zeros((DEPTH, LANES - N_HEADS), F32)
    ab_par = jnp.stack([jnp.concatenate([b_a_log, par_pad], axis=1),
                        jnp.concatenate([b_dt_bias, par_pad], axis=1)], axis=1)

    def sample_rows(a):
        return jnp.tile(jnp.repeat(jnp.transpose(a, (0, 2, 1)), HEAD_DIM, axis=2), (1, SAMPLE_GROUP, 1))

    ws_small = a_spatial_w[:, :, :n_tok, :n_tok]
    a_coef = jnp.stack([sample_rows(jnp.pad(jnp.diagonal(ws_small, offset=-d, axis1=2, axis2=3),
                                            ((0, 0), (0, 0), (d, 0)))) for d in range(n_tok)], axis=1)
    return dict(
        n_f1=norm_ffn1[:, None],
        n_mix=norm_mix[:, None], w_front=w_front, w_tail=w_tail, ab_par=ab_par,
        a_v_gain=a_v_gain[:, None], a_out_gain=a_out_gain[:, None], b_out_gain=b_out_gain[:, None],
        a_w_s=a_spatial_w, a_b_s_t=jnp.transpose(a_spatial_b, (0, 2, 1)),
        a_coef=a_coef, a_bias=sample_rows(a_spatial_b[:, :, :n_tok]),
        b_conv=b_conv_w, w_out=w_out.astype(BF16),
        n_f2=norm_ffn2[:, None],
        n_ple=norm_ple[:, None], w_ple_gate=w_ple_gate.astype(BF16), w_ple_proj=w_ple_proj.astype(BF16),
    )


def kernel(x_prompt, x_sample, state_S, state_conv, p_prompt, p_sample, norm_ffn1, w_ffn1_in, w_ffn1_out, norm_mix, w_in, a_v_gain, a_spatial_w, a_spatial_b, a_out_gain, b_conv_w, b_a_log, b_dt_bias, b_out_gain, w_out, norm_ffn2, w_ffn2_in, w_ffn2_out, norm_ple, w_ple_gate, w_ple_proj, final_norm):
    bsz, length, _ = x_prompt.shape
    dec_bsz, n_tok, _ = x_sample.shape
    assert length % MIX_ROWS == 0 and MIX_ROWS % CHUNK_A == 0 and MIX_ROWS % GROUP == 0
    assert dec_bsz % SAMPLE_GROUP == 0
    assert n_tok % CHUNK_A != 0 and n_tok % CHUNK_D != 0
    assert 2 * n_tok == 8 and N_HEADS * SAMPLE_GROUP * n_tok == HEAD_DIM and n_tok >= CONV_W - 1

    w = _prep_weights(n_tok, norm_ffn1, norm_mix, w_in, a_v_gain, a_spatial_w, a_spatial_b, a_out_gain, b_conv_w,
                      b_a_log, b_dt_bias, b_out_gain, w_out, norm_ffn2, norm_ple, w_ple_gate, w_ple_proj)
    ffn1_w = (w_ffn1_in[0:1].astype(BF16), w_ffn1_out[0:1].astype(BF16))
    final = final_norm[None, None]
    xp = x_prompt.reshape(bsz * length, D_MODEL)
    xs = x_sample.reshape(dec_bsz * n_tok, D_MODEL)
    pp = p_prompt.reshape(DEPTH, bsz * length, PLE_DIM)
    ps = p_sample.reshape(DEPTH, dec_bsz * n_tok, PLE_DIM)
    keep = CONV_W - 1
    cpad = jnp.pad(state_conv, ((0, 0), (0, 0), (n_tok - keep, 0), (0, 0))).reshape(DEPTH, dec_bsz * n_tok, 3 * B_WIDTH)

    s_prompt, c_prompt, c_sample, v_sample = [], [], [], []
    s_sample = jnp.zeros(state_S.shape, F32)
    for i in range(DEPTH):
        last = dict(final_gain=final) if i == DEPTH - 1 else {}
        ple = (w["n_ple"], w["w_ple_gate"], w["w_ple_proj"])

        xp, *ffn2_w = _ffn(xp, i, w["n_f1"], *ffn1_w, cast=(w_ffn2_in, w_ffn2_out, i))
        xs = _ffn(xs, i, w["n_f1"], *ffn1_w)
        xp, sp, cp = _mix_prompt(xp.reshape(bsz, length, D_MODEL), i, w)
        xs, s_sample, zq, vs = _mix_sample(xs, i, state_S, cpad, s_sample, w, n_tok)
        xp = xp.reshape(bsz * length, D_MODEL)
        if i < DEPTH - 1:
            xp, *ffn1_w = _ffn(xp, i, w["n_f2"], *ffn2_w, ple=(pp,) + ple, cast=(w_ffn1_in, w_ffn1_out, i + 1))
        else:
            xp = _ffn(xp, i, w["n_f2"], *ffn2_w, ple=(pp,) + ple, **last)
        xs = _ffn(xs, i, w["n_f2"], *ffn2_w, ple=(ps,) + ple, **last)

        s_prompt.append(sp)
        c_prompt.append(cp)
        c_sample.append(zq.reshape(dec_bsz, n_tok, 3 * B_WIDTH)[:, n_tok - keep:])
        v_sample.append(vs.reshape(dec_bsz, n_tok, N_HEADS, HEAD_DIM))

    return (xp.reshape(bsz, length, D_MODEL), xs.reshape(dec_bsz, n_tok, D_MODEL), jnp.stack(s_prompt),
            jnp.stack(c_prompt), s_sample, jnp.stack(c_sample), jnp.stack(v_sample))
```

```python
import functools

import jax
import jax.numpy as jnp
from jax import lax
from jax.experimental import pallas as pl
from jax.experimental.pallas import tpu as pltpu

F32 = jnp.float32
BF16 = jnp.bfloat16
EPS = 1e-6

D_MODEL = 1024
D_FF = 2816
DEPTH = 4
N_HEADS = 4
HEAD_DIM = 128
A_WIDTH = N_HEADS * HEAD_DIM
B_WIDTH = N_HEADS * HEAD_DIM
CHUNK_A = 128
CHUNK_D = 64
CONV_W = 4
PLE_DIM = 256
Z_FRONT = 2 * A_WIDTH + 3 * B_WIDTH
Z_TAIL = 2 * N_HEADS + B_WIDTH
OFF_QKV = 2 * A_WIDTH
OFF_GATE = OFF_QKV + 3 * B_WIDTH

VMEM_LIMIT_BYTES = 52 * 1024 * 1024
LANES = 128
BF16_SUBLANES = 16
MXU_N = 256
FFN_ROWS = 512
GROUP = 4 * CHUNK_D
MIX_ROWS = 512
SAMPLE_GROUP = 8


def _rms(x, gain):
    return x * lax.rsqrt(jnp.mean(x * x, axis=-1, keepdims=True) + EPS) * gain


def _l2(x):
    return x * lax.rsqrt(jnp.sum(x * x, axis=-1, keepdims=True) + EPS)


def _silu(x):
    return x * jax.nn.sigmoid(x)


def _softplus(x):
    return jnp.maximum(x, 0.0) + jnp.log1p(jnp.exp(-jnp.abs(x)))


def _dot(a, b):
    return jnp.dot(a.astype(BF16), b.astype(BF16), preferred_element_type=F32)


def _dot_nt(a, b):
    return lax.dot_general(a.astype(BF16), b.astype(BF16), (((1,), (1,)), ((), ())),
                           preferred_element_type=F32)


def _split3(a):
    p1 = a.astype(BF16)
    r1 = a - p1.astype(F32)
    p2 = r1.astype(BF16)
    p3 = (r1 - p2.astype(F32)).astype(BF16)
    return p1, p2, p3


def _layer_spec(shape, layer, block=None):
    index = (layer,) + (0,) * (len(shape) - 1) + (0 if block is None else block,)
    return pl.BlockSpec((None,) + tuple(shape), lambda *_: index, pipeline_mode=pl.Buffered(1))


def _ffn_kernel(*refs, with_ple, with_final, cast_plan):
    x_ref, gain_ref, wg_ref, wu_ref, wo_ref = refs[:5]
    n_in = 5 + 4 * with_ple + with_final
    cast_in = refs[n_in:n_in + len(cast_plan)]
    o_ref = refs[n_in + len(cast_plan)]
    cast_out = iter(refs[n_in + len(cast_plan) + 1:])
    for src_ref, splits in zip(cast_in, cast_plan):
        for lo, hi in splits:
            next(cast_out)[...] = src_ref[:, lo:hi].astype(BF16)
    x = x_ref[...]
    xn = _rms(x, gain_ref[...]).astype(BF16)
    acc = jnp.zeros_like(x)
    for c in range(D_FF // MXU_N):
        sl = slice(c * MXU_N, (c + 1) * MXU_N)
        gate = jnp.dot(xn, wg_ref[:, sl], preferred_element_type=F32)
        up = jnp.dot(xn, wu_ref[:, sl], preferred_element_type=F32)
        h = (_silu(gate) * up).astype(BF16)
        acc = acc + jnp.dot(h, wo_ref[sl, :], preferred_element_type=F32)
    x = x + 0.5 * acc
    if with_ple:
        p_ref, npl_ref, wpg_ref, wpp_ref = refs[5:9]
        emb = _dot(p_ref[...], wpp_ref[...])
        gate = _dot(_rms(x, npl_ref[...]), wpg_ref[...])
        x = x + emb * jax.nn.sigmoid(gate)
    if with_final:
        x = _rms(x, refs[9][...])
    o_ref[...] = x


def _cast_row_blocks(n_rows, n_steps):
    blocks = n_steps
    while n_rows % blocks or (n_rows // blocks) % BF16_SUBLANES:
        assert blocks % 2 == 0, (n_rows, n_steps)
        blocks //= 2
    return blocks


def _ffn(x, layer, gain, w_in, w_out, ple=None, final_gain=None, casts=()):
    rows = x.shape[0]
    tm = min(FFN_ROWS, rows)
    n_steps = rows // tm
    row_spec = pl.BlockSpec((tm, D_MODEL), lambda i: (i, 0))
    in_specs = [row_spec, _layer_spec((1, D_MODEL), layer),
                _layer_spec((D_MODEL, D_FF), 0, block=0), _layer_spec((D_MODEL, D_FF), 0, block=1),
                _layer_spec((D_FF, D_MODEL), 0)]
    args = [x, gain, w_in, w_in, w_out]
    out_specs = [row_spec]
    out_shape = [jax.ShapeDtypeStruct(x.shape, F32)]
    if ple is not None:
        p, n_ple, w_gate, w_proj = ple
        in_specs += [pl.BlockSpec((None, tm, PLE_DIM), lambda i: (layer, i, 0)), _layer_spec((1, D_MODEL), layer),
                     _layer_spec((D_MODEL, D_MODEL), 0), _layer_spec((PLE_DIM, D_MODEL), 0)]
        args += [p, n_ple, w_gate, w_proj]
    if final_gain is not None:
        in_specs.append(_layer_spec((1, D_MODEL), 0))
        args.append(final_gain)
    cast_plan = []
    for src, layer_c, splits in casts:
        _, n_rows, n_cols = src.shape
        splits = tuple(splits) if splits is not None else ((0, n_cols),)
        blocks = _cast_row_blocks(n_rows, n_steps)
        block_of = functools.partial(lambda i, every: i // every, every=n_steps // blocks)
        in_specs.append(pl.BlockSpec((None, n_rows // blocks, n_cols),
                                     lambda i, layer_c=layer_c, block_of=block_of: (layer_c, block_of(i), 0)))
        args.append(src)
        for lo, hi in splits:
            out_specs.append(pl.BlockSpec((None, n_rows // blocks, hi - lo),
                                          lambda i, block_of=block_of: (0, block_of(i), 0)))
            out_shape.append(jax.ShapeDtypeStruct((1, n_rows, hi - lo), BF16))
        cast_plan.append(splits)
    out = pl.pallas_call(
        functools.partial(_ffn_kernel, with_ple=ple is not None, with_final=final_gain is not None,
                          cast_plan=tuple(cast_plan)),
        grid=(n_steps,),
        in_specs=in_specs,
        out_specs=out_specs,
        out_shape=out_shape,
        compiler_params=pltpu.CompilerParams(dimension_semantics=("arbitrary",),
                                             vmem_limit_bytes=VMEM_LIMIT_BYTES),
    )(*args)
    return (out[0], out[1:]) if casts else out[0]


def _split(a):
    hi = a.astype(BF16)
    lo = (a - hi.astype(F32)).astype(BF16)
    return hi, lo


def _dot3(a_hi, a_lo, b_hi, b_lo):
    m = a_hi.shape[0]
    both = jnp.dot(jnp.concatenate([a_hi, a_lo], axis=0), b_hi, preferred_element_type=F32)
    return both[:m] + both[m:] + jnp.dot(a_hi, b_lo, preferred_element_type=F32)


def _fold(block_diag):
    n = block_diag.shape[0] // CHUNK_D
    out = block_diag[0:CHUNK_D]
    for g in range(1, n):
        out = out + block_diag[g * CHUNK_D:(g + 1) * CHUNK_D]
    return out


def _expand(packed, diag_ones_ref):
    n = packed.shape[1] // CHUNK_D
    return jnp.concatenate([packed] * n, axis=0) * diag_ones_ref[...]


def _unit_lower_inverses_packed(l_packed_list, diag_ones_ref):
    c, width = l_packed_list[0].shape
    row = lax.broadcasted_iota(jnp.int32, (c, width), 0)
    col = lax.broadcasted_iota(jnp.int32, (c, width), 1) % c
    zero = jnp.zeros((), BF16)

    def lower_left(bs):
        return (row // (2 * bs) == col // (2 * bs)) & ((row // bs) % 2 == 1) & ((col // bs) % 2 == 0)

    l_bf = [l.astype(BF16) for l in l_packed_list]
    xs = [jnp.where(row == col, 1.0, 0.0) - jnp.where(lower_left(1), l, 0.0) for l in l_packed_list]
    bs = 2
    while bs < c:
        sel = lower_left(bs)
        x_bf = [x.astype(BF16) for x in xs]
        ys = [jnp.dot(jnp.where(sel, l, zero), _expand(x, diag_ones_ref), preferred_element_type=F32)
              for l, x in zip(l_bf, x_bf)]
        xs = [x - jnp.dot(xb, _expand(y.astype(BF16), diag_ones_ref), preferred_element_type=F32)
              for x, xb, y in zip(xs, x_bf, ys)]
        bs *= 2
    return xs


def _mix_prompt_kernel(x_ref, nmix_ref, wmain_ref, wtail_ref, avg_ref, aog_ref, bog_ref, wsp_ref, bsp_ref,
                       cw_ref, abp_ref, wout_ref,
                       xo_ref, s_ref, ct_ref,
                       zlast_ref, ob_ref, bd_ref, wgate_ref):
    tl = x_ref.shape[0]
    step = pl.program_id(1)

    @pl.when((pl.program_id(0) == 0) & (step == 0))
    def _():
        wgate_ref[...] = wtail_ref[:, 2 * N_HEADS:]

    @pl.when(step == 0)
    def _():
        s_ref[...] = jnp.zeros_like(s_ref)
        zlast_ref[...] = jnp.zeros_like(zlast_ref)

    x = x_ref[...]
    xn = _rms(x, nmix_ref[...]).astype(BF16)
    def in_proj(lo, hi):
        return jnp.dot(xn, wmain_ref[:, lo:hi], preferred_element_type=F32)

    zab = jnp.dot(xn, wtail_ref[:, :LANES], preferred_element_type=F32)
    first_tile = lax.broadcasted_iota(jnp.int32, (8, 1), 0)
    qkv_parts = []
    for part in range(3):
        cols = slice(part * B_WIDTH, (part + 1) * B_WIDTH)
        zc = in_proj(OFF_QKV + part * B_WIDTH, OFF_QKV + (part + 1) * B_WIDTH)
        cw = cw_ref[:, cols]
        carried = zlast_ref[:, cols]
        y = zc * cw[CONV_W - 1:CONV_W]
        for d in range(1, CONV_W):
            rolled = pltpu.roll(zc, d, axis=0)
            top = jnp.where(first_tile < d, pltpu.roll(carried, d, axis=0), rolled[0:8])
            y = y + jnp.concatenate([top, rolled[8:]], axis=0) * cw[CONV_W - 1 - d:CONV_W - d]
        zlast_ref[:, cols] = zc[tl - 8:tl]
        ct_ref[:, cols] = zc[tl - (CONV_W - 1):tl]
        qkv_parts.append(_silu(y))

    uv = jax.nn.gelu(in_proj(0, OFF_QKV))
    row = lax.broadcasted_iota(jnp.int32, (CHUNK_A, CHUNK_A), 0)
    col = lax.broadcasted_iota(jnp.int32, (CHUNK_A, CHUNK_A), 1)
    causal = col <= row
    for h in range(N_HEADS):
        hs = slice(h * HEAD_DIM, (h + 1) * HEAD_DIM)
        u_h = uv[:, hs]
        v_h = _rms(uv[:, A_WIDTH + h * HEAD_DIM:A_WIDTH + (h + 1) * HEAD_DIM], avg_ref[...]).astype(BF16)
        w_h = jnp.where(causal, wsp_ref[h], 0.0).astype(BF16)
        bias_h = bsp_ref[:, h:h + 1]
        for c in range(tl // CHUNK_A):
            rs = slice(c * CHUNK_A, (c + 1) * CHUNK_A)
            mixed = jnp.dot(w_h, v_h[rs], preferred_element_type=F32) + bias_h
            ob_ref[rs, hs] = _rms(u_h[rs] * mixed, aog_ref[...]).astype(BF16)

    z_gate = jnp.dot(xn, wgate_ref[...], preferred_element_type=F32)
    abp = abp_ref[...]
    g = -jnp.exp(abp[0:1]) * _softplus(zab + abp[1:2])
    beta = jax.nn.sigmoid(zab)

    r2 = lax.broadcasted_iota(jnp.int32, (GROUP, GROUP), 0)
    c2 = lax.broadcasted_iota(jnp.int32, (GROUP, GROUP), 1)
    same = (r2 // CHUNK_D) == (c2 // CHUNK_D)
    strict_bd = same & (c2 < r2)
    col_ones = jnp.concatenate([jnp.where(same & (c2 <= r2), 1.0, 0.0), jnp.where(same, 1.0, 0.0)],
                               axis=0).astype(BF16)
    upper_ones = jnp.where(same & (r2 <= c2), 1.0, 0.0).astype(BF16)
    bd_ref[...] = jnp.where(same, 1.0, 0.0).astype(BF16)
    n_grp = tl // GROUP
    gam_parts, glast_parts, gam_t = [], [], []
    for gi in range(n_grp):
        g_grp = g[gi * GROUP:(gi + 1) * GROUP]
        by_col = jnp.dot(col_ones, jnp.concatenate(_split3(g_grp), axis=1), preferred_element_type=F32)
        by_col = by_col[:, :LANES] + by_col[:, LANES:2 * LANES] + by_col[:, 2 * LANES:]
        gam_parts.append(by_col[:GROUP])
        glast_parts.append(by_col[GROUP:])
        by_row = jnp.dot(jnp.concatenate(_split3(g_grp.T), axis=0), upper_ones, preferred_element_type=F32)
        gam_t.append(by_row[:LANES] + by_row[LANES:2 * LANES] + by_row[2 * LANES:])
    gam = jnp.concatenate(gam_parts, axis=0)
    glast = jnp.concatenate(glast_parts, axis=0)

    rb = lax.broadcasted_iota(jnp.int32, (CHUNK_D, CHUNK_D), 0)
    cb = lax.broadcasted_iota(jnp.int32, (CHUNK_D, CHUNK_D), 1)
    incl = cb <= rb

    heads, a_packed, rhs = [], [], []
    for h in range(N_HEADS):
        q_h = _l2(qkv_parts[0][:, h * HEAD_DIM:(h + 1) * HEAD_DIM]) * (HEAD_DIM ** -0.5)
        k_h = _l2(qkv_parts[1][:, h * HEAD_DIM:(h + 1) * HEAD_DIM])
        v_h = qkv_parts[2][:, h * HEAD_DIM:(h + 1) * HEAD_DIM]
        gc_h = gam[:, h:h + 1]
        gl_h = glast[:, h:h + 1]
        bc_h = beta[:, N_HEADS + h:N_HEADS + h + 1]
        eg_h = jnp.exp(gc_h)
        for gi in range(n_grp):
            gs = slice(gi * GROUP, (gi + 1) * GROUP)
            kk = _dot_nt(k_h[gs], k_h[gs])
            decay = jnp.exp(jnp.where(strict_bd, gc_h[gs] - gam_t[gi][h:h + 1, :], 0.0))
            a_packed.append(_fold(jnp.where(strict_bd, bc_h[gs] * kk * decay, 0.0)))
        rhs.append(_split(jnp.concatenate([bc_h * v_h, (bc_h * eg_h) * k_h], axis=1)))
        heads.append((q_h * eg_h, q_h, k_h, k_h * jnp.exp(gl_h - gc_h), gc_h, jnp.exp(gl_h)))
    inv_split = [_split(inv) for inv in _unit_lower_inverses_packed(a_packed, bd_ref)]
    sol = []
    for h in range(N_HEADS):
        sol.append([_dot3(_expand(inv_split[h * n_grp + gi][0], bd_ref), _expand(inv_split[h * n_grp + gi][1], bd_ref),
                          rhs[h][0][gi * GROUP:(gi + 1) * GROUP], rhs[h][1][gi * GROUP:(gi + 1) * GROUP])
                    for gi in range(n_grp)])

    for i in range(tl // CHUNK_D):
        rs = slice(i * CHUNK_D, (i + 1) * CHUNK_D)
        gi, j = divmod(i, GROUP // CHUNK_D)
        ls = slice(j * CHUNK_D, (j + 1) * CHUNK_D)
        for h in range(N_HEADS):
            qb_h, q_h, k_h, kend_h, gc_h, btot_h = heads[h]
            decay = jnp.where(incl, jnp.exp(jnp.where(incl, gc_h[rs] - gam_t[gi][h:h + 1, ls], 0.0)), 0.0)
            qk = _dot_nt(q_h[rs], k_h[rs]) * decay
            s_old = s_ref[h]
            from_s = _dot(jnp.concatenate([sol[h][gi][ls, HEAD_DIM:], qb_h[rs]], axis=0), s_old)
            u = sol[h][gi][ls, :HEAD_DIM] - from_s[:CHUNK_D]
            from_u = _dot(jnp.concatenate([qk, kend_h[rs].T], axis=0), u)
            o = from_s[CHUNK_D:] + from_u[:CHUNK_D]
            s_ref[h] = btot_h[i * CHUNK_D:i * CHUNK_D + 1] * s_old + from_u[CHUNK_D:]
            gate = z_gate[rs, h * HEAD_DIM:(h + 1) * HEAD_DIM]
            ob_ref[rs, A_WIDTH + h * HEAD_DIM:A_WIDTH + (h + 1) * HEAD_DIM] = (
                _rms(o, bog_ref[...]) * _silu(gate)).astype(BF16)

    xo_ref[...] = x + jnp.dot(ob_ref[...], wout_ref[...], preferred_element_type=F32)


def _mixer_weight_specs(layer):
    return [_layer_spec((1, D_MODEL), layer), _layer_spec((D_MODEL, Z_FRONT), 0), _layer_spec((D_MODEL, Z_TAIL), 0),
            _layer_spec((1, HEAD_DIM), layer), _layer_spec((1, HEAD_DIM), layer), _layer_spec((1, HEAD_DIM), layer)]


def _mixer_weights(w):
    return [w["n_mix"], w["w_front"], w["w_tail"], w["a_v_gain"], w["a_out_gain"], w["b_out_gain"]]


def _mix_prompt(x, layer, w):
    bsz, length, _ = x.shape
    tl = MIX_ROWS
    row_spec = pl.BlockSpec((None, tl, D_MODEL), lambda b, t: (b, t, 0))
    in_specs = [row_spec] + _mixer_weight_specs(layer) + [
        _layer_spec((N_HEADS, CHUNK_A, CHUNK_A), layer), _layer_spec((CHUNK_A, N_HEADS), layer),
        _layer_spec((CONV_W, 3 * B_WIDTH), layer), _layer_spec((2, LANES), layer), _layer_spec((D_MODEL, D_MODEL), 0)]
    out_specs = [row_spec,
                 pl.BlockSpec((None, N_HEADS, HEAD_DIM, HEAD_DIM), lambda b, t: (b, 0, 0, 0)),
                 pl.BlockSpec((None, CONV_W - 1, 3 * B_WIDTH), lambda b, t: (b, 0, 0))]
    out_shape = [jax.ShapeDtypeStruct(x.shape, F32),
                 jax.ShapeDtypeStruct((bsz, N_HEADS, HEAD_DIM, HEAD_DIM), F32),
                 jax.ShapeDtypeStruct((bsz, CONV_W - 1, 3 * B_WIDTH), F32)]
    return pl.pallas_call(
        _mix_prompt_kernel,
        grid=(bsz, length // tl),
        in_specs=in_specs,
        out_specs=out_specs,
        out_shape=out_shape,
        scratch_shapes=[pltpu.VMEM((8, 3 * B_WIDTH), F32), pltpu.VMEM((tl, D_MODEL), BF16),
                        pltpu.VMEM((GROUP, GROUP), BF16), pltpu.VMEM((D_MODEL, B_WIDTH), BF16)],
        compiler_params=pltpu.CompilerParams(dimension_semantics=("arbitrary", "arbitrary"),
                                             vmem_limit_bytes=VMEM_LIMIT_BYTES),
    )(x, *_mixer_weights(w), w["a_w_s"], w["a_b_s_t"], w["b_conv"], w["ab_par"], w["w_out"])


def _mix_sample_kernel(x_ref, s_ref, cpad_ref, s_all_ref, nmix_ref, wmain_ref, wtail_ref, avg_ref, aog_ref, bog_ref,
                       coef_ref, bias_ref, cw_ref, abp_ref, wout_ref,
                       xo_ref, so_ref, zq_ref, vo_ref,
                       z_ref, ztail_ref, ob_ref, *, n_tok):
    del s_all_ref
    rows = zq_ref.shape[0]
    nb = rows // n_tok
    step = pl.program_id(0)

    @pl.when(step == 0)
    def _():
        xn = _rms(x_ref[...], nmix_ref[...]).astype(BF16)
        z_ref[...] = jnp.dot(xn, wmain_ref[...], preferred_element_type=F32)
        ztail_ref[...] = jnp.dot(xn, wtail_ref[...], preferred_element_type=F32)

    here = pl.ds(pl.multiple_of(step * rows, rows), rows)
    z = z_ref[here, :]
    z_tail = ztail_ref[here, :]
    zab = z_tail[:, :LANES]
    tok = lax.broadcasted_iota(jnp.int32, (rows, 1), 0) % n_tok

    def prev(a, d):
        return pltpu.roll(a, d, axis=0)

    def prev_or_zero(a, d):
        return a if d == 0 else jnp.where(tok >= d, prev(a, d), 0.0)

    def per_head(fn, a):
        return jnp.concatenate([fn(a[:, h * HEAD_DIM:(h + 1) * HEAD_DIM]) for h in range(N_HEADS)], axis=1)

    uv = jax.nn.gelu(z[:, :2 * A_WIDTH])
    vn = per_head(lambda a: _rms(a, avg_ref[...]), uv[:, A_WIDTH:])
    vo_ref[...] = vn
    mixed = bias_ref[...]
    for d in range(n_tok):
        mixed = mixed + coef_ref[d] * prev_or_zero(vn, d)
    ob_ref[here, :A_WIDTH] = per_head(lambda a: _rms(a, aog_ref[...]), uv[:, :A_WIDTH] * mixed)

    zq = z[:, OFF_QKV:OFF_GATE]
    zq_ref[...] = zq
    cpad = cpad_ref[...]
    cw = cw_ref[...]
    y = zq * cw[CONV_W - 1:CONV_W]
    for d in range(1, CONV_W):
        carried = pltpu.roll(cpad, rows - (n_tok - d), axis=0)
        y = y + jnp.where(tok >= d, prev(zq, d), carried) * cw[CONV_W - 1 - d:CONV_W - d]
    qkv = _silu(y)

    abp = abp_ref[...]
    g_all = -jnp.exp(abp[0:1]) * _softplus(zab + abp[1:2])
    beta_all = jax.nn.sigmoid(zab)

    sub = lax.broadcasted_iota(jnp.int32, (8, 1), 0)
    first_half = sub < n_tok
    o_heads, kend_heads, u_heads, btot_heads = [], [], [], []
    for h in range(N_HEADS):
        q = _l2(qkv[:, h * HEAD_DIM:(h + 1) * HEAD_DIM]) * (HEAD_DIM ** -0.5)
        k = _l2(qkv[:, B_WIDTH + h * HEAD_DIM:B_WIDTH + (h + 1) * HEAD_DIM])
        v = qkv[:, 2 * B_WIDTH + h * HEAD_DIM:2 * B_WIDTH + (h + 1) * HEAD_DIM]
        g = jnp.broadcast_to(g_all[:, h:h + 1], (rows, HEAD_DIM))
        beta = jnp.broadcast_to(beta_all[:, N_HEADS + h:N_HEADS + h + 1], (rows, HEAD_DIM))
        gam = g
        for d in range(1, n_tok):
            gam = gam + prev_or_zero(g, d)
        gam_last = jnp.where(tok == n_tok - 1, gam, 0.0)
        for d in range(1, n_tok):
            gam_last = gam_last + jnp.where(tok == n_tok - 1 - d, pltpu.roll(gam, rows - d, axis=0), 0.0)
        eg = jnp.exp(gam)

        def decay_to(d, gam=gam):
            return jnp.exp(jnp.where(tok >= d, gam - prev(gam, d), 0.0))

        a_sub = [None] + [jnp.where(tok >= d, beta * jnp.sum(k * prev(k, d), axis=-1, keepdims=True) * decay_to(d),
                                    0.0) for d in range(1, n_tok)]
        def forward_substitute(rhs, a_sub=a_sub):
            sol = rhs
            for t in range(1, n_tok):
                acc = rhs
                for d in range(1, t + 1):
                    acc = acc - a_sub[d] * prev(sol, d)
                sol = jnp.where(tok == t, acc, sol)
            return sol

        w_blk = forward_substitute(beta * v)
        kb_blk = forward_substitute((beta * eg) * k)
        qb = q * eg

        kb_s, qb_s = [], []
        for p in range(rows // 8):
            kb_t, qb_t = kb_blk[8 * p:8 * p + 8], qb[8 * p:8 * p + 8]
            f0 = _dot(jnp.where(first_half, kb_t, pltpu.roll(qb_t, n_tok, axis=0)), s_ref[2 * p, h])
            f1 = _dot(jnp.where(first_half, pltpu.roll(kb_t, n_tok, axis=0), qb_t), s_ref[2 * p + 1, h])
            kb_s.append(jnp.where(first_half, f0, pltpu.roll(f1, n_tok, axis=0)))
            qb_s.append(jnp.where(first_half, pltpu.roll(f0, n_tok, axis=0), f1))
        u = w_blk - jnp.concatenate(kb_s, axis=0)
        o = jnp.concatenate(qb_s, axis=0)
        for d in range(n_tok):
            qk = jnp.where(tok >= d, jnp.sum(q * prev(k, d), axis=-1, keepdims=True) * decay_to(d), 0.0)
            o = o + qk * prev_or_zero(u, d)
        o_heads.append(o)
        kend_heads.append(k * jnp.exp(gam_last - gam))
        u_heads.append(u)
        btot_heads.append(jnp.broadcast_to(jnp.exp(gam_last), (rows, HEAD_DIM)))

    kend_t = jnp.concatenate(kend_heads, axis=0).T
    u_all = jnp.concatenate(u_heads, axis=0).astype(BF16)
    owner = lax.broadcasted_iota(jnp.int32, (1, N_HEADS * rows), 1) // n_tok
    for h in range(N_HEADS):
        for b in range(nb):
            mine = jnp.where(owner == h * nb + b, kend_t, 0.0).astype(BF16)
            last = b * n_tok + n_tok - 1
            so_ref[b, h] = btot_heads[h][last:last + 1] * s_ref[b, h] + jnp.dot(
                mine, u_all, preferred_element_type=F32)

    gate = z_tail[:, 2 * N_HEADS:]
    ob_ref[here, A_WIDTH:] = per_head(lambda a: _rms(a, bog_ref[...]), jnp.concatenate(o_heads, axis=1)) * _silu(gate)

    @pl.when(step == pl.num_programs(0) - 1)
    def _():
        xo_ref[...] = x_ref[...] + _dot(ob_ref[...], wout_ref[...])


def _mix_sample(x, layer, state_s, cpad, s_all, w, n_tok):
    rows_total = x.shape[0]
    nb = SAMPLE_GROUP
    rows = nb * n_tok
    all_rows = lambda width: pl.BlockSpec((rows_total, width), lambda i: (0, 0))
    row_spec = lambda width: pl.BlockSpec((rows, width), lambda i: (i, 0))
    s_spec = pl.BlockSpec((None, nb, N_HEADS, HEAD_DIM, HEAD_DIM), lambda i: (layer, i, 0, 0, 0))
    in_specs = [all_rows(D_MODEL), s_spec, pl.BlockSpec((None, rows, 3 * B_WIDTH), lambda i: (layer, i, 0)),
                pl.BlockSpec(memory_space=pl.ANY)] + _mixer_weight_specs(layer) + [
        _layer_spec((n_tok, rows, A_WIDTH), layer), _layer_spec((rows, A_WIDTH), layer),
        _layer_spec((CONV_W, 3 * B_WIDTH), layer), _layer_spec((2, LANES), layer), _layer_spec((D_MODEL, D_MODEL), 0)]
    out_specs = [all_rows(D_MODEL), s_spec, row_spec(3 * B_WIDTH), row_spec(A_WIDTH)]
    out_shape = [jax.ShapeDtypeStruct(x.shape, F32), jax.ShapeDtypeStruct(state_s.shape, F32),
                 jax.ShapeDtypeStruct((rows_total, 3 * B_WIDTH), F32), jax.ShapeDtypeStruct((rows_total, A_WIDTH), F32)]
    return pl.pallas_call(
        functools.partial(_mix_sample_kernel, n_tok=n_tok),
        grid=(rows_total // rows,),
        in_specs=in_specs,
        out_specs=out_specs,
        out_shape=out_shape,
        input_output_aliases={3: 1},
        scratch_shapes=[pltpu.VMEM((rows_total, Z_FRONT), F32), pltpu.VMEM((rows_total, Z_TAIL), F32),
                        pltpu.VMEM((rows_total, D_MODEL), F32)],
        compiler_params=pltpu.CompilerParams(dimension_semantics=("arbitrary",),
                                             vmem_limit_bytes=VMEM_LIMIT_BYTES),
    )(x, state_s, cpad, s_all, *_mixer_weights(w), w["a_coef"], w["a_bias"], w["b_conv"], w["ab_par"], w["w_out"])


def _prep_tables(n_tok, norm_ffn1, norm_mix, a_v_gain, a_spatial_w, a_spatial_b, a_out_gain, b_conv_w, b_a_log,
                 b_dt_bias, b_out_gain, norm_ffn2, norm_ple):
    par_pad = jnp.zeros((DEPTH, LANES - N_HEADS), F32)
    ab_par = jnp.stack([jnp.concatenate([b_a_log, par_pad], axis=1),
                        jnp.concatenate([b_dt_bias, par_pad], axis=1)], axis=1)

    def sample_rows(a):
        return jnp.tile(jnp.repeat(jnp.transpose(a, (0, 2, 1)), HEAD_DIM, axis=2), (1, SAMPLE_GROUP, 1))

    ws_small = a_spatial_w[:, :, :n_tok, :n_tok]
    a_coef = jnp.stack([sample_rows(jnp.pad(jnp.diagonal(ws_small, offset=-d, axis1=2, axis2=3),
                                            ((0, 0), (0, 0), (d, 0)))) for d in range(n_tok)], axis=1)
    return dict(
        n_f1=norm_ffn1[:, None], n_mix=norm_mix[:, None], ab_par=ab_par,
        a_v_gain=a_v_gain[:, None], a_out_gain=a_out_gain[:, None], b_out_gain=b_out_gain[:, None],
        a_w_s=a_spatial_w, a_b_s_t=jnp.transpose(a_spatial_b, (0, 2, 1)),
        a_coef=a_coef, a_bias=sample_rows(a_spatial_b[:, :, :n_tok]),
        b_conv=b_conv_w, n_f2=norm_ffn2[:, None], n_ple=norm_ple[:, None],
    )


def kernel(x_prompt, x_sample, state_S, state_conv, p_prompt, p_sample, norm_ffn1, w_ffn1_in, w_ffn1_out, norm_mix, w_in, a_v_gain, a_spatial_w, a_spatial_b, a_out_gain, b_conv_w, b_a_log, b_dt_bias, b_out_gain, w_out, norm_ffn2, w_ffn2_in, w_ffn2_out, norm_ple, w_ple_gate, w_ple_proj, final_norm):
    bsz, length, _ = x_prompt.shape
    dec_bsz, n_tok, _ = x_sample.shape
    assert length % MIX_ROWS == 0 and MIX_ROWS % CHUNK_A == 0 and MIX_ROWS % GROUP == 0
    assert dec_bsz % SAMPLE_GROUP == 0
    assert n_tok % CHUNK_A != 0 and n_tok % CHUNK_D != 0
    assert 2 * n_tok == 8 and N_HEADS * SAMPLE_GROUP * n_tok == HEAD_DIM and n_tok >= CONV_W - 1

    w = _prep_tables(n_tok, norm_ffn1, norm_mix, a_v_gain, a_spatial_w, a_spatial_b, a_out_gain, b_conv_w, b_a_log,
                     b_dt_bias, b_out_gain, norm_ffn2, norm_ple)
    ffn1_w = (w_ffn1_in[0:1].astype(BF16), w_ffn1_out[0:1].astype(BF16))
    in_proj_splits = ((0, Z_FRONT), (Z_FRONT, Z_FRONT + Z_TAIL))
    final = final_norm[None, None]
    xp = x_prompt.reshape(bsz * length, D_MODEL)
    xs = x_sample.reshape(dec_bsz * n_tok, D_MODEL)
    pp = p_prompt.reshape(DEPTH, bsz * length, PLE_DIM)
    ps = p_sample.reshape(DEPTH, dec_bsz * n_tok, PLE_DIM)
    keep = CONV_W - 1
    cpad = jnp.pad(state_conv, ((0, 0), (0, 0), (n_tok - keep, 0), (0, 0))).reshape(DEPTH, dec_bsz * n_tok, 3 * B_WIDTH)

    s_prompt, c_prompt, c_sample, v_sample = [], [], [], []
    s_sample = jnp.zeros(state_S.shape, F32)
    for i in range(DEPTH):
        last = dict(final_gain=final) if i == DEPTH - 1 else {}

        xp, (f2_in, f2_out, w_front, w_tail, w_o, ple_gate, ple_proj) = _ffn(
            xp, i, w["n_f1"], *ffn1_w,
            casts=[(w_ffn2_in, i, None), (w_ffn2_out, i, None), (w_in, i, in_proj_splits), (w_out, i, None),
                   (w_ple_gate, i, None), (w_ple_proj, i, None)])
        xs = _ffn(xs, i, w["n_f1"], *ffn1_w)
        mix_w = dict(w, w_front=w_front, w_tail=w_tail, w_out=w_o)
        xp, sp, cp = _mix_prompt(xp.reshape(bsz, length, D_MODEL), i, mix_w)
        xs, s_sample, zq, vs = _mix_sample(xs, i, state_S, cpad, s_sample, mix_w, n_tok)
        xp = xp.reshape(bsz * length, D_MODEL)
        ple = (w["n_ple"], ple_gate, ple_proj)
        if i < DEPTH - 1:
            xp, ffn1_w = _ffn(xp, i, w["n_f2"], f2_in, f2_out, ple=(pp,) + ple,
                              casts=[(w_ffn1_in, i + 1, None), (w_ffn1_out, i + 1, None)])
        else:
            xp = _ffn(xp, i, w["n_f2"], f2_in, f2_out, ple=(pp,) + ple, **last)
        xs = _ffn(xs, i, w["n_f2"], f2_in, f2_out, ple=(ps,) + ple, **last)

        s_prompt.append(sp)
        c_prompt.append(cp)
        c_sample.append(zq.reshape(dec_bsz, n_tok, 3 * B_WIDTH)[:, n_tok - keep:])
        v_sample.append(vs.reshape(dec_bsz, n_tok, N_HEADS, HEAD_DIM))

    return (xp.reshape(bsz, length, D_MODEL), xs.reshape(dec_bsz, n_tok, D_MODEL), jnp.stack(s_prompt),
            jnp.stack(c_prompt), s_sample, jnp.stack(c_sample), jnp.stack(v_sample))
```

```python
import functools

import jax
import jax.numpy as jnp
from jax import lax
from jax.experimental import pallas as pl
from jax.experimental.pallas import tpu as pltpu

F32 = jnp.float32
BF16 = jnp.bfloat16
EPS = 1e-6

D_MODEL = 1024
D_FF = 2816
DEPTH = 4
N_HEADS = 4
HEAD_DIM = 128
A_WIDTH = N_HEADS * HEAD_DIM
B_WIDTH = N_HEADS * HEAD_DIM
CHUNK_A = 128
CHUNK_D = 64
CONV_W = 4
PLE_DIM = 256
Z_FRONT = 2 * A_WIDTH + 3 * B_WIDTH
Z_TAIL = 2 * N_HEADS + B_WIDTH
OFF_QKV = 2 * A_WIDTH
OFF_GATE = OFF_QKV + 3 * B_WIDTH

VMEM_LIMIT_BYTES = 52 * 1024 * 1024
LANES = 128
BF16_SUBLANES = 16
MXU_N = 256
FFN_ROWS = 512
GROUP = 4 * CHUNK_D
MIX_ROWS = 512
SAMPLE_GROUP = 8


def _rms(x, gain):
    return x * lax.rsqrt(jnp.mean(x * x, axis=-1, keepdims=True) + EPS) * gain


def _l2(x):
    return x * lax.rsqrt(jnp.sum(x * x, axis=-1, keepdims=True) + EPS)


def _silu(x):
    return x * jax.nn.sigmoid(x)


def _softplus(x):
    return jnp.maximum(x, 0.0) + jnp.log1p(jnp.exp(-jnp.abs(x)))


def _dot(a, b):
    return jnp.dot(a.astype(BF16), b.astype(BF16), preferred_element_type=F32)


def _dot_nt(a, b):
    return lax.dot_general(a.astype(BF16), b.astype(BF16), (((1,), (1,)), ((), ())),
                           preferred_element_type=F32)


def _split3(a):
    p1 = a.astype(BF16)
    r1 = a - p1.astype(F32)
    p2 = r1.astype(BF16)
    p3 = (r1 - p2.astype(F32)).astype(BF16)
    return p1, p2, p3


def _layer_spec(shape, layer, block=None):
    index = (layer,) + (0,) * (len(shape) - 1) + (0 if block is None else block,)
    return pl.BlockSpec((None,) + tuple(shape), lambda *_: index, pipeline_mode=pl.Buffered(1))


def _ffn_kernel(*refs, with_ple, with_final, cast_plan):
    x_ref, gain_ref, wg_ref, wu_ref, wo_ref = refs[:5]
    n_in = 5 + 4 * with_ple + with_final
    cast_in = refs[n_in:n_in + len(cast_plan)]
    o_ref = refs[n_in + len(cast_plan)]
    cast_out = iter(refs[n_in + len(cast_plan) + 1:])
    for src_ref, splits in zip(cast_in, cast_plan):
        for lo, hi in splits:
            next(cast_out)[...] = src_ref[:, lo:hi].astype(BF16)
    x = x_ref[...]
    xn = _rms(x, gain_ref[...]).astype(BF16)
    acc = jnp.zeros_like(x)
    for c in range(D_FF // MXU_N):
        sl = slice(c * MXU_N, (c + 1) * MXU_N)
        gate = jnp.dot(xn, wg_ref[:, sl], preferred_element_type=F32)
        up = jnp.dot(xn, wu_ref[:, sl], preferred_element_type=F32)
        h = (_silu(gate) * up).astype(BF16)
        acc = acc + jnp.dot(h, wo_ref[sl, :], preferred_element_type=F32)
    x = x + 0.5 * acc
    if with_ple:
        p_ref, npl_ref, wpg_ref, wpp_ref = refs[5:9]
        emb = _dot(p_ref[...], wpp_ref[...])
        gate = _dot(_rms(x, npl_ref[...]), wpg_ref[...])
        x = x + emb * jax.nn.sigmoid(gate)
    if with_final:
        x = _rms(x, refs[9][...])
    o_ref[...] = x


def _ffn_streamed_kernel(*refs, with_ple, with_final):
    x_ref, gain_ref, wg_ref, wu_ref, wo_ref = refs[:5]
    o_ref, xn_ref, acc_ref = refs[-3:]
    chunk = pl.program_id(0)

    @pl.when(chunk == 0)
    def _():
        xn_ref[...] = _rms(x_ref[...], gain_ref[...]).astype(BF16)
        acc_ref[...] = jnp.zeros_like(acc_ref)

    xn = xn_ref[...]
    gate = jnp.dot(xn, wg_ref[...], preferred_element_type=F32)
    up = jnp.dot(xn, wu_ref[...], preferred_element_type=F32)
    acc_ref[...] += jnp.dot((_silu(gate) * up).astype(BF16), wo_ref[...], preferred_element_type=F32)

    @pl.when(chunk == pl.num_programs(0) - 1)
    def _():
        x = x_ref[...] + 0.5 * acc_ref[...]
        if with_ple:
            p_ref, npl_ref, wpg_ref, wpp_ref = refs[5:9]
            emb = _dot(p_ref[...], wpp_ref[...])
            gate = _dot(_rms(x, npl_ref[...]), wpg_ref[...])
            x = x + emb * jax.nn.sigmoid(gate)
        if with_final:
            x = _rms(x, refs[9][...])
        o_ref[...] = x


def _cast_row_blocks(n_rows, n_steps):
    blocks = n_steps
    while n_rows % blocks or (n_rows // blocks) % BF16_SUBLANES:
        assert blocks % 2 == 0, (n_rows, n_steps)
        blocks //= 2
    return blocks


def _ffn(x, layer, gain, w_in, w_out, ple=None, final_gain=None, casts=()):
    rows = x.shape[0]
    tm = min(FFN_ROWS, rows)
    n_steps = rows // tm
    streamed = n_steps == 1
    assert not (streamed and casts)
    n_chunks = D_FF // MXU_N
    if streamed:
        row_spec = pl.BlockSpec((tm, D_MODEL), lambda c: (0, 0))
        p_rows = lambda c: (layer, 0, 0)
        in_specs = [row_spec, _layer_spec((1, D_MODEL), layer),
                    pl.BlockSpec((None, D_MODEL, MXU_N), lambda c: (0, 0, c)),
                    pl.BlockSpec((None, D_MODEL, MXU_N), lambda c: (0, 0, n_chunks + c)),
                    pl.BlockSpec((None, MXU_N, D_MODEL), lambda c: (0, c, 0))]
    else:
        row_spec = pl.BlockSpec((tm, D_MODEL), lambda i: (i, 0))
        p_rows = lambda i: (layer, i, 0)
        in_specs = [row_spec, _layer_spec((1, D_MODEL), layer),
                    _layer_spec((D_MODEL, D_FF), 0, block=0), _layer_spec((D_MODEL, D_FF), 0, block=1),
                    _layer_spec((D_FF, D_MODEL), 0)]
    args = [x, gain, w_in, w_in, w_out]
    out_specs = [row_spec]
    out_shape = [jax.ShapeDtypeStruct(x.shape, F32)]
    if ple is not None:
        p, n_ple, w_gate, w_proj = ple
        in_specs += [pl.BlockSpec((None, tm, PLE_DIM), p_rows), _layer_spec((1, D_MODEL), layer),
                     _layer_spec((D_MODEL, D_MODEL), 0), _layer_spec((PLE_DIM, D_MODEL), 0)]
        args += [p, n_ple, w_gate, w_proj]
    if final_gain is not None:
        in_specs.append(_layer_spec((1, D_MODEL), 0))
        args.append(final_gain)
    cast_plan = []
    for src, layer_c, splits in casts:
        _, n_rows, n_cols = src.shape
        splits = tuple(splits) if splits is not None else ((0, n_cols),)
        blocks = _cast_row_blocks(n_rows, n_steps)
        block_of = functools.partial(lambda i, every: i // every, every=n_steps // blocks)
        in_specs.append(pl.BlockSpec((None, n_rows // blocks, n_cols),
                                     lambda i, layer_c=layer_c, block_of=block_of: (layer_c, block_of(i), 0)))
        args.append(src)
        for lo, hi in splits:
            out_specs.append(pl.BlockSpec((None, n_rows // blocks, hi - lo),
                                          lambda i, block_of=block_of: (0, block_of(i), 0)))
            out_shape.append(jax.ShapeDtypeStruct((1, n_rows, hi - lo), BF16))
        cast_plan.append(splits)
    flags = dict(with_ple=ple is not None, with_final=final_gain is not None)
    out = pl.pallas_call(
        functools.partial(_ffn_streamed_kernel, **flags) if streamed else
        functools.partial(_ffn_kernel, cast_plan=tuple(cast_plan), **flags),
        grid=(n_chunks if streamed else n_steps,),
        in_specs=in_specs,
        out_specs=out_specs,
        out_shape=out_shape,
        scratch_shapes=[pltpu.VMEM((tm, D_MODEL), BF16), pltpu.VMEM((tm, D_MODEL), F32)] if streamed else [],
        compiler_params=pltpu.CompilerParams(dimension_semantics=("arbitrary",),
                                             vmem_limit_bytes=VMEM_LIMIT_BYTES),
    )(*args)
    return (out[0], out[1:]) if casts else out[0]


def _split(a):
    hi = a.astype(BF16)
    lo = (a - hi.astype(F32)).astype(BF16)
    return hi, lo


def _dot3(a_hi, a_lo, b_hi, b_lo):
    m = a_hi.shape[0]
    both = jnp.dot(jnp.concatenate([a_hi, a_lo], axis=0), b_hi, preferred_element_type=F32)
    return both[:m] + both[m:] + jnp.dot(a_hi, b_lo, preferred_element_type=F32)


def _fold(block_diag):
    n = block_diag.shape[0] // CHUNK_D
    out = block_diag[0:CHUNK_D]
    for g in range(1, n):
        out = out + block_diag[g * CHUNK_D:(g + 1) * CHUNK_D]
    return out


def _expand(packed, diag_ones_ref):
    n = packed.shape[1] // CHUNK_D
    return jnp.concatenate([packed] * n, axis=0) * diag_ones_ref[...]


def _unit_lower_inverses_packed(l_packed_list, diag_ones_ref):
    c, width = l_packed_list[0].shape
    row = lax.broadcasted_iota(jnp.int32, (c, width), 0)
    col = lax.broadcasted_iota(jnp.int32, (c, width), 1) % c
    zero = jnp.zeros((), BF16)

    def lower_left(bs):
        return (row // (2 * bs) == col // (2 * bs)) & ((row // bs) % 2 == 1) & ((col // bs) % 2 == 0)

    l_bf = [l.astype(BF16) for l in l_packed_list]
    xs = [jnp.where(row == col, 1.0, 0.0) - jnp.where(lower_left(1), l, 0.0) for l in l_packed_list]
    bs = 2
    while bs < c:
        sel = lower_left(bs)
        x_bf = [x.astype(BF16) for x in xs]
        ys = [jnp.dot(jnp.where(sel, l, zero), _expand(x, diag_ones_ref), preferred_element_type=F32)
              for l, x in zip(l_bf, x_bf)]
        xs = [x - jnp.dot(xb, _expand(y.astype(BF16), diag_ones_ref), preferred_element_type=F32)
              for x, xb, y in zip(xs, x_bf, ys)]
        bs *= 2
    return xs


def _mix_prompt_kernel(x_ref, nmix_ref, wmain_ref, wtail_ref, avg_ref, aog_ref, bog_ref, wsp_ref, bsp_ref,
                       cw_ref, abp_ref, wout_ref,
                       xo_ref, s_ref, ct_ref,
                       zlast_ref, ob_ref, bd_ref, wgate_ref):
    tl = x_ref.shape[0]
    step = pl.program_id(1)

    @pl.when((pl.program_id(0) == 0) & (step == 0))
    def _():
        wgate_ref[...] = wtail_ref[:, 2 * N_HEADS:]

    @pl.when(step == 0)
    def _():
        s_ref[...] = jnp.zeros_like(s_ref)
        zlast_ref[...] = jnp.zeros_like(zlast_ref)

    x = x_ref[...]
    xn = _rms(x, nmix_ref[...]).astype(BF16)
    def in_proj(lo, hi):
        return jnp.dot(xn, wmain_ref[:, lo:hi], preferred_element_type=F32)

    zab = jnp.dot(xn, wtail_ref[:, :LANES], preferred_element_type=F32)
    first_tile = lax.broadcasted_iota(jnp.int32, (8, 1), 0)
    qkv_parts = []
    for part in range(3):
        cols = slice(part * B_WIDTH, (part + 1) * B_WIDTH)
        zc = in_proj(OFF_QKV + part * B_WIDTH, OFF_QKV + (part + 1) * B_WIDTH)
        cw = cw_ref[:, cols]
        carried = zlast_ref[:, cols]
        y = zc * cw[CONV_W - 1:CONV_W]
        for d in range(1, CONV_W):
            rolled = pltpu.roll(zc, d, axis=0)
            top = jnp.where(first_tile < d, pltpu.roll(carried, d, axis=0), rolled[0:8])
            y = y + jnp.concatenate([top, rolled[8:]], axis=0) * cw[CONV_W - 1 - d:CONV_W - d]
        zlast_ref[:, cols] = zc[tl - 8:tl]
        ct_ref[:, cols] = zc[tl - (CONV_W - 1):tl]
        qkv_parts.append(_silu(y))

    uv = jax.nn.gelu(in_proj(0, OFF_QKV))
    row = lax.broadcasted_iota(jnp.int32, (CHUNK_A, CHUNK_A), 0)
    col = lax.broadcasted_iota(jnp.int32, (CHUNK_A, CHUNK_A), 1)
    causal = col <= row
    for h in range(N_HEADS):
        hs = slice(h * HEAD_DIM, (h + 1) * HEAD_DIM)
        u_h = uv[:, hs]
        v_h = _rms(uv[:, A_WIDTH + h * HEAD_DIM:A_WIDTH + (h + 1) * HEAD_DIM], avg_ref[...]).astype(BF16)
        w_h = jnp.where(causal, wsp_ref[h], 0.0).astype(BF16)
        bias_h = bsp_ref[:, h:h + 1]
        for c in range(tl // CHUNK_A):
            rs = slice(c * CHUNK_A, (c + 1) * CHUNK_A)
            mixed = jnp.dot(w_h, v_h[rs], preferred_element_type=F32) + bias_h
            ob_ref[rs, hs] = _rms(u_h[rs] * mixed, aog_ref[...]).astype(BF16)

    z_gate = jnp.dot(xn, wgate_ref[...], preferred_element_type=F32)
    abp = abp_ref[...]
    g = -jnp.exp(abp[0:1]) * _softplus(zab + abp[1:2])
    beta = jax.nn.sigmoid(zab)

    r2 = lax.broadcasted_iota(jnp.int32, (GROUP, GROUP), 0)
    c2 = lax.broadcasted_iota(jnp.int32, (GROUP, GROUP), 1)
    same = (r2 // CHUNK_D) == (c2 // CHUNK_D)
    strict_bd = same & (c2 < r2)
    col_ones = jnp.concatenate([jnp.where(same & (c2 <= r2), 1.0, 0.0), jnp.where(same, 1.0, 0.0)],
                               axis=0).astype(BF16)
    upper_ones = jnp.where(same & (r2 <= c2), 1.0, 0.0).astype(BF16)
    bd_ref[...] = jnp.where(same, 1.0, 0.0).astype(BF16)
    n_grp = tl // GROUP
    gam_parts, glast_parts, gam_t = [], [], []
    for gi in range(n_grp):
        g_grp = g[gi * GROUP:(gi + 1) * GROUP]
        by_col = jnp.dot(col_ones, jnp.concatenate(_split3(g_grp), axis=1), preferred_element_type=F32)
        by_col = by_col[:, :LANES] + by_col[:, LANES:2 * LANES] + by_col[:, 2 * LANES:]
        gam_parts.append(by_col[:GROUP])
        glast_parts.append(by_col[GROUP:])
        by_row = jnp.dot(jnp.concatenate(_split3(g_grp.T), axis=0), upper_ones, preferred_element_type=F32)
        gam_t.append(by_row[:LANES] + by_row[LANES:2 * LANES] + by_row[2 * LANES:])
    gam = jnp.concatenate(gam_parts, axis=0)
    glast = jnp.concatenate(glast_parts, axis=0)

    rb = lax.broadcasted_iota(jnp.int32, (CHUNK_D, CHUNK_D), 0)
    cb = lax.broadcasted_iota(jnp.int32, (CHUNK_D, CHUNK_D), 1)
    incl = cb <= rb

    heads, a_packed, rhs = [], [], []
    for h in range(N_HEADS):
        q_h = _l2(qkv_parts[0][:, h * HEAD_DIM:(h + 1) * HEAD_DIM]) * (HEAD_DIM ** -0.5)
        k_h = _l2(qkv_parts[1][:, h * HEAD_DIM:(h + 1) * HEAD_DIM])
        v_h = qkv_parts[2][:, h * HEAD_DIM:(h + 1) * HEAD_DIM]
        gc_h = gam[:, h:h + 1]
        gl_h = glast[:, h:h + 1]
        bc_h = beta[:, N_HEADS + h:N_HEADS + h + 1]
        eg_h = jnp.exp(gc_h)
        for gi in range(n_grp):
            gs = slice(gi * GROUP, (gi + 1) * GROUP)
            kk = _dot_nt(k_h[gs], k_h[gs])
            decay = jnp.exp(jnp.where(strict_bd, gc_h[gs] - gam_t[gi][h:h + 1, :], 0.0))
            a_packed.append(_fold(jnp.where(strict_bd, bc_h[gs] * kk * decay, 0.0)))
        rhs.append(_split(jnp.concatenate([bc_h * v_h, (bc_h * eg_h) * k_h], axis=1)))
        heads.append((q_h * eg_h, q_h, k_h, k_h * jnp.exp(gl_h - gc_h), gc_h, jnp.exp(gl_h)))
    inv_split = [_split(inv) for inv in _unit_lower_inverses_packed(a_packed, bd_ref)]
    sol = []
    for h in range(N_HEADS):
        sol.append([_dot3(_expand(inv_split[h * n_grp + gi][0], bd_ref), _expand(inv_split[h * n_grp + gi][1], bd_ref),
                          rhs[h][0][gi * GROUP:(gi + 1) * GROUP], rhs[h][1][gi * GROUP:(gi + 1) * GROUP])
                    for gi in range(n_grp)])

    for i in range(tl // CHUNK_D):
        rs = slice(i * CHUNK_D, (i + 1) * CHUNK_D)
        gi, j = divmod(i, GROUP // CHUNK_D)
        ls = slice(j * CHUNK_D, (j + 1) * CHUNK_D)
        for h in range(N_HEADS):
            qb_h, q_h, k_h, kend_h, gc_h, btot_h = heads[h]
            decay = jnp.where(incl, jnp.exp(jnp.where(incl, gc_h[rs] - gam_t[gi][h:h + 1, ls], 0.0)), 0.0)
            qk = _dot_nt(q_h[rs], k_h[rs]) * decay
            s_old = s_ref[h]
            from_s = _dot(jnp.concatenate([sol[h][gi][ls, HEAD_DIM:], qb_h[rs]], axis=0), s_old)
            u = sol[h][gi][ls, :HEAD_DIM] - from_s[:CHUNK_D]
            from_u = _dot(jnp.concatenate([qk, kend_h[rs].T], axis=0), u)
            o = from_s[CHUNK_D:] + from_u[:CHUNK_D]
            s_ref[h] = btot_h[i * CHUNK_D:i * CHUNK_D + 1] * s_old + from_u[CHUNK_D:]
            gate = z_gate[rs, h * HEAD_DIM:(h + 1) * HEAD_DIM]
            ob_ref[rs, A_WIDTH + h * HEAD_DIM:A_WIDTH + (h + 1) * HEAD_DIM] = (
                _rms(o, bog_ref[...]) * _silu(gate)).astype(BF16)

    xo_ref[...] = x_ref[...] + jnp.dot(ob_ref[...], wout_ref[...], preferred_element_type=F32)


def _mixer_weight_specs(layer):
    return [_layer_spec((1, D_MODEL), layer), _layer_spec((D_MODEL, Z_FRONT), 0), _layer_spec((D_MODEL, Z_TAIL), 0),
            _layer_spec((1, HEAD_DIM), layer), _layer_spec((1, HEAD_DIM), layer), _layer_spec((1, HEAD_DIM), layer)]


def _mixer_weights(w):
    return [w["n_mix"], w["w_front"], w["w_tail"], w["a_v_gain"], w["a_out_gain"], w["b_out_gain"]]


def _mix_prompt(x, layer, w):
    bsz, length, _ = x.shape
    tl = MIX_ROWS
    row_spec = pl.BlockSpec((None, tl, D_MODEL), lambda b, t: (b, t, 0))
    in_specs = [row_spec] + _mixer_weight_specs(layer) + [
        _layer_spec((N_HEADS, CHUNK_A, CHUNK_A), layer), _layer_spec((CHUNK_A, N_HEADS), layer),
        _layer_spec((CONV_W, 3 * B_WIDTH), layer), _layer_spec((2, LANES), layer), _layer_spec((D_MODEL, D_MODEL), 0)]
    out_specs = [row_spec,
                 pl.BlockSpec((None, N_HEADS, HEAD_DIM, HEAD_DIM), lambda b, t: (b, 0, 0, 0)),
                 pl.BlockSpec((None, CONV_W - 1, 3 * B_WIDTH), lambda b, t: (b, 0, 0))]
    out_shape = [jax.ShapeDtypeStruct(x.shape, F32),
                 jax.ShapeDtypeStruct((bsz, N_HEADS, HEAD_DIM, HEAD_DIM), F32),
                 jax.ShapeDtypeStruct((bsz, CONV_W - 1, 3 * B_WIDTH), F32)]
    return pl.pallas_call(
        _mix_prompt_kernel,
        grid=(bsz, length // tl),
        in_specs=in_specs,
        out_specs=out_specs,
        out_shape=out_shape,
        scratch_shapes=[pltpu.VMEM((8, 3 * B_WIDTH), F32), pltpu.VMEM((tl, D_MODEL), BF16),
                        pltpu.VMEM((GROUP, GROUP), BF16), pltpu.VMEM((D_MODEL, B_WIDTH), BF16)],
        compiler_params=pltpu.CompilerParams(dimension_semantics=("arbitrary", "arbitrary"),
                                             vmem_limit_bytes=VMEM_LIMIT_BYTES),
    )(x, *_mixer_weights(w), w["a_w_s"], w["a_b_s_t"], w["b_conv"], w["ab_par"], w["w_out"])


def _mix_sample_kernel(x_ref, s_ref, cpad_ref, s_all_ref, nmix_ref, wmain_ref, wtail_ref, avg_ref, aog_ref, bog_ref,
                       coef_ref, bias_ref, cw_ref, abp_ref, wout_ref,
                       xo_ref, so_ref, zq_ref, vo_ref,
                       z_ref, ztail_ref, ob_ref, *, n_tok):
    del s_all_ref
    rows = zq_ref.shape[0]
    nb = rows // n_tok
    step = pl.program_id(0)

    @pl.when(step == 0)
    def _():
        xn = _rms(x_ref[...], nmix_ref[...]).astype(BF16)
        z_ref[...] = jnp.dot(xn, wmain_ref[...], preferred_element_type=F32)
        ztail_ref[...] = jnp.dot(xn, wtail_ref[...], preferred_element_type=F32)

    here = pl.ds(pl.multiple_of(step * rows, rows), rows)
    z = z_ref[here, :]
    z_tail = ztail_ref[here, :]
    zab = z_tail[:, :LANES]
    tok = lax.broadcasted_iota(jnp.int32, (rows, 1), 0) % n_tok

    def prev(a, d):
        return pltpu.roll(a, d, axis=0)

    def prev_or_zero(a, d):
        return a if d == 0 else jnp.where(tok >= d, prev(a, d), 0.0)

    def per_head(fn, a):
        return jnp.concatenate([fn(a[:, h * HEAD_DIM:(h + 1) * HEAD_DIM]) for h in range(N_HEADS)], axis=1)

    uv = jax.nn.gelu(z[:, :2 * A_WIDTH])
    vn = per_head(lambda a: _rms(a, avg_ref[...]), uv[:, A_WIDTH:])
    vo_ref[...] = vn
    mixed = bias_ref[...]
    for d in range(n_tok):
        mixed = mixed + coef_ref[d] * prev_or_zero(vn, d)
    ob_ref[here, :A_WIDTH] = per_head(lambda a: _rms(a, aog_ref[...]), uv[:, :A_WIDTH] * mixed)

    zq = z[:, OFF_QKV:OFF_GATE]
    zq_ref[...] = zq
    cpad = cpad_ref[...]
    cw = cw_ref[...]
    y = zq * cw[CONV_W - 1:CONV_W]
    for d in range(1, CONV_W):
        carried = pltpu.roll(cpad, rows - (n_tok - d), axis=0)
        y = y + jnp.where(tok >= d, prev(zq, d), carried) * cw[CONV_W - 1 - d:CONV_W - d]
    qkv = _silu(y)

    abp = abp_ref[...]
    g_all = -jnp.exp(abp[0:1]) * _softplus(zab + abp[1:2])
    beta_all = jax.nn.sigmoid(zab)

    sub = lax.broadcasted_iota(jnp.int32, (8, 1), 0)
    first_half = sub < n_tok
    o_heads, kend_heads, u_heads, btot_heads = [], [], [], []
    for h in range(N_HEADS):
        q = _l2(qkv[:, h * HEAD_DIM:(h + 1) * HEAD_DIM]) * (HEAD_DIM ** -0.5)
        k = _l2(qkv[:, B_WIDTH + h * HEAD_DIM:B_WIDTH + (h + 1) * HEAD_DIM])
        v = qkv[:, 2 * B_WIDTH + h * HEAD_DIM:2 * B_WIDTH + (h + 1) * HEAD_DIM]
        g = jnp.broadcast_to(g_all[:, h:h + 1], (rows, HEAD_DIM))
        beta = jnp.broadcast_to(beta_all[:, N_HEADS + h:N_HEADS + h + 1], (rows, HEAD_DIM))
        gam = g
        for d in range(1, n_tok):
            gam = gam + prev_or_zero(g, d)
        gam_last = jnp.where(tok == n_tok - 1, gam, 0.0)
        for d in range(1, n_tok):
            gam_last = gam_last + jnp.where(tok == n_tok - 1 - d, pltpu.roll(gam, rows - d, axis=0), 0.0)
        eg = jnp.exp(gam)

        def decay_to(d, gam=gam):
            return jnp.exp(jnp.where(tok >= d, gam - prev(gam, d), 0.0))

        a_sub = [None] + [jnp.where(tok >= d, beta * jnp.sum(k * prev(k, d), axis=-1, keepdims=True) * decay_to(d),
                                    0.0) for d in range(1, n_tok)]
        def forward_substitute(rhs, a_sub=a_sub):
            sol = rhs
            for t in range(1, n_tok):
                acc = rhs
                for d in range(1, t + 1):
                    acc = acc - a_sub[d] * prev(sol, d)
                sol = jnp.where(tok == t, acc, sol)
            return sol

        w_blk = forward_substitute(beta * v)
        kb_blk = forward_substitute((beta * eg) * k)
        qb = q * eg

        kb_s, qb_s = [], []
        for p in range(rows // 8):
            kb_t, qb_t = kb_blk[8 * p:8 * p + 8], qb[8 * p:8 * p + 8]
            f0 = _dot(jnp.where(first_half, kb_t, pltpu.roll(qb_t, n_tok, axis=0)), s_ref[2 * p, h])
            f1 = _dot(jnp.where(first_half, pltpu.roll(kb_t, n_tok, axis=0), qb_t), s_ref[2 * p + 1, h])
            kb_s.append(jnp.where(first_half, f0, pltpu.roll(f1, n_tok, axis=0)))
            qb_s.append(jnp.where(first_half, pltpu.roll(f0, n_tok, axis=0), f1))
        u = w_blk - jnp.concatenate(kb_s, axis=0)
        o = jnp.concatenate(qb_s, axis=0)
        for d in range(n_tok):
            qk = jnp.where(tok >= d, jnp.sum(q * prev(k, d), axis=-1, keepdims=True) * decay_to(d), 0.0)
            o = o + qk * prev_or_zero(u, d)
        o_heads.append(o)
        kend_heads.append(k * jnp.exp(gam_last - gam))
        u_heads.append(u)
        btot_heads.append(jnp.broadcast_to(jnp.exp(gam_last), (rows, HEAD_DIM)))

    kend_t = jnp.concatenate(kend_heads, axis=0).T
    u_all = jnp.concatenate(u_heads, axis=0).astype(BF16)
    owner = lax.broadcasted_iota(jnp.int32, (1, N_HEADS * rows), 1) // n_tok
    for h in range(N_HEADS):
        for b in range(nb):
            mine = jnp.where(owner == h * nb + b, kend_t, 0.0).astype(BF16)
            last = b * n_tok + n_tok - 1
            so_ref[b, h] = btot_heads[h][last:last + 1] * s_ref[b, h] + jnp.dot(
                mine, u_all, preferred_element_type=F32)

    gate = z_tail[:, 2 * N_HEADS:]
    ob_ref[here, A_WIDTH:] = per_head(lambda a: _rms(a, bog_ref[...]), jnp.concatenate(o_heads, axis=1)) * _silu(gate)

    @pl.when(step == pl.num_programs(0) - 1)
    def _():
        xo_ref[...] = x_ref[...] + _dot(ob_ref[...], wout_ref[...])


def _mix_sample(x, layer, state_s, cpad, s_all, w, n_tok):
    rows_total = x.shape[0]
    nb = SAMPLE_GROUP
    rows = nb * n_tok
    all_rows = lambda width: pl.BlockSpec((rows_total, width), lambda i: (0, 0))
    row_spec = lambda width: pl.BlockSpec((rows, width), lambda i: (i, 0))
    s_spec = pl.BlockSpec((None, nb, N_HEADS, HEAD_DIM, HEAD_DIM), lambda i: (layer, i, 0, 0, 0))
    in_specs = [all_rows(D_MODEL), s_spec, pl.BlockSpec((None, rows, 3 * B_WIDTH), lambda i: (layer, i, 0)),
                pl.BlockSpec(memory_space=pl.ANY)] + _mixer_weight_specs(layer) + [
        _layer_spec((n_tok, rows, A_WIDTH), layer), _layer_spec((rows, A_WIDTH), layer),
        _layer_spec((CONV_W, 3 * B_WIDTH), layer), _layer_spec((2, LANES), layer), _layer_spec((D_MODEL, D_MODEL), 0)]
    out_specs = [all_rows(D_MODEL), s_spec, row_spec(3 * B_WIDTH), row_spec(A_WIDTH)]
    out_shape = [jax.ShapeDtypeStruct(x.shape, F32), jax.ShapeDtypeStruct(state_s.shape, F32),
                 jax.ShapeDtypeStruct((rows_total, 3 * B_WIDTH), F32), jax.ShapeDtypeStruct((rows_total, A_WIDTH), F32)]
    return pl.pallas_call(
        functools.partial(_mix_sample_kernel, n_tok=n_tok),
        grid=(rows_total // rows,),
        in_specs=in_specs,
        out_specs=out_specs,
        out_shape=out_shape,
        input_output_aliases={3: 1},
        scratch_shapes=[pltpu.VMEM((rows_total, Z_FRONT), F32), pltpu.VMEM((rows_total, Z_TAIL), F32),
                        pltpu.VMEM((rows_total, D_MODEL), F32)],
        compiler_params=pltpu.CompilerParams(dimension_semantics=("arbitrary",),
                                             vmem_limit_bytes=VMEM_LIMIT_BYTES),
    )(x, state_s, cpad, s_all, *_mixer_weights(w), w["a_coef"], w["a_bias"], w["b_conv"], w["ab_par"], w["w_out"])


def _prep_tables(n_tok, norm_ffn1, norm_mix, a_v_gain, a_spatial_w, a_spatial_b, a_out_gain, b_conv_w, b_a_log,
                 b_dt_bias, b_out_gain, norm_ffn2, norm_ple):
    par_pad = jnp.zeros((DEPTH, LANES - N_HEADS), F32)
    ab_par = jnp.stack([jnp.concatenate([b_a_log, par_pad], axis=1),
                        jnp.concatenate([b_dt_bias, par_pad], axis=1)], axis=1)

    def sample_rows(a):
        return jnp.tile(jnp.repeat(jnp.transpose(a, (0, 2, 1)), HEAD_DIM, axis=2), (1, SAMPLE_GROUP, 1))

    ws_small = a_spatial_w[:, :, :n_tok, :n_tok]
    a_coef = jnp.stack([sample_rows(jnp.pad(jnp.diagonal(ws_small, offset=-d, axis1=2, axis2=3),
                                            ((0, 0), (0, 0), (d, 0)))) for d in range(n_tok)], axis=1)
    return dict(
        n_f1=norm_ffn1[:, None], n_mix=norm_mix[:, None], ab_par=ab_par,
        a_v_gain=a_v_gain[:, None], a_out_gain=a_out_gain[:, None], b_out_gain=b_out_gain[:, None],
        a_w_s=a_spatial_w, a_b_s_t=jnp.transpose(a_spatial_b, (0, 2, 1)),
        a_coef=a_coef, a_bias=sample_rows(a_spatial_b[:, :, :n_tok]),
        b_conv=b_conv_w, n_f2=norm_ffn2[:, None], n_ple=norm_ple[:, None],
    )


def kernel(x_prompt, x_sample, state_S, state_conv, p_prompt, p_sample, norm_ffn1, w_ffn1_in, w_ffn1_out, norm_mix, w_in, a_v_gain, a_spatial_w, a_spatial_b, a_out_gain, b_conv_w, b_a_log, b_dt_bias, b_out_gain, w_out, norm_ffn2, w_ffn2_in, w_ffn2_out, norm_ple, w_ple_gate, w_ple_proj, final_norm):
    bsz, length, _ = x_prompt.shape
    dec_bsz, n_tok, _ = x_sample.shape
    assert length % MIX_ROWS == 0 and MIX_ROWS % CHUNK_A == 0 and MIX_ROWS % GROUP == 0
    assert dec_bsz % SAMPLE_GROUP == 0
    assert n_tok % CHUNK_A != 0 and n_tok % CHUNK_D != 0
    assert 2 * n_tok == 8 and N_HEADS * SAMPLE_GROUP * n_tok == HEAD_DIM and n_tok >= CONV_W - 1

    w = _prep_tables(n_tok, norm_ffn1, norm_mix, a_v_gain, a_spatial_w, a_spatial_b, a_out_gain, b_conv_w, b_a_log,
                     b_dt_bias, b_out_gain, norm_ffn2, norm_ple)
    ffn1_w = (w_ffn1_in[0:1].astype(BF16), w_ffn1_out[0:1].astype(BF16))
    in_proj_splits = ((0, Z_FRONT), (Z_FRONT, Z_FRONT + Z_TAIL))
    final = final_norm[None, None]
    xp = x_prompt.reshape(bsz * length, D_MODEL)
    xs = x_sample.reshape(dec_bsz * n_tok, D_MODEL)
    pp = p_prompt.reshape(DEPTH, bsz * length, PLE_DIM)
    ps = p_sample.reshape(DEPTH, dec_bsz * n_tok, PLE_DIM)
    keep = CONV_W - 1
    cpad = jnp.pad(state_conv, ((0, 0), (0, 0), (n_tok - keep, 0), (0, 0))).reshape(DEPTH, dec_bsz * n_tok, 3 * B_WIDTH)

    s_prompt, c_prompt, c_sample, v_sample = [], [], [], []
    s_sample = jnp.zeros(state_S.shape, F32)
    for i in range(DEPTH):
        last = dict(final_gain=final) if i == DEPTH - 1 else {}

        xp, (f2_in, f2_out, w_front, w_tail, w_o, ple_gate, ple_proj) = _ffn(
            xp, i, w["n_f1"], *ffn1_w,
            casts=[(w_ffn2_in, i, None), (w_ffn2_out, i, None), (w_in, i, in_proj_splits), (w_out, i, None),
                   (w_ple_gate, i, None), (w_ple_proj, i, None)])
        xs = _ffn(xs, i, w["n_f1"], *ffn1_w)
        mix_w = dict(w, w_front=w_front, w_tail=w_tail, w_out=w_o)
        xp, sp, cp = _mix_prompt(xp.reshape(bsz, length, D_MODEL), i, mix_w)
        xs, s_sample, zq, vs = _mix_sample(xs, i, state_S, cpad, s_sample, mix_w, n_tok)
        xp = xp.reshape(bsz * length, D_MODEL)
        ple = (w["n_ple"], ple_gate, ple_proj)
        if i < DEPTH - 1:
            xp, ffn1_w = _ffn(xp, i, w["n_f2"], f2_in, f2_out, ple=(pp,) + ple,
                              casts=[(w_ffn1_in, i + 1, None), (w_ffn1_out, i + 1, None)])
        else:
            xp = _ffn(xp, i, w["n_f2"], f2_in, f2_out, ple=(pp,) + ple, **last)
        xs = _ffn(xs, i, w["n_f2"], f2_in, f2_out, ple=(ps,) + ple, **last)

        s_prompt.append(sp)
        c_prompt.append(cp)
        c_sample.append(zq.reshape(dec_bsz, n_tok, 3 * B_WIDTH)[:, n_tok - keep:])
        v_sample.append(vs.reshape(dec_bsz, n_tok, N_HEADS, HEAD_DIM))

    return (xp.reshape(bsz, length, D_MODEL), xs.reshape(dec_bsz, n_tok, D_MODEL), jnp.stack(s_prompt),
            jnp.stack(c_prompt), s_sample, jnp.stack(c_sample), jnp.stack(v_sample))
```

```python
import functools

import jax
import jax.numpy as jnp
from jax import lax
from jax.experimental import pallas as pl
from jax.experimental.pallas import tpu as pltpu

F32 = jnp.float32
BF16 = jnp.bfloat16
EPS = 1e-6

D_MODEL = 1024
D_FF = 2816
DEPTH = 4
N_HEADS = 4
HEAD_DIM = 128
A_WIDTH = N_HEADS * HEAD_DIM
B_WIDTH = N_HEADS * HEAD_DIM
CHUNK_A = 128
CHUNK_D = 64
CONV_W = 4
PLE_DIM = 256
Z_FRONT = 2 * A_WIDTH + 3 * B_WIDTH
Z_TAIL = 2 * N_HEADS + B_WIDTH
OFF_QKV = 2 * A_WIDTH
OFF_GATE = OFF_QKV + 3 * B_WIDTH

VMEM_LIMIT_BYTES = 52 * 1024 * 1024
LANES = 128
BF16_SUBLANES = 16
MXU_N = 256
FFN_ROWS = 512
GROUP = 4 * CHUNK_D
MIX_ROWS = 512
SAMPLE_GROUP = 16


def _rms(x, gain):
    return x * lax.rsqrt(jnp.mean(x * x, axis=-1, keepdims=True) + EPS) * gain


def _l2(x):
    return x * lax.rsqrt(jnp.sum(x * x, axis=-1, keepdims=True) + EPS)


def _silu(x):
    return x * jax.nn.sigmoid(x)


def _softplus(x):
    return jnp.maximum(x, 0.0) + jnp.log1p(jnp.exp(-jnp.abs(x)))


def _dot(a, b):
    return jnp.dot(a.astype(BF16), b.astype(BF16), preferred_element_type=F32)


def _dot_nt(a, b):
    return lax.dot_general(a.astype(BF16), b.astype(BF16), (((1,), (1,)), ((), ())),
                           preferred_element_type=F32)


def _split3(a):
    p1 = a.astype(BF16)
    r1 = a - p1.astype(F32)
    p2 = r1.astype(BF16)
    p3 = (r1 - p2.astype(F32)).astype(BF16)
    return p1, p2, p3


def _layer_spec(shape, layer, block=None):
    index = (layer,) + (0,) * (len(shape) - 1) + (0 if block is None else block,)
    return pl.BlockSpec((None,) + tuple(shape), lambda *_: index, pipeline_mode=pl.Buffered(1))


def _ffn_kernel(*refs, with_ple, with_final, cast_plan):
    x_ref, gain_ref, wg_ref, wu_ref, wo_ref = refs[:5]
    n_in = 5 + 4 * with_ple + with_final
    cast_in = refs[n_in:n_in + len(cast_plan)]
    o_ref = refs[n_in + len(cast_plan)]
    cast_out = iter(refs[n_in + len(cast_plan) + 1:])
    for src_ref, splits in zip(cast_in, cast_plan):
        for lo, hi in splits:
            next(cast_out)[...] = src_ref[:, lo:hi].astype(BF16)
    x = x_ref[...]
    xn = _rms(x, gain_ref[...]).astype(BF16)
    acc = jnp.zeros_like(x)
    for c in range(D_FF // MXU_N):
        sl = slice(c * MXU_N, (c + 1) * MXU_N)
        gate = jnp.dot(xn, wg_ref[:, sl], preferred_element_type=F32)
        up = jnp.dot(xn, wu_ref[:, sl], preferred_element_type=F32)
        h = (_silu(gate) * up).astype(BF16)
        acc = acc + jnp.dot(h, wo_ref[sl, :], preferred_element_type=F32)
    x = x + 0.5 * acc
    if with_ple:
        p_ref, npl_ref, wpg_ref, wpp_ref = refs[5:9]
        emb = _dot(p_ref[...], wpp_ref[...])
        gate = _dot(_rms(x, npl_ref[...]), wpg_ref[...])
        x = x + emb * jax.nn.sigmoid(gate)
    if with_final:
        x = _rms(x, refs[9][...])
    o_ref[...] = x


def _cast_row_blocks(n_rows, n_steps):
    blocks = n_steps
    while n_rows % blocks or (n_rows // blocks) % BF16_SUBLANES:
        assert blocks % 2 == 0, (n_rows, n_steps)
        blocks //= 2
    return blocks


def _ffn(x, layer, gain, w_in, w_out, ple=None, final_gain=None, casts=()):
    rows = x.shape[0]
    tm = min(FFN_ROWS, rows)
    n_steps = rows // tm
    row_spec = pl.BlockSpec((tm, D_MODEL), lambda i: (i, 0))
    in_specs = [row_spec, _layer_spec((1, D_MODEL), layer),
                _layer_spec((D_MODEL, D_FF), 0, block=0), _layer_spec((D_MODEL, D_FF), 0, block=1),
                _layer_spec((D_FF, D_MODEL), 0)]
    args = [x, gain, w_in, w_in, w_out]
    out_specs = [row_spec]
    out_shape = [jax.ShapeDtypeStruct(x.shape, F32)]
    if ple is not None:
        p, n_ple, w_gate, w_proj = ple
        in_specs += [pl.BlockSpec((None, tm, PLE_DIM), lambda i: (layer, i, 0)), _layer_spec((1, D_MODEL), layer),
                     _layer_spec((D_MODEL, D_MODEL), 0), _layer_spec((PLE_DIM, D_MODEL), 0)]
        args += [p, n_ple, w_gate, w_proj]
    if final_gain is not None:
        in_specs.append(_layer_spec((1, D_MODEL), 0))
        args.append(final_gain)
    cast_plan = []
    for src, layer_c, splits in casts:
        _, n_rows, n_cols = src.shape
        splits = tuple(splits) if splits is not None else ((0, n_cols),)
        blocks = _cast_row_blocks(n_rows, n_steps)
        block_of = functools.partial(lambda i, every: i // every, every=n_steps // blocks)
        in_specs.append(pl.BlockSpec((None, n_rows // blocks, n_cols),
                                     lambda i, layer_c=layer_c, block_of=block_of: (layer_c, block_of(i), 0)))
        args.append(src)
        for lo, hi in splits:
            out_specs.append(pl.BlockSpec((None, n_rows // blocks, hi - lo),
                                          lambda i, block_of=block_of: (0, block_of(i), 0)))
            out_shape.append(jax.ShapeDtypeStruct((1, n_rows, hi - lo), BF16))
        cast_plan.append(splits)
    out = pl.pallas_call(
        functools.partial(_ffn_kernel, with_ple=ple is not None, with_final=final_gain is not None,
                          cast_plan=tuple(cast_plan)),
        grid=(n_steps,),
        in_specs=in_specs,
        out_specs=out_specs,
        out_shape=out_shape,
        compiler_params=pltpu.CompilerParams(dimension_semantics=("arbitrary",),
                                             vmem_limit_bytes=VMEM_LIMIT_BYTES),
    )(*args)
    return (out[0], out[1:]) if casts else out[0]


def _split(a):
    hi = a.astype(BF16)
    lo = (a - hi.astype(F32)).astype(BF16)
    return hi, lo


def _dot3(a_hi, a_lo, b_hi, b_lo):
    m = a_hi.shape[0]
    both = jnp.dot(jnp.concatenate([a_hi, a_lo], axis=0), b_hi, preferred_element_type=F32)
    return both[:m] + both[m:] + jnp.dot(a_hi, b_lo, preferred_element_type=F32)


def _fold(block_diag):
    n = block_diag.shape[0] // CHUNK_D
    out = block_diag[0:CHUNK_D]
    for g in range(1, n):
        out = out + block_diag[g * CHUNK_D:(g + 1) * CHUNK_D]
    return out


def _expand(packed, diag_ones_ref):
    n = packed.shape[1] // CHUNK_D
    return jnp.concatenate([packed] * n, axis=0) * diag_ones_ref[...]


def _unit_lower_inverses_packed(l_packed_list, diag_ones_ref):
    c, width = l_packed_list[0].shape
    row = lax.broadcasted_iota(jnp.int32, (c, width), 0)
    col = lax.broadcasted_iota(jnp.int32, (c, width), 1) % c
    zero = jnp.zeros((), BF16)

    def lower_left(bs):
        return (row // (2 * bs) == col // (2 * bs)) & ((row // bs) % 2 == 1) & ((col // bs) % 2 == 0)

    l_bf = [l.astype(BF16) for l in l_packed_list]
    xs = [jnp.where(row == col, 1.0, 0.0) - jnp.where(lower_left(1), l, 0.0) for l in l_packed_list]
    bs = 2
    while bs < c:
        sel = lower_left(bs)
        x_bf = [x.astype(BF16) for x in xs]
        ys = [jnp.dot(jnp.where(sel, l, zero), _expand(x, diag_ones_ref), preferred_element_type=F32)
              for l, x in zip(l_bf, x_bf)]
        xs = [x - jnp.dot(xb, _expand(y.astype(BF16), diag_ones_ref), preferred_element_type=F32)
              for x, xb, y in zip(xs, x_bf, ys)]
        bs *= 2
    return xs


def _mix_prompt_kernel(x_ref, nmix_ref, wmain_ref, wtail_ref, avg_ref, aog_ref, bog_ref, wsp_ref, bsp_ref,
                       cw_ref, abp_ref, wout_ref,
                       xo_ref, s_ref, ct_ref,
                       zlast_ref, ob_ref, bd_ref, wgate_ref):
    tl = x_ref.shape[0]
    step = pl.program_id(1)

    @pl.when((pl.program_id(0) == 0) & (step == 0))
    def _():
        wgate_ref[...] = wtail_ref[:, 2 * N_HEADS:]

    @pl.when(step == 0)
    def _():
        s_ref[...] = jnp.zeros_like(s_ref)
        zlast_ref[...] = jnp.zeros_like(zlast_ref)

    x = x_ref[...]
    xn = _rms(x, nmix_ref[...]).astype(BF16)
    def in_proj(lo, hi):
        return jnp.dot(xn, wmain_ref[:, lo:hi], preferred_element_type=F32)

    zab = jnp.dot(xn, wtail_ref[:, :LANES], preferred_element_type=F32)
    first_tile = lax.broadcasted_iota(jnp.int32, (8, 1), 0)
    qkv_parts = []
    for part in range(3):
        cols = slice(part * B_WIDTH, (part + 1) * B_WIDTH)
        zc = in_proj(OFF_QKV + part * B_WIDTH, OFF_QKV + (part + 1) * B_WIDTH)
        cw = cw_ref[:, cols]
        carried = zlast_ref[:, cols]
        y = zc * cw[CONV_W - 1:CONV_W]
        for d in range(1, CONV_W):
            rolled = pltpu.roll(zc, d, axis=0)
            top = jnp.where(first_tile < d, pltpu.roll(carried, d, axis=0), rolled[0:8])
            y = y + jnp.concatenate([top, rolled[8:]], axis=0) * cw[CONV_W - 1 - d:CONV_W - d]
        zlast_ref[:, cols] = zc[tl - 8:tl]
        ct_ref[:, cols] = zc[tl - (CONV_W - 1):tl]
        qkv_parts.append(_silu(y))

    uv = jax.nn.gelu(in_proj(0, OFF_QKV))
    row = lax.broadcasted_iota(jnp.int32, (CHUNK_A, CHUNK_A), 0)
    col = lax.broadcasted_iota(jnp.int32, (CHUNK_A, CHUNK_A), 1)
    causal = col <= row
    for h in range(N_HEADS):
        hs = slice(h * HEAD_DIM, (h + 1) * HEAD_DIM)
        u_h = uv[:, hs]
        v_h = _rms(uv[:, A_WIDTH + h * HEAD_DIM:A_WIDTH + (h + 1) * HEAD_DIM], avg_ref[...]).astype(BF16)
        w_h = jnp.where(causal, wsp_ref[h], 0.0).astype(BF16)
        bias_h = bsp_ref[:, h:h + 1]
        for c in range(tl // CHUNK_A):
            rs = slice(c * CHUNK_A, (c + 1) * CHUNK_A)
            mixed = jnp.dot(w_h, v_h[rs], preferred_element_type=F32) + bias_h
            ob_ref[rs, hs] = _rms(u_h[rs] * mixed, aog_ref[...]).astype(BF16)

    z_gate = jnp.dot(xn, wgate_ref[...], preferred_element_type=F32)
    abp = abp_ref[...]
    g = -jnp.exp(abp[0:1]) * _softplus(zab + abp[1:2])
    beta = jax.nn.sigmoid(zab)

    r2 = lax.broadcasted_iota(jnp.int32, (GROUP, GROUP), 0)
    c2 = lax.broadcasted_iota(jnp.int32, (GROUP, GROUP), 1)
    same = (r2 // CHUNK_D) == (c2 // CHUNK_D)
    strict_bd = same & (c2 < r2)
    col_ones = jnp.concatenate([jnp.where(same & (c2 <= r2), 1.0, 0.0), jnp.where(same, 1.0, 0.0)],
                               axis=0).astype(BF16)
    upper_ones = jnp.where(same & (r2 <= c2), 1.0, 0.0).astype(BF16)
    bd_ref[...] = jnp.where(same, 1.0, 0.0).astype(BF16)
    n_grp = tl // GROUP
    gam_parts, glast_parts, gam_t = [], [], []
    for gi in range(n_grp):
        g_grp = g[gi * GROUP:(gi + 1) * GROUP]
        by_col = jnp.dot(col_ones, jnp.concatenate(_split3(g_grp), axis=1), preferred_element_type=F32)
        by_col = by_col[:, :LANES] + by_col[:, LANES:2 * LANES] + by_col[:, 2 * LANES:]
        gam_parts.append(by_col[:GROUP])
        glast_parts.append(by_col[GROUP:])
        by_row = jnp.dot(jnp.concatenate(_split3(g_grp.T), axis=0), upper_ones, preferred_element_type=F32)
        gam_t.append(by_row[:LANES] + by_row[LANES:2 * LANES] + by_row[2 * LANES:])
    gam = jnp.concatenate(gam_parts, axis=0)
    glast = jnp.concatenate(glast_parts, axis=0)

    rb = lax.broadcasted_iota(jnp.int32, (CHUNK_D, CHUNK_D), 0)
    cb = lax.broadcasted_iota(jnp.int32, (CHUNK_D, CHUNK_D), 1)
    incl = cb <= rb

    heads, a_packed, rhs = [], [], []
    for h in range(N_HEADS):
        q_h = _l2(qkv_parts[0][:, h * HEAD_DIM:(h + 1) * HEAD_DIM]) * (HEAD_DIM ** -0.5)
        k_h = _l2(qkv_parts[1][:, h * HEAD_DIM:(h + 1) * HEAD_DIM])
        v_h = qkv_parts[2][:, h * HEAD_DIM:(h + 1) * HEAD_DIM]
        gc_h = gam[:, h:h + 1]
        gl_h = glast[:, h:h + 1]
        bc_h = beta[:, N_HEADS + h:N_HEADS + h + 1]
        eg_h = jnp.exp(gc_h)
        for gi in range(n_grp):
            gs = slice(gi * GROUP, (gi + 1) * GROUP)
            kk = _dot_nt(k_h[gs], k_h[gs])
            decay = jnp.exp(jnp.where(strict_bd, gc_h[gs] - gam_t[gi][h:h + 1, :], 0.0))
            a_packed.append(_fold(jnp.where(strict_bd, bc_h[gs] * kk * decay, 0.0)))
        rhs.append(_split(jnp.concatenate([bc_h * v_h, (bc_h * eg_h) * k_h], axis=1)))
        heads.append((q_h * eg_h, q_h, k_h, k_h * jnp.exp(gl_h - gc_h), gc_h, jnp.exp(gl_h)))
    inv_split = [_split(inv) for inv in _unit_lower_inverses_packed(a_packed, bd_ref)]
    sol = []
    for h in range(N_HEADS):
        sol.append([_dot3(_expand(inv_split[h * n_grp + gi][0], bd_ref), _expand(inv_split[h * n_grp + gi][1], bd_ref),
                          rhs[h][0][gi * GROUP:(gi + 1) * GROUP], rhs[h][1][gi * GROUP:(gi + 1) * GROUP])
                    for gi in range(n_grp)])

    for i in range(tl // CHUNK_D):
        rs = slice(i * CHUNK_D, (i + 1) * CHUNK_D)
        gi, j = divmod(i, GROUP // CHUNK_D)
        ls = slice(j * CHUNK_D, (j + 1) * CHUNK_D)
        for h in range(N_HEADS):
            qb_h, q_h, k_h, kend_h, gc_h, btot_h = heads[h]
            decay = jnp.where(incl, jnp.exp(jnp.where(incl, gc_h[rs] - gam_t[gi][h:h + 1, ls], 0.0)), 0.0)
            qk = _dot_nt(q_h[rs], k_h[rs]) * decay
            s_old = s_ref[h]
            from_s = _dot(jnp.concatenate([sol[h][gi][ls, HEAD_DIM:], qb_h[rs]], axis=0), s_old)
            u = sol[h][gi][ls, :HEAD_DIM] - from_s[:CHUNK_D]
            from_u = _dot(jnp.concatenate([qk, kend_h[rs].T], axis=0), u)
            o = from_s[CHUNK_D:] + from_u[:CHUNK_D]
            s_ref[h] = btot_h[i * CHUNK_D:i * CHUNK_D + 1] * s_old + from_u[CHUNK_D:]
            gate = z_gate[rs, h * HEAD_DIM:(h + 1) * HEAD_DIM]
            ob_ref[rs, A_WIDTH + h * HEAD_DIM:A_WIDTH + (h + 1) * HEAD_DIM] = (
                _rms(o, bog_ref[...]) * _silu(gate)).astype(BF16)

    xo_ref[...] = x + jnp.dot(ob_ref[...], wout_ref[...], preferred_element_type=F32)


def _mixer_weight_specs(layer):
    return [_layer_spec((1, D_MODEL), layer), _layer_spec((D_MODEL, Z_FRONT), 0), _layer_spec((D_MODEL, Z_TAIL), 0),
            _layer_spec((1, HEAD_DIM), layer), _layer_spec((1, HEAD_DIM), layer), _layer_spec((1, HEAD_DIM), layer)]


def _mixer_weights(w):
    return [w["n_mix"], w["w_front"], w["w_tail"], w["a_v_gain"], w["a_out_gain"], w["b_out_gain"]]


def _mix_prompt(x, layer, w):
    bsz, length, _ = x.shape
    tl = MIX_ROWS
    row_spec = pl.BlockSpec((None, tl, D_MODEL), lambda b, t: (b, t, 0))
    in_specs = [row_spec] + _mixer_weight_specs(layer) + [
        _layer_spec((N_HEADS, CHUNK_A, CHUNK_A), layer), _layer_spec((CHUNK_A, N_HEADS), layer),
        _layer_spec((CONV_W, 3 * B_WIDTH), layer), _layer_spec((2, LANES), layer), _layer_spec((D_MODEL, D_MODEL), 0)]
    out_specs = [row_spec,
                 pl.BlockSpec((None, N_HEADS, HEAD_DIM, HEAD_DIM), lambda b, t: (b, 0, 0, 0)),
                 pl.BlockSpec((None, CONV_W - 1, 3 * B_WIDTH), lambda b, t: (b, 0, 0))]
    out_shape = [jax.ShapeDtypeStruct(x.shape, F32),
                 jax.ShapeDtypeStruct((bsz, N_HEADS, HEAD_DIM, HEAD_DIM), F32),
                 jax.ShapeDtypeStruct((bsz, CONV_W - 1, 3 * B_WIDTH), F32)]
    return pl.pallas_call(
        _mix_prompt_kernel,
        grid=(bsz, length // tl),
        in_specs=in_specs,
        out_specs=out_specs,
        out_shape=out_shape,
        scratch_shapes=[pltpu.VMEM((8, 3 * B_WIDTH), F32), pltpu.VMEM((tl, D_MODEL), BF16),
                        pltpu.VMEM((GROUP, GROUP), BF16), pltpu.VMEM((D_MODEL, B_WIDTH), BF16)],
        compiler_params=pltpu.CompilerParams(dimension_semantics=("arbitrary", "arbitrary"),
                                             vmem_limit_bytes=VMEM_LIMIT_BYTES),
    )(x, *_mixer_weights(w), w["a_w_s"], w["a_b_s_t"], w["b_conv"], w["ab_par"], w["w_out"])


def _mix_sample_kernel(x_ref, s_ref, cpad_ref, s_all_ref, nmix_ref, wmain_ref, wtail_ref, avg_ref, aog_ref, bog_ref,
                       coef_ref, bias_ref, cw_ref, abp_ref, wout_ref,
                       xo_ref, so_ref, zq_ref, vo_ref,
                       z_ref, ztail_ref, ob_ref, *, n_tok):
    del s_all_ref
    rows = zq_ref.shape[0]
    nb = rows // n_tok
    step = pl.program_id(0)

    @pl.when(step == 0)
    def _():
        xn = _rms(x_ref[...], nmix_ref[...]).astype(BF16)
        z_ref[...] = jnp.dot(xn, wmain_ref[...], preferred_element_type=F32)
        ztail_ref[...] = jnp.dot(xn, wtail_ref[...], preferred_element_type=F32)

    here = pl.ds(pl.multiple_of(step * rows, rows), rows)
    z = z_ref[here, :]
    z_tail = ztail_ref[here, :]
    zab = z_tail[:, :LANES]
    tok = lax.broadcasted_iota(jnp.int32, (rows, 1), 0) % n_tok

    def prev(a, d):
        return pltpu.roll(a, d, axis=0)

    def prev_or_zero(a, d):
        return a if d == 0 else jnp.where(tok >= d, prev(a, d), 0.0)

    def per_head(fn, a):
        return jnp.concatenate([fn(a[:, h * HEAD_DIM:(h + 1) * HEAD_DIM]) for h in range(N_HEADS)], axis=1)

    uv = jax.nn.gelu(z[:, :2 * A_WIDTH])
    vn = per_head(lambda a: _rms(a, avg_ref[...]), uv[:, A_WIDTH:])
    vo_ref[...] = vn
    mixed = bias_ref[...]
    for d in range(n_tok):
        mixed = mixed + coef_ref[d] * prev_or_zero(vn, d)
    ob_ref[here, :A_WIDTH] = per_head(lambda a: _rms(a, aog_ref[...]), uv[:, :A_WIDTH] * mixed)

    zq = z[:, OFF_QKV:OFF_GATE]
    zq_ref[...] = zq
    cpad = cpad_ref[...]
    cw = cw_ref[...]
    y = zq * cw[CONV_W - 1:CONV_W]
    for d in range(1, CONV_W):
        carried = pltpu.roll(cpad, rows - (n_tok - d), axis=0)
        y = y + jnp.where(tok >= d, prev(zq, d), carried) * cw[CONV_W - 1 - d:CONV_W - d]
    qkv = _silu(y)

    abp = abp_ref[...]
    g_all = -jnp.exp(abp[0:1]) * _softplus(zab + abp[1:2])
    beta_all = jax.nn.sigmoid(zab)

    sub = lax.broadcasted_iota(jnp.int32, (8, 1), 0)
    first_half = sub < n_tok
    o_heads, kend_heads, u_heads, btot_heads = [], [], [], []
    for h in range(N_HEADS):
        q = _l2(qkv[:, h * HEAD_DIM:(h + 1) * HEAD_DIM]) * (HEAD_DIM ** -0.5)
        k = _l2(qkv[:, B_WIDTH + h * HEAD_DIM:B_WIDTH + (h + 1) * HEAD_DIM])
        v = qkv[:, 2 * B_WIDTH + h * HEAD_DIM:2 * B_WIDTH + (h + 1) * HEAD_DIM]
        g = jnp.broadcast_to(g_all[:, h:h + 1], (rows, HEAD_DIM))
        beta = jnp.broadcast_to(beta_all[:, N_HEADS + h:N_HEADS + h + 1], (rows, HEAD_DIM))
        gam = g
        for d in range(1, n_tok):
            gam = gam + prev_or_zero(g, d)
        gam_last = jnp.where(tok == n_tok - 1, gam, 0.0)
        for d in range(1, n_tok):
            gam_last = gam_last + jnp.where(tok == n_tok - 1 - d, pltpu.roll(gam, rows - d, axis=0), 0.0)
        eg = jnp.exp(gam)

        def decay_to(d, gam=gam):
            return jnp.exp(jnp.where(tok >= d, gam - prev(gam, d), 0.0))

        a_sub = [None] + [jnp.where(tok >= d, beta * jnp.sum(k * prev(k, d), axis=-1, keepdims=True) * decay_to(d),
                                    0.0) for d in range(1, n_tok)]
        def forward_substitute(rhs, a_sub=a_sub):
            sol = rhs
            for t in range(1, n_tok):
                acc = rhs
                for d in range(1, t + 1):
                    acc = acc - a_sub[d] * prev(sol, d)
                sol = jnp.where(tok == t, acc, sol)
            return sol

        w_blk = forward_substitute(beta * v)
        kb_blk = forward_substitute((beta * eg) * k)
        qb = q * eg

        kb_s, qb_s = [], []
        for p in range(rows // 8):
            kb_t, qb_t = kb_blk[8 * p:8 * p + 8], qb[8 * p:8 * p + 8]
            f0 = _dot(jnp.where(first_half, kb_t, pltpu.roll(qb_t, n_tok, axis=0)), s_ref[2 * p, h])
            f1 = _dot(jnp.where(first_half, pltpu.roll(kb_t, n_tok, axis=0), qb_t), s_ref[2 * p + 1, h])
            kb_s.append(jnp.where(first_half, f0, pltpu.roll(f1, n_tok, axis=0)))
            qb_s.append(jnp.where(first_half, pltpu.roll(f0, n_tok, axis=0), f1))
        u = w_blk - jnp.concatenate(kb_s, axis=0)
        o = jnp.concatenate(qb_s, axis=0)
        for d in range(n_tok):
            qk = jnp.where(tok >= d, jnp.sum(q * prev(k, d), axis=-1, keepdims=True) * decay_to(d), 0.0)
            o = o + qk * prev_or_zero(u, d)
        o_heads.append(o)
        kend_heads.append(k * jnp.exp(gam_last - gam))
        u_heads.append(u)
        btot_heads.append(jnp.broadcast_to(jnp.exp(gam_last), (rows, HEAD_DIM)))

    kend_t = jnp.concatenate(kend_heads, axis=0).T
    u_all = jnp.concatenate(u_heads, axis=0).astype(BF16)
    owner = lax.broadcasted_iota(jnp.int32, (1, N_HEADS * rows), 1) // n_tok
    for h in range(N_HEADS):
        for b in range(nb):
            mine = jnp.where(owner == h * nb + b, kend_t, 0.0).astype(BF16)
            last = b * n_tok + n_tok - 1
            so_ref[b, h] = btot_heads[h][last:last + 1] * s_ref[b, h] + jnp.dot(
                mine, u_all, preferred_element_type=F32)

    gate = z_tail[:, 2 * N_HEADS:]
    ob_ref[here, A_WIDTH:] = per_head(lambda a: _rms(a, bog_ref[...]), jnp.concatenate(o_heads, axis=1)) * _silu(gate)

    @pl.when(step == pl.num_programs(0) - 1)
    def _():
        xo_ref[...] = x_ref[...] + _dot(ob_ref[...], wout_ref[...])


def _mix_sample(x, layer, state_s, cpad, s_all, w, n_tok):
    rows_total = x.shape[0]
    nb = SAMPLE_GROUP
    rows = nb * n_tok
    all_rows = lambda width: pl.BlockSpec((rows_total, width), lambda i: (0, 0))
    row_spec = lambda width: pl.BlockSpec((rows, width), lambda i: (i, 0))
    s_spec = pl.BlockSpec((None, nb, N_HEADS, HEAD_DIM, HEAD_DIM), lambda i: (layer, i, 0, 0, 0))
    in_specs = [all_rows(D_MODEL), s_spec, pl.BlockSpec((None, rows, 3 * B_WIDTH), lambda i: (layer, i, 0)),
                pl.BlockSpec(memory_space=pl.ANY)] + _mixer_weight_specs(layer) + [
        _layer_spec((n_tok, rows, A_WIDTH), layer), _layer_spec((rows, A_WIDTH), layer),
        _layer_spec((CONV_W, 3 * B_WIDTH), layer), _layer_spec((2, LANES), layer), _layer_spec((D_MODEL, D_MODEL), 0)]
    out_specs = [all_rows(D_MODEL), s_spec, row_spec(3 * B_WIDTH), row_spec(A_WIDTH)]
    out_shape = [jax.ShapeDtypeStruct(x.shape, F32), jax.ShapeDtypeStruct(state_s.shape, F32),
                 jax.ShapeDtypeStruct((rows_total, 3 * B_WIDTH), F32), jax.ShapeDtypeStruct((rows_total, A_WIDTH), F32)]
    return pl.pallas_call(
        functools.partial(_mix_sample_kernel, n_tok=n_tok),
        grid=(rows_total // rows,),
        in_specs=in_specs,
        out_specs=out_specs,
        out_shape=out_shape,
        input_output_aliases={3: 1},
        scratch_shapes=[pltpu.VMEM((rows_total, Z_FRONT), F32), pltpu.VMEM((rows_total, Z_TAIL), F32),
                        pltpu.VMEM((rows_total, D_MODEL), F32)],
        compiler_params=pltpu.CompilerParams(dimension_semantics=("arbitrary",),
                                             vmem_limit_bytes=VMEM_LIMIT_BYTES),
    )(x, state_s, cpad, s_all, *_mixer_weights(w), w["a_coef"], w["a_bias"], w["b_conv"], w["ab_par"], w["w_out"])


def _prep_tables(n_tok, norm_ffn1, norm_mix, a_v_gain, a_spatial_w, a_spatial_b, a_out_gain, b_conv_w, b_a_log,
                 b_dt_bias, b_out_gain, norm_ffn2, norm_ple):
    par_pad = jnp.zeros((DEPTH, LANES - N_HEADS), F32)
    ab_par = jnp.stack([jnp.concatenate([b_a_log, par_pad], axis=1),
                        jnp.concatenate([b_dt_bias, par_pad], axis=1)], axis=1)

    def sample_rows(a):
        return jnp.tile(jnp.repeat(jnp.transpose(a, (0, 2, 1)), HEAD_DIM, axis=2), (1, SAMPLE_GROUP, 1))

    ws_small = a_spatial_w[:, :, :n_tok, :n_tok]
    a_coef = jnp.stack([sample_rows(jnp.pad(jnp.diagonal(ws_small, offset=-d, axis1=2, axis2=3),
                                            ((0, 0), (0, 0), (d, 0)))) for d in range(n_tok)], axis=1)
    return dict(
        n_f1=norm_ffn1[:, None], n_mix=norm_mix[:, None], ab_par=ab_par,
        a_v_gain=a_v_gain[:, None], a_out_gain=a_out_gain[:, None], b_out_gain=b_out_gain[:, None],
        a_w_s=a_spatial_w, a_b_s_t=jnp.transpose(a_spatial_b, (0, 2, 1)),
        a_coef=a_coef, a_bias=sample_rows(a_spatial_b[:, :, :n_tok]),
        b_conv=b_conv_w, n_f2=norm_ffn2[:, None], n_ple=norm_ple[:, None],
    )


def kernel(x_prompt, x_sample, state_S, state_conv, p_prompt, p_sample, norm_ffn1, w_ffn1_in, w_ffn1_out, norm_mix, w_in, a_v_gain, a_spatial_w, a_spatial_b, a_out_gain, b_conv_w, b_a_log, b_dt_bias, b_out_gain, w_out, norm_ffn2, w_ffn2_in, w_ffn2_out, norm_ple, w_ple_gate, w_ple_proj, final_norm):
    bsz, length, _ = x_prompt.shape
    dec_bsz, n_tok, _ = x_sample.shape
    assert length % MIX_ROWS == 0 and MIX_ROWS % CHUNK_A == 0 and MIX_ROWS % GROUP == 0
    assert dec_bsz % SAMPLE_GROUP == 0
    assert n_tok % CHUNK_A != 0 and n_tok % CHUNK_D != 0
    assert 2 * n_tok == 8 and (N_HEADS * SAMPLE_GROUP * n_tok) % HEAD_DIM == 0 and n_tok >= CONV_W - 1

    w = _prep_tables(n_tok, norm_ffn1, norm_mix, a_v_gain, a_spatial_w, a_spatial_b, a_out_gain, b_conv_w, b_a_log,
                     b_dt_bias, b_out_gain, norm_ffn2, norm_ple)
    ffn1_w = (w_ffn1_in[0:1].astype(BF16), w_ffn1_out[0:1].astype(BF16))
    in_proj_splits = ((0, Z_FRONT), (Z_FRONT, Z_FRONT + Z_TAIL))
    final = final_norm[None, None]
    xp = x_prompt.reshape(bsz * length, D_MODEL)
    xs = x_sample.reshape(dec_bsz * n_tok, D_MODEL)
    pp = p_prompt.reshape(DEPTH, bsz * length, PLE_DIM)
    ps = p_sample.reshape(DEPTH, dec_bsz * n_tok, PLE_DIM)
    keep = CONV_W - 1
    cpad = jnp.pad(state_conv, ((0, 0), (0, 0), (n_tok - keep, 0), (0, 0))).reshape(DEPTH, dec_bsz * n_tok, 3 * B_WIDTH)

    s_prompt, c_prompt, c_sample, v_sample = [], [], [], []
    s_sample = jnp.zeros(state_S.shape, F32)
    for i in range(DEPTH):
        last = dict(final_gain=final) if i == DEPTH - 1 else {}

        xp, (f2_in, f2_out, w_front, w_tail, w_o, ple_gate, ple_proj) = _ffn(
            xp, i, w["n_f1"], *ffn1_w,
            casts=[(w_ffn2_in, i, None), (w_ffn2_out, i, None), (w_in, i, in_proj_splits), (w_out, i, None),
                   (w_ple_gate, i, None), (w_ple_proj, i, None)])
        xs = _ffn(xs, i, w["n_f1"], *ffn1_w)
        mix_w = dict(w, w_front=w_front, w_tail=w_tail, w_out=w_o)
        xp, sp, cp = _mix_prompt(xp.reshape(bsz, length, D_MODEL), i, mix_w)
        xs, s_sample, zq, vs = _mix_sample(xs, i, state_S, cpad, s_sample, mix_w, n_tok)
        xp = xp.reshape(bsz * length, D_MODEL)
        ple = (w["n_ple"], ple_gate, ple_proj)
        if i < DEPTH - 1:
            xp, ffn1_w = _ffn(xp, i, w["n_f2"], f2_in, f2_out, ple=(pp,) + ple,
                              casts=[(w_ffn1_in, i + 1, None), (w_ffn1_out, i + 1, None)])
        else:
            xp = _ffn(xp, i, w["n_f2"], f2_in, f2_out, ple=(pp,) + ple, **last)
        xs = _ffn(xs, i, w["n_f2"], f2_in, f2_out, ple=(ps,) + ple, **last)

        s_prompt.append(sp)
        c_prompt.append(cp)
        c_sample.append(zq.reshape(dec_bsz, n_tok, 3 * B_WIDTH)[:, n_tok - keep:])
        v_sample.append(vs.reshape(dec_bsz, n_tok, N_HEADS, HEAD_DIM))

    return (xp.reshape(bsz, length, D_MODEL), xs.reshape(dec_bsz, n_tok, D_MODEL), jnp.stack(s_prompt),
            jnp.stack(c_prompt), s_sample, jnp.stack(c_sample), jnp.stack(v_sample))
```

```python
import functools

import jax
import jax.numpy as jnp
from jax import lax
from jax.experimental import pallas as pl
from jax.experimental.pallas import tpu as pltpu

F32 = jnp.float32
BF16 = jnp.bfloat16
EPS = 1e-6

D_MODEL = 1024
D_FF = 2816
DEPTH = 4
N_HEADS = 4
HEAD_DIM = 128
A_WIDTH = N_HEADS * HEAD_DIM
B_WIDTH = N_HEADS * HEAD_DIM
CHUNK_A = 128
CHUNK_D = 64
CONV_W = 4
PLE_DIM = 256
Z_FRONT = 2 * A_WIDTH + 3 * B_WIDTH
Z_TAIL = 2 * N_HEADS + B_WIDTH
OFF_QKV = 2 * A_WIDTH
OFF_GATE = OFF_QKV + 3 * B_WIDTH

VMEM_LIMIT_BYTES = 60 * 1024 * 1024
LANES = 128
BF16_SUBLANES = 16
MXU_N = 256
FFN_ROWS = 1024
GROUP = 4 * CHUNK_D
MIX_ROWS = 512
SAMPLE_GROUP = 8


def _rms(x, gain):
    return x * lax.rsqrt(jnp.mean(x * x, axis=-1, keepdims=True) + EPS) * gain


def _l2(x):
    return x * lax.rsqrt(jnp.sum(x * x, axis=-1, keepdims=True) + EPS)


def _silu(x):
    return x * jax.nn.sigmoid(x)


def _softplus(x):
    return jnp.maximum(x, 0.0) + jnp.log1p(jnp.exp(-jnp.abs(x)))


def _dot(a, b):
    return jnp.dot(a.astype(BF16), b.astype(BF16), preferred_element_type=F32)


def _dot_nt(a, b):
    return lax.dot_general(a.astype(BF16), b.astype(BF16), (((1,), (1,)), ((), ())),
                           preferred_element_type=F32)


def _split3(a):
    p1 = a.astype(BF16)
    r1 = a - p1.astype(F32)
    p2 = r1.astype(BF16)
    p3 = (r1 - p2.astype(F32)).astype(BF16)
    return p1, p2, p3


def _layer_spec(shape, layer, block=None):
    index = (layer,) + (0,) * (len(shape) - 1) + (0 if block is None else block,)
    return pl.BlockSpec((None,) + tuple(shape), lambda *_: index, pipeline_mode=pl.Buffered(1))


def _ffn_kernel(*refs, with_ple, with_final, cast_plan):
    x_ref, gain_ref, wg_ref, wu_ref, wo_ref = refs[:5]
    n_in = 5 + 4 * with_ple + with_final
    cast_in = refs[n_in:n_in + len(cast_plan)]
    o_ref = refs[n_in + len(cast_plan)]
    cast_out = iter(refs[n_in + len(cast_plan) + 1:])
    for src_ref, splits in zip(cast_in, cast_plan):
        for lo, hi in splits:
            next(cast_out)[...] = src_ref[:, lo:hi].astype(BF16)
    x = x_ref[...]
    xn = _rms(x, gain_ref[...]).astype(BF16)
    acc = jnp.zeros_like(x)
    for c in range(D_FF // MXU_N):
        sl = slice(c * MXU_N, (c + 1) * MXU_N)
        gate = jnp.dot(xn, wg_ref[:, sl], preferred_element_type=F32)
        up = jnp.dot(xn, wu_ref[:, sl], preferred_element_type=F32)
        h = (_silu(gate) * up).astype(BF16)
        acc = acc + jnp.dot(h, wo_ref[sl, :], preferred_element_type=F32)
    x = x + 0.5 * acc
    if with_ple:
        p_ref, npl_ref, wpg_ref, wpp_ref = refs[5:9]
        emb = _dot(p_ref[...], wpp_ref[...])
        gate = _dot(_rms(x, npl_ref[...]), wpg_ref[...])
        x = x + emb * jax.nn.sigmoid(gate)
    if with_final:
        x = _rms(x, refs[9][...])
    o_ref[...] = x


def _cast_row_blocks(n_rows, n_steps):
    blocks = n_steps
    while n_rows % blocks or (n_rows // blocks) % BF16_SUBLANES:
        assert blocks % 2 == 0, (n_rows, n_steps)
        blocks //= 2
    return blocks


def _ffn(x, layer, gain, w_in, w_out, ple=None, final_gain=None, casts=()):
    rows = x.shape[0]
    tm = min(FFN_ROWS, rows)
    n_steps = rows // tm
    row_spec = pl.BlockSpec((tm, D_MODEL), lambda i: (i, 0))
    in_specs = [row_spec, _layer_spec((1, D_MODEL), layer),
                _layer_spec((D_MODEL, D_FF), 0, block=0), _layer_spec((D_MODEL, D_FF), 0, block=1),
                _layer_spec((D_FF, D_MODEL), 0)]
    args = [x, gain, w_in, w_in, w_out]
    out_specs = [row_spec]
    out_shape = [jax.ShapeDtypeStruct(x.shape, F32)]
    if ple is not None:
        p, n_ple, w_gate, w_proj = ple
        in_specs += [pl.BlockSpec((None, tm, PLE_DIM), lambda i: (layer, i, 0)), _layer_spec((1, D_MODEL), layer),
                     _layer_spec((D_MODEL, D_MODEL), 0), _layer_spec((PLE_DIM, D_MODEL), 0)]
        args += [p, n_ple, w_gate, w_proj]
    if final_gain is not None:
        in_specs.append(_layer_spec((1, D_MODEL), 0))
        args.append(final_gain)
    cast_plan = []
    for src, layer_c, splits in casts:
        _, n_rows, n_cols = src.shape
        splits = tuple(splits) if splits is not None else ((0, n_cols),)
        blocks = _cast_row_blocks(n_rows, n_steps)
        block_of = functools.partial(lambda i, every: i // every, every=n_steps // blocks)
        in_specs.append(pl.BlockSpec((None, n_rows // blocks, n_cols),
                                     lambda i, layer_c=layer_c, block_of=block_of: (layer_c, block_of(i), 0)))
        args.append(src)
        for lo, hi in splits:
            out_specs.append(pl.BlockSpec((None, n_rows // blocks, hi - lo),
                                          lambda i, block_of=block_of: (0, block_of(i), 0)))
            out_shape.append(jax.ShapeDtypeStruct((1, n_rows, hi - lo), BF16))
        cast_plan.append(splits)
    out = pl.pallas_call(
        functools.partial(_ffn_kernel, with_ple=ple is not None, with_final=final_gain is not None,
                          cast_plan=tuple(cast_plan)),
        grid=(n_steps,),
        in_specs=in_specs,
        out_specs=out_specs,
        out_shape=out_shape,
        compiler_params=pltpu.CompilerParams(dimension_semantics=("arbitrary",),
                                             vmem_limit_bytes=VMEM_LIMIT_BYTES),
    )(*args)
    return (out[0], out[1:]) if casts else out[0]


def _split(a):
    hi = a.astype(BF16)
    lo = (a - hi.astype(F32)).astype(BF16)
    return hi, lo


def _dot3(a_hi, a_lo, b_hi, b_lo):
    m = a_hi.shape[0]
    both = jnp.dot(jnp.concatenate([a_hi, a_lo], axis=0), b_hi, preferred_element_type=F32)
    return both[:m] + both[m:] + jnp.dot(a_hi, b_lo, preferred_element_type=F32)


def _fold(block_diag):
    n = block_diag.shape[0] // CHUNK_D
    out = block_diag[0:CHUNK_D]
    for g in range(1, n):
        out = out + block_diag[g * CHUNK_D:(g + 1) * CHUNK_D]
    return out


def _expand(packed, diag_ones_ref):
    n = packed.shape[1] // CHUNK_D
    return jnp.concatenate([packed] * n, axis=0) * diag_ones_ref[...]


def _unit_lower_inverses_packed(l_packed_list, diag_ones_ref):
    c, width = l_packed_list[0].shape
    row = lax.broadcasted_iota(jnp.int32, (c, width), 0)
    col = lax.broadcasted_iota(jnp.int32, (c, width), 1) % c
    zero = jnp.zeros((), BF16)

    def lower_left(bs):
        return (row // (2 * bs) == col // (2 * bs)) & ((row // bs) % 2 == 1) & ((col // bs) % 2 == 0)

    l_bf = [l.astype(BF16) for l in l_packed_list]
    xs = [jnp.where(row == col, 1.0, 0.0) - jnp.where(lower_left(1), l, 0.0) for l in l_packed_list]
    bs = 2
    while bs < c:
        sel = lower_left(bs)
        x_bf = [x.astype(BF16) for x in xs]
        ys = [jnp.dot(jnp.where(sel, l, zero), _expand(x, diag_ones_ref), preferred_element_type=F32)
              for l, x in zip(l_bf, x_bf)]
        xs = [x - jnp.dot(xb, _expand(y.astype(BF16), diag_ones_ref), preferred_element_type=F32)
              for x, xb, y in zip(xs, x_bf, ys)]
        bs *= 2
    return xs


def _mix_prompt_kernel(x_ref, nmix_ref, wmain_ref, wtail_ref, avg_ref, aog_ref, bog_ref, wsp_ref, bsp_ref,
                       cw_ref, abp_ref, wout_ref,
                       xo_ref, s_ref, ct_ref,
                       zlast_ref, ob_ref, bd_ref, wgate_ref):
    tl = x_ref.shape[0]
    step = pl.program_id(1)

    @pl.when((pl.program_id(0) == 0) & (step == 0))
    def _():
        wgate_ref[...] = wtail_ref[:, 2 * N_HEADS:]

    @pl.when(step == 0)
    def _():
        s_ref[...] = jnp.zeros_like(s_ref)
        zlast_ref[...] = jnp.zeros_like(zlast_ref)

    x = x_ref[...]
    xn = _rms(x, nmix_ref[...]).astype(BF16)
    def in_proj(lo, hi):
        return jnp.dot(xn, wmain_ref[:, lo:hi], preferred_element_type=F32)

    zab = jnp.dot(xn, wtail_ref[:, :LANES], preferred_element_type=F32)
    first_tile = lax.broadcasted_iota(jnp.int32, (8, 1), 0)
    qkv_parts = []
    for part in range(3):
        cols = slice(part * B_WIDTH, (part + 1) * B_WIDTH)
        zc = in_proj(OFF_QKV + part * B_WIDTH, OFF_QKV + (part + 1) * B_WIDTH)
        cw = cw_ref[:, cols]
        carried = zlast_ref[:, cols]
        y = zc * cw[CONV_W - 1:CONV_W]
        for d in range(1, CONV_W):
            rolled = pltpu.roll(zc, d, axis=0)
            top = jnp.where(first_tile < d, pltpu.roll(carried, d, axis=0), rolled[0:8])
            y = y + jnp.concatenate([top, rolled[8:]], axis=0) * cw[CONV_W - 1 - d:CONV_W - d]
        zlast_ref[:, cols] = zc[tl - 8:tl]
        ct_ref[:, cols] = zc[tl - (CONV_W - 1):tl]
        qkv_parts.append(_silu(y))

    uv = jax.nn.gelu(in_proj(0, OFF_QKV))
    row = lax.broadcasted_iota(jnp.int32, (CHUNK_A, CHUNK_A), 0)
    col = lax.broadcasted_iota(jnp.int32, (CHUNK_A, CHUNK_A), 1)
    causal = col <= row
    for h in range(N_HEADS):
        hs = slice(h * HEAD_DIM, (h + 1) * HEAD_DIM)
        u_h = uv[:, hs]
        v_h = _rms(uv[:, A_WIDTH + h * HEAD_DIM:A_WIDTH + (h + 1) * HEAD_DIM], avg_ref[...]).astype(BF16)
        w_h = jnp.where(causal, wsp_ref[h], 0.0).astype(BF16)
        bias_h = bsp_ref[:, h:h + 1]
        for c in range(tl // CHUNK_A):
            rs = slice(c * CHUNK_A, (c + 1) * CHUNK_A)
            mixed = jnp.dot(w_h, v_h[rs], preferred_element_type=F32) + bias_h
            ob_ref[rs, hs] = _rms(u_h[rs] * mixed, aog_ref[...]).astype(BF16)

    z_gate = jnp.dot(xn, wgate_ref[...], preferred_element_type=F32)
    abp = abp_ref[...]
    g = -jnp.exp(abp[0:1]) * _softplus(zab + abp[1:2])
    beta = jax.nn.sigmoid(zab)

    r2 = lax.broadcasted_iota(jnp.int32, (GROUP, GROUP), 0)
    c2 = lax.broadcasted_iota(jnp.int32, (GROUP, GROUP), 1)
    same = (r2 // CHUNK_D) == (c2 // CHUNK_D)
    strict_bd = same & (c2 < r2)
    col_ones = jnp.concatenate([jnp.where(same & (c2 <= r2), 1.0, 0.0), jnp.where(same, 1.0, 0.0)],
                               axis=0).astype(BF16)
    upper_ones = jnp.where(same & (r2 <= c2), 1.0, 0.0).astype(BF16)
    bd_ref[...] = jnp.where(same, 1.0, 0.0).astype(BF16)
    n_grp = tl // GROUP
    gam_parts, glast_parts, gam_t = [], [], []
    for gi in range(n_grp):
        g_grp = g[gi * GROUP:(gi + 1) * GROUP]
        by_col = jnp.dot(col_ones, jnp.concatenate(_split3(g_grp), axis=1), preferred_element_type=F32)
        by_col = by_col[:, :LANES] + by_col[:, LANES:2 * LANES] + by_col[:, 2 * LANES:]
        gam_parts.append(by_col[:GROUP])
        glast_parts.append(by_col[GROUP:])
        by_row = jnp.dot(jnp.concatenate(_split3(g_grp.T), axis=0), upper_ones, preferred_element_type=F32)
        gam_t.append(by_row[:LANES] + by_row[LANES:2 * LANES] + by_row[2 * LANES:])
    gam = jnp.concatenate(gam_parts, axis=0)
    glast = jnp.concatenate(glast_parts, axis=0)

    rb = lax.broadcasted_iota(jnp.int32, (CHUNK_D, CHUNK_D), 0)
    cb = lax.broadcasted_iota(jnp.int32, (CHUNK_D, CHUNK_D), 1)
    incl = cb <= rb

    heads, a_packed, rhs = [], [], []
    for h in range(N_HEADS):
        q_h = _l2(qkv_parts[0][:, h * HEAD_DIM:(h + 1) * HEAD_DIM]) * (HEAD_DIM ** -0.5)
        k_h = _l2(qkv_parts[1][:, h * HEAD_DIM:(h + 1) * HEAD_DIM])
        v_h = qkv_parts[2][:, h * HEAD_DIM:(h + 1) * HEAD_DIM]
        gc_h = gam[:, h:h + 1]
        gl_h = glast[:, h:h + 1]
        bc_h = beta[:, N_HEADS + h:N_HEADS + h + 1]
        eg_h = jnp.exp(gc_h)
        for gi in range(n_grp):
            gs = slice(gi * GROUP, (gi + 1) * GROUP)
            kk = _dot_nt(k_h[gs], k_h[gs])
            decay = jnp.exp(jnp.where(strict_bd, gc_h[gs] - gam_t[gi][h:h + 1, :], 0.0))
            a_packed.append(_fold(jnp.where(strict_bd, bc_h[gs] * kk * decay, 0.0)))
        rhs.append(_split(jnp.concatenate([bc_h * v_h, (bc_h * eg_h) * k_h], axis=1)))
        heads.append((q_h * eg_h, q_h, k_h, k_h * jnp.exp(gl_h - gc_h), gc_h, jnp.exp(gl_h)))
    inv_split = [_split(inv) for inv in _unit_lower_inverses_packed(a_packed, bd_ref)]
    sol = []
    for h in range(N_HEADS):
        sol.append([_dot3(_expand(inv_split[h * n_grp + gi][0], bd_ref), _expand(inv_split[h * n_grp + gi][1], bd_ref),
                          rhs[h][0][gi * GROUP:(gi + 1) * GROUP], rhs[h][1][gi * GROUP:(gi + 1) * GROUP])
                    for gi in range(n_grp)])

    for i in range(tl // CHUNK_D):
        rs = slice(i * CHUNK_D, (i + 1) * CHUNK_D)
        gi, j = divmod(i, GROUP // CHUNK_D)
        ls = slice(j * CHUNK_D, (j + 1) * CHUNK_D)
        for h in range(N_HEADS):
            qb_h, q_h, k_h, kend_h, gc_h, btot_h = heads[h]
            decay = jnp.where(incl, jnp.exp(jnp.where(incl, gc_h[rs] - gam_t[gi][h:h + 1, ls], 0.0)), 0.0)
            qk = _dot_nt(q_h[rs], k_h[rs]) * decay
            s_old = s_ref[h]
            from_s = _dot(jnp.concatenate([sol[h][gi][ls, HEAD_DIM:], qb_h[rs]], axis=0), s_old)
            u = sol[h][gi][ls, :HEAD_DIM] - from_s[:CHUNK_D]
            from_u = _dot(jnp.concatenate([qk, kend_h[rs].T], axis=0), u)
            o = from_s[CHUNK_D:] + from_u[:CHUNK_D]
            s_ref[h] = btot_h[i * CHUNK_D:i * CHUNK_D + 1] * s_old + from_u[CHUNK_D:]
            gate = z_gate[rs, h * HEAD_DIM:(h + 1) * HEAD_DIM]
            ob_ref[rs, A_WIDTH + h * HEAD_DIM:A_WIDTH + (h + 1) * HEAD_DIM] = (
                _rms(o, bog_ref[...]) * _silu(gate)).astype(BF16)

    xo_ref[...] = x + jnp.dot(ob_ref[...], wout_ref[...], preferred_element_type=F32)


def _mixer_weight_specs(layer):
    return [_layer_spec((1, D_MODEL), layer), _layer_spec((D_MODEL, Z_FRONT), 0), _layer_spec((D_MODEL, Z_TAIL), 0),
            _layer_spec((1, HEAD_DIM), layer), _layer_spec((1, HEAD_DIM), layer), _layer_spec((1, HEAD_DIM), layer)]


def _mixer_weights(w):
    return [w["n_mix"], w["w_front"], w["w_tail"], w["a_v_gain"], w["a_out_gain"], w["b_out_gain"]]


def _mix_prompt(x, layer, w):
    bsz, length, _ = x.shape
    tl = MIX_ROWS
    row_spec = pl.BlockSpec((None, tl, D_MODEL), lambda b, t: (b, t, 0))
    in_specs = [row_spec] + _mixer_weight_specs(layer) + [
        _layer_spec((N_HEADS, CHUNK_A, CHUNK_A), layer), _layer_spec((CHUNK_A, N_HEADS), layer),
        _layer_spec((CONV_W, 3 * B_WIDTH), layer), _layer_spec((2, LANES), layer), _layer_spec((D_MODEL, D_MODEL), 0)]
    out_specs = [row_spec,
                 pl.BlockSpec((None, N_HEADS, HEAD_DIM, HEAD_DIM), lambda b, t: (b, 0, 0, 0)),
                 pl.BlockSpec((None, CONV_W - 1, 3 * B_WIDTH), lambda b, t: (b, 0, 0))]
    out_shape = [jax.ShapeDtypeStruct(x.shape, F32),
                 jax.ShapeDtypeStruct((bsz, N_HEADS, HEAD_DIM, HEAD_DIM), F32),
                 jax.ShapeDtypeStruct((bsz, CONV_W - 1, 3 * B_WIDTH), F32)]
    return pl.pallas_call(
        _mix_prompt_kernel,
        grid=(bsz, length // tl),
        in_specs=in_specs,
        out_specs=out_specs,
        out_shape=out_shape,
        scratch_shapes=[pltpu.VMEM((8, 3 * B_WIDTH), F32), pltpu.VMEM((tl, D_MODEL), BF16),
                        pltpu.VMEM((GROUP, GROUP), BF16), pltpu.VMEM((D_MODEL, B_WIDTH), BF16)],
        compiler_params=pltpu.CompilerParams(dimension_semantics=("arbitrary", "arbitrary"),
                                             vmem_limit_bytes=VMEM_LIMIT_BYTES),
    )(x, *_mixer_weights(w), w["a_w_s"], w["a_b_s_t"], w["b_conv"], w["ab_par"], w["w_out"])


def _mix_sample_kernel(x_ref, s_ref, cpad_ref, s_all_ref, nmix_ref, wmain_ref, wtail_ref, avg_ref, aog_ref, bog_ref,
                       coef_ref, bias_ref, cw_ref, abp_ref, wout_ref,
                       xo_ref, so_ref, zq_ref, vo_ref,
                       z_ref, ztail_ref, ob_ref, *, n_tok):
    del s_all_ref
    rows = zq_ref.shape[0]
    nb = rows // n_tok
    step = pl.program_id(0)

    @pl.when(step == 0)
    def _():
        xn = _rms(x_ref[...], nmix_ref[...]).astype(BF16)
        z_ref[...] = jnp.dot(xn, wmain_ref[...], preferred_element_type=F32)
        ztail_ref[...] = jnp.dot(xn, wtail_ref[...], preferred_element_type=F32)

    here = pl.ds(pl.multiple_of(step * rows, rows), rows)
    z = z_ref[here, :]
    z_tail = ztail_ref[here, :]
    zab = z_tail[:, :LANES]
    tok = lax.broadcasted_iota(jnp.int32, (rows, 1), 0) % n_tok

    def prev(a, d):
        return pltpu.roll(a, d, axis=0)

    def prev_or_zero(a, d):
        return a if d == 0 else jnp.where(tok >= d, prev(a, d), 0.0)

    def per_head(fn, a):
        return jnp.concatenate([fn(a[:, h * HEAD_DIM:(h + 1) * HEAD_DIM]) for h in range(N_HEADS)], axis=1)

    uv = jax.nn.gelu(z[:, :2 * A_WIDTH])
    vn = per_head(lambda a: _rms(a, avg_ref[...]), uv[:, A_WIDTH:])
    vo_ref[...] = vn
    mixed = bias_ref[...]
    for d in range(n_tok):
        mixed = mixed + coef_ref[d] * prev_or_zero(vn, d)
    ob_ref[here, :A_WIDTH] = per_head(lambda a: _rms(a, aog_ref[...]), uv[:, :A_WIDTH] * mixed)

    zq = z[:, OFF_QKV:OFF_GATE]
    zq_ref[...] = zq
    cpad = cpad_ref[...]
    cw = cw_ref[...]
    y = zq * cw[CONV_W - 1:CONV_W]
    for d in range(1, CONV_W):
        carried = pltpu.roll(cpad, rows - (n_tok - d), axis=0)
        y = y + jnp.where(tok >= d, prev(zq, d), carried) * cw[CONV_W - 1 - d:CONV_W - d]
    qkv = _silu(y)

    abp = abp_ref[...]
    g_all = -jnp.exp(abp[0:1]) * _softplus(zab + abp[1:2])
    beta_all = jax.nn.sigmoid(zab)

    sub = lax.broadcasted_iota(jnp.int32, (8, 1), 0)
    first_half = sub < n_tok
    o_heads, kend_heads, u_heads, btot_heads = [], [], [], []
    for h in range(N_HEADS):
        q = _l2(qkv[:, h * HEAD_DIM:(h + 1) * HEAD_DIM]) * (HEAD_DIM ** -0.5)
        k = _l2(qkv[:, B_WIDTH + h * HEAD_DIM:B_WIDTH + (h + 1) * HEAD_DIM])
        v = qkv[:, 2 * B_WIDTH + h * HEAD_DIM:2 * B_WIDTH + (h + 1) * HEAD_DIM]
        g = jnp.broadcast_to(g_all[:, h:h + 1], (rows, HEAD_DIM))
        beta = jnp.broadcast_to(beta_all[:, N_HEADS + h:N_HEADS + h + 1], (rows, HEAD_DIM))
        gam = g
        for d in range(1, n_tok):
            gam = gam + prev_or_zero(g, d)
        gam_last = jnp.where(tok == n_tok - 1, gam, 0.0)
        for d in range(1, n_tok):
            gam_last = gam_last + jnp.where(tok == n_tok - 1 - d, pltpu.roll(gam, rows - d, axis=0), 0.0)
        eg = jnp.exp(gam)

        def decay_to(d, gam=gam):
            return jnp.exp(jnp.where(tok >= d, gam - prev(gam, d), 0.0))

        a_sub = [None] + [jnp.where(tok >= d, beta * jnp.sum(k * prev(k, d), axis=-1, keepdims=True) * decay_to(d),
                                    0.0) for d in range(1, n_tok)]
        def forward_substitute(rhs, a_sub=a_sub):
            sol = rhs
            for t in range(1, n_tok):
                acc = rhs
                for d in range(1, t + 1):
                    acc = acc - a_sub[d] * prev(sol, d)
                sol = jnp.where(tok == t, acc, sol)
            return sol

        w_blk = forward_substitute(beta * v)
        kb_blk = forward_substitute((beta * eg) * k)
        qb = q * eg

        kb_s, qb_s = [], []
        for p in range(rows // 8):
            kb_t, qb_t = kb_blk[8 * p:8 * p + 8], qb[8 * p:8 * p + 8]
            f0 = _dot(jnp.where(first_half, kb_t, pltpu.roll(qb_t, n_tok, axis=0)), s_ref[2 * p, h])
            f1 = _dot(jnp.where(first_half, pltpu.roll(kb_t, n_tok, axis=0), qb_t), s_ref[2 * p + 1, h])
            kb_s.append(jnp.where(first_half, f0, pltpu.roll(f1, n_tok, axis=0)))
            qb_s.append(jnp.where(first_half, pltpu.roll(f0, n_tok, axis=0), f1))
        u = w_blk - jnp.concatenate(kb_s, axis=0)
        o = jnp.concatenate(qb_s, axis=0)
        for d in range(n_tok):
            qk = jnp.where(tok >= d, jnp.sum(q * prev(k, d), axis=-1, keepdims=True) * decay_to(d), 0.0)
            o = o + qk * prev_or_zero(u, d)
        o_heads.append(o)
        kend_heads.append(k * jnp.exp(gam_last - gam))
        u_heads.append(u)
        btot_heads.append(jnp.broadcast_to(jnp.exp(gam_last), (rows, HEAD_DIM)))

    kend_t = jnp.concatenate(kend_heads, axis=0).T
    u_all = jnp.concatenate(u_heads, axis=0).astype(BF16)
    owner = lax.broadcasted_iota(jnp.int32, (1, N_HEADS * rows), 1) // n_tok
    for h in range(N_HEADS):
        for b in range(nb):
            mine = jnp.where(owner == h * nb + b, kend_t, 0.0).astype(BF16)
            last = b * n_tok + n_tok - 1
            so_ref[b, h] = btot_heads[h][last:last + 1] * s_ref[b, h] + jnp.dot(
                mine, u_all, preferred_element_type=F32)

    gate = z_tail[:, 2 * N_HEADS:]
    ob_ref[here, A_WIDTH:] = per_head(lambda a: _rms(a, bog_ref[...]), jnp.concatenate(o_heads, axis=1)) * _silu(gate)

    @pl.when(step == pl.num_programs(0) - 1)
    def _():
        xo_ref[...] = x_ref[...] + _dot(ob_ref[...], wout_ref[...])


def _mix_sample(x, layer, state_s, cpad, s_all, w, n_tok):
    rows_total = x.shape[0]
    nb = SAMPLE_GROUP
    rows = nb * n_tok
    all_rows = lambda width: pl.BlockSpec((rows_total, width), lambda i: (0, 0))
    row_spec = lambda width: pl.BlockSpec((rows, width), lambda i: (i, 0))
    s_spec = pl.BlockSpec((None, nb, N_HEADS, HEAD_DIM, HEAD_DIM), lambda i: (layer, i, 0, 0, 0))
    in_specs = [all_rows(D_MODEL), s_spec, pl.BlockSpec((None, rows, 3 * B_WIDTH), lambda i: (layer, i, 0)),
                pl.BlockSpec(memory_space=pl.ANY)] + _mixer_weight_specs(layer) + [
        _layer_spec((n_tok, rows, A_WIDTH), layer), _layer_spec((rows, A_WIDTH), layer),
        _layer_spec((CONV_W, 3 * B_WIDTH), layer), _layer_spec((2, LANES), layer), _layer_spec((D_MODEL, D_MODEL), 0)]
    out_specs = [all_rows(D_MODEL), s_spec, row_spec(3 * B_WIDTH), row_spec(A_WIDTH)]
    out_shape = [jax.ShapeDtypeStruct(x.shape, F32), jax.ShapeDtypeStruct(state_s.shape, F32),
                 jax.ShapeDtypeStruct((rows_total, 3 * B_WIDTH), F32), jax.ShapeDtypeStruct((rows_total, A_WIDTH), F32)]
    return pl.pallas_call(
        functools.partial(_mix_sample_kernel, n_tok=n_tok),
        grid=(rows_total // rows,),
        in_specs=in_specs,
        out_specs=out_specs,
        out_shape=out_shape,
        input_output_aliases={3: 1},
        scratch_shapes=[pltpu.VMEM((rows_total, Z_FRONT), F32), pltpu.VMEM((rows_total, Z_TAIL), F32),
                        pltpu.VMEM((rows_total, D_MODEL), F32)],
        compiler_params=pltpu.CompilerParams(dimension_semantics=("arbitrary",),
                                             vmem_limit_bytes=VMEM_LIMIT_BYTES),
    )(x, state_s, cpad, s_all, *_mixer_weights(w), w["a_coef"], w["a_bias"], w["b_conv"], w["ab_par"], w["w_out"])


def _prep_tables(n_tok, norm_ffn1, norm_mix, a_v_gain, a_spatial_w, a_spatial_b, a_out_gain, b_conv_w, b_a_log,
                 b_dt_bias, b_out_gain, norm_ffn2, norm_ple):
    par_pad = jnp.zeros((DEPTH, LANES - N_HEADS), F32)
    ab_par = jnp.stack([jnp.concatenate([b_a_log, par_pad], axis=1),
                        jnp.concatenate([b_dt_bias, par_pad], axis=1)], axis=1)

    def sample_rows(a):
        return jnp.tile(jnp.repeat(jnp.transpose(a, (0, 2, 1)), HEAD_DIM, axis=2), (1, SAMPLE_GROUP, 1))

    ws_small = a_spatial_w[:, :, :n_tok, :n_tok]
    a_coef = jnp.stack([sample_rows(jnp.pad(jnp.diagonal(ws_small, offset=-d, axis1=2, axis2=3),
                                            ((0, 0), (0, 0), (d, 0)))) for d in range(n_tok)], axis=1)
    return dict(
        n_f1=norm_ffn1[:, None], n_mix=norm_mix[:, None], ab_par=ab_par,
        a_v_gain=a_v_gain[:, None], a_out_gain=a_out_gain[:, None], b_out_gain=b_out_gain[:, None],
        a_w_s=a_spatial_w, a_b_s_t=jnp.transpose(a_spatial_b, (0, 2, 1)),
        a_coef=a_coef, a_bias=sample_rows(a_spatial_b[:, :, :n_tok]),
        b_conv=b_conv_w, n_f2=norm_ffn2[:, None], n_ple=norm_ple[:, None],
    )


def kernel(x_prompt, x_sample, state_S, state_conv, p_prompt, p_sample, norm_ffn1, w_ffn1_in, w_ffn1_out, norm_mix, w_in, a_v_gain, a_spatial_w, a_spatial_b, a_out_gain, b_conv_w, b_a_log, b_dt_bias, b_out_gain, w_out, norm_ffn2, w_ffn2_in, w_ffn2_out, norm_ple, w_ple_gate, w_ple_proj, final_norm):
    bsz, length, _ = x_prompt.shape
    dec_bsz, n_tok, _ = x_sample.shape
    assert length % MIX_ROWS == 0 and MIX_ROWS % CHUNK_A == 0 and MIX_ROWS % GROUP == 0
    assert dec_bsz % SAMPLE_GROUP == 0
    assert n_tok % CHUNK_A != 0 and n_tok % CHUNK_D != 0
    assert 2 * n_tok == 8 and (N_HEADS * SAMPLE_GROUP * n_tok) % HEAD_DIM == 0 and n_tok >= CONV_W - 1

    w = _prep_tables(n_tok, norm_ffn1, norm_mix, a_v_gain, a_spatial_w, a_spatial_b, a_out_gain, b_conv_w, b_a_log,
                     b_dt_bias, b_out_gain, norm_ffn2, norm_ple)
    ffn1_w = (w_ffn1_in[0:1].astype(BF16), w_ffn1_out[0:1].astype(BF16))
    in_proj_splits = ((0, Z_FRONT), (Z_FRONT, Z_FRONT + Z_TAIL))
    final = final_norm[None, None]
    xp = x_prompt.reshape(bsz * length, D_MODEL)
    xs = x_sample.reshape(dec_bsz * n_tok, D_MODEL)
    pp = p_prompt.reshape(DEPTH, bsz * length, PLE_DIM)
    ps = p_sample.reshape(DEPTH, dec_bsz * n_tok, PLE_DIM)
    keep = CONV_W - 1
    cpad = jnp.pad(state_conv, ((0, 0), (0, 0), (n_tok - keep, 0), (0, 0))).reshape(DEPTH, dec_bsz * n_tok, 3 * B_WIDTH)

    s_prompt, c_prompt, c_sample, v_sample = [], [], [], []
    s_sample = jnp.zeros(state_S.shape, F32)
    for i in range(DEPTH):
        last = dict(final_gain=final) if i == DEPTH - 1 else {}

        xp, (f2_in, f2_out, w_front, w_tail, w_o, ple_gate, ple_proj) = _ffn(
            xp, i, w["n_f1"], *ffn1_w,
            casts=[(w_ffn2_in, i, None), (w_ffn2_out, i, None), (w_in, i, in_proj_splits), (w_out, i, None),
                   (w_ple_gate, i, None), (w_ple_proj, i, None)])
        xs = _ffn(xs, i, w["n_f1"], *ffn1_w)
        mix_w = dict(w, w_front=w_front, w_tail=w_tail, w_out=w_o)
        xp, sp, cp = _mix_prompt(xp.reshape(bsz, length, D_MODEL), i, mix_w)
        xs, s_sample, zq, vs = _mix_sample(xs, i, state_S, cpad, s_sample, mix_w, n_tok)
        xp = xp.reshape(bsz * length, D_MODEL)
        ple = (w["n_ple"], ple_gate, ple_proj)
        if i < DEPTH - 1:
            xp, ffn1_w = _ffn(xp, i, w["n_f2"], f2_in, f2_out, ple=(pp,) + ple,
                              casts=[(w_ffn1_in, i + 1, None), (w_ffn1_out, i + 1, None)])
        else:
            xp = _ffn(xp, i, w["n_f2"], f2_in, f2_out, ple=(pp,) + ple, **last)
        xs = _ffn(xs, i, w["n_f2"], f2_in, f2_out, ple=(ps,) + ple, **last)

        s_prompt.append(sp)
        c_prompt.append(cp)
        c_sample.append(zq.reshape(dec_bsz, n_tok, 3 * B_WIDTH)[:, n_tok - keep:])
        v_sample.append(vs.reshape(dec_bsz, n_tok, N_HEADS, HEAD_DIM))

    return (xp.reshape(bsz, length, D_MODEL), xs.reshape(dec_bsz, n_tok, D_MODEL), jnp.stack(s_prompt),
            jnp.stack(c_prompt), s_sample, jnp.stack(c_sample), jnp.stack(v_sample))
```

```python
import functools

import jax
import jax.numpy as jnp
from jax import lax
from jax.experimental import pallas as pl
from jax.experimental.pallas import tpu as pltpu

F32 = jnp.float32
BF16 = jnp.bfloat16
EPS = 1e-6

D_MODEL = 1024
D_FF = 2816
DEPTH = 4
N_HEADS = 4
HEAD_DIM = 128
A_WIDTH = N_HEADS * HEAD_DIM
B_WIDTH = N_HEADS * HEAD_DIM
CHUNK_A = 128
CHUNK_D = 64
CONV_W = 4
PLE_DIM = 256
Z_FRONT = 2 * A_WIDTH + 3 * B_WIDTH
Z_TAIL = 2 * N_HEADS + B_WIDTH
OFF_QKV = 2 * A_WIDTH
OFF_GATE = OFF_QKV + 3 * B_WIDTH

VMEM_LIMIT_BYTES = 52 * 1024 * 1024
LANES = 128
BF16_SUBLANES = 16
MXU_N = 256
FFN_ROWS = 512
GROUP = 4 * CHUNK_D
MIX_ROWS = 512
SAMPLE_GROUP = 8


def _rms(x, gain):
    return x * lax.rsqrt(jnp.mean(x * x, axis=-1, keepdims=True) + EPS) * gain


def _l2(x):
    return x * lax.rsqrt(jnp.sum(x * x, axis=-1, keepdims=True) + EPS)


def _silu(x):
    return x * jax.nn.sigmoid(x)


def _softplus(x):
    return jnp.maximum(x, 0.0) + jnp.log1p(jnp.exp(-jnp.abs(x)))


def _dot(a, b):
    return jnp.dot(a.astype(BF16), b.astype(BF16), preferred_element_type=F32)


def _dot_nt(a, b):
    return lax.dot_general(a.astype(BF16), b.astype(BF16), (((1,), (1,)), ((), ())),
                           preferred_element_type=F32)


def _split3(a):
    p1 = a.astype(BF16)
    r1 = a - p1.astype(F32)
    p2 = r1.astype(BF16)
    p3 = (r1 - p2.astype(F32)).astype(BF16)
    return p1, p2, p3


def _layer_spec(shape, layer, block=None):
    index = (layer,) + (0,) * (len(shape) - 1) + (0 if block is None else block,)
    return pl.BlockSpec((None,) + tuple(shape), lambda *_: index, pipeline_mode=pl.Buffered(1))


def _ffn_kernel(*refs, with_ple, with_final, cast_plan):
    x_ref, gain_ref, wg_ref, wu_ref, wo_ref = refs[:5]
    n_in = 5 + 4 * with_ple + with_final
    cast_in = refs[n_in:n_in + len(cast_plan)]
    o_ref = refs[n_in + len(cast_plan)]
    cast_out = iter(refs[n_in + len(cast_plan) + 1:])
    for src_ref, splits in zip(cast_in, cast_plan):
        for lo, hi in splits:
            next(cast_out)[...] = src_ref[:, lo:hi].astype(BF16)
    x = x_ref[...]
    xn = _rms(x, gain_ref[...]).astype(BF16)
    acc = jnp.zeros_like(x)
    for c in range(D_FF // MXU_N):
        sl = slice(c * MXU_N, (c + 1) * MXU_N)
        gate = jnp.dot(xn, wg_ref[:, sl], preferred_element_type=F32)
        up = jnp.dot(xn, wu_ref[:, sl], preferred_element_type=F32)
        h = (_silu(gate) * up).astype(BF16)
        acc = acc + jnp.dot(h, wo_ref[sl, :], preferred_element_type=F32)
    x = x + 0.5 * acc
    if with_ple:
        p_ref, npl_ref, wpg_ref, wpp_ref = refs[5:9]
        emb = _dot(p_ref[...], wpp_ref[...])
        gate = _dot(_rms(x, npl_ref[...]), wpg_ref[...])
        x = x + emb * jax.nn.sigmoid(gate)
    if with_final:
        x = _rms(x, refs[9][...])
    o_ref[...] = x


def _cast_row_blocks(n_rows, n_steps):
    blocks = n_steps
    while n_rows % blocks or (n_rows // blocks) % BF16_SUBLANES:
        assert blocks % 2 == 0, (n_rows, n_steps)
        blocks //= 2
    return blocks


def _ffn(x, layer, gain, w_in, w_out, ple=None, final_gain=None, casts=()):
    rows = x.shape[0]
    tm = min(FFN_ROWS, rows)
    n_steps = rows // tm
    row_spec = pl.BlockSpec((tm, D_MODEL), lambda i: (i, 0))
    in_specs = [row_spec, _layer_spec((1, D_MODEL), layer),
                _layer_spec((D_MODEL, D_FF), 0, block=0), _layer_spec((D_MODEL, D_FF), 0, block=1),
                _layer_spec((D_FF, D_MODEL), 0)]
    args = [x, gain, w_in, w_in, w_out]
    out_specs = [row_spec]
    out_shape = [jax.ShapeDtypeStruct(x.shape, F32)]
    if ple is not None:
        p, n_ple, w_gate, w_proj = ple
        in_specs += [pl.BlockSpec((None, tm, PLE_DIM), lambda i: (layer, i, 0)), _layer_spec((1, D_MODEL), layer),
                     _layer_spec((D_MODEL, D_MODEL), 0), _layer_spec((PLE_DIM, D_MODEL), 0)]
        args += [p, n_ple, w_gate, w_proj]
    if final_gain is not None:
        in_specs.append(_layer_spec((1, D_MODEL), 0))
        args.append(final_gain)
    cast_plan = []
    for src, layer_c, splits in casts:
        _, n_rows, n_cols = src.shape
        splits = tuple(splits) if splits is not None else ((0, n_cols),)
        blocks = _cast_row_blocks(n_rows, n_steps)
        every = n_steps // blocks
        in_specs.append(pl.BlockSpec((None, n_rows // blocks, n_cols),
                                     lambda i, layer_c=layer_c, every=every: (layer_c, i // every, 0)))
        args.append(src)
        for lo, hi in splits:
            out_specs.append(pl.BlockSpec((None, n_rows // blocks, hi - lo),
                                          lambda i, every=every: (0, i // every, 0)))
            out_shape.append(jax.ShapeDtypeStruct((1, n_rows, hi - lo), BF16))
        cast_plan.append(splits)
    out = pl.pallas_call(
        functools.partial(_ffn_kernel, with_ple=ple is not None, with_final=final_gain is not None,
                          cast_plan=tuple(cast_plan)),
        grid=(n_steps,),
        in_specs=in_specs,
        out_specs=out_specs,
        out_shape=out_shape,
        compiler_params=pltpu.CompilerParams(dimension_semantics=("arbitrary",),
                                             vmem_limit_bytes=VMEM_LIMIT_BYTES),
    )(*args)
    return (out[0], out[1:]) if casts else out[0]


def _split(a):
    hi = a.astype(BF16)
    lo = (a - hi.astype(F32)).astype(BF16)
    return hi, lo


def _dot3(a_hi, a_lo, b_hi, b_lo):
    m = a_hi.shape[0]
    both = jnp.dot(jnp.concatenate([a_hi, a_lo], axis=0), b_hi, preferred_element_type=F32)
    return both[:m] + both[m:] + jnp.dot(a_hi, b_lo, preferred_element_type=F32)


def _fold(block_diag):
    n = block_diag.shape[0] // CHUNK_D
    out = block_diag[0:CHUNK_D]
    for g in range(1, n):
        out = out + block_diag[g * CHUNK_D:(g + 1) * CHUNK_D]
    return out


def _expand(packed, diag_ones_ref):
    n = packed.shape[1] // CHUNK_D
    return jnp.concatenate([packed] * n, axis=0) * diag_ones_ref[...]


def _unit_lower_inverses_packed(l_packed_list, diag_ones_ref):
    c, width = l_packed_list[0].shape
    row = lax.broadcasted_iota(jnp.int32, (c, width), 0)
    col = lax.broadcasted_iota(jnp.int32, (c, width), 1) % c
    zero = jnp.zeros((), BF16)

    def lower_left(bs):
        return (row // (2 * bs) == col // (2 * bs)) & ((row // bs) % 2 == 1) & ((col // bs) % 2 == 0)

    l_bf = [l.astype(BF16) for l in l_packed_list]
    xs = [jnp.where(row == col, 1.0, 0.0) - jnp.where(lower_left(1), l, 0.0) for l in l_packed_list]
    bs = 2
    while bs < c:
        sel = lower_left(bs)
        x_bf = [x.astype(BF16) for x in xs]
        ys = [jnp.dot(jnp.where(sel, l, zero), _expand(x, diag_ones_ref), preferred_element_type=F32)
              for l, x in zip(l_bf, x_bf)]
        xs = [x - jnp.dot(xb, _expand(y.astype(BF16), diag_ones_ref), preferred_element_type=F32)
              for x, xb, y in zip(xs, x_bf, ys)]
        bs *= 2
    return xs


def _mix_prompt_kernel(x_ref, nmix_ref, wmain_ref, wtail_ref, avg_ref, aog_ref, bog_ref, wsp_ref, bsp_ref,
                       cw_ref, abp_ref, wout_ref,
                       xo_ref, s_ref, ct_ref,
                       zlast_ref, ob_ref, bd_ref, wgate_ref):
    tl = x_ref.shape[0]
    step = pl.program_id(1)

    @pl.when((pl.program_id(0) == 0) & (step == 0))
    def _():
        wgate_ref[...] = wtail_ref[:, 2 * N_HEADS:]

    @pl.when(step == 0)
    def _():
        s_ref[...] = jnp.zeros_like(s_ref)
        zlast_ref[...] = jnp.zeros_like(zlast_ref)

    x = x_ref[...]
    xn = _rms(x, nmix_ref[...]).astype(BF16)
    def in_proj(lo, hi):
        return jnp.dot(xn, wmain_ref[:, lo:hi], preferred_element_type=F32)

    zab = jnp.dot(xn, wtail_ref[:, :LANES], preferred_element_type=F32)
    first_tile = lax.broadcasted_iota(jnp.int32, (8, 1), 0)
    qkv_parts = []
    for part in range(3):
        cols = slice(part * B_WIDTH, (part + 1) * B_WIDTH)
        zc = in_proj(OFF_QKV + part * B_WIDTH, OFF_QKV + (part + 1) * B_WIDTH)
        cw = cw_ref[:, cols]
        carried = zlast_ref[:, cols]
        y = zc * cw[CONV_W - 1:CONV_W]
        for d in range(1, CONV_W):
            rolled = pltpu.roll(zc, d, axis=0)
            top = jnp.where(first_tile < d, pltpu.roll(carried, d, axis=0), rolled[0:8])
            y = y + jnp.concatenate([top, rolled[8:]], axis=0) * cw[CONV_W - 1 - d:CONV_W - d]
        zlast_ref[:, cols] = zc[tl - 8:tl]
        ct_ref[:, cols] = zc[tl - (CONV_W - 1):tl]
        qkv_parts.append(_silu(y))

    uv = jax.nn.gelu(in_proj(0, OFF_QKV))
    row = lax.broadcasted_iota(jnp.int32, (CHUNK_A, CHUNK_A), 0)
    col = lax.broadcasted_iota(jnp.int32, (CHUNK_A, CHUNK_A), 1)
    causal = col <= row
    for h in range(N_HEADS):
        hs = slice(h * HEAD_DIM, (h + 1) * HEAD_DIM)
        u_h = uv[:, hs]
        v_h = _rms(uv[:, A_WIDTH + h * HEAD_DIM:A_WIDTH + (h + 1) * HEAD_DIM], avg_ref[...]).astype(BF16)
        w_h = jnp.where(causal, wsp_ref[h], 0.0).astype(BF16)
        bias_h = bsp_ref[:, h:h + 1]
        for c in range(tl // CHUNK_A):
            rs = slice(c * CHUNK_A, (c + 1) * CHUNK_A)
            mixed = jnp.dot(w_h, v_h[rs], preferred_element_type=F32) + bias_h
            ob_ref[rs, hs] = _rms(u_h[rs] * mixed, aog_ref[...]).astype(BF16)

    z_gate = jnp.dot(xn, wgate_ref[...], preferred_element_type=F32)
    abp = abp_ref[...]
    g = -jnp.exp(abp[0:1]) * _softplus(zab + abp[1:2])
    beta = jax.nn.sigmoid(zab)

    r2 = lax.broadcasted_iota(jnp.int32, (GROUP, GROUP), 0)
    c2 = lax.broadcasted_iota(jnp.int32, (GROUP, GROUP), 1)
    same = (r2 // CHUNK_D) == (c2 // CHUNK_D)
    strict_bd = same & (c2 < r2)
    col_ones = jnp.concatenate([jnp.where(same & (c2 <= r2), 1.0, 0.0), jnp.where(same, 1.0, 0.0)],
                               axis=0).astype(BF16)
    upper_ones = jnp.where(same & (r2 <= c2), 1.0, 0.0).astype(BF16)
    bd_ref[...] = jnp.where(same, 1.0, 0.0).astype(BF16)
    n_grp = tl // GROUP
    gam_parts, glast_parts, gam_t = [], [], []
    for gi in range(n_grp):
        g_grp = g[gi * GROUP:(gi + 1) * GROUP]
        by_col = jnp.dot(col_ones, jnp.concatenate(_split3(g_grp), axis=1), preferred_element_type=F32)
        by_col = by_col[:, :LANES] + by_col[:, LANES:2 * LANES] + by_col[:, 2 * LANES:]
        gam_parts.append(by_col[:GROUP])
        glast_parts.append(by_col[GROUP:])
        by_row = jnp.dot(jnp.concatenate(_split3(g_grp.T), axis=0), upper_ones, preferred_element_type=F32)
        gam_t.append(by_row[:LANES] + by_row[LANES:2 * LANES] + by_row[2 * LANES:])
    gam = jnp.concatenate(gam_parts, axis=0)
    glast = jnp.concatenate(glast_parts, axis=0)

    rb = lax.broadcasted_iota(jnp.int32, (CHUNK_D, CHUNK_D), 0)
    cb = lax.broadcasted_iota(jnp.int32, (CHUNK_D, CHUNK_D), 1)
    incl = cb <= rb

    heads, a_packed, rhs = [], [], []
    for h in range(N_HEADS):
        q_h = _l2(qkv_parts[0][:, h * HEAD_DIM:(h + 1) * HEAD_DIM]) * (HEAD_DIM ** -0.5)
        k_h = _l2(qkv_parts[1][:, h * HEAD_DIM:(h + 1) * HEAD_DIM])
        v_h = qkv_parts[2][:, h * HEAD_DIM:(h + 1) * HEAD_DIM]
        gc_h = gam[:, h:h + 1]
        gl_h = glast[:, h:h + 1]
        bc_h = beta[:, N_HEADS + h:N_HEADS + h + 1]
        eg_h = jnp.exp(gc_h)
        for gi in range(n_grp):
            gs = slice(gi * GROUP, (gi + 1) * GROUP)
            kk = _dot_nt(k_h[gs], k_h[gs])
            decay = jnp.exp(jnp.where(strict_bd, gc_h[gs] - gam_t[gi][h:h + 1, :], 0.0))
            a_packed.append(_fold(jnp.where(strict_bd, bc_h[gs] * kk * decay, 0.0)))
        rhs.append(_split(jnp.concatenate([bc_h * v_h, (bc_h * eg_h) * k_h], axis=1)))
        heads.append((q_h * eg_h, q_h, k_h, k_h * jnp.exp(gl_h - gc_h), gc_h, jnp.exp(gl_h)))
    inv_split = [_split(inv) for inv in _unit_lower_inverses_packed(a_packed, bd_ref)]
    sol = []
    for h in range(N_HEADS):
        sol.append([_dot3(_expand(inv_split[h * n_grp + gi][0], bd_ref), _expand(inv_split[h * n_grp + gi][1], bd_ref),
                          rhs[h][0][gi * GROUP:(gi + 1) * GROUP], rhs[h][1][gi * GROUP:(gi + 1) * GROUP])
                    for gi in range(n_grp)])

    for i in range(tl // CHUNK_D):
        rs = slice(i * CHUNK_D, (i + 1) * CHUNK_D)
        gi, j = divmod(i, GROUP // CHUNK_D)
        ls = slice(j * CHUNK_D, (j + 1) * CHUNK_D)
        for h in range(N_HEADS):
            qb_h, q_h, k_h, kend_h, gc_h, btot_h = heads[h]
            decay = jnp.where(incl, jnp.exp(jnp.where(incl, gc_h[rs] - gam_t[gi][h:h + 1, ls], 0.0)), 0.0)
            qk = _dot_nt(q_h[rs], k_h[rs]) * decay
            s_old = s_ref[h]
            from_s = _dot(jnp.concatenate([sol[h][gi][ls, HEAD_DIM:], qb_h[rs]], axis=0), s_old)
            u = sol[h][gi][ls, :HEAD_DIM] - from_s[:CHUNK_D]
            from_u = _dot(jnp.concatenate([qk, kend_h[rs].T], axis=0), u)
            o = from_s[CHUNK_D:] + from_u[:CHUNK_D]
            s_ref[h] = btot_h[i * CHUNK_D:i * CHUNK_D + 1] * s_old + from_u[CHUNK_D:]
            gate = z_gate[rs, h * HEAD_DIM:(h + 1) * HEAD_DIM]
            ob_ref[rs, A_WIDTH + h * HEAD_DIM:A_WIDTH + (h + 1) * HEAD_DIM] = (
                _rms(o, bog_ref[...]) * _silu(gate)).astype(BF16)

    xo_ref[...] = x + jnp.dot(ob_ref[...], wout_ref[...], preferred_element_type=F32)


def _mixer_weight_specs(layer):
    return [_layer_spec((1, D_MODEL), layer), _layer_spec((D_MODEL, Z_FRONT), 0), _layer_spec((D_MODEL, Z_TAIL), 0),
            _layer_spec((1, HEAD_DIM), layer), _layer_spec((1, HEAD_DIM), layer), _layer_spec((1, HEAD_DIM), layer)]


def _mixer_weights(w):
    return [w["n_mix"], w["w_front"], w["w_tail"], w["a_v_gain"], w["a_out_gain"], w["b_out_gain"]]


def _mix_prompt(x, layer, w):
    bsz, length, _ = x.shape
    tl = MIX_ROWS
    row_spec = pl.BlockSpec((None, tl, D_MODEL), lambda b, t: (b, t, 0))
    in_specs = [row_spec] + _mixer_weight_specs(layer) + [
        _layer_spec((N_HEADS, CHUNK_A, CHUNK_A), layer), _layer_spec((CHUNK_A, N_HEADS), layer),
        _layer_spec((CONV_W, 3 * B_WIDTH), layer), _layer_spec((2, LANES), layer), _layer_spec((D_MODEL, D_MODEL), 0)]
    out_specs = [row_spec,
                 pl.BlockSpec((None, N_HEADS, HEAD_DIM, HEAD_DIM), lambda b, t: (b, 0, 0, 0)),
                 pl.BlockSpec((None, CONV_W - 1, 3 * B_WIDTH), lambda b, t: (b, 0, 0))]
    out_shape = [jax.ShapeDtypeStruct(x.shape, F32),
                 jax.ShapeDtypeStruct((bsz, N_HEADS, HEAD_DIM, HEAD_DIM), F32),
                 jax.ShapeDtypeStruct((bsz, CONV_W - 1, 3 * B_WIDTH), F32)]
    return pl.pallas_call(
        _mix_prompt_kernel,
        grid=(bsz, length // tl),
        in_specs=in_specs,
        out_specs=out_specs,
        out_shape=out_shape,
        scratch_shapes=[pltpu.VMEM((8, 3 * B_WIDTH), F32), pltpu.VMEM((tl, D_MODEL), BF16),
                        pltpu.VMEM((GROUP, GROUP), BF16), pltpu.VMEM((D_MODEL, B_WIDTH), BF16)],
        compiler_params=pltpu.CompilerParams(dimension_semantics=("arbitrary", "arbitrary"),
                                             vmem_limit_bytes=VMEM_LIMIT_BYTES),
    )(x, *_mixer_weights(w), w["a_w_s"], w["a_b_s_t"], w["b_conv"], w["ab_par"], w["w_out"])


def _mix_sample_kernel(x_ref, s_ref, cpad_ref, s_all_ref, nmix_ref, wmain_ref, wtail_ref, avg_ref, aog_ref, bog_ref,
                       coef_ref, bias_ref, cw_ref, abp_ref, wout_ref,
                       xo_ref, so_ref, zq_ref, vo_ref,
                       z_ref, ztail_ref, ob_ref, *, n_tok):
    del s_all_ref
    rows = zq_ref.shape[0]
    nb = rows // n_tok
    step = pl.program_id(0)

    @pl.when(step == 0)
    def _():
        xn = _rms(x_ref[...], nmix_ref[...]).astype(BF16)
        z_ref[...] = jnp.dot(xn, wmain_ref[...], preferred_element_type=F32)
        ztail_ref[...] = jnp.dot(xn, wtail_ref[...], preferred_element_type=F32)

    here = pl.ds(pl.multiple_of(step * rows, rows), rows)
    z = z_ref[here, :]
    z_tail = ztail_ref[here, :]
    zab = z_tail[:, :LANES]
    tok = lax.broadcasted_iota(jnp.int32, (rows, 1), 0) % n_tok

    def prev(a, d):
        return pltpu.roll(a, d, axis=0)

    def prev_or_zero(a, d):
        return a if d == 0 else jnp.where(tok >= d, prev(a, d), 0.0)

    def per_head(fn, a):
        return jnp.concatenate([fn(a[:, h * HEAD_DIM:(h + 1) * HEAD_DIM]) for h in range(N_HEADS)], axis=1)

    uv = jax.nn.gelu(z[:, :2 * A_WIDTH])
    vn = per_head(lambda a: _rms(a, avg_ref[...]), uv[:, A_WIDTH:])
    vo_ref[...] = vn
    mixed = bias_ref[...]
    for d in range(n_tok):
        mixed = mixed + coef_ref[d] * prev_or_zero(vn, d)
    ob_ref[here, :A_WIDTH] = per_head(lambda a: _rms(a, aog_ref[...]), uv[:, :A_WIDTH] * mixed)

    zq = z[:, OFF_QKV:OFF_GATE]
    zq_ref[...] = zq
    cpad = cpad_ref[...]
    cw = cw_ref[...]
    y = zq * cw[CONV_W - 1:CONV_W]
    for d in range(1, CONV_W):
        carried = pltpu.roll(cpad, rows - (n_tok - d), axis=0)
        y = y + jnp.where(tok >= d, prev(zq, d), carried) * cw[CONV_W - 1 - d:CONV_W - d]
    qkv = _silu(y)

    abp = abp_ref[...]
    g_all = -jnp.exp(abp[0:1]) * _softplus(zab + abp[1:2])
    beta_all = jax.nn.sigmoid(zab)

    sub = lax.broadcasted_iota(jnp.int32, (8, 1), 0)
    first_half = sub < n_tok
    o_heads, kend_heads, u_heads, btot_heads = [], [], [], []
    for h in range(N_HEADS):
        q = _l2(qkv[:, h * HEAD_DIM:(h + 1) * HEAD_DIM]) * (HEAD_DIM ** -0.5)
        k = _l2(qkv[:, B_WIDTH + h * HEAD_DIM:B_WIDTH + (h + 1) * HEAD_DIM])
        v = qkv[:, 2 * B_WIDTH + h * HEAD_DIM:2 * B_WIDTH + (h + 1) * HEAD_DIM]
        g = jnp.broadcast_to(g_all[:, h:h + 1], (rows, HEAD_DIM))
        beta = jnp.broadcast_to(beta_all[:, N_HEADS + h:N_HEADS + h + 1], (rows, HEAD_DIM))
        gam = g
        for d in range(1, n_tok):
            gam = gam + prev_or_zero(g, d)
        gam_last = jnp.where(tok == n_tok - 1, gam, 0.0)
        for d in range(1, n_tok):
            gam_last = gam_last + jnp.where(tok == n_tok - 1 - d, pltpu.roll(gam, rows - d, axis=0), 0.0)
        eg = jnp.exp(gam)

        def decay_to(d, gam=gam):
            return jnp.exp(jnp.where(tok >= d, gam - prev(gam, d), 0.0))

        a_sub = [None] + [jnp.where(tok >= d, beta * jnp.sum(k * prev(k, d), axis=-1, keepdims=True) * decay_to(d),
                                    0.0) for d in range(1, n_tok)]
        def forward_substitute(rhs, a_sub=a_sub):
            sol = rhs
            for t in range(1, n_tok):
                acc = rhs
                for d in range(1, t + 1):
                    acc = acc - a_sub[d] * prev(sol, d)
                sol = jnp.where(tok == t, acc, sol)
            return sol

        w_blk = forward_substitute(beta * v)
        kb_blk = forward_substitute((beta * eg) * k)
        qb = q * eg

        kb_s, qb_s = [], []
        for p in range(rows // 8):
            kb_t, qb_t = kb_blk[8 * p:8 * p + 8], qb[8 * p:8 * p + 8]
            f0 = _dot(jnp.where(first_half, kb_t, pltpu.roll(qb_t, n_tok, axis=0)), s_ref[2 * p, h])
            f1 = _dot(jnp.where(first_half, pltpu.roll(kb_t, n_tok, axis=0), qb_t), s_ref[2 * p + 1, h])
            kb_s.append(jnp.where(first_half, f0, pltpu.roll(f1, n_tok, axis=0)))
            qb_s.append(jnp.where(first_half, pltpu.roll(f0, n_tok, axis=0), f1))
        u = w_blk - jnp.concatenate(kb_s, axis=0)
        o = jnp.concatenate(qb_s, axis=0)
        for d in range(n_tok):
            qk = jnp.where(tok >= d, jnp.sum(q * prev(k, d), axis=-1, keepdims=True) * decay_to(d), 0.0)
            o = o + qk * prev_or_zero(u, d)
        o_heads.append(o)
        kend_heads.append(k * jnp.exp(gam_last - gam))
        u_heads.append(u)
        btot_heads.append(jnp.broadcast_to(jnp.exp(gam_last), (rows, HEAD_DIM)))

    kend_t = jnp.concatenate(kend_heads, axis=0).T
    u_all = jnp.concatenate(u_heads, axis=0).astype(BF16)
    owner = lax.broadcasted_iota(jnp.int32, (1, N_HEADS * rows), 1) // n_tok
    for h in range(N_HEADS):
        for b in range(nb):
            mine = jnp.where(owner == h * nb + b, kend_t, 0.0).astype(BF16)
            last = b * n_tok + n_tok - 1
            so_ref[b, h] = btot_heads[h][last:last + 1] * s_ref[b, h] + jnp.dot(
                mine, u_all, preferred_element_type=F32)

    gate = z_tail[:, 2 * N_HEADS:]
    ob_ref[here, A_WIDTH:] = per_head(lambda a: _rms(a, bog_ref[...]), jnp.concatenate(o_heads, axis=1)) * _silu(gate)

    @pl.when(step == pl.num_programs(0) - 1)
    def _():
        xo_ref[...] = x_ref[...] + _dot(ob_ref[...], wout_ref[...])


def _mix_sample(x, layer, state_s, cpad, s_all, w, n_tok):
    rows_total = x.shape[0]
    nb = SAMPLE_GROUP
    rows = nb * n_tok
    all_rows = lambda width: pl.BlockSpec((rows_total, width), lambda i: (0, 0))
    row_spec = lambda width: pl.BlockSpec((rows, width), lambda i: (i, 0))
    s_spec = pl.BlockSpec((None, nb, N_HEADS, HEAD_DIM, HEAD_DIM), lambda i: (layer, i, 0, 0, 0))
    in_specs = [all_rows(D_MODEL), s_spec, pl.BlockSpec((None, rows, 3 * B_WIDTH), lambda i: (layer, i, 0)),
                pl.BlockSpec(memory_space=pl.ANY)] + _mixer_weight_specs(layer) + [
        _layer_spec((n_tok, rows, A_WIDTH), layer), _layer_spec((rows, A_WIDTH), layer),
        _layer_spec((CONV_W, 3 * B_WIDTH), layer), _layer_spec((2, LANES), layer), _layer_spec((D_MODEL, D_MODEL), 0)]
    out_specs = [all_rows(D_MODEL), s_spec, row_spec(3 * B_WIDTH), row_spec(A_WIDTH)]
    out_shape = [jax.ShapeDtypeStruct(x.shape, F32), jax.ShapeDtypeStruct(state_s.shape, F32),
                 jax.ShapeDtypeStruct((rows_total, 3 * B_WIDTH), F32), jax.ShapeDtypeStruct((rows_total, A_WIDTH), F32)]
    return pl.pallas_call(
        functools.partial(_mix_sample_kernel, n_tok=n_tok),
        grid=(rows_total // rows,),
        in_specs=in_specs,
        out_specs=out_specs,
        out_shape=out_shape,
        input_output_aliases={3: 1},
        scratch_shapes=[pltpu.VMEM((rows_total, Z_FRONT), F32), pltpu.VMEM((rows_total, Z_TAIL), F32),
                        pltpu.VMEM((rows_total, D_MODEL), F32)],
        compiler_params=pltpu.CompilerParams(dimension_semantics=("arbitrary",),
                                             vmem_limit_bytes=VMEM_LIMIT_BYTES),
    )(x, state_s, cpad, s_all, *_mixer_weights(w), w["a_coef"], w["a_bias"], w["b_conv"], w["ab_par"], w["w_out"])


def _prep_tables(n_tok, norm_ffn1, norm_mix, a_v_gain, a_spatial_w, a_spatial_b, a_out_gain, b_conv_w, b_a_log,
                 b_dt_bias, b_out_gain, norm_ffn2, norm_ple):
    par_pad = jnp.zeros((DEPTH, LANES - N_HEADS), F32)
    ab_par = jnp.stack([jnp.concatenate([b_a_log, par_pad], axis=1),
                        jnp.concatenate([b_dt_bias, par_pad], axis=1)], axis=1)

    def sample_rows(a):
        return jnp.tile(jnp.repeat(jnp.transpose(a, (0, 2, 1)), HEAD_DIM, axis=2), (1, SAMPLE_GROUP, 1))

    ws_small = a_spatial_w[:, :, :n_tok, :n_tok]
    a_coef = jnp.stack([sample_rows(jnp.pad(jnp.diagonal(ws_small, offset=-d, axis1=2, axis2=3),
                                            ((0, 0), (0, 0), (d, 0)))) for d in range(n_tok)], axis=1)
    return dict(
        n_f1=norm_ffn1[:, None], n_mix=norm_mix[:, None], ab_par=ab_par,
        a_v_gain=a_v_gain[:, None], a_out_gain=a_out_gain[:, None], b_out_gain=b_out_gain[:, None],
        a_w_s=a_spatial_w, a_b_s_t=jnp.transpose(a_spatial_b, (0, 2, 1)),
        a_coef=a_coef, a_bias=sample_rows(a_spatial_b[:, :, :n_tok]),
        b_conv=b_conv_w, n_f2=norm_ffn2[:, None], n_ple=norm_ple[:, None],
    )


def kernel(x_prompt, x_sample, state_S, state_conv, p_prompt, p_sample, norm_ffn1, w_ffn1_in, w_ffn1_out, norm_mix, w_in, a_v_gain, a_spatial_w, a_spatial_b, a_out_gain, b_conv_w, b_a_log, b_dt_bias, b_out_gain, w_out, norm_ffn2, w_ffn2_in, w_ffn2_out, norm_ple, w_ple_gate, w_ple_proj, final_norm):
    bsz, length, _ = x_prompt.shape
    dec_bsz, n_tok, _ = x_sample.shape
    assert length % MIX_ROWS == 0 and MIX_ROWS % CHUNK_A == 0 and MIX_ROWS % GROUP == 0
    assert dec_bsz % SAMPLE_GROUP == 0
    assert n_tok % CHUNK_A != 0 and n_tok % CHUNK_D != 0
    assert 2 * n_tok == 8 and N_HEADS * SAMPLE_GROUP * n_tok == HEAD_DIM and n_tok >= CONV_W - 1

    w = _prep_tables(n_tok, norm_ffn1, norm_mix, a_v_gain, a_spatial_w, a_spatial_b, a_out_gain, b_conv_w, b_a_log,
                     b_dt_bias, b_out_gain, norm_ffn2, norm_ple)
    ffn1_w = (w_ffn1_in[0:1].astype(BF16), w_ffn1_out[0:1].astype(BF16))
    in_proj_splits = ((0, Z_FRONT), (Z_FRONT, Z_FRONT + Z_TAIL))
    final = final_norm[None, None]
    xp = x_prompt.reshape(bsz * length, D_MODEL)
    xs = x_sample.reshape(dec_bsz * n_tok, D_MODEL)
    pp = p_prompt.reshape(DEPTH, bsz * length, PLE_DIM)
    ps = p_sample.reshape(DEPTH, dec_bsz * n_tok, PLE_DIM)
    keep = CONV_W - 1
    cpad = jnp.pad(state_conv, ((0, 0), (0, 0), (n_tok - keep, 0), (0, 0))).reshape(DEPTH, dec_bsz * n_tok, 3 * B_WIDTH)

    s_prompt, c_prompt, c_sample, v_sample = [], [], [], []
    s_sample = jnp.zeros(state_S.shape, F32)
    for i in range(DEPTH):
        last = dict(final_gain=final) if i == DEPTH - 1 else {}

        xp, (f2_in, f2_out, w_front, w_tail, w_o, ple_gate, ple_proj) = _ffn(
            xp, i, w["n_f1"], *ffn1_w,
            casts=[(w_ffn2_in, i, None), (w_ffn2_out, i, None), (w_in, i, in_proj_splits), (w_out, i, None),
                   (w_ple_gate, i, None), (w_ple_proj, i, None)])
        xs = _ffn(xs, i, w["n_f1"], *ffn1_w)
        mix_w = dict(w, w_front=w_front, w_tail=w_tail, w_out=w_o)
        xp, sp, cp = _mix_prompt(xp.reshape(bsz, length, D_MODEL), i, mix_w)
        xs, s_sample, zq, vs = _mix_sample(xs, i, state_S, cpad, s_sample, mix_w, n_tok)
        xp = xp.reshape(bsz * length, D_MODEL)
        ple = (w["n_ple"], ple_gate, ple_proj)
        if i < DEPTH - 1:
            xp, ffn1_w = _ffn(xp, i, w["n_f2"], f2_in, f2_out, ple=(pp,) + ple,
                              casts=[(w_ffn1_in, i + 1, None), (w_ffn1_out, i + 1, None)])
        else:
            xp = _ffn(xp, i, w["n_f2"], f2_in, f2_out, ple=(pp,) + ple, **last)
        xs = _ffn(xs, i, w["n_f2"], f2_in, f2_out, ple=(ps,) + ple, **last)

        s_prompt.append(sp)
        c_prompt.append(cp)
        c_sample.append(zq.reshape(dec_bsz, n_tok, 3 * B_WIDTH)[:, n_tok - keep:])
        v_sample.append(vs.reshape(dec_bsz, n_tok, N_HEADS, HEAD_DIM))

    return (xp.reshape(bsz, length, D_MODEL), xs.reshape(dec_bsz, n_tok, D_MODEL), jnp.stack(s_prompt),
            jnp.stack(c_prompt), s_sample, jnp.stack(c_sample), jnp.stack(v_sample))
```

```python
import functools

import jax
import jax.numpy as jnp
from jax import lax
from jax.experimental import pallas as pl
from jax.experimental.pallas import tpu as pltpu

F32 = jnp.float32
BF16 = jnp.bfloat16
EPS = 1e-6

D_MODEL = 1024
D_FF = 2816
DEPTH = 4
N_HEADS = 4
HEAD_DIM = 128
A_WIDTH = N_HEADS * HEAD_DIM
B_WIDTH = N_HEADS * HEAD_DIM
CHUNK_A = 128
CHUNK_D = 64
CONV_W = 4
PLE_DIM = 256
Z_FRONT = 2 * A_WIDTH + 3 * B_WIDTH
Z_TAIL = 2 * N_HEADS + B_WIDTH
OFF_QKV = 2 * A_WIDTH
OFF_GATE = OFF_QKV + 3 * B_WIDTH

VMEM_LIMIT_BYTES = 52 * 1024 * 1024
LANES = 128
BF16_SUBLANES = 16
MXU_N = 256
FFN_ROWS = 512
GROUP = 4 * CHUNK_D
MIX_SEQS = 2
MIX_ROWS = 256
SAMPLE_GROUP = 8


def _rms(x, gain):
    return x * lax.rsqrt(jnp.mean(x * x, axis=-1, keepdims=True) + EPS) * gain


def _l2(x):
    return x * lax.rsqrt(jnp.sum(x * x, axis=-1, keepdims=True) + EPS)


def _silu(x):
    return x * jax.nn.sigmoid(x)


def _softplus(x):
    return jnp.maximum(x, 0.0) + jnp.log1p(jnp.exp(-jnp.abs(x)))


def _dot(a, b):
    return jnp.dot(a.astype(BF16), b.astype(BF16), preferred_element_type=F32)


def _dot_nt(a, b):
    return lax.dot_general(a.astype(BF16), b.astype(BF16), (((1,), (1,)), ((), ())),
                           preferred_element_type=F32)


def _split3(a):
    p1 = a.astype(BF16)
    r1 = a - p1.astype(F32)
    p2 = r1.astype(BF16)
    p3 = (r1 - p2.astype(F32)).astype(BF16)
    return p1, p2, p3


def _layer_spec(shape, layer, block=None):
    index = (layer,) + (0,) * (len(shape) - 1) + (0 if block is None else block,)
    return pl.BlockSpec((None,) + tuple(shape), lambda *_: index, pipeline_mode=pl.Buffered(1))


def _ffn_kernel(*refs, with_ple, with_final, cast_plan):
    x_ref, gain_ref, wg_ref, wu_ref, wo_ref = refs[:5]
    n_in = 5 + 4 * with_ple + with_final
    cast_in = refs[n_in:n_in + len(cast_plan)]
    o_ref = refs[n_in + len(cast_plan)]
    cast_out = iter(refs[n_in + len(cast_plan) + 1:])
    for src_ref, splits in zip(cast_in, cast_plan):
        for lo, hi in splits:
            next(cast_out)[...] = src_ref[:, lo:hi].astype(BF16)
    x = x_ref[...]
    xn = _rms(x, gain_ref[...]).astype(BF16)
    acc = jnp.zeros_like(x)
    for c in range(D_FF // MXU_N):
        sl = slice(c * MXU_N, (c + 1) * MXU_N)
        gate = jnp.dot(xn, wg_ref[:, sl], preferred_element_type=F32)
        up = jnp.dot(xn, wu_ref[:, sl], preferred_element_type=F32)
        h = (_silu(gate) * up).astype(BF16)
        acc = acc + jnp.dot(h, wo_ref[sl, :], preferred_element_type=F32)
    x = x + 0.5 * acc
    if with_ple:
        p_ref, npl_ref, wpg_ref, wpp_ref = refs[5:9]
        emb = _dot(p_ref[...], wpp_ref[...])
        gate = _dot(_rms(x, npl_ref[...]), wpg_ref[...])
        x = x + emb * jax.nn.sigmoid(gate)
    if with_final:
        x = _rms(x, refs[9][...])
    o_ref[...] = x


def _cast_row_blocks(n_rows, n_steps):
    blocks = n_steps
    while n_rows % blocks or (n_rows // blocks) % BF16_SUBLANES:
        assert blocks % 2 == 0, (n_rows, n_steps)
        blocks //= 2
    return blocks


def _ffn(x, layer, gain, w_in, w_out, ple=None, final_gain=None, casts=()):
    rows = x.shape[0]
    tm = min(FFN_ROWS, rows)
    n_steps = rows // tm
    row_spec = pl.BlockSpec((tm, D_MODEL), lambda i: (i, 0))
    in_specs = [row_spec, _layer_spec((1, D_MODEL), layer),
                _layer_spec((D_MODEL, D_FF), 0, block=0), _layer_spec((D_MODEL, D_FF), 0, block=1),
                _layer_spec((D_FF, D_MODEL), 0)]
    args = [x, gain, w_in, w_in, w_out]
    out_specs = [row_spec]
    out_shape = [jax.ShapeDtypeStruct(x.shape, F32)]
    if ple is not None:
        p, n_ple, w_gate, w_proj = ple
        in_specs += [pl.BlockSpec((None, tm, PLE_DIM), lambda i: (layer, i, 0)), _layer_spec((1, D_MODEL), layer),
                     _layer_spec((D_MODEL, D_MODEL), 0), _layer_spec((PLE_DIM, D_MODEL), 0)]
        args += [p, n_ple, w_gate, w_proj]
    if final_gain is not None:
        in_specs.append(_layer_spec((1, D_MODEL), 0))
        args.append(final_gain)
    cast_plan = []
    for src, layer_c, splits in casts:
        _, n_rows, n_cols = src.shape
        splits = tuple(splits) if splits is not None else ((0, n_cols),)
        blocks = _cast_row_blocks(n_rows, n_steps)
        block_of = functools.partial(lambda i, every: i // every, every=n_steps // blocks)
        in_specs.append(pl.BlockSpec((None, n_rows // blocks, n_cols),
                                     lambda i, layer_c=layer_c, block_of=block_of: (layer_c, block_of(i), 0)))
        args.append(src)
        for lo, hi in splits:
            out_specs.append(pl.BlockSpec((None, n_rows // blocks, hi - lo),
                                          lambda i, block_of=block_of: (0, block_of(i), 0)))
            out_shape.append(jax.ShapeDtypeStruct((1, n_rows, hi - lo), BF16))
        cast_plan.append(splits)
    out = pl.pallas_call(
        functools.partial(_ffn_kernel, with_ple=ple is not None, with_final=final_gain is not None,
                          cast_plan=tuple(cast_plan)),
        grid=(n_steps,),
        in_specs=in_specs,
        out_specs=out_specs,
        out_shape=out_shape,
        compiler_params=pltpu.CompilerParams(dimension_semantics=("arbitrary",),
                                             vmem_limit_bytes=VMEM_LIMIT_BYTES),
    )(*args)
    return (out[0], out[1:]) if casts else out[0]


def _split(a):
    hi = a.astype(BF16)
    lo = (a - hi.astype(F32)).astype(BF16)
    return hi, lo


def _dot3(a_hi, a_lo, b_hi, b_lo):
    m = a_hi.shape[0]
    both = jnp.dot(jnp.concatenate([a_hi, a_lo], axis=0), b_hi, preferred_element_type=F32)
    return both[:m] + both[m:] + jnp.dot(a_hi, b_lo, preferred_element_type=F32)


def _fold(block_diag):
    n = block_diag.shape[0] // CHUNK_D
    out = block_diag[0:CHUNK_D]
    for g in range(1, n):
        out = out + block_diag[g * CHUNK_D:(g + 1) * CHUNK_D]
    return out


def _expand(packed, diag_ones_ref):
    n = packed.shape[1] // CHUNK_D
    return jnp.concatenate([packed] * n, axis=0) * diag_ones_ref[...]


def _unit_lower_inverses_packed(l_packed_list, diag_ones_ref):
    c, width = l_packed_list[0].shape
    row = lax.broadcasted_iota(jnp.int32, (c, width), 0)
    col = lax.broadcasted_iota(jnp.int32, (c, width), 1) % c
    zero = jnp.zeros((), BF16)

    def lower_left(bs):
        return (row // (2 * bs) == col // (2 * bs)) & ((row // bs) % 2 == 1) & ((col // bs) % 2 == 0)

    l_bf = [l.astype(BF16) for l in l_packed_list]
    xs = [jnp.where(row == col, 1.0, 0.0) - jnp.where(lower_left(1), l, 0.0) for l in l_packed_list]
    bs = 2
    while bs < c:
        sel = lower_left(bs)
        x_bf = [x.astype(BF16) for x in xs]
        ys = [jnp.dot(jnp.where(sel, l, zero), _expand(x, diag_ones_ref), preferred_element_type=F32)
              for l, x in zip(l_bf, x_bf)]
        xs = [x - jnp.dot(xb, _expand(y.astype(BF16), diag_ones_ref), preferred_element_type=F32)
              for x, xb, y in zip(xs, x_bf, ys)]
        bs *= 2
    return xs


def _mix_prompt_kernel(x_ref, nmix_ref, wmain_ref, wtail_ref, avg_ref, aog_ref, bog_ref, wsp_ref, bsp_ref,
                       cw_ref, abp_ref, wout_ref,
                       xo_ref, s_ref, ct_ref,
                       zlast_ref, ob_ref, bd_ref, wgate_ref):
    n_seq, tl = x_ref.shape[0], x_ref.shape[1]
    rows = n_seq * tl
    step = pl.program_id(1)

    @pl.when((pl.program_id(0) == 0) & (step == 0))
    def _():
        wgate_ref[...] = wtail_ref[:, 2 * N_HEADS:]

    @pl.when(step == 0)
    def _():
        s_ref[...] = jnp.zeros_like(s_ref)
        zlast_ref[...] = jnp.zeros_like(zlast_ref)

    x = x_ref[...].reshape(rows, D_MODEL)
    xn = _rms(x, nmix_ref[...]).astype(BF16)
    def in_proj(lo, hi):
        return jnp.dot(xn, wmain_ref[:, lo:hi], preferred_element_type=F32)

    zab = jnp.dot(xn, wtail_ref[:, :LANES], preferred_element_type=F32)
    first_tile = lax.broadcasted_iota(jnp.int32, (8, 1), 0)
    qkv_parts = []
    for part in range(3):
        cols = slice(part * B_WIDTH, (part + 1) * B_WIDTH)
        zc = in_proj(OFF_QKV + part * B_WIDTH, OFF_QKV + (part + 1) * B_WIDTH)
        cw = cw_ref[:, cols]
        y = zc * cw[CONV_W - 1:CONV_W]
        for d in range(1, CONV_W):
            rolled = pltpu.roll(zc, d, axis=0)
            pieces = []
            for s in range(n_seq):
                carried = pltpu.roll(zlast_ref[s, :, cols], d, axis=0)
                pieces += [jnp.where(first_tile < d, carried, rolled[s * tl:s * tl + 8]),
                           rolled[s * tl + 8:(s + 1) * tl]]
            y = y + jnp.concatenate(pieces, axis=0) * cw[CONV_W - 1 - d:CONV_W - d]
        for s in range(n_seq):
            zlast_ref[s, :, cols] = zc[(s + 1) * tl - 8:(s + 1) * tl]
            ct_ref[s, :, cols] = zc[(s + 1) * tl - (CONV_W - 1):(s + 1) * tl]
        qkv_parts.append(_silu(y))

    uv = jax.nn.gelu(in_proj(0, OFF_QKV))
    row = lax.broadcasted_iota(jnp.int32, (CHUNK_A, CHUNK_A), 0)
    col = lax.broadcasted_iota(jnp.int32, (CHUNK_A, CHUNK_A), 1)
    causal = col <= row
    for h in range(N_HEADS):
        hs = slice(h * HEAD_DIM, (h + 1) * HEAD_DIM)
        u_h = uv[:, hs]
        v_h = _rms(uv[:, A_WIDTH + h * HEAD_DIM:A_WIDTH + (h + 1) * HEAD_DIM], avg_ref[...]).astype(BF16)
        w_h = jnp.where(causal, wsp_ref[h], 0.0).astype(BF16)
        bias_h = bsp_ref[:, h:h + 1]
        for c in range(rows // CHUNK_A):
            rs = slice(c * CHUNK_A, (c + 1) * CHUNK_A)
            mixed = jnp.dot(w_h, v_h[rs], preferred_element_type=F32) + bias_h
            ob_ref[rs, hs] = _rms(u_h[rs] * mixed, aog_ref[...]).astype(BF16)

    z_gate = jnp.dot(xn, wgate_ref[...], preferred_element_type=F32)
    abp = abp_ref[...]
    g = -jnp.exp(abp[0:1]) * _softplus(zab + abp[1:2])
    beta = jax.nn.sigmoid(zab)

    r2 = lax.broadcasted_iota(jnp.int32, (GROUP, GROUP), 0)
    c2 = lax.broadcasted_iota(jnp.int32, (GROUP, GROUP), 1)
    same = (r2 // CHUNK_D) == (c2 // CHUNK_D)
    strict_bd = same & (c2 < r2)
    col_ones = jnp.concatenate([jnp.where(same & (c2 <= r2), 1.0, 0.0), jnp.where(same, 1.0, 0.0)],
                               axis=0).astype(BF16)
    upper_ones = jnp.where(same & (r2 <= c2), 1.0, 0.0).astype(BF16)
    bd_ref[...] = jnp.where(same, 1.0, 0.0).astype(BF16)
    n_grp = rows // GROUP
    gam_parts, glast_parts, gam_t = [], [], []
    for gi in range(n_grp):
        g_grp = g[gi * GROUP:(gi + 1) * GROUP]
        by_col = jnp.dot(col_ones, jnp.concatenate(_split3(g_grp), axis=1), preferred_element_type=F32)
        by_col = by_col[:, :LANES] + by_col[:, LANES:2 * LANES] + by_col[:, 2 * LANES:]
        gam_parts.append(by_col[:GROUP])
        glast_parts.append(by_col[GROUP:])
        by_row = jnp.dot(jnp.concatenate(_split3(g_grp.T), axis=0), upper_ones, preferred_element_type=F32)
        gam_t.append(by_row[:LANES] + by_row[LANES:2 * LANES] + by_row[2 * LANES:])
    gam = jnp.concatenate(gam_parts, axis=0)
    glast = jnp.concatenate(glast_parts, axis=0)

    rb = lax.broadcasted_iota(jnp.int32, (CHUNK_D, CHUNK_D), 0)
    cb = lax.broadcasted_iota(jnp.int32, (CHUNK_D, CHUNK_D), 1)
    incl = cb <= rb

    heads, a_packed, rhs = [], [], []
    for h in range(N_HEADS):
        q_h = _l2(qkv_parts[0][:, h * HEAD_DIM:(h + 1) * HEAD_DIM]) * (HEAD_DIM ** -0.5)
        k_h = _l2(qkv_parts[1][:, h * HEAD_DIM:(h + 1) * HEAD_DIM])
        v_h = qkv_parts[2][:, h * HEAD_DIM:(h + 1) * HEAD_DIM]
        gc_h = gam[:, h:h + 1]
        gl_h = glast[:, h:h + 1]
        bc_h = beta[:, N_HEADS + h:N_HEADS + h + 1]
        eg_h = jnp.exp(gc_h)
        for gi in range(n_grp):
            gs = slice(gi * GROUP, (gi + 1) * GROUP)
            kk = _dot_nt(k_h[gs], k_h[gs])
            decay = jnp.exp(jnp.where(strict_bd, gc_h[gs] - gam_t[gi][h:h + 1, :], 0.0))
            a_packed.append(_fold(jnp.where(strict_bd, bc_h[gs] * kk * decay, 0.0)))
        rhs.append(_split(jnp.concatenate([bc_h * v_h, (bc_h * eg_h) * k_h], axis=1)))
        heads.append((q_h * eg_h, q_h, k_h, k_h * jnp.exp(gl_h - gc_h), gc_h, jnp.exp(gl_h)))
    inv_split = [_split(inv) for inv in _unit_lower_inverses_packed(a_packed, bd_ref)]
    sol = []
    for h in range(N_HEADS):
        sol.append([_dot3(_expand(inv_split[h * n_grp + gi][0], bd_ref), _expand(inv_split[h * n_grp + gi][1], bd_ref),
                          rhs[h][0][gi * GROUP:(gi + 1) * GROUP], rhs[h][1][gi * GROUP:(gi + 1) * GROUP])
                    for gi in range(n_grp)])

    blocks_per_seq = tl // CHUNK_D
    for b in range(blocks_per_seq):
        for seq in range(n_seq):
            i = seq * blocks_per_seq + b
            rs = slice(i * CHUNK_D, (i + 1) * CHUNK_D)
            gi, j = divmod(i, GROUP // CHUNK_D)
            ls = slice(j * CHUNK_D, (j + 1) * CHUNK_D)
            for h in range(N_HEADS):
                qb_h, q_h, k_h, kend_h, gc_h, btot_h = heads[h]
                decay = jnp.where(incl, jnp.exp(jnp.where(incl, gc_h[rs] - gam_t[gi][h:h + 1, ls], 0.0)), 0.0)
                qk = _dot_nt(q_h[rs], k_h[rs]) * decay
                s_old = s_ref[seq, h]
                from_s = _dot(jnp.concatenate([sol[h][gi][ls, HEAD_DIM:], qb_h[rs]], axis=0), s_old)
                u = sol[h][gi][ls, :HEAD_DIM] - from_s[:CHUNK_D]
                from_u = _dot(jnp.concatenate([qk, kend_h[rs].T], axis=0), u)
                o = from_s[CHUNK_D:] + from_u[:CHUNK_D]
                s_ref[seq, h] = btot_h[i * CHUNK_D:i * CHUNK_D + 1] * s_old + from_u[CHUNK_D:]
                gate = z_gate[rs, h * HEAD_DIM:(h + 1) * HEAD_DIM]
                ob_ref[rs, A_WIDTH + h * HEAD_DIM:A_WIDTH + (h + 1) * HEAD_DIM] = (
                    _rms(o, bog_ref[...]) * _silu(gate)).astype(BF16)

    out = x + jnp.dot(ob_ref[...], wout_ref[...], preferred_element_type=F32)
    xo_ref[...] = out.reshape(n_seq, tl, D_MODEL)


def _mixer_weight_specs(layer):
    return [_layer_spec((1, D_MODEL), layer), _layer_spec((D_MODEL, Z_FRONT), 0), _layer_spec((D_MODEL, Z_TAIL), 0),
            _layer_spec((1, HEAD_DIM), layer), _layer_spec((1, HEAD_DIM), layer), _layer_spec((1, HEAD_DIM), layer)]


def _mixer_weights(w):
    return [w["n_mix"], w["w_front"], w["w_tail"], w["a_v_gain"], w["a_out_gain"], w["b_out_gain"]]


def _mix_prompt(x, layer, w):
    bsz, length, _ = x.shape
    ns, tl = MIX_SEQS, MIX_ROWS
    row_spec = pl.BlockSpec((ns, tl, D_MODEL), lambda p, t: (p, t, 0))
    in_specs = [row_spec] + _mixer_weight_specs(layer) + [
        _layer_spec((N_HEADS, CHUNK_A, CHUNK_A), layer), _layer_spec((CHUNK_A, N_HEADS), layer),
        _layer_spec((CONV_W, 3 * B_WIDTH), layer), _layer_spec((2, LANES), layer), _layer_spec((D_MODEL, D_MODEL), 0)]
    out_specs = [row_spec,
                 pl.BlockSpec((ns, N_HEADS, HEAD_DIM, HEAD_DIM), lambda p, t: (p, 0, 0, 0)),
                 pl.BlockSpec((ns, CONV_W - 1, 3 * B_WIDTH), lambda p, t: (p, 0, 0))]
    out_shape = [jax.ShapeDtypeStruct(x.shape, F32),
                 jax.ShapeDtypeStruct((bsz, N_HEADS, HEAD_DIM, HEAD_DIM), F32),
                 jax.ShapeDtypeStruct((bsz, CONV_W - 1, 3 * B_WIDTH), F32)]
    return pl.pallas_call(
        _mix_prompt_kernel,
        grid=(bsz // ns, length // tl),
        in_specs=in_specs,
        out_specs=out_specs,
        out_shape=out_shape,
        scratch_shapes=[pltpu.VMEM((ns, 8, 3 * B_WIDTH), F32), pltpu.VMEM((ns * tl, D_MODEL), BF16),
                        pltpu.VMEM((GROUP, GROUP), BF16), pltpu.VMEM((D_MODEL, B_WIDTH), BF16)],
        compiler_params=pltpu.CompilerParams(dimension_semantics=("arbitrary", "arbitrary"),
                                             vmem_limit_bytes=VMEM_LIMIT_BYTES),
    )(x, *_mixer_weights(w), w["a_w_s"], w["a_b_s_t"], w["b_conv"], w["ab_par"], w["w_out"])


def _mix_sample_kernel(x_ref, s_ref, cpad_ref, s_all_ref, nmix_ref, wmain_ref, wtail_ref, avg_ref, aog_ref, bog_ref,
                       coef_ref, bias_ref, cw_ref, abp_ref, wout_ref,
                       xo_ref, so_ref, zq_ref, vo_ref,
                       z_ref, ztail_ref, ob_ref, *, n_tok):
    del s_all_ref
    rows = zq_ref.shape[0]
    nb = rows // n_tok
    step = pl.program_id(0)

    @pl.when(step == 0)
    def _():
        xn = _rms(x_ref[...], nmix_ref[...]).astype(BF16)
        z_ref[...] = jnp.dot(xn, wmain_ref[...], preferred_element_type=F32)
        ztail_ref[...] = jnp.dot(xn, wtail_ref[...], preferred_element_type=F32)

    here = pl.ds(pl.multiple_of(step * rows, rows), rows)
    z = z_ref[here, :]
    z_tail = ztail_ref[here, :]
    zab = z_tail[:, :LANES]
    tok = lax.broadcasted_iota(jnp.int32, (rows, 1), 0) % n_tok

    def prev(a, d):
        return pltpu.roll(a, d, axis=0)

    def prev_or_zero(a, d):
        return a if d == 0 else jnp.where(tok >= d, prev(a, d), 0.0)

    def per_head(fn, a):
        return jnp.concatenate([fn(a[:, h * HEAD_DIM:(h + 1) * HEAD_DIM]) for h in range(N_HEADS)], axis=1)

    uv = jax.nn.gelu(z[:, :2 * A_WIDTH])
    vn = per_head(lambda a: _rms(a, avg_ref[...]), uv[:, A_WIDTH:])
    vo_ref[...] = vn
    mixed = bias_ref[...]
    for d in range(n_tok):
        mixed = mixed + coef_ref[d] * prev_or_zero(vn, d)
    ob_ref[here, :A_WIDTH] = per_head(lambda a: _rms(a, aog_ref[...]), uv[:, :A_WIDTH] * mixed)

    zq = z[:, OFF_QKV:OFF_GATE]
    zq_ref[...] = zq
    cpad = cpad_ref[...]
    cw = cw_ref[...]
    y = zq * cw[CONV_W - 1:CONV_W]
    for d in range(1, CONV_W):
        carried = pltpu.roll(cpad, rows - (n_tok - d), axis=0)
        y = y + jnp.where(tok >= d, prev(zq, d), carried) * cw[CONV_W - 1 - d:CONV_W - d]
    qkv = _silu(y)

    abp = abp_ref[...]
    g_all = -jnp.exp(abp[0:1]) * _softplus(zab + abp[1:2])
    beta_all = jax.nn.sigmoid(zab)

    sub = lax.broadcasted_iota(jnp.int32, (8, 1), 0)
    first_half = sub < n_tok
    o_heads, kend_heads, u_heads, btot_heads = [], [], [], []
    for h in range(N_HEADS):
        q = _l2(qkv[:, h * HEAD_DIM:(h + 1) * HEAD_DIM]) * (HEAD_DIM ** -0.5)
        k = _l2(qkv[:, B_WIDTH + h * HEAD_DIM:B_WIDTH + (h + 1) * HEAD_DIM])
        v = qkv[:, 2 * B_WIDTH + h * HEAD_DIM:2 * B_WIDTH + (h + 1) * HEAD_DIM]
        g = jnp.broadcast_to(g_all[:, h:h + 1], (rows, HEAD_DIM))
        beta = jnp.broadcast_to(beta_all[:, N_HEADS + h:N_HEADS + h + 1], (rows, HEAD_DIM))
        gam = g
        for d in range(1, n_tok):
            gam = gam + prev_or_zero(g, d)
        gam_last = jnp.where(tok == n_tok - 1, gam, 0.0)
        for d in range(1, n_tok):
            gam_last = gam_last + jnp.where(tok == n_tok - 1 - d, pltpu.roll(gam, rows - d, axis=0), 0.0)
        eg = jnp.exp(gam)

        def decay_to(d, gam=gam):
            return jnp.exp(jnp.where(tok >= d, gam - prev(gam, d), 0.0))

        a_sub = [None] + [jnp.where(tok >= d, beta * jnp.sum(k * prev(k, d), axis=-1, keepdims=True) * decay_to(d),
                                    0.0) for d in range(1, n_tok)]
        def forward_substitute(rhs, a_sub=a_sub):
            sol = rhs
            for t in range(1, n_tok):
                acc = rhs
                for d in range(1, t + 1):
                    acc = acc - a_sub[d] * prev(sol, d)
                sol = jnp.where(tok == t, acc, sol)
            return sol

        w_blk = forward_substitute(beta * v)
        kb_blk = forward_substitute((beta * eg) * k)
        qb = q * eg

        kb_s, qb_s = [], []
        for p in range(rows // 8):
            kb_t, qb_t = kb_blk[8 * p:8 * p + 8], qb[8 * p:8 * p + 8]
            f0 = _dot(jnp.where(first_half, kb_t, pltpu.roll(qb_t, n_tok, axis=0)), s_ref[2 * p, h])
            f1 = _dot(jnp.where(first_half, pltpu.roll(kb_t, n_tok, axis=0), qb_t), s_ref[2 * p + 1, h])
            kb_s.append(jnp.where(first_half, f0, pltpu.roll(f1, n_tok, axis=0)))
            qb_s.append(jnp.where(first_half, pltpu.roll(f0, n_tok, axis=0), f1))
        u = w_blk - jnp.concatenate(kb_s, axis=0)
        o = jnp.concatenate(qb_s, axis=0)
        for d in range(n_tok):
            qk = jnp.where(tok >= d, jnp.sum(q * prev(k, d), axis=-1, keepdims=True) * decay_to(d), 0.0)
            o = o + qk * prev_or_zero(u, d)
        o_heads.append(o)
        kend_heads.append(k * jnp.exp(gam_last - gam))
        u_heads.append(u)
        btot_heads.append(jnp.broadcast_to(jnp.exp(gam_last), (rows, HEAD_DIM)))

    kend_t = jnp.concatenate(kend_heads, axis=0).T
    u_all = jnp.concatenate(u_heads, axis=0).astype(BF16)
    owner = lax.broadcasted_iota(jnp.int32, (1, N_HEADS * rows), 1) // n_tok
    for h in range(N_HEADS):
        for b in range(nb):
            mine = jnp.where(owner == h * nb + b, kend_t, 0.0).astype(BF16)
            last = b * n_tok + n_tok - 1
            so_ref[b, h] = btot_heads[h][last:last + 1] * s_ref[b, h] + jnp.dot(
                mine, u_all, preferred_element_type=F32)

    gate = z_tail[:, 2 * N_HEADS:]
    ob_ref[here, A_WIDTH:] = per_head(lambda a: _rms(a, bog_ref[...]), jnp.concatenate(o_heads, axis=1)) * _silu(gate)

    @pl.when(step == pl.num_programs(0) - 1)
    def _():
        xo_ref[...] = x_ref[...] + _dot(ob_ref[...], wout_ref[...])


def _mix_sample(x, layer, state_s, cpad, s_all, w, n_tok):
    rows_total = x.shape[0]
    nb = SAMPLE_GROUP
    rows = nb * n_tok
    all_rows = lambda width: pl.BlockSpec((rows_total, width), lambda i: (0, 0))
    row_spec = lambda width: pl.BlockSpec((rows, width), lambda i: (i, 0))
    s_spec = pl.BlockSpec((None, nb, N_HEADS, HEAD_DIM, HEAD_DIM), lambda i: (layer, i, 0, 0, 0))
    in_specs = [all_rows(D_MODEL), s_spec, pl.BlockSpec((None, rows, 3 * B_WIDTH), lambda i: (layer, i, 0)),
                pl.BlockSpec(memory_space=pl.ANY)] + _mixer_weight_specs(layer) + [
        _layer_spec((n_tok, rows, A_WIDTH), layer), _layer_spec((rows, A_WIDTH), layer),
        _layer_spec((CONV_W, 3 * B_WIDTH), layer), _layer_spec((2, LANES), layer), _layer_spec((D_MODEL, D_MODEL), 0)]
    out_specs = [all_rows(D_MODEL), s_spec, row_spec(3 * B_WIDTH), row_spec(A_WIDTH)]
    out_shape = [jax.ShapeDtypeStruct(x.shape, F32), jax.ShapeDtypeStruct(state_s.shape, F32),
                 jax.ShapeDtypeStruct((rows_total, 3 * B_WIDTH), F32), jax.ShapeDtypeStruct((rows_total, A_WIDTH), F32)]
    return pl.pallas_call(
        functools.partial(_mix_sample_kernel, n_tok=n_tok),
        grid=(rows_total // rows,),
        in_specs=in_specs,
        out_specs=out_specs,
        out_shape=out_shape,
        input_output_aliases={3: 1},
        scratch_shapes=[pltpu.VMEM((rows_total, Z_FRONT), F32), pltpu.VMEM((rows_total, Z_TAIL), F32),
                        pltpu.VMEM((rows_total, D_MODEL), F32)],
        compiler_params=pltpu.CompilerParams(dimension_semantics=("arbitrary",),
                                             vmem_limit_bytes=VMEM_LIMIT_BYTES),
    )(x, state_s, cpad, s_all, *_mixer_weights(w), w["a_coef"], w["a_bias"], w["b_conv"], w["ab_par"], w["w_out"])


def _prep_tables(n_tok, norm_ffn1, norm_mix, a_v_gain, a_spatial_w, a_spatial_b, a_out_gain, b_conv_w, b_a_log,
                 b_dt_bias, b_out_gain, norm_ffn2, norm_ple):
    par_pad = jnp.zeros((DEPTH, LANES - N_HEADS), F32)
    ab_par = jnp.stack([jnp.concatenate([b_a_log, par_pad], axis=1),
                        jnp.concatenate([b_dt_bias, par_pad], axis=1)], axis=1)

    def sample_rows(a):
        return jnp.tile(jnp.repeat(jnp.transpose(a, (0, 2, 1)), HEAD_DIM, axis=2), (1, SAMPLE_GROUP, 1))

    ws_small = a_spatial_w[:, :, :n_tok, :n_tok]
    a_coef = jnp.stack([sample_rows(jnp.pad(jnp.diagonal(ws_small, offset=-d, axis1=2, axis2=3),
                                            ((0, 0), (0, 0), (d, 0)))) for d in range(n_tok)], axis=1)
    return dict(
        n_f1=norm_ffn1[:, None], n_mix=norm_mix[:, None], ab_par=ab_par,
        a_v_gain=a_v_gain[:, None], a_out_gain=a_out_gain[:, None], b_out_gain=b_out_gain[:, None],
        a_w_s=a_spatial_w, a_b_s_t=jnp.transpose(a_spatial_b, (0, 2, 1)),
        a_coef=a_coef, a_bias=sample_rows(a_spatial_b[:, :, :n_tok]),
        b_conv=b_conv_w, n_f2=norm_ffn2[:, None], n_ple=norm_ple[:, None],
    )


def kernel(x_prompt, x_sample, state_S, state_conv, p_prompt, p_sample, norm_ffn1, w_ffn1_in, w_ffn1_out, norm_mix, w_in, a_v_gain, a_spatial_w, a_spatial_b, a_out_gain, b_conv_w, b_a_log, b_dt_bias, b_out_gain, w_out, norm_ffn2, w_ffn2_in, w_ffn2_out, norm_ple, w_ple_gate, w_ple_proj, final_norm):
    bsz, length, _ = x_prompt.shape
    dec_bsz, n_tok, _ = x_sample.shape
    assert length % MIX_ROWS == 0 and MIX_ROWS % CHUNK_A == 0 and MIX_ROWS % GROUP == 0 and bsz % MIX_SEQS == 0
    assert dec_bsz % SAMPLE_GROUP == 0
    assert n_tok % CHUNK_A != 0 and n_tok % CHUNK_D != 0
    assert 2 * n_tok == 8 and N_HEADS * SAMPLE_GROUP * n_tok == HEAD_DIM and n_tok >= CONV_W - 1

    w = _prep_tables(n_tok, norm_ffn1, norm_mix, a_v_gain, a_spatial_w, a_spatial_b, a_out_gain, b_conv_w, b_a_log,
                     b_dt_bias, b_out_gain, norm_ffn2, norm_ple)
    ffn1_w = (w_ffn1_in[0:1].astype(BF16), w_ffn1_out[0:1].astype(BF16))
    in_proj_splits = ((0, Z_FRONT), (Z_FRONT, Z_FRONT + Z_TAIL))
    final = final_norm[None, None]
    xp = x_prompt.reshape(bsz * length, D_MODEL)
    xs = x_sample.reshape(dec_bsz * n_tok, D_MODEL)
    pp = p_prompt.reshape(DEPTH, bsz * length, PLE_DIM)
    ps = p_sample.reshape(DEPTH, dec_bsz * n_tok, PLE_DIM)
    keep = CONV_W - 1
    cpad = jnp.pad(state_conv, ((0, 0), (0, 0), (n_tok - keep, 0), (0, 0))).reshape(DEPTH, dec_bsz * n_tok, 3 * B_WIDTH)

    s_prompt, c_prompt, c_sample, v_sample = [], [], [], []
    s_sample = jnp.zeros(state_S.shape, F32)
    for i in range(DEPTH):
        last = dict(final_gain=final) if i == DEPTH - 1 else {}

        xp, (f2_in, f2_out, w_front, w_tail, w_o, ple_gate, ple_proj) = _ffn(
            xp, i, w["n_f1"], *ffn1_w,
            casts=[(w_ffn2_in, i, None), (w_ffn2_out, i, None), (w_in, i, in_proj_splits), (w_out, i, None),
                   (w_ple_gate, i, None), (w_ple_proj, i, None)])
        xs = _ffn(xs, i, w["n_f1"], *ffn1_w)
        mix_w = dict(w, w_front=w_front, w_tail=w_tail, w_out=w_o)
        xp, sp, cp = _mix_prompt(xp.reshape(bsz, length, D_MODEL), i, mix_w)
        xs, s_sample, zq, vs = _mix_sample(xs, i, state_S, cpad, s_sample, mix_w, n_tok)
        xp = xp.reshape(bsz * length, D_MODEL)
        ple = (w["n_ple"], ple_gate, ple_proj)
        if i < DEPTH - 1:
            xp, ffn1_w = _ffn(xp, i, w["n_f2"], f2_in, f2_out, ple=(pp,) + ple,
                              casts=[(w_ffn1_in, i + 1, None), (w_ffn1_out, i + 1, None)])
        else:
            xp = _ffn(xp, i, w["n_f2"], f2_in, f2_out, ple=(pp,) + ple, **last)
        xs = _ffn(xs, i, w["n_f2"], f2_in, f2_out, ple=(ps,) + ple, **last)

        s_prompt.append(sp)
        c_prompt.append(cp)
        c_sample.append(zq.reshape(dec_bsz, n_tok, 3 * B_WIDTH)[:, n_tok - keep:])
        v_sample.append(vs.reshape(dec_bsz, n_tok, N_HEADS, HEAD_DIM))

    return (xp.reshape(bsz, length, D_MODEL), xs.reshape(dec_bsz, n_tok, D_MODEL), jnp.stack(s_prompt),
            jnp.stack(c_prompt), s_sample, jnp.stack(c_sample), jnp.stack(v_sample))
```

```python
import functools

import jax
import jax.numpy as jnp
from jax import lax
from jax.experimental import pallas as pl
from jax.experimental.pallas import tpu as pltpu

F32 = jnp.float32
BF16 = jnp.bfloat16
EPS = 1e-6

D_MODEL = 1024
D_FF = 2816
DEPTH = 4
N_HEADS = 4
HEAD_DIM = 128
A_WIDTH = N_HEADS * HEAD_DIM
B_WIDTH = N_HEADS * HEAD_DIM
CHUNK_A = 128
CHUNK_D = 128
CONV_W = 4
PLE_DIM = 256
Z_FRONT = 2 * A_WIDTH + 3 * B_WIDTH
Z_TAIL = 2 * N_HEADS + B_WIDTH
OFF_QKV = 2 * A_WIDTH
OFF_GATE = OFF_QKV + 3 * B_WIDTH

VMEM_LIMIT_BYTES = 52 * 1024 * 1024
LANES = 128
BF16_SUBLANES = 16
MXU_N = 256
FFN_ROWS = 512
GROUP = MXU_N
MIX_ROWS = 512
SAMPLE_GROUP = 8


def _rms(x, gain):
    return x * lax.rsqrt(jnp.mean(x * x, axis=-1, keepdims=True) + EPS) * gain


def _l2(x):
    return x * lax.rsqrt(jnp.sum(x * x, axis=-1, keepdims=True) + EPS)


def _silu(x):
    return x * jax.nn.sigmoid(x)


def _softplus(x):
    return jnp.maximum(x, 0.0) + jnp.log1p(jnp.exp(-jnp.abs(x)))


def _dot(a, b):
    return jnp.dot(a.astype(BF16), b.astype(BF16), preferred_element_type=F32)


def _dot_nt(a, b):
    return lax.dot_general(a.astype(BF16), b.astype(BF16), (((1,), (1,)), ((), ())),
                           preferred_element_type=F32)


def _split3(a):
    p1 = a.astype(BF16)
    r1 = a - p1.astype(F32)
    p2 = r1.astype(BF16)
    p3 = (r1 - p2.astype(F32)).astype(BF16)
    return p1, p2, p3


def _layer_spec(shape, layer, block=None):
    index = (layer,) + (0,) * (len(shape) - 1) + (0 if block is None else block,)
    return pl.BlockSpec((None,) + tuple(shape), lambda *_: index, pipeline_mode=pl.Buffered(1))


def _ffn_kernel(*refs, with_ple, with_final, cast_plan):
    x_ref, gain_ref, wg_ref, wu_ref, wo_ref = refs[:5]
    n_in = 5 + 4 * with_ple + with_final
    cast_in = refs[n_in:n_in + len(cast_plan)]
    o_ref = refs[n_in + len(cast_plan)]
    cast_out = iter(refs[n_in + len(cast_plan) + 1:])
    for src_ref, splits in zip(cast_in, cast_plan):
        for lo, hi in splits:
            next(cast_out)[...] = src_ref[:, lo:hi].astype(BF16)
    x = x_ref[...]
    xn = _rms(x, gain_ref[...]).astype(BF16)
    acc = jnp.zeros_like(x)
    for c in range(D_FF // MXU_N):
        sl = slice(c * MXU_N, (c + 1) * MXU_N)
        gate = jnp.dot(xn, wg_ref[:, sl], preferred_element_type=F32)
        up = jnp.dot(xn, wu_ref[:, sl], preferred_element_type=F32)
        h = (_silu(gate) * up).astype(BF16)
        acc = acc + jnp.dot(h, wo_ref[sl, :], preferred_element_type=F32)
    x = x + 0.5 * acc
    if with_ple:
        p_ref, npl_ref, wpg_ref, wpp_ref = refs[5:9]
        emb = _dot(p_ref[...], wpp_ref[...])
        gate = _dot(_rms(x, npl_ref[...]), wpg_ref[...])
        x = x + emb * jax.nn.sigmoid(gate)
    if with_final:
        x = _rms(x, refs[9][...])
    o_ref[...] = x


def _cast_row_blocks(n_rows, n_steps):
    blocks = n_steps
    while n_rows % blocks or (n_rows // blocks) % BF16_SUBLANES:
        assert blocks % 2 == 0, (n_rows, n_steps)
        blocks //= 2
    return blocks


def _ffn(x, layer, gain, w_in, w_out, ple=None, final_gain=None, casts=()):
    rows = x.shape[0]
    tm = min(FFN_ROWS, rows)
    n_steps = rows // tm
    row_spec = pl.BlockSpec((tm, D_MODEL), lambda i: (i, 0))
    in_specs = [row_spec, _layer_spec((1, D_MODEL), layer),
                _layer_spec((D_MODEL, D_FF), 0, block=0), _layer_spec((D_MODEL, D_FF), 0, block=1),
                _layer_spec((D_FF, D_MODEL), 0)]
    args = [x, gain, w_in, w_in, w_out]
    out_specs = [row_spec]
    out_shape = [jax.ShapeDtypeStruct(x.shape, F32)]
    if ple is not None:
        p, n_ple, w_gate, w_proj = ple
        in_specs += [pl.BlockSpec((None, tm, PLE_DIM), lambda i: (layer, i, 0)), _layer_spec((1, D_MODEL), layer),
                     _layer_spec((D_MODEL, D_MODEL), 0), _layer_spec((PLE_DIM, D_MODEL), 0)]
        args += [p, n_ple, w_gate, w_proj]
    if final_gain is not None:
        in_specs.append(_layer_spec((1, D_MODEL), 0))
        args.append(final_gain)
    cast_plan = []
    for src, layer_c, splits in casts:
        _, n_rows, n_cols = src.shape
        splits = tuple(splits) if splits is not None else ((0, n_cols),)
        blocks = _cast_row_blocks(n_rows, n_steps)
        block_of = functools.partial(lambda i, every: i // every, every=n_steps // blocks)
        in_specs.append(pl.BlockSpec((None, n_rows // blocks, n_cols),
                                     lambda i, layer_c=layer_c, block_of=block_of: (layer_c, block_of(i), 0)))
        args.append(src)
        for lo, hi in splits:
            out_specs.append(pl.BlockSpec((None, n_rows // blocks, hi - lo),
                                          lambda i, block_of=block_of: (0, block_of(i), 0)))
            out_shape.append(jax.ShapeDtypeStruct((1, n_rows, hi - lo), BF16))
        cast_plan.append(splits)
    out = pl.pallas_call(
        functools.partial(_ffn_kernel, with_ple=ple is not None, with_final=final_gain is not None,
                          cast_plan=tuple(cast_plan)),
        grid=(n_steps,),
        in_specs=in_specs,
        out_specs=out_specs,
        out_shape=out_shape,
        compiler_params=pltpu.CompilerParams(dimension_semantics=("arbitrary",),
                                             vmem_limit_bytes=VMEM_LIMIT_BYTES),
    )(*args)
    return (out[0], out[1:]) if casts else out[0]


def _split(a):
    hi = a.astype(BF16)
    lo = (a - hi.astype(F32)).astype(BF16)
    return hi, lo


def _dot3(a_hi, a_lo, b_hi, b_lo):
    m = a_hi.shape[0]
    both = jnp.dot(jnp.concatenate([a_hi, a_lo], axis=0), b_hi, preferred_element_type=F32)
    return both[:m] + both[m:] + jnp.dot(a_hi, b_lo, preferred_element_type=F32)


def _fold(block_diag):
    n = block_diag.shape[0] // CHUNK_D
    out = block_diag[0:CHUNK_D]
    for g in range(1, n):
        out = out + block_diag[g * CHUNK_D:(g + 1) * CHUNK_D]
    return out


def _expand(packed, diag_ones_ref):
    n = packed.shape[1] // CHUNK_D
    return jnp.concatenate([packed] * n, axis=0) * diag_ones_ref[...]


def _unit_lower_inverses_packed(l_packed_list, diag_ones_ref):
    c, width = l_packed_list[0].shape
    row = lax.broadcasted_iota(jnp.int32, (c, width), 0)
    col = lax.broadcasted_iota(jnp.int32, (c, width), 1) % c
    zero = jnp.zeros((), BF16)

    def lower_left(bs):
        return (row // (2 * bs) == col // (2 * bs)) & ((row // bs) % 2 == 1) & ((col // bs) % 2 == 0)

    l_bf = [l.astype(BF16) for l in l_packed_list]
    xs = [jnp.where(row == col, 1.0, 0.0) - jnp.where(lower_left(1), l, 0.0) for l in l_packed_list]
    bs = 2
    while bs < c:
        sel = lower_left(bs)
        x_bf = [x.astype(BF16) for x in xs]
        ys = [jnp.dot(jnp.where(sel, l, zero), _expand(x, diag_ones_ref), preferred_element_type=F32)
              for l, x in zip(l_bf, x_bf)]
        xs = [x - jnp.dot(xb, _expand(y.astype(BF16), diag_ones_ref), preferred_element_type=F32)
              for x, xb, y in zip(xs, x_bf, ys)]
        bs *= 2
    return xs


def _mix_prompt_kernel(x_ref, nmix_ref, wmain_ref, wtail_ref, avg_ref, aog_ref, bog_ref, wsp_ref, bsp_ref,
                       cw_ref, abp_ref, wout_ref,
                       xo_ref, s_ref, ct_ref,
                       zlast_ref, ob_ref, bd_ref, wgate_ref):
    tl = x_ref.shape[0]
    step = pl.program_id(1)

    @pl.when((pl.program_id(0) == 0) & (step == 0))
    def _():
        wgate_ref[...] = wtail_ref[:, 2 * N_HEADS:]

    @pl.when(step == 0)
    def _():
        s_ref[...] = jnp.zeros_like(s_ref)
        zlast_ref[...] = jnp.zeros_like(zlast_ref)

    x = x_ref[...]
    xn = _rms(x, nmix_ref[...]).astype(BF16)
    def in_proj(lo, hi):
        return jnp.dot(xn, wmain_ref[:, lo:hi], preferred_element_type=F32)

    zab = jnp.dot(xn, wtail_ref[:, :LANES], preferred_element_type=F32)
    first_tile = lax.broadcasted_iota(jnp.int32, (8, 1), 0)
    qkv_parts = []
    for part in range(3):
        cols = slice(part * B_WIDTH, (part + 1) * B_WIDTH)
        zc = in_proj(OFF_QKV + part * B_WIDTH, OFF_QKV + (part + 1) * B_WIDTH)
        cw = cw_ref[:, cols]
        carried = zlast_ref[:, cols]
        y = zc * cw[CONV_W - 1:CONV_W]
        for d in range(1, CONV_W):
            rolled = pltpu.roll(zc, d, axis=0)
            top = jnp.where(first_tile < d, pltpu.roll(carried, d, axis=0), rolled[0:8])
            y = y + jnp.concatenate([top, rolled[8:]], axis=0) * cw[CONV_W - 1 - d:CONV_W - d]
        zlast_ref[:, cols] = zc[tl - 8:tl]
        ct_ref[:, cols] = zc[tl - (CONV_W - 1):tl]
        qkv_parts.append(_silu(y))

    uv = jax.nn.gelu(in_proj(0, OFF_QKV))
    row = lax.broadcasted_iota(jnp.int32, (CHUNK_A, CHUNK_A), 0)
    col = lax.broadcasted_iota(jnp.int32, (CHUNK_A, CHUNK_A), 1)
    causal = col <= row
    for h in range(N_HEADS):
        hs = slice(h * HEAD_DIM, (h + 1) * HEAD_DIM)
        u_h = uv[:, hs]
        v_h = _rms(uv[:, A_WIDTH + h * HEAD_DIM:A_WIDTH + (h + 1) * HEAD_DIM], avg_ref[...]).astype(BF16)
        w_h = jnp.where(causal, wsp_ref[h], 0.0).astype(BF16)
        bias_h = bsp_ref[:, h:h + 1]
        for c in range(tl // CHUNK_A):
            rs = slice(c * CHUNK_A, (c + 1) * CHUNK_A)
            mixed = jnp.dot(w_h, v_h[rs], preferred_element_type=F32) + bias_h
            ob_ref[rs, hs] = _rms(u_h[rs] * mixed, aog_ref[...]).astype(BF16)

    z_gate = jnp.dot(xn, wgate_ref[...], preferred_element_type=F32)
    abp = abp_ref[...]
    g = -jnp.exp(abp[0:1]) * _softplus(zab + abp[1:2])
    beta = jax.nn.sigmoid(zab)

    r2 = lax.broadcasted_iota(jnp.int32, (GROUP, GROUP), 0)
    c2 = lax.broadcasted_iota(jnp.int32, (GROUP, GROUP), 1)
    same = (r2 // CHUNK_D) == (c2 // CHUNK_D)
    strict_bd = same & (c2 < r2)
    col_ones = jnp.concatenate([jnp.where(same & (c2 <= r2), 1.0, 0.0), jnp.where(same, 1.0, 0.0)],
                               axis=0).astype(BF16)
    upper_ones = jnp.where(same & (r2 <= c2), 1.0, 0.0).astype(BF16)
    bd_ref[...] = jnp.where(same, 1.0, 0.0).astype(BF16)
    n_grp = tl // GROUP
    gam_parts, glast_parts, gam_t = [], [], []
    for gi in range(n_grp):
        g_grp = g[gi * GROUP:(gi + 1) * GROUP]
        by_col = jnp.dot(col_ones, jnp.concatenate(_split3(g_grp), axis=1), preferred_element_type=F32)
        by_col = by_col[:, :LANES] + by_col[:, LANES:2 * LANES] + by_col[:, 2 * LANES:]
        gam_parts.append(by_col[:GROUP])
        glast_parts.append(by_col[GROUP:])
        by_row = jnp.dot(jnp.concatenate(_split3(g_grp.T), axis=0), upper_ones, preferred_element_type=F32)
        gam_t.append(by_row[:LANES] + by_row[LANES:2 * LANES] + by_row[2 * LANES:])
    gam = jnp.concatenate(gam_parts, axis=0)
    glast = jnp.concatenate(glast_parts, axis=0)

    rb = lax.broadcasted_iota(jnp.int32, (CHUNK_D, CHUNK_D), 0)
    cb = lax.broadcasted_iota(jnp.int32, (CHUNK_D, CHUNK_D), 1)
    incl = cb <= rb

    heads, a_packed, rhs = [], [], []
    for h in range(N_HEADS):
        q_h = _l2(qkv_parts[0][:, h * HEAD_DIM:(h + 1) * HEAD_DIM]) * (HEAD_DIM ** -0.5)
        k_h = _l2(qkv_parts[1][:, h * HEAD_DIM:(h + 1) * HEAD_DIM])
        v_h = qkv_parts[2][:, h * HEAD_DIM:(h + 1) * HEAD_DIM]
        gc_h = gam[:, h:h + 1]
        gl_h = glast[:, h:h + 1]
        bc_h = beta[:, N_HEADS + h:N_HEADS + h + 1]
        eg_h = jnp.exp(gc_h)
        for gi in range(n_grp):
            gs = slice(gi * GROUP, (gi + 1) * GROUP)
            kk = _dot_nt(k_h[gs], k_h[gs])
            decay = jnp.exp(jnp.where(strict_bd, gc_h[gs] - gam_t[gi][h:h + 1, :], 0.0))
            a_packed.append(_fold(jnp.where(strict_bd, bc_h[gs] * kk * decay, 0.0)))
        rhs.append(_split(jnp.concatenate([bc_h * v_h, (bc_h * eg_h) * k_h], axis=1)))
        heads.append((q_h * eg_h, q_h, k_h, k_h * jnp.exp(gl_h - gc_h), gc_h, jnp.exp(gl_h)))
    inv_split = [_split(inv) for inv in _unit_lower_inverses_packed(a_packed, bd_ref)]
    sol = []
    for h in range(N_HEADS):
        sol.append([_dot3(_expand(inv_split[h * n_grp + gi][0], bd_ref), _expand(inv_split[h * n_grp + gi][1], bd_ref),
                          rhs[h][0][gi * GROUP:(gi + 1) * GROUP], rhs[h][1][gi * GROUP:(gi + 1) * GROUP])
                    for gi in range(n_grp)])

    for i in range(tl // CHUNK_D):
        rs = slice(i * CHUNK_D, (i + 1) * CHUNK_D)
        gi, j = divmod(i, GROUP // CHUNK_D)
        ls = slice(j * CHUNK_D, (j + 1) * CHUNK_D)
        for h in range(N_HEADS):
            qb_h, q_h, k_h, kend_h, gc_h, btot_h = heads[h]
            decay = jnp.where(incl, jnp.exp(jnp.where(incl, gc_h[rs] - gam_t[gi][h:h + 1, ls], 0.0)), 0.0)
            qk = _dot_nt(q_h[rs], k_h[rs]) * decay
            s_old = s_ref[h]
            from_s = _dot(jnp.concatenate([sol[h][gi][ls, HEAD_DIM:], qb_h[rs]], axis=0), s_old)
            u = sol[h][gi][ls, :HEAD_DIM] - from_s[:CHUNK_D]
            from_u = _dot(jnp.concatenate([qk, kend_h[rs].T], axis=0), u)
            o = from_s[CHUNK_D:] + from_u[:CHUNK_D]
            s_ref[h] = btot_h[i * CHUNK_D:i * CHUNK_D + 1] * s_old + from_u[CHUNK_D:]
            gate = z_gate[rs, h * HEAD_DIM:(h + 1) * HEAD_DIM]
            ob_ref[rs, A_WIDTH + h * HEAD_DIM:A_WIDTH + (h + 1) * HEAD_DIM] = (
                _rms(o, bog_ref[...]) * _silu(gate)).astype(BF16)

    xo_ref[...] = x + jnp.dot(ob_ref[...], wout_ref[...], preferred_element_type=F32)


def _mixer_weight_specs(layer):
    return [_layer_spec((1, D_MODEL), layer), _layer_spec((D_MODEL, Z_FRONT), 0), _layer_spec((D_MODEL, Z_TAIL), 0),
            _layer_spec((1, HEAD_DIM), layer), _layer_spec((1, HEAD_DIM), layer), _layer_spec((1, HEAD_DIM), layer)]


def _mixer_weights(w):
    return [w["n_mix"], w["w_front"], w["w_tail"], w["a_v_gain"], w["a_out_gain"], w["b_out_gain"]]


def _mix_prompt(x, layer, w):
    bsz, length, _ = x.shape
    tl = MIX_ROWS
    row_spec = pl.BlockSpec((None, tl, D_MODEL), lambda b, t: (b, t, 0))
    in_specs = [row_spec] + _mixer_weight_specs(layer) + [
        _layer_spec((N_HEADS, CHUNK_A, CHUNK_A), layer), _layer_spec((CHUNK_A, N_HEADS), layer),
        _layer_spec((CONV_W, 3 * B_WIDTH), layer), _layer_spec((2, LANES), layer), _layer_spec((D_MODEL, D_MODEL), 0)]
    out_specs = [row_spec,
                 pl.BlockSpec((None, N_HEADS, HEAD_DIM, HEAD_DIM), lambda b, t: (b, 0, 0, 0)),
                 pl.BlockSpec((None, CONV_W - 1, 3 * B_WIDTH), lambda b, t: (b, 0, 0))]
    out_shape = [jax.ShapeDtypeStruct(x.shape, F32),
                 jax.ShapeDtypeStruct((bsz, N_HEADS, HEAD_DIM, HEAD_DIM), F32),
                 jax.ShapeDtypeStruct((bsz, CONV_W - 1, 3 * B_WIDTH), F32)]
    return pl.pallas_call(
        _mix_prompt_kernel,
        grid=(bsz, length // tl),
        in_specs=in_specs,
        out_specs=out_specs,
        out_shape=out_shape,
        scratch_shapes=[pltpu.VMEM((8, 3 * B_WIDTH), F32), pltpu.VMEM((tl, D_MODEL), BF16),
                        pltpu.VMEM((GROUP, GROUP), BF16), pltpu.VMEM((D_MODEL, B_WIDTH), BF16)],
        compiler_params=pltpu.CompilerParams(dimension_semantics=("arbitrary", "arbitrary"),
                                             vmem_limit_bytes=VMEM_LIMIT_BYTES),
    )(x, *_mixer_weights(w), w["a_w_s"], w["a_b_s_t"], w["b_conv"], w["ab_par"], w["w_out"])


def _mix_sample_kernel(x_ref, s_ref, cpad_ref, s_all_ref, nmix_ref, wmain_ref, wtail_ref, avg_ref, aog_ref, bog_ref,
                       coef_ref, bias_ref, cw_ref, abp_ref, wout_ref,
                       xo_ref, so_ref, zq_ref, vo_ref,
                       z_ref, ztail_ref, ob_ref, *, n_tok):
    del s_all_ref
    rows = zq_ref.shape[0]
    nb = rows // n_tok
    step = pl.program_id(0)

    @pl.when(step == 0)
    def _():
        xn = _rms(x_ref[...], nmix_ref[...]).astype(BF16)
        z_ref[...] = jnp.dot(xn, wmain_ref[...], preferred_element_type=F32)
        ztail_ref[...] = jnp.dot(xn, wtail_ref[...], preferred_element_type=F32)

    here = pl.ds(pl.multiple_of(step * rows, rows), rows)
    z = z_ref[here, :]
    z_tail = ztail_ref[here, :]
    zab = z_tail[:, :LANES]
    tok = lax.broadcasted_iota(jnp.int32, (rows, 1), 0) % n_tok

    def prev(a, d):
        return pltpu.roll(a, d, axis=0)

    def prev_or_zero(a, d):
        return a if d == 0 else jnp.where(tok >= d, prev(a, d), 0.0)

    def per_head(fn, a):
        return jnp.concatenate([fn(a[:, h * HEAD_DIM:(h + 1) * HEAD_DIM]) for h in range(N_HEADS)], axis=1)

    uv = jax.nn.gelu(z[:, :2 * A_WIDTH])
    vn = per_head(lambda a: _rms(a, avg_ref[...]), uv[:, A_WIDTH:])
    vo_ref[...] = vn
    mixed = bias_ref[...]
    for d in range(n_tok):
        mixed = mixed + coef_ref[d] * prev_or_zero(vn, d)
    ob_ref[here, :A_WIDTH] = per_head(lambda a: _rms(a, aog_ref[...]), uv[:, :A_WIDTH] * mixed)

    zq = z[:, OFF_QKV:OFF_GATE]
    zq_ref[...] = zq
    cpad = cpad_ref[...]
    cw = cw_ref[...]
    y = zq * cw[CONV_W - 1:CONV_W]
    for d in range(1, CONV_W):
        carried = pltpu.roll(cpad, rows - (n_tok - d), axis=0)
        y = y + jnp.where(tok >= d, prev(zq, d), carried) * cw[CONV_W - 1 - d:CONV_W - d]
    qkv = _silu(y)

    abp = abp_ref[...]
    g_all = -jnp.exp(abp[0:1]) * _softplus(zab + abp[1:2])
    beta_all = jax.nn.sigmoid(zab)

    sub = lax.broadcasted_iota(jnp.int32, (8, 1), 0)
    first_half = sub < n_tok
    o_heads, kend_heads, u_heads, btot_heads = [], [], [], []
    for h in range(N_HEADS):
        q = _l2(qkv[:, h * HEAD_DIM:(h + 1) * HEAD_DIM]) * (HEAD_DIM ** -0.5)
        k = _l2(qkv[:, B_WIDTH + h * HEAD_DIM:B_WIDTH + (h + 1) * HEAD_DIM])
        v = qkv[:, 2 * B_WIDTH + h * HEAD_DIM:2 * B_WIDTH + (h + 1) * HEAD_DIM]
        g = jnp.broadcast_to(g_all[:, h:h + 1], (rows, HEAD_DIM))
        beta = jnp.broadcast_to(beta_all[:, N_HEADS + h:N_HEADS + h + 1], (rows, HEAD_DIM))
        gam = g
        for d in range(1, n_tok):
            gam = gam + prev_or_zero(g, d)
        gam_last = jnp.where(tok == n_tok - 1, gam, 0.0)
        for d in range(1, n_tok):
            gam_last = gam_last + jnp.where(tok == n_tok - 1 - d, pltpu.roll(gam, rows - d, axis=0), 0.0)
        eg = jnp.exp(gam)

        def decay_to(d, gam=gam):
            return jnp.exp(jnp.where(tok >= d, gam - prev(gam, d), 0.0))

        a_sub = [None] + [jnp.where(tok >= d, beta * jnp.sum(k * prev(k, d), axis=-1, keepdims=True) * decay_to(d),
                                    0.0) for d in range(1, n_tok)]
        def forward_substitute(rhs, a_sub=a_sub):
            sol = rhs
            for t in range(1, n_tok):
                acc = rhs
                for d in range(1, t + 1):
                    acc = acc - a_sub[d] * prev(sol, d)
                sol = jnp.where(tok == t, acc, sol)
            return sol

        w_blk = forward_substitute(beta * v)
        kb_blk = forward_substitute((beta * eg) * k)
        qb = q * eg

        kb_s, qb_s = [], []
        for p in range(rows // 8):
            kb_t, qb_t = kb_blk[8 * p:8 * p + 8], qb[8 * p:8 * p + 8]
            f0 = _dot(jnp.where(first_half, kb_t, pltpu.roll(qb_t, n_tok, axis=0)), s_ref[2 * p, h])
            f1 = _dot(jnp.where(first_half, pltpu.roll(kb_t, n_tok, axis=0), qb_t), s_ref[2 * p + 1, h])
            kb_s.append(jnp.where(first_half, f0, pltpu.roll(f1, n_tok, axis=0)))
            qb_s.append(jnp.where(first_half, pltpu.roll(f0, n_tok, axis=0), f1))
        u = w_blk - jnp.concatenate(kb_s, axis=0)
        o = jnp.concatenate(qb_s, axis=0)
        for d in range(n_tok):
            qk = jnp.where(tok >= d, jnp.sum(q * prev(k, d), axis=-1, keepdims=True) * decay_to(d), 0.0)
            o = o + qk * prev_or_zero(u, d)
        o_heads.append(o)
        kend_heads.append(k * jnp.exp(gam_last - gam))
        u_heads.append(u)
        btot_heads.append(jnp.broadcast_to(jnp.exp(gam_last), (rows, HEAD_DIM)))

    kend_t = jnp.concatenate(kend_heads, axis=0).T
    u_all = jnp.concatenate(u_heads, axis=0).astype(BF16)
    owner = lax.broadcasted_iota(jnp.int32, (1, N_HEADS * rows), 1) // n_tok
    for h in range(N_HEADS):
        for b in range(nb):
            mine = jnp.where(owner == h * nb + b, kend_t, 0.0).astype(BF16)
            last = b * n_tok + n_tok - 1
            so_ref[b, h] = btot_heads[h][last:last + 1] * s_ref[b, h] + jnp.dot(
                mine, u_all, preferred_element_type=F32)

    gate = z_tail[:, 2 * N_HEADS:]
    ob_ref[here, A_WIDTH:] = per_head(lambda a: _rms(a, bog_ref[...]), jnp.concatenate(o_heads, axis=1)) * _silu(gate)

    @pl.when(step == pl.num_programs(0) - 1)
    def _():
        xo_ref[...] = x_ref[...] + _dot(ob_ref[...], wout_ref[...])


def _mix_sample(x, layer, state_s, cpad, s_all, w, n_tok):
    rows_total = x.shape[0]
    nb = SAMPLE_GROUP
    rows = nb * n_tok
    all_rows = lambda width: pl.BlockSpec((rows_total, width), lambda i: (0, 0))
    row_spec = lambda width: pl.BlockSpec((rows, width), lambda i: (i, 0))
    s_spec = pl.BlockSpec((None, nb, N_HEADS, HEAD_DIM, HEAD_DIM), lambda i: (layer, i, 0, 0, 0))
    in_specs = [all_rows(D_MODEL), s_spec, pl.BlockSpec((None, rows, 3 * B_WIDTH), lambda i: (layer, i, 0)),
                pl.BlockSpec(memory_space=pl.ANY)] + _mixer_weight_specs(layer) + [
        _layer_spec((n_tok, rows, A_WIDTH), layer), _layer_spec((rows, A_WIDTH), layer),
        _layer_spec((CONV_W, 3 * B_WIDTH), layer), _layer_spec((2, LANES), layer), _layer_spec((D_MODEL, D_MODEL), 0)]
    out_specs = [all_rows(D_MODEL), s_spec, row_spec(3 * B_WIDTH), row_spec(A_WIDTH)]
    out_shape = [jax.ShapeDtypeStruct(x.shape, F32), jax.ShapeDtypeStruct(state_s.shape, F32),
                 jax.ShapeDtypeStruct((rows_total, 3 * B_WIDTH), F32), jax.ShapeDtypeStruct((rows_total, A_WIDTH), F32)]
    return pl.pallas_call(
        functools.partial(_mix_sample_kernel, n_tok=n_tok),
        grid=(rows_total // rows,),
        in_specs=in_specs,
        out_specs=out_specs,
        out_shape=out_shape,
        input_output_aliases={3: 1},
        scratch_shapes=[pltpu.VMEM((rows_total, Z_FRONT), F32), pltpu.VMEM((rows_total, Z_TAIL), F32),
                        pltpu.VMEM((rows_total, D_MODEL), F32)],
        compiler_params=pltpu.CompilerParams(dimension_semantics=("arbitrary",),
                                             vmem_limit_bytes=VMEM_LIMIT_BYTES),
    )(x, state_s, cpad, s_all, *_mixer_weights(w), w["a_coef"], w["a_bias"], w["b_conv"], w["ab_par"], w["w_out"])


def _prep_tables(n_tok, norm_ffn1, norm_mix, a_v_gain, a_spatial_w, a_spatial_b, a_out_gain, b_conv_w, b_a_log,
                 b_dt_bias, b_out_gain, norm_ffn2, norm_ple):
    par_pad = jnp.zeros((DEPTH, LANES - N_HEADS), F32)
    ab_par = jnp.stack([jnp.concatenate([b_a_log, par_pad], axis=1),
                        jnp.concatenate([b_dt_bias, par_pad], axis=1)], axis=1)

    def sample_rows(a):
        return jnp.tile(jnp.repeat(jnp.transpose(a, (0, 2, 1)), HEAD_DIM, axis=2), (1, SAMPLE_GROUP, 1))

    ws_small = a_spatial_w[:, :, :n_tok, :n_tok]
    a_coef = jnp.stack([sample_rows(jnp.pad(jnp.diagonal(ws_small, offset=-d, axis1=2, axis2=3),
                                            ((0, 0), (0, 0), (d, 0)))) for d in range(n_tok)], axis=1)
    return dict(
        n_f1=norm_ffn1[:, None], n_mix=norm_mix[:, None], ab_par=ab_par,
        a_v_gain=a_v_gain[:, None], a_out_gain=a_out_gain[:, None], b_out_gain=b_out_gain[:, None],
        a_w_s=a_spatial_w, a_b_s_t=jnp.transpose(a_spatial_b, (0, 2, 1)),
        a_coef=a_coef, a_bias=sample_rows(a_spatial_b[:, :, :n_tok]),
        b_conv=b_conv_w, n_f2=norm_ffn2[:, None], n_ple=norm_ple[:, None],
    )


def kernel(x_prompt, x_sample, state_S, state_conv, p_prompt, p_sample, norm_ffn1, w_ffn1_in, w_ffn1_out, norm_mix, w_in, a_v_gain, a_spatial_w, a_spatial_b, a_out_gain, b_conv_w, b_a_log, b_dt_bias, b_out_gain, w_out, norm_ffn2, w_ffn2_in, w_ffn2_out, norm_ple, w_ple_gate, w_ple_proj, final_norm):
    bsz, length, _ = x_prompt.shape
    dec_bsz, n_tok, _ = x_sample.shape
    assert length % MIX_ROWS == 0 and MIX_ROWS % CHUNK_A == 0 and MIX_ROWS % GROUP == 0
    assert dec_bsz % SAMPLE_GROUP == 0
    assert n_tok % CHUNK_A != 0 and n_tok % CHUNK_D != 0
    assert 2 * n_tok == 8 and N_HEADS * SAMPLE_GROUP * n_tok == HEAD_DIM and n_tok >= CONV_W - 1

    w = _prep_tables(n_tok, norm_ffn1, norm_mix, a_v_gain, a_spatial_w, a_spatial_b, a_out_gain, b_conv_w, b_a_log,
                     b_dt_bias, b_out_gain, norm_ffn2, norm_ple)
    ffn1_w = (w_ffn1_in[0:1].astype(BF16), w_ffn1_out[0:1].astype(BF16))
    in_proj_splits = ((0, Z_FRONT), (Z_FRONT, Z_FRONT + Z_TAIL))
    final = final_norm[None, None]
    xp = x_prompt.reshape(bsz * length, D_MODEL)
    xs = x_sample.reshape(dec_bsz * n_tok, D_MODEL)
    pp = p_prompt.reshape(DEPTH, bsz * length, PLE_DIM)
    ps = p_sample.reshape(DEPTH, dec_bsz * n_tok, PLE_DIM)
    keep = CONV_W - 1
    cpad = jnp.pad(state_conv, ((0, 0), (0, 0), (n_tok - keep, 0), (0, 0))).reshape(DEPTH, dec_bsz * n_tok, 3 * B_WIDTH)

    s_prompt, c_prompt, c_sample, v_sample = [], [], [], []
    s_sample = jnp.zeros(state_S.shape, F32)
    for i in range(DEPTH):
        last = dict(final_gain=final) if i == DEPTH - 1 else {}

        xp, (f2_in, f2_out, w_front, w_tail, w_o, ple_gate, ple_proj) = _ffn(
            xp, i, w["n_f1"], *ffn1_w,
            casts=[(w_ffn2_in, i, None), (w_ffn2_out, i, None), (w_in, i, in_proj_splits), (w_out, i, None),
                   (w_ple_gate, i, None), (w_ple_proj, i, None)])
        xs = _ffn(xs, i, w["n_f1"], *ffn1_w)
        mix_w = dict(w, w_front=w_front, w_tail=w_tail, w_out=w_o)
        xp, sp, cp = _mix_prompt(xp.reshape(bsz, length, D_MODEL), i, mix_w)
        xs, s_sample, zq, vs = _mix_sample(xs, i, state_S, cpad, s_sample, mix_w, n_tok)
        xp = xp.reshape(bsz * length, D_MODEL)
        ple = (w["n_ple"], ple_gate, ple_proj)
        if i < DEPTH - 1:
            xp, ffn1_w = _ffn(xp, i, w["n_f2"], f2_in, f2_out, ple=(pp,) + ple,
                              casts=[(w_ffn1_in, i + 1, None), (w_ffn1_out, i + 1, None)])
        else:
            xp = _ffn(xp, i, w["n_f2"], f2_in, f2_out, ple=(pp,) + ple, **last)
        xs = _ffn(xs, i, w["n_f2"], f2_in, f2_out, ple=(ps,) + ple, **last)

        s_prompt.append(sp)
        c_prompt.append(cp)
        c_sample.append(zq.reshape(dec_bsz, n_tok, 3 * B_WIDTH)[:, n_tok - keep:])
        v_sample.append(vs.reshape(dec_bsz, n_tok, N_HEADS, HEAD_DIM))

    return (xp.reshape(bsz, length, D_MODEL), xs.reshape(dec_bsz, n_tok, D_MODEL), jnp.stack(s_prompt),
            jnp.stack(c_prompt), s_sample, jnp.stack(c_sample), jnp.stack(v_sample))
```

```python
import functools

import jax
import jax.numpy as jnp
from jax import lax
from jax.experimental import pallas as pl
from jax.experimental.pallas import tpu as pltpu

F32 = jnp.float32
BF16 = jnp.bfloat16
EPS = 1e-6

D_MODEL = 1024
D_FF = 2816
DEPTH = 4
N_HEADS = 4
HEAD_DIM = 128
A_WIDTH = N_HEADS * HEAD_DIM
B_WIDTH = N_HEADS * HEAD_DIM
CHUNK_A = 128
CHUNK_D = 128
CONV_W = 4
PLE_DIM = 256
Z_FRONT = 2 * A_WIDTH + 3 * B_WIDTH
Z_TAIL = 2 * N_HEADS + B_WIDTH
OFF_QKV = 2 * A_WIDTH
OFF_GATE = OFF_QKV + 3 * B_WIDTH

VMEM_LIMIT_BYTES = 52 * 1024 * 1024
LANES = 128
BF16_SUBLANES = 16
MXU_N = 256
FFN_ROWS = 512
GROUP = MXU_N
MIX_ROWS = 512
SAMPLE_GROUP = 8


def _rms(x, gain):
    return x * lax.rsqrt(jnp.mean(x * x, axis=-1, keepdims=True) + EPS) * gain


def _l2(x):
    return x * lax.rsqrt(jnp.sum(x * x, axis=-1, keepdims=True) + EPS)


def _silu(x):
    return x * jax.nn.sigmoid(x)


def _softplus(x):
    return jnp.maximum(x, 0.0) + jnp.log1p(jnp.exp(-jnp.abs(x)))


def _dot(a, b):
    return jnp.dot(a.astype(BF16), b.astype(BF16), preferred_element_type=F32)


def _dot_nt(a, b):
    return lax.dot_general(a.astype(BF16), b.astype(BF16), (((1,), (1,)), ((), ())),
                           preferred_element_type=F32)


def _split3(a):
    p1 = a.astype(BF16)
    r1 = a - p1.astype(F32)
    p2 = r1.astype(BF16)
    p3 = (r1 - p2.astype(F32)).astype(BF16)
    return p1, p2, p3


def _layer_spec(shape, layer, block=None):
    index = (layer,) + (0,) * (len(shape) - 1) + (0 if block is None else block,)
    return pl.BlockSpec((None,) + tuple(shape), lambda *_: index, pipeline_mode=pl.Buffered(1))


def _ffn_kernel(*refs, with_ple, with_final, cast_plan):
    x_ref, gain_ref, wg_ref, wu_ref, wo_ref = refs[:5]
    n_in = 5 + 4 * with_ple + with_final
    cast_in = refs[n_in:n_in + len(cast_plan)]
    o_ref = refs[n_in + len(cast_plan)]
    cast_out = iter(refs[n_in + len(cast_plan) + 1:])
    for src_ref, splits in zip(cast_in, cast_plan):
        for lo, hi in splits:
            next(cast_out)[...] = src_ref[:, lo:hi].astype(BF16)
    x = x_ref[...]
    xn = _rms(x, gain_ref[...]).astype(BF16)
    acc = jnp.zeros_like(x)
    for c in range(D_FF // MXU_N):
        sl = slice(c * MXU_N, (c + 1) * MXU_N)
        gate = jnp.dot(xn, wg_ref[:, sl], preferred_element_type=F32)
        up = jnp.dot(xn, wu_ref[:, sl], preferred_element_type=F32)
        h = (_silu(gate) * up).astype(BF16)
        acc = acc + jnp.dot(h, wo_ref[sl, :], preferred_element_type=F32)
    x = x + 0.5 * acc
    if with_ple:
        p_ref, npl_ref, wpg_ref, wpp_ref = refs[5:9]
        emb = _dot(p_ref[...], wpp_ref[...])
        gate = _dot(_rms(x, npl_ref[...]), wpg_ref[...])
        x = x + emb * jax.nn.sigmoid(gate)
    if with_final:
        x = _rms(x, refs[9][...])
    o_ref[...] = x


def _cast_row_blocks(n_rows, n_steps):
    blocks = n_steps
    while n_rows % blocks or (n_rows // blocks) % BF16_SUBLANES:
        assert blocks % 2 == 0, (n_rows, n_steps)
        blocks //= 2
    return blocks


def _ffn(x, layer, gain, w_in, w_out, ple=None, final_gain=None, casts=()):
    rows = x.shape[0]
    tm = min(FFN_ROWS, rows)
    n_steps = rows // tm
    row_spec = pl.BlockSpec((tm, D_MODEL), lambda i: (i, 0))
    in_specs = [row_spec, _layer_spec((1, D_MODEL), layer),
                _layer_spec((D_MODEL, D_FF), 0, block=0), _layer_spec((D_MODEL, D_FF), 0, block=1),
                _layer_spec((D_FF, D_MODEL), 0)]
    args = [x, gain, w_in, w_in, w_out]
    out_specs = [row_spec]
    out_shape = [jax.ShapeDtypeStruct(x.shape, F32)]
    if ple is not None:
        p, n_ple, w_gate, w_proj = ple
        in_specs += [pl.BlockSpec((None, tm, PLE_DIM), lambda i: (layer, i, 0)), _layer_spec((1, D_MODEL), layer),
                     _layer_spec((D_MODEL, D_MODEL), 0), _layer_spec((PLE_DIM, D_MODEL), 0)]
        args += [p, n_ple, w_gate, w_proj]
    if final_gain is not None:
        in_specs.append(_layer_spec((1, D_MODEL), 0))
        args.append(final_gain)
    cast_plan = []
    for src, layer_c, splits in casts:
        _, n_rows, n_cols = src.shape
        splits = tuple(splits) if splits is not None else ((0, n_cols),)
        blocks = _cast_row_blocks(n_rows, n_steps)
        every = n_steps // blocks
        in_specs.append(pl.BlockSpec((None, n_rows // blocks, n_cols),
                                     lambda i, layer_c=layer_c, every=every: (layer_c, i // every, 0)))
        args.append(src)
        for lo, hi in splits:
            out_specs.append(pl.BlockSpec((None, n_rows // blocks, hi - lo),
                                          lambda i, every=every: (0, i // every, 0)))
            out_shape.append(jax.ShapeDtypeStruct((1, n_rows, hi - lo), BF16))
        cast_plan.append(splits)
    out = pl.pallas_call(
        functools.partial(_ffn_kernel, with_ple=ple is not None, with_final=final_gain is not None,
                          cast_plan=tuple(cast_plan)),
        grid=(n_steps,),
        in_specs=in_specs,
        out_specs=out_specs,
        out_shape=out_shape,
        compiler_params=pltpu.CompilerParams(dimension_semantics=("arbitrary",),
                                             vmem_limit_bytes=VMEM_LIMIT_BYTES),
    )(*args)
    return (out[0], out[1:]) if casts else out[0]


def _split(a):
    hi = a.astype(BF16)
    lo = (a - hi.astype(F32)).astype(BF16)
    return hi, lo


def _dot3(a_hi, a_lo, b_hi, b_lo):
    m = a_hi.shape[0]
    both = jnp.dot(jnp.concatenate([a_hi, a_lo], axis=0), b_hi, preferred_element_type=F32)
    return both[:m] + both[m:] + jnp.dot(a_hi, b_lo, preferred_element_type=F32)


def _fold(block_diag):
    n = block_diag.shape[0] // CHUNK_D
    out = block_diag[0:CHUNK_D]
    for g in range(1, n):
        out = out + block_diag[g * CHUNK_D:(g + 1) * CHUNK_D]
    return out


def _expand(packed, diag_ones_ref):
    n = packed.shape[1] // CHUNK_D
    return jnp.concatenate([packed] * n, axis=0) * diag_ones_ref[...]


def _unit_lower_inverses_packed(l_packed_list, diag_ones_ref):
    c, width = l_packed_list[0].shape
    row = lax.broadcasted_iota(jnp.int32, (c, width), 0)
    col = lax.broadcasted_iota(jnp.int32, (c, width), 1) % c
    zero = jnp.zeros((), BF16)

    def lower_left(bs):
        return (row // (2 * bs) == col // (2 * bs)) & ((row // bs) % 2 == 1) & ((col // bs) % 2 == 0)

    l_bf = [l.astype(BF16) for l in l_packed_list]
    xs = [jnp.where(row == col, 1.0, 0.0) - jnp.where(lower_left(1), l, 0.0) for l in l_packed_list]
    bs = 2
    while bs < c:
        sel = lower_left(bs)
        x_bf = [x.astype(BF16) for x in xs]
        ys = [jnp.dot(jnp.where(sel, l, zero), _expand(x, diag_ones_ref), preferred_element_type=F32)
              for l, x in zip(l_bf, x_bf)]
        xs = [x - jnp.dot(xb, _expand(y.astype(BF16), diag_ones_ref), preferred_element_type=F32)
              for x, xb, y in zip(xs, x_bf, ys)]
        bs *= 2
    return xs


def _mix_prompt_kernel(x_ref, nmix_ref, wmain_ref, wtail_ref, avg_ref, aog_ref, bog_ref, wsp_ref, bsp_ref,
                       cw_ref, abp_ref, wout_ref,
                       xo_ref, s_ref, ct_ref,
                       zlast_ref, ob_ref, bd_ref, wgate_ref):
    tl = x_ref.shape[0]
    step = pl.program_id(1)

    @pl.when((pl.program_id(0) == 0) & (step == 0))
    def _():
        wgate_ref[...] = wtail_ref[:, 2 * N_HEADS:]

    @pl.when(step == 0)
    def _():
        s_ref[...] = jnp.zeros_like(s_ref)
        zlast_ref[...] = jnp.zeros_like(zlast_ref)

    x = x_ref[...]
    xn = _rms(x, nmix_ref[...]).astype(BF16)
    def in_proj(lo, hi):
        return jnp.dot(xn, wmain_ref[:, lo:hi], preferred_element_type=F32)

    zab = jnp.dot(xn, wtail_ref[:, :LANES], preferred_element_type=F32)
    first_tile = lax.broadcasted_iota(jnp.int32, (8, 1), 0)
    qkv_parts = []
    for part in range(3):
        cols = slice(part * B_WIDTH, (part + 1) * B_WIDTH)
        zc = in_proj(OFF_QKV + part * B_WIDTH, OFF_QKV + (part + 1) * B_WIDTH)
        cw = cw_ref[:, cols]
        carried = zlast_ref[:, cols]
        y = zc * cw[CONV_W - 1:CONV_W]
        for d in range(1, CONV_W):
            rolled = pltpu.roll(zc, d, axis=0)
            top = jnp.where(first_tile < d, pltpu.roll(carried, d, axis=0), rolled[0:8])
            y = y + jnp.concatenate([top, rolled[8:]], axis=0) * cw[CONV_W - 1 - d:CONV_W - d]
        zlast_ref[:, cols] = zc[tl - 8:tl]
        ct_ref[:, cols] = zc[tl - (CONV_W - 1):tl]
        qkv_parts.append(_silu(y))

    uv = jax.nn.gelu(in_proj(0, OFF_QKV))
    row = lax.broadcasted_iota(jnp.int32, (CHUNK_A, CHUNK_A), 0)
    col = lax.broadcasted_iota(jnp.int32, (CHUNK_A, CHUNK_A), 1)
    causal = col <= row
    for h in range(N_HEADS):
        hs = slice(h * HEAD_DIM, (h + 1) * HEAD_DIM)
        u_h = uv[:, hs]
        v_h = _rms(uv[:, A_WIDTH + h * HEAD_DIM:A_WIDTH + (h + 1) * HEAD_DIM], avg_ref[...]).astype(BF16)
        w_h = jnp.where(causal, wsp_ref[h], 0.0).astype(BF16)
        bias_h = bsp_ref[:, h:h + 1]
        for c in range(tl // CHUNK_A):
            rs = slice(c * CHUNK_A, (c + 1) * CHUNK_A)
            mixed = jnp.dot(w_h, v_h[rs], preferred_element_type=F32) + bias_h
            ob_ref[rs, hs] = _rms(u_h[rs] * mixed, aog_ref[...]).astype(BF16)

    z_gate = jnp.dot(xn, wgate_ref[...], preferred_element_type=F32)
    abp = abp_ref[...]
    g = -jnp.exp(abp[0:1]) * _softplus(zab + abp[1:2])
    beta = jax.nn.sigmoid(zab)

    r2 = lax.broadcasted_iota(jnp.int32, (GROUP, GROUP), 0)
    c2 = lax.broadcasted_iota(jnp.int32, (GROUP, GROUP), 1)
    same = (r2 // CHUNK_D) == (c2 // CHUNK_D)
    strict_bd = same & (c2 < r2)
    col_ones = jnp.concatenate([jnp.where(same & (c2 <= r2), 1.0, 0.0), jnp.where(same, 1.0, 0.0)],
                               axis=0).astype(BF16)
    upper_ones = jnp.where(same & (r2 <= c2), 1.0, 0.0).astype(BF16)
    bd_ref[...] = jnp.where(same, 1.0, 0.0).astype(BF16)
    n_grp = tl // GROUP
    gam_parts, glast_parts, gam_t = [], [], []
    for gi in range(n_grp):
        g_grp = g[gi * GROUP:(gi + 1) * GROUP]
        by_col = jnp.dot(col_ones, jnp.concatenate(_split3(g_grp), axis=1), preferred_element_type=F32)
        by_col = by_col[:, :LANES] + by_col[:, LANES:2 * LANES] + by_col[:, 2 * LANES:]
        gam_parts.append(by_col[:GROUP])
        glast_parts.append(by_col[GROUP:])
        by_row = jnp.dot(jnp.concatenate(_split3(g_grp.T), axis=0), upper_ones, preferred_element_type=F32)
        gam_t.append(by_row[:LANES] + by_row[LANES:2 * LANES] + by_row[2 * LANES:])
    gam = jnp.concatenate(gam_parts, axis=0)
    glast = jnp.concatenate(glast_parts, axis=0)

    rb = lax.broadcasted_iota(jnp.int32, (CHUNK_D, CHUNK_D), 0)
    cb = lax.broadcasted_iota(jnp.int32, (CHUNK_D, CHUNK_D), 1)
    incl = cb <= rb

    heads, a_packed, rhs = [], [], []
    for h in range(N_HEADS):
        q_h = _l2(qkv_parts[0][:, h * HEAD_DIM:(h + 1) * HEAD_DIM]) * (HEAD_DIM ** -0.5)
        k_h = _l2(qkv_parts[1][:, h * HEAD_DIM:(h + 1) * HEAD_DIM])
        v_h = qkv_parts[2][:, h * HEAD_DIM:(h + 1) * HEAD_DIM]
        gc_h = gam[:, h:h + 1]
        gl_h = glast[:, h:h + 1]
        bc_h = beta[:, N_HEADS + h:N_HEADS + h + 1]
        eg_h = jnp.exp(gc_h)
        for gi in range(n_grp):
            gs = slice(gi * GROUP, (gi + 1) * GROUP)
            kk = _dot_nt(k_h[gs], k_h[gs])
            decay = jnp.exp(jnp.where(strict_bd, gc_h[gs] - gam_t[gi][h:h + 1, :], 0.0))
            a_packed.append(_fold(jnp.where(strict_bd, bc_h[gs] * kk * decay, 0.0)))
        rhs.append(_split(jnp.concatenate([bc_h * v_h, (bc_h * eg_h) * k_h], axis=1)))
        heads.append((q_h * eg_h, q_h, k_h, k_h * jnp.exp(gl_h - gc_h), gc_h, jnp.exp(gl_h)))
    inv_split = [_split(inv) for inv in _unit_lower_inverses_packed(a_packed, bd_ref)]
    sol = []
    for h in range(N_HEADS):
        sol.append([_dot3(_expand(inv_split[h * n_grp + gi][0], bd_ref), _expand(inv_split[h * n_grp + gi][1], bd_ref),
                          rhs[h][0][gi * GROUP:(gi + 1) * GROUP], rhs[h][1][gi * GROUP:(gi + 1) * GROUP])
                    for gi in range(n_grp)])

    for i in range(tl // CHUNK_D):
        rs = slice(i * CHUNK_D, (i + 1) * CHUNK_D)
        gi, j = divmod(i, GROUP // CHUNK_D)
        ls = slice(j * CHUNK_D, (j + 1) * CHUNK_D)
        for h in range(N_HEADS):
            qb_h, q_h, k_h, kend_h, gc_h, btot_h = heads[h]
            decay = jnp.where(incl, jnp.exp(jnp.where(incl, gc_h[rs] - gam_t[gi][h:h + 1, ls], 0.0)), 0.0)
            qk = _dot_nt(q_h[rs], k_h[rs]) * decay
            s_old = s_ref[h]
            from_s = _dot(jnp.concatenate([sol[h][gi][ls, HEAD_DIM:], qb_h[rs]], axis=0), s_old)
            u = sol[h][gi][ls, :HEAD_DIM] - from_s[:CHUNK_D]
            from_u = _dot(jnp.concatenate([qk, kend_h[rs].T], axis=0), u)
            o = from_s[CHUNK_D:] + from_u[:CHUNK_D]
            s_ref[h] = btot_h[i * CHUNK_D:i * CHUNK_D + 1] * s_old + from_u[CHUNK_D:]
            gate = z_gate[rs, h * HEAD_DIM:(h + 1) * HEAD_DIM]
            ob_ref[rs, A_WIDTH + h * HEAD_DIM:A_WIDTH + (h + 1) * HEAD_DIM] = (
                _rms(o, bog_ref[...]) * _silu(gate)).astype(BF16)

    xo_ref[...] = x + jnp.dot(ob_ref[...], wout_ref[...], preferred_element_type=F32)


def _mixer_weight_specs(layer):
    return [_layer_spec((1, D_MODEL), layer), _layer_spec((D_MODEL, Z_FRONT), 0), _layer_spec((D_MODEL, Z_TAIL), 0),
            _layer_spec((1, HEAD_DIM), layer), _layer_spec((1, HEAD_DIM), layer), _layer_spec((1, HEAD_DIM), layer)]


def _mixer_weights(w):
    return [w["n_mix"], w["w_front"], w["w_tail"], w["a_v_gain"], w["a_out_gain"], w["b_out_gain"]]


def _mix_prompt(x, layer, w):
    bsz, length, _ = x.shape
    tl = MIX_ROWS
    row_spec = pl.BlockSpec((None, tl, D_MODEL), lambda b, t: (b, t, 0))
    in_specs = [row_spec] + _mixer_weight_specs(layer) + [
        _layer_spec((N_HEADS, CHUNK_A, CHUNK_A), layer), _layer_spec((CHUNK_A, N_HEADS), layer),
        _layer_spec((CONV_W, 3 * B_WIDTH), layer), _layer_spec((2, LANES), layer), _layer_spec((D_MODEL, D_MODEL), 0)]
    out_specs = [row_spec,
                 pl.BlockSpec((None, N_HEADS, HEAD_DIM, HEAD_DIM), lambda b, t: (b, 0, 0, 0)),
                 pl.BlockSpec((None, CONV_W - 1, 3 * B_WIDTH), lambda b, t: (b, 0, 0))]
    out_shape = [jax.ShapeDtypeStruct(x.shape, F32),
                 jax.ShapeDtypeStruct((bsz, N_HEADS, HEAD_DIM, HEAD_DIM), F32),
                 jax.ShapeDtypeStruct((bsz, CONV_W - 1, 3 * B_WIDTH), F32)]
    return pl.pallas_call(
        _mix_prompt_kernel,
        grid=(bsz, length // tl),
        in_specs=in_specs,
        out_specs=out_specs,
        out_shape=out_shape,
        scratch_shapes=[pltpu.VMEM((8, 3 * B_WIDTH), F32), pltpu.VMEM((tl, D_MODEL), BF16),
                        pltpu.VMEM((GROUP, GROUP), BF16), pltpu.VMEM((D_MODEL, B_WIDTH), BF16)],
        compiler_params=pltpu.CompilerParams(dimension_semantics=("arbitrary", "arbitrary"),
                                             vmem_limit_bytes=VMEM_LIMIT_BYTES),
    )(x, *_mixer_weights(w), w["a_w_s"], w["a_b_s_t"], w["b_conv"], w["ab_par"], w["w_out"])


def _mix_sample_kernel(x_ref, s_ref, cpad_ref, s_all_ref, nmix_ref, wmain_ref, wtail_ref, avg_ref, aog_ref, bog_ref,
                       coef_ref, bias_ref, cw_ref, abp_ref, wout_ref,
                       xo_ref, so_ref, zq_ref, vo_ref,
                       z_ref, ztail_ref, ob_ref, *, n_tok):
    del s_all_ref
    rows = zq_ref.shape[0]
    nb = rows // n_tok
    step = pl.program_id(0)

    @pl.when(step == 0)
    def _():
        xn = _rms(x_ref[...], nmix_ref[...]).astype(BF16)
        z_ref[...] = jnp.dot(xn, wmain_ref[...], preferred_element_type=F32)
        ztail_ref[...] = jnp.dot(xn, wtail_ref[...], preferred_element_type=F32)

    here = pl.ds(pl.multiple_of(step * rows, rows), rows)
    z = z_ref[here, :]
    z_tail = ztail_ref[here, :]
    zab = z_tail[:, :LANES]
    tok = lax.broadcasted_iota(jnp.int32, (rows, 1), 0) % n_tok

    def prev(a, d):
        return pltpu.roll(a, d, axis=0)

    def prev_or_zero(a, d):
        return a if d == 0 else jnp.where(tok >= d, prev(a, d), 0.0)

    def per_head(fn, a):
        return jnp.concatenate([fn(a[:, h * HEAD_DIM:(h + 1) * HEAD_DIM]) for h in range(N_HEADS)], axis=1)

    uv = jax.nn.gelu(z[:, :2 * A_WIDTH])
    vn = per_head(lambda a: _rms(a, avg_ref[...]), uv[:, A_WIDTH:])
    vo_ref[...] = vn
    mixed = bias_ref[...]
    for d in range(n_tok):
        mixed = mixed + coef_ref[d] * prev_or_zero(vn, d)
    ob_ref[here, :A_WIDTH] = per_head(lambda a: _rms(a, aog_ref[...]), uv[:, :A_WIDTH] * mixed)

    zq = z[:, OFF_QKV:OFF_GATE]
    zq_ref[...] = zq
    cpad = cpad_ref[...]
    cw = cw_ref[...]
    y = zq * cw[CONV_W - 1:CONV_W]
    for d in range(1, CONV_W):
        carried = pltpu.roll(cpad, rows - (n_tok - d), axis=0)
        y = y + jnp.where(tok >= d, prev(zq, d), carried) * cw[CONV_W - 1 - d:CONV_W - d]
    qkv = _silu(y)

    abp = abp_ref[...]
    g_all = -jnp.exp(abp[0:1]) * _softplus(zab + abp[1:2])
    beta_all = jax.nn.sigmoid(zab)

    sub = lax.broadcasted_iota(jnp.int32, (8, 1), 0)
    first_half = sub < n_tok
    o_heads, kend_heads, u_heads, btot_heads = [], [], [], []
    for h in range(N_HEADS):
        q = _l2(qkv[:, h * HEAD_DIM:(h + 1) * HEAD_DIM]) * (HEAD_DIM ** -0.5)
        k = _l2(qkv[:, B_WIDTH + h * HEAD_DIM:B_WIDTH + (h + 1) * HEAD_DIM])
        v = qkv[:, 2 * B_WIDTH + h * HEAD_DIM:2 * B_WIDTH + (h + 1) * HEAD_DIM]
        g = jnp.broadcast_to(g_all[:, h:h + 1], (rows, HEAD_DIM))
        beta = jnp.broadcast_to(beta_all[:, N_HEADS + h:N_HEADS + h + 1], (rows, HEAD_DIM))
        gam = g
        for d in range(1, n_tok):
            gam = gam + prev_or_zero(g, d)
        gam_last = jnp.where(tok == n_tok - 1, gam, 0.0)
        for d in range(1, n_tok):
            gam_last = gam_last + jnp.where(tok == n_tok - 1 - d, pltpu.roll(gam, rows - d, axis=0), 0.0)
        eg = jnp.exp(gam)

        def decay_to(d, gam=gam):
            return jnp.exp(jnp.where(tok >= d, gam - prev(gam, d), 0.0))

        a_sub = [None] + [jnp.where(tok >= d, beta * jnp.sum(k * prev(k, d), axis=-1, keepdims=True) * decay_to(d),
                                    0.0) for d in range(1, n_tok)]
        def forward_substitute(rhs, a_sub=a_sub):
            sol = rhs
            for t in range(1, n_tok):
                acc = rhs
                for d in range(1, t + 1):
                    acc = acc - a_sub[d] * prev(sol, d)
                sol = jnp.where(tok == t, acc, sol)
            return sol

        w_blk = forward_substitute(beta * v)
        kb_blk = forward_substitute((beta * eg) * k)
        qb = q * eg

        kb_s, qb_s = [], []
        for p in range(rows // 8):
            kb_t, qb_t = kb_blk[8 * p:8 * p + 8], qb[8 * p:8 * p + 8]
            f0 = _dot(jnp.where(first_half, kb_t, pltpu.roll(qb_t, n_tok, axis=0)), s_ref[2 * p, h])
            f1 = _dot(jnp.where(first_half, pltpu.roll(kb_t, n_tok, axis=0), qb_t), s_ref[2 * p + 1, h])
            kb_s.append(jnp.where(first_half, f0, pltpu.roll(f1, n_tok, axis=0)))
            qb_s.append(jnp.where(first_half, pltpu.roll(f0, n_tok, axis=0), f1))
        u = w_blk - jnp.concatenate(kb_s, axis=0)
        o = jnp.concatenate(qb_s, axis=0)
        for d in range(n_tok):
            qk = jnp.where(tok >= d, jnp.sum(q * prev(k, d), axis=-1, keepdims=True) * decay_to(d), 0.0)
            o = o + qk * prev_or_zero(u, d)
        o_heads.append(o)
        kend_heads.append(k * jnp.exp(gam_last - gam))
        u_heads.append(u)
        btot_heads.append(jnp.broadcast_to(jnp.exp(gam_last), (rows, HEAD_DIM)))

    kend_t = jnp.concatenate(kend_heads, axis=0).T
    u_all = jnp.concatenate(u_heads, axis=0).astype(BF16)
    owner = lax.broadcasted_iota(jnp.int32, (1, N_HEADS * rows), 1) // n_tok
    for h in range(N_HEADS):
        for b in range(nb):
            mine = jnp.where(owner == h * nb + b, kend_t, 0.0).astype(BF16)
            last = b * n_tok + n_tok - 1
            so_ref[b, h] = btot_heads[h][last:last + 1] * s_ref[b, h] + jnp.dot(
                mine, u_all, preferred_element_type=F32)

    gate = z_tail[:, 2 * N_HEADS:]
    ob_ref[here, A_WIDTH:] = per_head(lambda a: _rms(a, bog_ref[...]), jnp.concatenate(o_heads, axis=1)) * _silu(gate)

    @pl.when(step == pl.num_programs(0) - 1)
    def _():
        xo_ref[...] = x_ref[...] + _dot(ob_ref[...], wout_ref[...])


def _mix_sample(x, layer, state_s, cpad, s_all, w, n_tok):
    rows_total = x.shape[0]
    nb = SAMPLE_GROUP
    rows = nb * n_tok
    all_rows = lambda width: pl.BlockSpec((rows_total, width), lambda i: (0, 0))
    row_spec = lambda width: pl.BlockSpec((rows, width), lambda i: (i, 0))
    s_spec = pl.BlockSpec((None, nb, N_HEADS, HEAD_DIM, HEAD_DIM), lambda i: (layer, i, 0, 0, 0))
    in_specs = [all_rows(D_MODEL), s_spec, pl.BlockSpec((None, rows, 3 * B_WIDTH), lambda i: (layer, i, 0)),
                pl.BlockSpec(memory_space=pl.ANY)] + _mixer_weight_specs(layer) + [
        _layer_spec((n_tok, rows, A_WIDTH), layer), _layer_spec((rows, A_WIDTH), layer),
        _layer_spec((CONV_W, 3 * B_WIDTH), layer), _layer_spec((2, LANES), layer), _layer_spec((D_MODEL, D_MODEL), 0)]
    out_specs = [all_rows(D_MODEL), s_spec, row_spec(3 * B_WIDTH), row_spec(A_WIDTH)]
    out_shape = [jax.ShapeDtypeStruct(x.shape, F32), jax.ShapeDtypeStruct(state_s.shape, F32),
                 jax.ShapeDtypeStruct((rows_total, 3 * B_WIDTH), F32), jax.ShapeDtypeStruct((rows_total, A_WIDTH), F32)]
    return pl.pallas_call(
        functools.partial(_mix_sample_kernel, n_tok=n_tok),
        grid=(rows_total // rows,),
        in_specs=in_specs,
        out_specs=out_specs,
        out_shape=out_shape,
        input_output_aliases={3: 1},
        scratch_shapes=[pltpu.VMEM((rows_total, Z_FRONT), F32), pltpu.VMEM((rows_total, Z_TAIL), F32),
                        pltpu.VMEM((rows_total, D_MODEL), F32)],
        compiler_params=pltpu.CompilerParams(dimension_semantics=("arbitrary",),
                                             vmem_limit_bytes=VMEM_LIMIT_BYTES),
    )(x, state_s, cpad, s_all, *_mixer_weights(w), w["a_coef"], w["a_bias"], w["b_conv"], w["ab_par"], w["w_out"])


def _prep_tables(n_tok, norm_ffn1, norm_mix, a_v_gain, a_spatial_w, a_spatial_b, a_out_gain, b_conv_w, b_a_log,
                 b_dt_bias, b_out_gain, norm_ffn2, norm_ple):
    par_pad = jnp.zeros((DEPTH, LANES - N_HEADS), F32)
    ab_par = jnp.stack([jnp.concatenate([b_a_log, par_pad], axis=1),
                        jnp.concatenate([b_dt_bias, par_pad], axis=1)], axis=1)

    def sample_rows(a):
        return jnp.tile(jnp.repeat(jnp.transpose(a, (0, 2, 1)), HEAD_DIM, axis=2), (1, SAMPLE_GROUP, 1))

    ws_small = a_spatial_w[:, :, :n_tok, :n_tok]
    a_coef = jnp.stack([sample_rows(jnp.pad(jnp.diagonal(ws_small, offset=-d, axis1=2, axis2=3),
                                            ((0, 0), (0, 0), (d, 0)))) for d in range(n_tok)], axis=1)
    return dict(
        n_f1=norm_ffn1[:, None], n_mix=norm_mix[:, None], ab_par=ab_par,
        a_v_gain=a_v_gain[:, None], a_out_gain=a_out_gain[:, None], b_out_gain=b_out_gain[:, None],
        a_w_s=a_spatial_w, a_b_s_t=jnp.transpose(a_spatial_b, (0, 2, 1)),
        a_coef=a_coef, a_bias=sample_rows(a_spatial_b[:, :, :n_tok]),
        b_conv=b_conv_w, n_f2=norm_ffn2[:, None], n_ple=norm_ple[:, None],
    )


def kernel(x_prompt, x_sample, state_S, state_conv, p_prompt, p_sample, norm_ffn1, w_ffn1_in, w_ffn1_out, norm_mix, w_in, a_v_gain, a_spatial_w, a_spatial_b, a_out_gain, b_conv_w, b_a_log, b_dt_bias, b_out_gain, w_out, norm_ffn2, w_ffn2_in, w_ffn2_out, norm_ple, w_ple_gate, w_ple_proj, final_norm):
    bsz, length, _ = x_prompt.shape
    dec_bsz, n_tok, _ = x_sample.shape
    assert length % MIX_ROWS == 0 and MIX_ROWS % CHUNK_A == 0 and MIX_ROWS % GROUP == 0
    assert dec_bsz % SAMPLE_GROUP == 0
    assert n_tok % CHUNK_A != 0 and n_tok % CHUNK_D != 0
    assert 2 * n_tok == 8 and N_HEADS * SAMPLE_GROUP * n_tok == HEAD_DIM and n_tok >= CONV_W - 1

    w = _prep_tables(n_tok, norm_ffn1, norm_mix, a_v_gain, a_spatial_w, a_spatial_b, a_out_gain, b_conv_w, b_a_log,
                     b_dt_bias, b_out_gain, norm_ffn2, norm_ple)
    ffn1_w = (w_ffn1_in[0:1].astype(BF16), w_ffn1_out[0:1].astype(BF16))
    in_proj_splits = ((0, Z_FRONT), (Z_FRONT, Z_FRONT + Z_TAIL))
    final = final_norm[None, None]
    xp = x_prompt.reshape(bsz * length, D_MODEL)
    xs = x_sample.reshape(dec_bsz * n_tok, D_MODEL)
    pp = p_prompt.reshape(DEPTH, bsz * length, PLE_DIM)
    ps = p_sample.reshape(DEPTH, dec_bsz * n_tok, PLE_DIM)
    keep = CONV_W - 1
    cpad = jnp.pad(state_conv, ((0, 0), (0, 0), (n_tok - keep, 0), (0, 0))).reshape(DEPTH, dec_bsz * n_tok, 3 * B_WIDTH)

    s_prompt, c_prompt, c_sample, v_sample = [], [], [], []
    s_sample = jnp.zeros(state_S.shape, F32)
    for i in range(DEPTH):
        last = dict(final_gain=final) if i == DEPTH - 1 else {}

        xp, (f2_in, f2_out, w_front, w_tail, w_o, ple_gate, ple_proj) = _ffn(
            xp, i, w["n_f1"], *ffn1_w,
            casts=[(w_ffn2_in, i, None), (w_ffn2_out, i, None), (w_in, i, in_proj_splits), (w_out, i, None),
                   (w_ple_gate, i, None), (w_ple_proj, i, None)])
        xs = _ffn(xs, i, w["n_f1"], *ffn1_w)
        mix_w = dict(w, w_front=w_front, w_tail=w_tail, w_out=w_o)
        xp, sp, cp = _mix_prompt(xp.reshape(bsz, length, D_MODEL), i, mix_w)
        xs, s_sample, zq, vs = _mix_sample(xs, i, state_S, cpad, s_sample, mix_w, n_tok)
        xp = xp.reshape(bsz * length, D_MODEL)
        ple = (w["n_ple"], ple_gate, ple_proj)
        if i < DEPTH - 1:
            xp, ffn1_w = _ffn(xp, i, w["n_f2"], f2_in, f2_out, ple=(pp,) + ple,
                              casts=[(w_ffn1_in, i + 1, None), (w_ffn1_out, i + 1, None)])
        else:
            xp = _ffn(xp, i, w["n_f2"], f2_in, f2_out, ple=(pp,) + ple, **last)
        xs = _ffn(xs, i, w["n_f2"], f2_in, f2_out, ple=(ps,) + ple, **last)

        s_prompt.append(sp)
        c_prompt.append(cp)
        c_sample.append(zq.reshape(dec_bsz, n_tok, 3 * B_WIDTH)[:, n_tok - keep:])
        v_sample.append(vs.reshape(dec_bsz, n_tok, N_HEADS, HEAD_DIM))

    return (xp.reshape(bsz, length, D_MODEL), xs.reshape(dec_bsz, n_tok, D_MODEL), jnp.stack(s_prompt),
            jnp.stack(c_prompt), s_sample, jnp.stack(c_sample), jnp.stack(v_sample))
```

```python
import functools

import jax
import jax.numpy as jnp
from jax import lax
from jax.experimental import pallas as pl
from jax.experimental.pallas import tpu as pltpu

F32 = jnp.float32
BF16 = jnp.bfloat16
EPS = 1e-6

D_MODEL = 1024
D_FF = 2816
DEPTH = 4
N_HEADS = 4
HEAD_DIM = 128
A_WIDTH = N_HEADS * HEAD_DIM
B_WIDTH = N_HEADS * HEAD_DIM
CHUNK_A = 128
CHUNK_D = 128
CONV_W = 4
PLE_DIM = 256
Z_FRONT = 2 * A_WIDTH + 3 * B_WIDTH
Z_TAIL = 2 * N_HEADS + B_WIDTH
OFF_QKV = 2 * A_WIDTH
OFF_GATE = OFF_QKV + 3 * B_WIDTH

VMEM_LIMIT_BYTES = 52 * 1024 * 1024
LANES = 128
BF16_SUBLANES = 16
MXU_N = 256
FFN_ROWS = 512
GROUP = MXU_N
MIX_ROWS = 512
SAMPLE_GROUP = 8


def _rms(x, gain):
    return x * lax.rsqrt(jnp.mean(x * x, axis=-1, keepdims=True) + EPS) * gain


def _l2(x):
    return x * lax.rsqrt(jnp.sum(x * x, axis=-1, keepdims=True) + EPS)


def _silu(x):
    return x * jax.nn.sigmoid(x)


def _softplus(x):
    return jnp.maximum(x, 0.0) + jnp.log1p(jnp.exp(-jnp.abs(x)))


def _dot(a, b):
    return jnp.dot(a.astype(BF16), b.astype(BF16), preferred_element_type=F32)


def _dot_nt(a, b):
    return lax.dot_general(a.astype(BF16), b.astype(BF16), (((1,), (1,)), ((), ())),
                           preferred_element_type=F32)


def _split3(a):
    p1 = a.astype(BF16)
    r1 = a - p1.astype(F32)
    p2 = r1.astype(BF16)
    p3 = (r1 - p2.astype(F32)).astype(BF16)
    return p1, p2, p3


def _layer_spec(shape, layer, block=None):
    index = (layer,) + (0,) * (len(shape) - 1) + (0 if block is None else block,)
    return pl.BlockSpec((None,) + tuple(shape), lambda *_: index, pipeline_mode=pl.Buffered(1))


def _ffn_kernel(*refs, with_ple, with_final, cast_plan):
    x_ref, gain_ref, wg_ref, wu_ref, wo_ref = refs[:5]
    n_in = 5 + 4 * with_ple + with_final
    cast_in = refs[n_in:n_in + len(cast_plan)]
    o_ref = refs[n_in + len(cast_plan)]
    cast_out = iter(refs[n_in + len(cast_plan) + 1:])
    for src_ref, splits in zip(cast_in, cast_plan):
        for lo, hi in splits:
            next(cast_out)[...] = src_ref[:, lo:hi].astype(BF16)
    x = x_ref[...]
    xn = _rms(x, gain_ref[...]).astype(BF16)
    acc = jnp.zeros_like(x)
    for c in range(D_FF // MXU_N):
        sl = slice(c * MXU_N, (c + 1) * MXU_N)
        gate = jnp.dot(xn, wg_ref[:, sl], preferred_element_type=F32)
        up = jnp.dot(xn, wu_ref[:, sl], preferred_element_type=F32)
        h = (_silu(gate) * up).astype(BF16)
        acc = acc + jnp.dot(h, wo_ref[sl, :], preferred_element_type=F32)
    x = x + 0.5 * acc
    if with_ple:
        p_ref, npl_ref, wpg_ref, wpp_ref = refs[5:9]
        emb = _dot(p_ref[...], wpp_ref[...])
        gate = _dot(_rms(x, npl_ref[...]), wpg_ref[...])
        x = x + emb * jax.nn.sigmoid(gate)
    if with_final:
        x = _rms(x, refs[9][...])
    o_ref[...] = x


def _cast_row_blocks(n_rows, n_steps):
    blocks = n_steps
    while n_rows % blocks or (n_rows // blocks) % BF16_SUBLANES:
        assert blocks % 2 == 0, (n_rows, n_steps)
        blocks //= 2
    return blocks


def _ffn(x, layer, gain, w_in, w_out, ple=None, final_gain=None, casts=()):
    rows = x.shape[0]
    tm = min(FFN_ROWS, rows)
    n_steps = rows // tm
    row_spec = pl.BlockSpec((tm, D_MODEL), lambda i: (i, 0))
    in_specs = [row_spec, _layer_spec((1, D_MODEL), layer),
                _layer_spec((D_MODEL, D_FF), 0, block=0), _layer_spec((D_MODEL, D_FF), 0, block=1),
                _layer_spec((D_FF, D_MODEL), 0)]
    args = [x, gain, w_in, w_in, w_out]
    out_specs = [row_spec]
    out_shape = [jax.ShapeDtypeStruct(x.shape, F32)]
    if ple is not None:
        p, n_ple, w_gate, w_proj = ple
        in_specs += [pl.BlockSpec((None, tm, PLE_DIM), lambda i: (layer, i, 0)), _layer_spec((1, D_MODEL), layer),
                     _layer_spec((D_MODEL, D_MODEL), 0), _layer_spec((PLE_DIM, D_MODEL), 0)]
        args += [p, n_ple, w_gate, w_proj]
    if final_gain is not None:
        in_specs.append(_layer_spec((1, D_MODEL), 0))
        args.append(final_gain)
    cast_plan = []
    for src, layer_c, splits in casts:
        _, n_rows, n_cols = src.shape
        splits = tuple(splits) if splits is not None else ((0, n_cols),)
        blocks = _cast_row_blocks(n_rows, n_steps)
        every = n_steps // blocks
        in_specs.append(pl.BlockSpec((None, n_rows // blocks, n_cols),
                                     lambda i, layer_c=layer_c, every=every: (layer_c, i // every, 0)))
        args.append(src)
        for lo, hi in splits:
            out_specs.append(pl.BlockSpec((None, n_rows // blocks, hi - lo),
                                          lambda i, every=every: (0, i // every, 0)))
            out_shape.append(jax.ShapeDtypeStruct((1, n_rows, hi - lo), BF16))
        cast_plan.append(splits)
    out = pl.pallas_call(
        functools.partial(_ffn_kernel, with_ple=ple is not None, with_final=final_gain is not None,
                          cast_plan=tuple(cast_plan)),
        grid=(n_steps,),
        in_specs=in_specs,
        out_specs=out_specs,
        out_shape=out_shape,
        compiler_params=pltpu.CompilerParams(dimension_semantics=("arbitrary",),
                                             vmem_limit_bytes=VMEM_LIMIT_BYTES),
    )(*args)
    return (out[0], out[1:]) if casts else out[0]


def _split(a):
    hi = a.astype(BF16)
    lo = (a - hi.astype(F32)).astype(BF16)
    return hi, lo


def _dot3(a_hi, a_lo, b_hi, b_lo):
    m = a_hi.shape[0]
    both = jnp.dot(jnp.concatenate([a_hi, a_lo], axis=0), b_hi, preferred_element_type=F32)
    return both[:m] + both[m:] + jnp.dot(a_hi, b_lo, preferred_element_type=F32)


def _expand(packed, diag_ones_ref):
    n = packed.shape[1] // CHUNK_D
    return jnp.concatenate([packed] * n, axis=0) * diag_ones_ref[...]


def _unit_lower_inverses_packed(l_packed_list, diag_ones_ref):
    c, width = l_packed_list[0].shape
    row = lax.broadcasted_iota(jnp.int32, (c, width), 0)
    col = lax.broadcasted_iota(jnp.int32, (c, width), 1) % c
    zero = jnp.zeros((), BF16)

    def lower_left(bs):
        return (row // (2 * bs) == col // (2 * bs)) & ((row // bs) % 2 == 1) & ((col // bs) % 2 == 0)

    l_bf = [l.astype(BF16) for l in l_packed_list]
    xs = [jnp.where(row == col, 1.0, 0.0) - jnp.where(lower_left(1), l, 0.0) for l in l_packed_list]
    bs = 2
    while bs < c:
        sel = lower_left(bs)
        x_bf = [x.astype(BF16) for x in xs]
        ys = [jnp.dot(jnp.where(sel, l, zero), _expand(x, diag_ones_ref), preferred_element_type=F32)
              for l, x in zip(l_bf, x_bf)]
        xs = [x - jnp.dot(xb, _expand(y.astype(BF16), diag_ones_ref), preferred_element_type=F32)
              for x, xb, y in zip(xs, x_bf, ys)]
        bs *= 2
    return xs


def _mix_prompt_kernel(x_ref, nmix_ref, wmain_ref, wtail_ref, avg_ref, aog_ref, bog_ref, wsp_ref, bsp_ref,
                       cw_ref, abp_ref, wout_ref,
                       xo_ref, s_ref, ct_ref,
                       zlast_ref, ob_ref, bd_ref, wgate_ref):
    tl = x_ref.shape[0]
    step = pl.program_id(1)

    @pl.when((pl.program_id(0) == 0) & (step == 0))
    def _():
        wgate_ref[...] = wtail_ref[:, 2 * N_HEADS:]

    @pl.when(step == 0)
    def _():
        s_ref[...] = jnp.zeros_like(s_ref)
        zlast_ref[...] = jnp.zeros_like(zlast_ref)

    x = x_ref[...]
    xn = _rms(x, nmix_ref[...]).astype(BF16)
    def in_proj(lo, hi):
        return jnp.dot(xn, wmain_ref[:, lo:hi], preferred_element_type=F32)

    zab = jnp.dot(xn, wtail_ref[:, :LANES], preferred_element_type=F32)
    first_tile = lax.broadcasted_iota(jnp.int32, (8, 1), 0)
    qkv_parts = []
    for part in range(3):
        cols = slice(part * B_WIDTH, (part + 1) * B_WIDTH)
        zc = in_proj(OFF_QKV + part * B_WIDTH, OFF_QKV + (part + 1) * B_WIDTH)
        cw = cw_ref[:, cols]
        carried = zlast_ref[:, cols]
        y = zc * cw[CONV_W - 1:CONV_W]
        for d in range(1, CONV_W):
            rolled = pltpu.roll(zc, d, axis=0)
            top = jnp.where(first_tile < d, pltpu.roll(carried, d, axis=0), rolled[0:8])
            y = y + jnp.concatenate([top, rolled[8:]], axis=0) * cw[CONV_W - 1 - d:CONV_W - d]
        zlast_ref[:, cols] = zc[tl - 8:tl]
        ct_ref[:, cols] = zc[tl - (CONV_W - 1):tl]
        qkv_parts.append(_silu(y))

    uv = jax.nn.gelu(in_proj(0, OFF_QKV))
    row = lax.broadcasted_iota(jnp.int32, (CHUNK_A, CHUNK_A), 0)
    col = lax.broadcasted_iota(jnp.int32, (CHUNK_A, CHUNK_A), 1)
    causal = col <= row
    for h in range(N_HEADS):
        hs = slice(h * HEAD_DIM, (h + 1) * HEAD_DIM)
        u_h = uv[:, hs]
        v_h = _rms(uv[:, A_WIDTH + h * HEAD_DIM:A_WIDTH + (h + 1) * HEAD_DIM], avg_ref[...]).astype(BF16)
        w_h = jnp.where(causal, wsp_ref[h], 0.0).astype(BF16)
        bias_h = bsp_ref[:, h:h + 1]
        for c in range(tl // CHUNK_A):
            rs = slice(c * CHUNK_A, (c + 1) * CHUNK_A)
            mixed = jnp.dot(w_h, v_h[rs], preferred_element_type=F32) + bias_h
            ob_ref[rs, hs] = _rms(u_h[rs] * mixed, aog_ref[...]).astype(BF16)

    z_gate = jnp.dot(xn, wgate_ref[...], preferred_element_type=F32)
    abp = abp_ref[...]
    g = -jnp.exp(abp[0:1]) * _softplus(zab + abp[1:2])
    beta = jax.nn.sigmoid(zab)

    r2 = lax.broadcasted_iota(jnp.int32, (GROUP, GROUP), 0)
    c2 = lax.broadcasted_iota(jnp.int32, (GROUP, GROUP), 1)
    same = (r2 // CHUNK_D) == (c2 // CHUNK_D)
    col_ones = jnp.concatenate([jnp.where(same & (c2 <= r2), 1.0, 0.0), jnp.where(same, 1.0, 0.0)],
                               axis=0).astype(BF16)
    upper_ones = jnp.where(same & (r2 <= c2), 1.0, 0.0).astype(BF16)
    bd_ref[...] = jnp.where(same, 1.0, 0.0).astype(BF16)
    n_grp = tl // GROUP
    gam_parts, glast_parts, gam_t = [], [], []
    for gi in range(n_grp):
        g_grp = g[gi * GROUP:(gi + 1) * GROUP]
        by_col = jnp.dot(col_ones, jnp.concatenate(_split3(g_grp), axis=1), preferred_element_type=F32)
        by_col = by_col[:, :LANES] + by_col[:, LANES:2 * LANES] + by_col[:, 2 * LANES:]
        gam_parts.append(by_col[:GROUP])
        glast_parts.append(by_col[GROUP:])
        by_row = jnp.dot(jnp.concatenate(_split3(g_grp.T), axis=0), upper_ones, preferred_element_type=F32)
        gam_t.append(by_row[:LANES] + by_row[LANES:2 * LANES] + by_row[2 * LANES:])
    gam = jnp.concatenate(gam_parts, axis=0)
    glast = jnp.concatenate(glast_parts, axis=0)

    rb = lax.broadcasted_iota(jnp.int32, (CHUNK_D, CHUNK_D), 0)
    cb = lax.broadcasted_iota(jnp.int32, (CHUNK_D, CHUNK_D), 1)
    strict = cb < rb
    eye = jnp.where(cb == rb, 1.0, 0.0)

    heads, a_packed, rhs = [], [], []
    for h in range(N_HEADS):
        q_h = _l2(qkv_parts[0][:, h * HEAD_DIM:(h + 1) * HEAD_DIM]) * (HEAD_DIM ** -0.5)
        k_h = _l2(qkv_parts[1][:, h * HEAD_DIM:(h + 1) * HEAD_DIM])
        v_h = qkv_parts[2][:, h * HEAD_DIM:(h + 1) * HEAD_DIM]
        gc_h = gam[:, h:h + 1]
        gl_h = glast[:, h:h + 1]
        bc_h = beta[:, N_HEADS + h:N_HEADS + h + 1]
        eg_h = jnp.exp(gc_h)
        qk_blocks = []
        for gi in range(n_grp):
            packed = []
            for j in range(GROUP // CHUNK_D):
                rs = slice(gi * GROUP + j * CHUNK_D, gi * GROUP + (j + 1) * CHUNK_D)
                grams = _dot_nt(jnp.concatenate([k_h[rs], q_h[rs]], axis=0), k_h[rs])
                diff = gc_h[rs] - gam_t[gi][h:h + 1, j * CHUNK_D:(j + 1) * CHUNK_D]
                decay = jnp.where(strict, jnp.exp(jnp.where(strict, diff, 0.0)), 0.0)
                packed.append(bc_h[rs] * grams[:CHUNK_D] * decay)
                qk_blocks.append(grams[CHUNK_D:] * (decay + eye))
            a_packed.append(jnp.concatenate(packed, axis=1))
        rhs.append(_split(jnp.concatenate([bc_h * v_h, (bc_h * eg_h) * k_h], axis=1)))
        heads.append((q_h * eg_h, qk_blocks, k_h * jnp.exp(gl_h - gc_h), jnp.exp(gl_h)))
    inv_split = [_split(inv) for inv in _unit_lower_inverses_packed(a_packed, bd_ref)]
    sol = []
    for h in range(N_HEADS):
        sol.append([_dot3(_expand(inv_split[h * n_grp + gi][0], bd_ref), _expand(inv_split[h * n_grp + gi][1], bd_ref),
                          rhs[h][0][gi * GROUP:(gi + 1) * GROUP], rhs[h][1][gi * GROUP:(gi + 1) * GROUP])
                    for gi in range(n_grp)])

    for i in range(tl // CHUNK_D):
        rs = slice(i * CHUNK_D, (i + 1) * CHUNK_D)
        gi, j = divmod(i, GROUP // CHUNK_D)
        ls = slice(j * CHUNK_D, (j + 1) * CHUNK_D)
        for h in range(N_HEADS):
            qb_h, qk_blocks, kend_h, btot_h = heads[h]
            qk = qk_blocks[i]
            s_old = s_ref[h]
            from_s = _dot(jnp.concatenate([sol[h][gi][ls, HEAD_DIM:], qb_h[rs]], axis=0), s_old)
            u = sol[h][gi][ls, :HEAD_DIM] - from_s[:CHUNK_D]
            from_u = _dot(jnp.concatenate([qk, kend_h[rs].T], axis=0), u)
            o = from_s[CHUNK_D:] + from_u[:CHUNK_D]
            s_ref[h] = btot_h[i * CHUNK_D:i * CHUNK_D + 1] * s_old + from_u[CHUNK_D:]
            gate = z_gate[rs, h * HEAD_DIM:(h + 1) * HEAD_DIM]
            ob_ref[rs, A_WIDTH + h * HEAD_DIM:A_WIDTH + (h + 1) * HEAD_DIM] = (
                _rms(o, bog_ref[...]) * _silu(gate)).astype(BF16)

    xo_ref[...] = x + jnp.dot(ob_ref[...], wout_ref[...], preferred_element_type=F32)


def _mixer_weight_specs(layer):
    return [_layer_spec((1, D_MODEL), layer), _layer_spec((D_MODEL, Z_FRONT), 0), _layer_spec((D_MODEL, Z_TAIL), 0),
            _layer_spec((1, HEAD_DIM), layer), _layer_spec((1, HEAD_DIM), layer), _layer_spec((1, HEAD_DIM), layer)]


def _mixer_weights(w):
    return [w["n_mix"], w["w_front"], w["w_tail"], w["a_v_gain"], w["a_out_gain"], w["b_out_gain"]]


def _mix_prompt(x, layer, w):
    bsz, length, _ = x.shape
    tl = MIX_ROWS
    row_spec = pl.BlockSpec((None, tl, D_MODEL), lambda b, t: (b, t, 0))
    in_specs = [row_spec] + _mixer_weight_specs(layer) + [
        _layer_spec((N_HEADS, CHUNK_A, CHUNK_A), layer), _layer_spec((CHUNK_A, N_HEADS), layer),
        _layer_spec((CONV_W, 3 * B_WIDTH), layer), _layer_spec((2, LANES), layer), _layer_spec((D_MODEL, D_MODEL), 0)]
    out_specs = [row_spec,
                 pl.BlockSpec((None, N_HEADS, HEAD_DIM, HEAD_DIM), lambda b, t: (b, 0, 0, 0)),
                 pl.BlockSpec((None, CONV_W - 1, 3 * B_WIDTH), lambda b, t: (b, 0, 0))]
    out_shape = [jax.ShapeDtypeStruct(x.shape, F32),
                 jax.ShapeDtypeStruct((bsz, N_HEADS, HEAD_DIM, HEAD_DIM), F32),
                 jax.ShapeDtypeStruct((bsz, CONV_W - 1, 3 * B_WIDTH), F32)]
    return pl.pallas_call(
        _mix_prompt_kernel,
        grid=(bsz, length // tl),
        in_specs=in_specs,
        out_specs=out_specs,
        out_shape=out_shape,
        scratch_shapes=[pltpu.VMEM((8, 3 * B_WIDTH), F32), pltpu.VMEM((tl, D_MODEL), BF16),
                        pltpu.VMEM((GROUP, GROUP), BF16), pltpu.VMEM((D_MODEL, B_WIDTH), BF16)],
        compiler_params=pltpu.CompilerParams(dimension_semantics=("arbitrary", "arbitrary"),
                                             vmem_limit_bytes=VMEM_LIMIT_BYTES),
    )(x, *_mixer_weights(w), w["a_w_s"], w["a_b_s_t"], w["b_conv"], w["ab_par"], w["w_out"])


def _mix_sample_kernel(x_ref, s_ref, cpad_ref, s_all_ref, nmix_ref, wmain_ref, wtail_ref, avg_ref, aog_ref, bog_ref,
                       coef_ref, bias_ref, cw_ref, abp_ref, wout_ref,
                       xo_ref, so_ref, zq_ref, vo_ref,
                       z_ref, ztail_ref, ob_ref, *, n_tok):
    del s_all_ref
    rows = zq_ref.shape[0]
    nb = rows // n_tok
    step = pl.program_id(0)

    @pl.when(step == 0)
    def _():
        xn = _rms(x_ref[...], nmix_ref[...]).astype(BF16)
        z_ref[...] = jnp.dot(xn, wmain_ref[...], preferred_element_type=F32)
        ztail_ref[...] = jnp.dot(xn, wtail_ref[...], preferred_element_type=F32)

    here = pl.ds(pl.multiple_of(step * rows, rows), rows)
    z = z_ref[here, :]
    z_tail = ztail_ref[here, :]
    zab = z_tail[:, :LANES]
    tok = lax.broadcasted_iota(jnp.int32, (rows, 1), 0) % n_tok

    def prev(a, d):
        return pltpu.roll(a, d, axis=0)

    def prev_or_zero(a, d):
        return a if d == 0 else jnp.where(tok >= d, prev(a, d), 0.0)

    def per_head(fn, a):
        return jnp.concatenate([fn(a[:, h * HEAD_DIM:(h + 1) * HEAD_DIM]) for h in range(N_HEADS)], axis=1)

    uv = jax.nn.gelu(z[:, :2 * A_WIDTH])
    vn = per_head(lambda a: _rms(a, avg_ref[...]), uv[:, A_WIDTH:])
    vo_ref[...] = vn
    mixed = bias_ref[...]
    for d in range(n_tok):
        mixed = mixed + coef_ref[d] * prev_or_zero(vn, d)
    ob_ref[here, :A_WIDTH] = per_head(lambda a: _rms(a, aog_ref[...]), uv[:, :A_WIDTH] * mixed)

    zq = z[:, OFF_QKV:OFF_GATE]
    zq_ref[...] = zq
    cpad = cpad_ref[...]
    cw = cw_ref[...]
    y = zq * cw[CONV_W - 1:CONV_W]
    for d in range(1, CONV_W):
        carried = pltpu.roll(cpad, rows - (n_tok - d), axis=0)
        y = y + jnp.where(tok >= d, prev(zq, d), carried) * cw[CONV_W - 1 - d:CONV_W - d]
    qkv = _silu(y)

    abp = abp_ref[...]
    g_all = -jnp.exp(abp[0:1]) * _softplus(zab + abp[1:2])
    beta_all = jax.nn.sigmoid(zab)

    sub = lax.broadcasted_iota(jnp.int32, (8, 1), 0)
    first_half = sub < n_tok
    o_heads, kend_heads, u_heads, btot_heads = [], [], [], []
    for h in range(N_HEADS):
        q = _l2(qkv[:, h * HEAD_DIM:(h + 1) * HEAD_DIM]) * (HEAD_DIM ** -0.5)
        k = _l2(qkv[:, B_WIDTH + h * HEAD_DIM:B_WIDTH + (h + 1) * HEAD_DIM])
        v = qkv[:, 2 * B_WIDTH + h * HEAD_DIM:2 * B_WIDTH + (h + 1) * HEAD_DIM]
        g = jnp.broadcast_to(g_all[:, h:h + 1], (rows, HEAD_DIM))
        beta = jnp.broadcast_to(beta_all[:, N_HEADS + h:N_HEADS + h + 1], (rows, HEAD_DIM))
        gam = g
        for d in range(1, n_tok):
            gam = gam + prev_or_zero(g, d)
        gam_last = jnp.where(tok == n_tok - 1, gam, 0.0)
        for d in range(1, n_tok):
            gam_last = gam_last + jnp.where(tok == n_tok - 1 - d, pltpu.roll(gam, rows - d, axis=0), 0.0)
        eg = jnp.exp(gam)

        def decay_to(d, gam=gam):
            return jnp.exp(jnp.where(tok >= d, gam - prev(gam, d), 0.0))

        a_sub = [None] + [jnp.where(tok >= d, beta * jnp.sum(k * prev(k, d), axis=-1, keepdims=True) * decay_to(d),
                                    0.0) for d in range(1, n_tok)]
        def forward_substitute(rhs, a_sub=a_sub):
            sol = rhs
            for t in range(1, n_tok):
                acc = rhs
                for d in range(1, t + 1):
                    acc = acc - a_sub[d] * prev(sol, d)
                sol = jnp.where(tok == t, acc, sol)
            return sol

        w_blk = forward_substitute(beta * v)
        kb_blk = forward_substitute((beta * eg) * k)
        qb = q * eg

        kb_s, qb_s = [], []
        for p in range(rows // 8):
            kb_t, qb_t = kb_blk[8 * p:8 * p + 8], qb[8 * p:8 * p + 8]
            f0 = _dot(jnp.where(first_half, kb_t, pltpu.roll(qb_t, n_tok, axis=0)), s_ref[2 * p, h])
            f1 = _dot(jnp.where(first_half, pltpu.roll(kb_t, n_tok, axis=0), qb_t), s_ref[2 * p + 1, h])
            kb_s.append(jnp.where(first_half, f0, pltpu.roll(f1, n_tok, axis=0)))
            qb_s.append(jnp.where(first_half, pltpu.roll(f0, n_tok, axis=0), f1))
        u = w_blk - jnp.concatenate(kb_s, axis=0)
        o = jnp.concatenate(qb_s, axis=0)
        for d in range(n_tok):
            qk = jnp.where(tok >= d, jnp.sum(q * prev(k, d), axis=-1, keepdims=True) * decay_to(d), 0.0)
            o = o + qk * prev_or_zero(u, d)
        o_heads.append(o)
        kend_heads.append(k * jnp.exp(gam_last - gam))
        u_heads.append(u)
        btot_heads.append(jnp.broadcast_to(jnp.exp(gam_last), (rows, HEAD_DIM)))

    kend_t = jnp.concatenate(kend_heads, axis=0).T
    u_all = jnp.concatenate(u_heads, axis=0).astype(BF16)
    owner = lax.broadcasted_iota(jnp.int32, (1, N_HEADS * rows), 1) // n_tok
    for h in range(N_HEADS):
        for b in range(nb):
            mine = jnp.where(owner == h * nb + b, kend_t, 0.0).astype(BF16)
            last = b * n_tok + n_tok - 1
            so_ref[b, h] = btot_heads[h][last:last + 1] * s_ref[b, h] + jnp.dot(
                mine, u_all, preferred_element_type=F32)

    gate = z_tail[:, 2 * N_HEADS:]
    ob_ref[here, A_WIDTH:] = per_head(lambda a: _rms(a, bog_ref[...]), jnp.concatenate(o_heads, axis=1)) * _silu(gate)

    @pl.when(step == pl.num_programs(0) - 1)
    def _():
        xo_ref[...] = x_ref[...] + _dot(ob_ref[...], wout_ref[...])


def _mix_sample(x, layer, state_s, cpad, s_all, w, n_tok):
    rows_total = x.shape[0]
    nb = SAMPLE_GROUP
    rows = nb * n_tok
    all_rows = lambda width: pl.BlockSpec((rows_total, width), lambda i: (0, 0))
    row_spec = lambda width: pl.BlockSpec((rows, width), lambda i: (i, 0))
    s_spec = pl.BlockSpec((None, nb, N_HEADS, HEAD_DIM, HEAD_DIM), lambda i: (layer, i, 0, 0, 0))
    in_specs = [all_rows(D_MODEL), s_spec, pl.BlockSpec((None, rows, 3 * B_WIDTH), lambda i: (layer, i, 0)),
                pl.BlockSpec(memory_space=pl.ANY)] + _mixer_weight_specs(layer) + [
        _layer_spec((n_tok, rows, A_WIDTH), layer), _layer_spec((rows, A_WIDTH), layer),
        _layer_spec((CONV_W, 3 * B_WIDTH), layer), _layer_spec((2, LANES), layer), _layer_spec((D_MODEL, D_MODEL), 0)]
    out_specs = [all_rows(D_MODEL), s_spec, row_spec(3 * B_WIDTH), row_spec(A_WIDTH)]
    out_shape = [jax.ShapeDtypeStruct(x.shape, F32), jax.ShapeDtypeStruct(state_s.shape, F32),
                 jax.ShapeDtypeStruct((rows_total, 3 * B_WIDTH), F32), jax.ShapeDtypeStruct((rows_total, A_WIDTH), F32)]
    return pl.pallas_call(
        functools.partial(_mix_sample_kernel, n_tok=n_tok),
        grid=(rows_total // rows,),
        in_specs=in_specs,
        out_specs=out_specs,
        out_shape=out_shape,
        input_output_aliases={3: 1},
        scratch_shapes=[pltpu.VMEM((rows_total, Z_FRONT), F32), pltpu.VMEM((rows_total, Z_TAIL), F32),
                        pltpu.VMEM((rows_total, D_MODEL), F32)],
        compiler_params=pltpu.CompilerParams(dimension_semantics=("arbitrary",),
                                             vmem_limit_bytes=VMEM_LIMIT_BYTES),
    )(x, state_s, cpad, s_all, *_mixer_weights(w), w["a_coef"], w["a_bias"], w["b_conv"], w["ab_par"], w["w_out"])


def _prep_tables(n_tok, norm_ffn1, norm_mix, a_v_gain, a_spatial_w, a_spatial_b, a_out_gain, b_conv_w, b_a_log,
                 b_dt_bias, b_out_gain, norm_ffn2, norm_ple):
    par_pad = jnp.zeros((DEPTH, LANES - N_HEADS), F32)
    ab_par = jnp.stack([jnp.concatenate([b_a_log, par_pad], axis=1),
                        jnp.concatenate([b_dt_bias, par_pad], axis=1)], axis=1)

    def sample_rows(a):
        return jnp.tile(jnp.repeat(jnp.transpose(a, (0, 2, 1)), HEAD_DIM, axis=2), (1, SAMPLE_GROUP, 1))

    ws_small = a_spatial_w[:, :, :n_tok, :n_tok]
    a_coef = jnp.stack([sample_rows(jnp.pad(jnp.diagonal(ws_small, offset=-d, axis1=2, axis2=3),
                                            ((0, 0), (0, 0), (d, 0)))) for d in range(n_tok)], axis=1)
    return dict(
        n_f1=norm_ffn1[:, None], n_mix=norm_mix[:, None], ab_par=ab_par,
        a_v_gain=a_v_gain[:, None], a_out_gain=a_out_gain[:, None], b_out_gain=b_out_gain[:, None],
        a_w_s=a_spatial_w, a_b_s_t=jnp.transpose(a_spatial_b, (0, 2, 1)),
        a_coef=a_coef, a_bias=sample_rows(a_spatial_b[:, :, :n_tok]),
        b_conv=b_conv_w, n_f2=norm_ffn2[:, None], n_ple=norm_ple[:, None],
    )


def kernel(x_prompt, x_sample, state_S, state_conv, p_prompt, p_sample, norm_ffn1, w_ffn1_in, w_ffn1_out, norm_mix, w_in, a_v_gain, a_spatial_w, a_spatial_b, a_out_gain, b_conv_w, b_a_log, b_dt_bias, b_out_gain, w_out, norm_ffn2, w_ffn2_in, w_ffn2_out, norm_ple, w_ple_gate, w_ple_proj, final_norm):
    bsz, length, _ = x_prompt.shape
    dec_bsz, n_tok, _ = x_sample.shape
    assert length % MIX_ROWS == 0 and MIX_ROWS % CHUNK_A == 0 and MIX_ROWS % GROUP == 0
    assert dec_bsz % SAMPLE_GROUP == 0
    assert n_tok % CHUNK_A != 0 and n_tok % CHUNK_D != 0
    assert 2 * n_tok == 8 and N_HEADS * SAMPLE_GROUP * n_tok == HEAD_DIM and n_tok >= CONV_W - 1

    w = _prep_tables(n_tok, norm_ffn1, norm_mix, a_v_gain, a_spatial_w, a_spatial_b, a_out_gain, b_conv_w, b_a_log,
                     b_dt_bias, b_out_gain, norm_ffn2, norm_ple)
    ffn1_w = (w_ffn1_in[0:1].astype(BF16), w_ffn1_out[0:1].astype(BF16))
    in_proj_splits = ((0, Z_FRONT), (Z_FRONT, Z_FRONT + Z_TAIL))
    final = final_norm[None, None]
    xp = x_prompt.reshape(bsz * length, D_MODEL)
    xs = x_sample.reshape(dec_bsz * n_tok, D_MODEL)
    pp = p_prompt.reshape(DEPTH, bsz * length, PLE_DIM)
    ps = p_sample.reshape(DEPTH, dec_bsz * n_tok, PLE_DIM)
    keep = CONV_W - 1
    cpad = jnp.pad(state_conv, ((0, 0), (0, 0), (n_tok - keep, 0), (0, 0))).reshape(DEPTH, dec_bsz * n_tok, 3 * B_WIDTH)

    s_prompt, c_prompt, c_sample, v_sample = [], [], [], []
    s_sample = jnp.zeros(state_S.shape, F32)
    for i in range(DEPTH):
        last = dict(final_gain=final) if i == DEPTH - 1 else {}

        xp, (f2_in, f2_out, w_front, w_tail, w_o, ple_gate, ple_proj) = _ffn(
            xp, i, w["n_f1"], *ffn1_w,
            casts=[(w_ffn2_in, i, None), (w_ffn2_out, i, None), (w_in, i, in_proj_splits), (w_out, i, None),
                   (w_ple_gate, i, None), (w_ple_proj, i, None)])
        xs = _ffn(xs, i, w["n_f1"], *ffn1_w)
        mix_w = dict(w, w_front=w_front, w_tail=w_tail, w_out=w_o)
        xp, sp, cp = _mix_prompt(xp.reshape(bsz, length, D_MODEL), i, mix_w)
        xs, s_sample, zq, vs = _mix_sample(xs, i, state_S, cpad, s_sample, mix_w, n_tok)
        xp = xp.reshape(bsz * length, D_MODEL)
        ple = (w["n_ple"], ple_gate, ple_proj)
        if i < DEPTH - 1:
            xp, ffn1_w = _ffn(xp, i, w["n_f2"], f2_in, f2_out, ple=(pp,) + ple,
                              casts=[(w_ffn1_in, i + 1, None), (w_ffn1_out, i + 1, None)])
        else:
            xp = _ffn(xp, i, w["n_f2"], f2_in, f2_out, ple=(pp,) + ple, **last)
        xs = _ffn(xs, i, w["n_f2"], f2_in, f2_out, ple=(ps,) + ple, **last)

        s_prompt.append(sp)
        c_prompt.append(cp)
        c_sample.append(zq.reshape(dec_bsz, n_tok, 3 * B_WIDTH)[:, n_tok - keep:])
        v_sample.append(vs.reshape(dec_bsz, n_tok, N_HEADS, HEAD_DIM))

    return (xp.reshape(bsz, length, D_MODEL), xs.reshape(dec_bsz, n_tok, D_MODEL), jnp.stack(s_prompt),
            jnp.stack(c_prompt), s_sample, jnp.stack(c_sample), jnp.stack(v_sample))
```

```python
import functools

import jax
import jax.numpy as jnp
from jax import lax
from jax.experimental import pallas as pl
from jax.experimental.pallas import tpu as pltpu

F32 = jnp.float32
BF16 = jnp.bfloat16
EPS = 1e-6

D_MODEL = 1024
D_FF = 2816
DEPTH = 4
N_HEADS = 4
HEAD_DIM = 128
A_WIDTH = N_HEADS * HEAD_DIM
B_WIDTH = N_HEADS * HEAD_DIM
CHUNK_A = 128
CHUNK_D = 128
CONV_W = 4
PLE_DIM = 256
Z_FRONT = 2 * A_WIDTH + 3 * B_WIDTH
Z_TAIL = 2 * N_HEADS + B_WIDTH
OFF_QKV = 2 * A_WIDTH
OFF_GATE = OFF_QKV + 3 * B_WIDTH

VMEM_LIMIT_BYTES = 52 * 1024 * 1024
LANES = 128
BF16_SUBLANES = 16
MXU_N = 256
FFN_ROWS = 512
GROUP = CHUNK_D
MIX_ROWS = 512
SAMPLE_GROUP = 8


def _rms(x, gain):
    return x * lax.rsqrt(jnp.mean(x * x, axis=-1, keepdims=True) + EPS) * gain


def _l2(x):
    return x * lax.rsqrt(jnp.sum(x * x, axis=-1, keepdims=True) + EPS)


def _silu(x):
    return x * jax.nn.sigmoid(x)


def _softplus(x):
    return jnp.maximum(x, 0.0) + jnp.log1p(jnp.exp(-jnp.abs(x)))


def _dot(a, b):
    return jnp.dot(a.astype(BF16), b.astype(BF16), preferred_element_type=F32)


def _dot_nt(a, b):
    return lax.dot_general(a.astype(BF16), b.astype(BF16), (((1,), (1,)), ((), ())),
                           preferred_element_type=F32)


def _split3(a):
    p1 = a.astype(BF16)
    r1 = a - p1.astype(F32)
    p2 = r1.astype(BF16)
    p3 = (r1 - p2.astype(F32)).astype(BF16)
    return p1, p2, p3


def _layer_spec(shape, layer, block=None):
    index = (layer,) + (0,) * (len(shape) - 1) + (0 if block is None else block,)
    return pl.BlockSpec((None,) + tuple(shape), lambda *_: index, pipeline_mode=pl.Buffered(1))


def _ffn_kernel(*refs, with_ple, with_final, cast_plan):
    x_ref, gain_ref, wg_ref, wu_ref, wo_ref = refs[:5]
    n_in = 5 + 4 * with_ple + with_final
    cast_in = refs[n_in:n_in + len(cast_plan)]
    o_ref = refs[n_in + len(cast_plan)]
    cast_out = iter(refs[n_in + len(cast_plan) + 1:])
    for src_ref, splits in zip(cast_in, cast_plan):
        for lo, hi in splits:
            next(cast_out)[...] = src_ref[:, lo:hi].astype(BF16)
    x = x_ref[...]
    xn = _rms(x, gain_ref[...]).astype(BF16)
    acc = jnp.zeros_like(x)
    for c in range(D_FF // MXU_N):
        sl = slice(c * MXU_N, (c + 1) * MXU_N)
        gate = jnp.dot(xn, wg_ref[:, sl], preferred_element_type=F32)
        up = jnp.dot(xn, wu_ref[:, sl], preferred_element_type=F32)
        h = (_silu(gate) * up).astype(BF16)
        acc = acc + jnp.dot(h, wo_ref[sl, :], preferred_element_type=F32)
    x = x + 0.5 * acc
    if with_ple:
        p_ref, npl_ref, wpg_ref, wpp_ref = refs[5:9]
        emb = _dot(p_ref[...], wpp_ref[...])
        gate = _dot(_rms(x, npl_ref[...]), wpg_ref[...])
        x = x + emb * jax.nn.sigmoid(gate)
    if with_final:
        x = _rms(x, refs[9][...])
    o_ref[...] = x


def _cast_row_blocks(n_rows, n_steps):
    blocks = n_steps
    while n_rows % blocks or (n_rows // blocks) % BF16_SUBLANES:
        assert blocks % 2 == 0, (n_rows, n_steps)
        blocks //= 2
    return blocks


def _ffn(x, layer, gain, w_in, w_out, ple=None, final_gain=None, casts=()):
    rows = x.shape[0]
    tm = min(FFN_ROWS, rows)
    n_steps = rows // tm
    row_spec = pl.BlockSpec((tm, D_MODEL), lambda i: (i, 0))
    in_specs = [row_spec, _layer_spec((1, D_MODEL), layer),
                _layer_spec((D_MODEL, D_FF), 0, block=0), _layer_spec((D_MODEL, D_FF), 0, block=1),
                _layer_spec((D_FF, D_MODEL), 0)]
    args = [x, gain, w_in, w_in, w_out]
    out_specs = [row_spec]
    out_shape = [jax.ShapeDtypeStruct(x.shape, F32)]
    if ple is not None:
        p, n_ple, w_gate, w_proj = ple
        in_specs += [pl.BlockSpec((None, tm, PLE_DIM), lambda i: (layer, i, 0)), _layer_spec((1, D_MODEL), layer),
                     _layer_spec((D_MODEL, D_MODEL), 0), _layer_spec((PLE_DIM, D_MODEL), 0)]
        args += [p, n_ple, w_gate, w_proj]
    if final_gain is not None:
        in_specs.append(_layer_spec((1, D_MODEL), 0))
        args.append(final_gain)
    cast_plan = []
    for src, layer_c, splits in casts:
        _, n_rows, n_cols = src.shape
        splits = tuple(splits) if splits is not None else ((0, n_cols),)
        blocks = _cast_row_blocks(n_rows, n_steps)
        every = n_steps // blocks
        in_specs.append(pl.BlockSpec((None, n_rows // blocks, n_cols),
                                     lambda i, layer_c=layer_c, every=every: (layer_c, i // every, 0)))
        args.append(src)
        for lo, hi in splits:
            out_specs.append(pl.BlockSpec((None, n_rows // blocks, hi - lo),
                                          lambda i, every=every: (0, i // every, 0)))
            out_shape.append(jax.ShapeDtypeStruct((1, n_rows, hi - lo), BF16))
        cast_plan.append(splits)
    out = pl.pallas_call(
        functools.partial(_ffn_kernel, with_ple=ple is not None, with_final=final_gain is not None,
                          cast_plan=tuple(cast_plan)),
        grid=(n_steps,),
        in_specs=in_specs,
        out_specs=out_specs,
        out_shape=out_shape,
        compiler_params=pltpu.CompilerParams(dimension_semantics=("arbitrary",),
                                             vmem_limit_bytes=VMEM_LIMIT_BYTES),
    )(*args)
    return (out[0], out[1:]) if casts else out[0]


def _split(a):
    hi = a.astype(BF16)
    lo = (a - hi.astype(F32)).astype(BF16)
    return hi, lo


def _dot3(a_hi, a_lo, b_hi, b_lo):
    m = a_hi.shape[0]
    both = jnp.dot(jnp.concatenate([a_hi, a_lo], axis=0), b_hi, preferred_element_type=F32)
    return both[:m] + both[m:] + jnp.dot(a_hi, b_lo, preferred_element_type=F32)


def _expand(packed, diag_ones_ref):
    n = packed.shape[1] // CHUNK_D
    return jnp.concatenate([packed] * n, axis=0) * diag_ones_ref[...]


def _unit_lower_inverses_packed(l_packed_list, diag_ones_ref):
    c, width = l_packed_list[0].shape
    row = lax.broadcasted_iota(jnp.int32, (c, width), 0)
    col = lax.broadcasted_iota(jnp.int32, (c, width), 1) % c
    zero = jnp.zeros((), BF16)

    def lower_left(bs):
        return (row // (2 * bs) == col // (2 * bs)) & ((row // bs) % 2 == 1) & ((col // bs) % 2 == 0)

    l_bf = [l.astype(BF16) for l in l_packed_list]
    xs = [jnp.where(row == col, 1.0, 0.0) - jnp.where(lower_left(1), l, 0.0) for l in l_packed_list]
    bs = 2
    while bs < c:
        sel = lower_left(bs)
        x_bf = [x.astype(BF16) for x in xs]
        ys = [jnp.dot(jnp.where(sel, l, zero), _expand(x, diag_ones_ref), preferred_element_type=F32)
              for l, x in zip(l_bf, x_bf)]
        xs = [x - jnp.dot(xb, _expand(y.astype(BF16), diag_ones_ref), preferred_element_type=F32)
              for x, xb, y in zip(xs, x_bf, ys)]
        bs *= 2
    return xs


def _mix_prompt_kernel(x_ref, nmix_ref, wmain_ref, wtail_ref, avg_ref, aog_ref, bog_ref, wsp_ref, bsp_ref,
                       cw_ref, abp_ref, wout_ref,
                       xo_ref, s_ref, ct_ref,
                       zlast_ref, ob_ref, bd_ref, wgate_ref):
    tl = x_ref.shape[0]
    step = pl.program_id(1)

    @pl.when((pl.program_id(0) == 0) & (step == 0))
    def _():
        wgate_ref[...] = wtail_ref[:, 2 * N_HEADS:]

    @pl.when(step == 0)
    def _():
        s_ref[...] = jnp.zeros_like(s_ref)
        zlast_ref[...] = jnp.zeros_like(zlast_ref)

    x = x_ref[...]
    xn = _rms(x, nmix_ref[...]).astype(BF16)
    def in_proj(lo, hi):
        return jnp.dot(xn, wmain_ref[:, lo:hi], preferred_element_type=F32)

    zab = jnp.dot(xn, wtail_ref[:, :LANES], preferred_element_type=F32)
    first_tile = lax.broadcasted_iota(jnp.int32, (8, 1), 0)
    qkv_parts = []
    for part in range(3):
        cols = slice(part * B_WIDTH, (part + 1) * B_WIDTH)
        zc = in_proj(OFF_QKV + part * B_WIDTH, OFF_QKV + (part + 1) * B_WIDTH)
        cw = cw_ref[:, cols]
        carried = zlast_ref[:, cols]
        y = zc * cw[CONV_W - 1:CONV_W]
        for d in range(1, CONV_W):
            rolled = pltpu.roll(zc, d, axis=0)
            top = jnp.where(first_tile < d, pltpu.roll(carried, d, axis=0), rolled[0:8])
            y = y + jnp.concatenate([top, rolled[8:]], axis=0) * cw[CONV_W - 1 - d:CONV_W - d]
        zlast_ref[:, cols] = zc[tl - 8:tl]
        ct_ref[:, cols] = zc[tl - (CONV_W - 1):tl]
        qkv_parts.append(_silu(y))

    uv = jax.nn.gelu(in_proj(0, OFF_QKV))
    row = lax.broadcasted_iota(jnp.int32, (CHUNK_A, CHUNK_A), 0)
    col = lax.broadcasted_iota(jnp.int32, (CHUNK_A, CHUNK_A), 1)
    causal = col <= row
    for h in range(N_HEADS):
        hs = slice(h * HEAD_DIM, (h + 1) * HEAD_DIM)
        u_h = uv[:, hs]
        v_h = _rms(uv[:, A_WIDTH + h * HEAD_DIM:A_WIDTH + (h + 1) * HEAD_DIM], avg_ref[...]).astype(BF16)
        w_h = jnp.where(causal, wsp_ref[h], 0.0).astype(BF16)
        bias_h = bsp_ref[:, h:h + 1]
        for c in range(tl // CHUNK_A):
            rs = slice(c * CHUNK_A, (c + 1) * CHUNK_A)
            mixed = jnp.dot(w_h, v_h[rs], preferred_element_type=F32) + bias_h
            ob_ref[rs, hs] = _rms(u_h[rs] * mixed, aog_ref[...]).astype(BF16)

    z_gate = jnp.dot(xn, wgate_ref[...], preferred_element_type=F32)
    abp = abp_ref[...]
    g = -jnp.exp(abp[0:1]) * _softplus(zab + abp[1:2])
    beta = jax.nn.sigmoid(zab)

    r2 = lax.broadcasted_iota(jnp.int32, (GROUP, GROUP), 0)
    c2 = lax.broadcasted_iota(jnp.int32, (GROUP, GROUP), 1)
    same = (r2 // CHUNK_D) == (c2 // CHUNK_D)
    col_ones = jnp.concatenate([jnp.where(same & (c2 <= r2), 1.0, 0.0), jnp.where(same, 1.0, 0.0)],
                               axis=0).astype(BF16)
    upper_ones = jnp.where(same & (r2 <= c2), 1.0, 0.0).astype(BF16)
    bd_ref[...] = jnp.where(same, 1.0, 0.0).astype(BF16)
    n_grp = tl // GROUP
    gam_parts, glast_parts, gam_t = [], [], []
    for gi in range(n_grp):
        g_grp = g[gi * GROUP:(gi + 1) * GROUP]
        by_col = jnp.dot(col_ones, jnp.concatenate(_split3(g_grp), axis=1), preferred_element_type=F32)
        by_col = by_col[:, :LANES] + by_col[:, LANES:2 * LANES] + by_col[:, 2 * LANES:]
        gam_parts.append(by_col[:GROUP])
        glast_parts.append(by_col[GROUP:])
        by_row = jnp.dot(jnp.concatenate(_split3(g_grp.T), axis=0), upper_ones, preferred_element_type=F32)
        gam_t.append(by_row[:LANES] + by_row[LANES:2 * LANES] + by_row[2 * LANES:])
    gam = jnp.concatenate(gam_parts, axis=0)
    glast = jnp.concatenate(glast_parts, axis=0)

    rb = lax.broadcasted_iota(jnp.int32, (CHUNK_D, CHUNK_D), 0)
    cb = lax.broadcasted_iota(jnp.int32, (CHUNK_D, CHUNK_D), 1)
    strict = cb < rb
    eye = jnp.where(cb == rb, 1.0, 0.0)

    heads, a_packed, rhs = [], [], []
    for h in range(N_HEADS):
        q_h = _l2(qkv_parts[0][:, h * HEAD_DIM:(h + 1) * HEAD_DIM]) * (HEAD_DIM ** -0.5)
        k_h = _l2(qkv_parts[1][:, h * HEAD_DIM:(h + 1) * HEAD_DIM])
        v_h = qkv_parts[2][:, h * HEAD_DIM:(h + 1) * HEAD_DIM]
        gc_h = gam[:, h:h + 1]
        gl_h = glast[:, h:h + 1]
        bc_h = beta[:, N_HEADS + h:N_HEADS + h + 1]
        eg_h = jnp.exp(gc_h)
        qk_blocks = []
        for gi in range(n_grp):
            packed = []
            for j in range(GROUP // CHUNK_D):
                rs = slice(gi * GROUP + j * CHUNK_D, gi * GROUP + (j + 1) * CHUNK_D)
                grams = _dot_nt(jnp.concatenate([k_h[rs], q_h[rs]], axis=0), k_h[rs])
                diff = gc_h[rs] - gam_t[gi][h:h + 1, j * CHUNK_D:(j + 1) * CHUNK_D]
                decay = jnp.where(strict, jnp.exp(jnp.where(strict, diff, 0.0)), 0.0)
                packed.append(bc_h[rs] * grams[:CHUNK_D] * decay)
                qk_blocks.append(grams[CHUNK_D:] * (decay + eye))
            a_packed.append(jnp.concatenate(packed, axis=1))
        rhs.append(_split(jnp.concatenate([bc_h * v_h, (bc_h * eg_h) * k_h], axis=1)))
        heads.append((q_h * eg_h, qk_blocks, k_h * jnp.exp(gl_h - gc_h), jnp.exp(gl_h)))
    inv_split = [_split(inv) for inv in _unit_lower_inverses_packed(a_packed, bd_ref)]
    sol = []
    for h in range(N_HEADS):
        sol.append([_dot3(_expand(inv_split[h * n_grp + gi][0], bd_ref), _expand(inv_split[h * n_grp + gi][1], bd_ref),
                          rhs[h][0][gi * GROUP:(gi + 1) * GROUP], rhs[h][1][gi * GROUP:(gi + 1) * GROUP])
                    for gi in range(n_grp)])

    for i in range(tl // CHUNK_D):
        rs = slice(i * CHUNK_D, (i + 1) * CHUNK_D)
        gi, j = divmod(i, GROUP // CHUNK_D)
        ls = slice(j * CHUNK_D, (j + 1) * CHUNK_D)
        for h in range(N_HEADS):
            qb_h, qk_blocks, kend_h, btot_h = heads[h]
            qk = qk_blocks[i]
            s_old = s_ref[h]
            from_s = _dot(jnp.concatenate([sol[h][gi][ls, HEAD_DIM:], qb_h[rs]], axis=0), s_old)
            u = sol[h][gi][ls, :HEAD_DIM] - from_s[:CHUNK_D]
            from_u = _dot(jnp.concatenate([qk, kend_h[rs].T], axis=0), u)
            o = from_s[CHUNK_D:] + from_u[:CHUNK_D]
            s_ref[h] = btot_h[i * CHUNK_D:i * CHUNK_D + 1] * s_old + from_u[CHUNK_D:]
            gate = z_gate[rs, h * HEAD_DIM:(h + 1) * HEAD_DIM]
            ob_ref[rs, A_WIDTH + h * HEAD_DIM:A_WIDTH + (h + 1) * HEAD_DIM] = (
                _rms(o, bog_ref[...]) * _silu(gate)).astype(BF16)

    xo_ref[...] = x + jnp.dot(ob_ref[...], wout_ref[...], preferred_element_type=F32)


def _mixer_weight_specs(layer):
    return [_layer_spec((1, D_MODEL), layer), _layer_spec((D_MODEL, Z_FRONT), 0), _layer_spec((D_MODEL, Z_TAIL), 0),
            _layer_spec((1, HEAD_DIM), layer), _layer_spec((1, HEAD_DIM), layer), _layer_spec((1, HEAD_DIM), layer)]


def _mixer_weights(w):
    return [w["n_mix"], w["w_front"], w["w_tail"], w["a_v_gain"], w["a_out_gain"], w["b_out_gain"]]


def _mix_prompt(x, layer, w):
    bsz, length, _ = x.shape
    tl = MIX_ROWS
    row_spec = pl.BlockSpec((None, tl, D_MODEL), lambda b, t: (b, t, 0))
    in_specs = [row_spec] + _mixer_weight_specs(layer) + [
        _layer_spec((N_HEADS, CHUNK_A, CHUNK_A), layer), _layer_spec((CHUNK_A, N_HEADS), layer),
        _layer_spec((CONV_W, 3 * B_WIDTH), layer), _layer_spec((2, LANES), layer), _layer_spec((D_MODEL, D_MODEL), 0)]
    out_specs = [row_spec,
                 pl.BlockSpec((None, N_HEADS, HEAD_DIM, HEAD_DIM), lambda b, t: (b, 0, 0, 0)),
                 pl.BlockSpec((None, CONV_W - 1, 3 * B_WIDTH), lambda b, t: (b, 0, 0))]
    out_shape = [jax.ShapeDtypeStruct(x.shape, F32),
                 jax.ShapeDtypeStruct((bsz, N_HEADS, HEAD_DIM, HEAD_DIM), F32),
                 jax.ShapeDtypeStruct((bsz, CONV_W - 1, 3 * B_WIDTH), F32)]
    return pl.pallas_call(
        _mix_prompt_kernel,
        grid=(bsz, length // tl),
        in_specs=in_specs,
        out_specs=out_specs,
        out_shape=out_shape,
        scratch_shapes=[pltpu.VMEM((8, 3 * B_WIDTH), F32), pltpu.VMEM((tl, D_MODEL), BF16),
                        pltpu.VMEM((GROUP, GROUP), BF16), pltpu.VMEM((D_MODEL, B_WIDTH), BF16)],
        compiler_params=pltpu.CompilerParams(dimension_semantics=("arbitrary", "arbitrary"),
                                             vmem_limit_bytes=VMEM_LIMIT_BYTES),
    )(x, *_mixer_weights(w), w["a_w_s"], w["a_b_s_t"], w["b_conv"], w["ab_par"], w["w_out"])


def _mix_sample_kernel(x_ref, s_ref, cpad_ref, s_all_ref, nmix_ref, wmain_ref, wtail_ref, avg_ref, aog_ref, bog_ref,
                       coef_ref, bias_ref, cw_ref, abp_ref, wout_ref,
                       xo_ref, so_ref, zq_ref, vo_ref,
                       z_ref, ztail_ref, ob_ref, *, n_tok):
    del s_all_ref
    rows = zq_ref.shape[0]
    nb = rows // n_tok
    step = pl.program_id(0)

    @pl.when(step == 0)
    def _():
        xn = _rms(x_ref[...], nmix_ref[...]).astype(BF16)
        z_ref[...] = jnp.dot(xn, wmain_ref[...], preferred_element_type=F32)
        ztail_ref[...] = jnp.dot(xn, wtail_ref[...], preferred_element_type=F32)

    here = pl.ds(pl.multiple_of(step * rows, rows), rows)
    z = z_ref[here, :]
    z_tail = ztail_ref[here, :]
    zab = z_tail[:, :LANES]
    tok = lax.broadcasted_iota(jnp.int32, (rows, 1), 0) % n_tok

    def prev(a, d):
        return pltpu.roll(a, d, axis=0)

    def prev_or_zero(a, d):
        return a if d == 0 else jnp.where(tok >= d, prev(a, d), 0.0)

    def per_head(fn, a):
        return jnp.concatenate([fn(a[:, h * HEAD_DIM:(h + 1) * HEAD_DIM]) for h in range(N_HEADS)], axis=1)

    uv = jax.nn.gelu(z[:, :2 * A_WIDTH])
    vn = per_head(lambda a: _rms(a, avg_ref[...]), uv[:, A_WIDTH:])
    vo_ref[...] = vn
    mixed = bias_ref[...]
    for d in range(n_tok):
        mixed = mixed + coef_ref[d] * prev_or_zero(vn, d)
    ob_ref[here, :A_WIDTH] = per_head(lambda a: _rms(a, aog_ref[...]), uv[:, :A_WIDTH] * mixed)

    zq = z[:, OFF_QKV:OFF_GATE]
    zq_ref[...] = zq
    cpad = cpad_ref[...]
    cw = cw_ref[...]
    y = zq * cw[CONV_W - 1:CONV_W]
    for d in range(1, CONV_W):
        carried = pltpu.roll(cpad, rows - (n_tok - d), axis=0)
        y = y + jnp.where(tok >= d, prev(zq, d), carried) * cw[CONV_W - 1 - d:CONV_W - d]
    qkv = _silu(y)

    abp = abp_ref[...]
    g_all = -jnp.exp(abp[0:1]) * _softplus(zab + abp[1:2])
    beta_all = jax.nn.sigmoid(zab)

    sub = lax.broadcasted_iota(jnp.int32, (8, 1), 0)
    first_half = sub < n_tok
    o_heads, kend_heads, u_heads, btot_heads = [], [], [], []
    for h in range(N_HEADS):
        q = _l2(qkv[:, h * HEAD_DIM:(h + 1) * HEAD_DIM]) * (HEAD_DIM ** -0.5)
        k = _l2(qkv[:, B_WIDTH + h * HEAD_DIM:B_WIDTH + (h + 1) * HEAD_DIM])
        v = qkv[:, 2 * B_WIDTH + h * HEAD_DIM:2 * B_WIDTH + (h + 1) * HEAD_DIM]
        g = jnp.broadcast_to(g_all[:, h:h + 1], (rows, HEAD_DIM))
        beta = jnp.broadcast_to(beta_all[:, N_HEADS + h:N_HEADS + h + 1], (rows, HEAD_DIM))
        gam = g
        for d in range(1, n_tok):
            gam = gam + prev_or_zero(g, d)
        gam_last = jnp.where(tok == n_tok - 1, gam, 0.0)
        for d in range(1, n_tok):
            gam_last = gam_last + jnp.where(tok == n_tok - 1 - d, pltpu.roll(gam, rows - d, axis=0), 0.0)
        eg = jnp.exp(gam)

        def decay_to(d, gam=gam):
            return jnp.exp(jnp.where(tok >= d, gam - prev(gam, d), 0.0))

        a_sub = [None] + [jnp.where(tok >= d, beta * jnp.sum(k * prev(k, d), axis=-1, keepdims=True) * decay_to(d),
                                    0.0) for d in range(1, n_tok)]
        def forward_substitute(rhs, a_sub=a_sub):
            sol = rhs
            for t in range(1, n_tok):
                acc = rhs
                for d in range(1, t + 1):
                    acc = acc - a_sub[d] * prev(sol, d)
                sol = jnp.where(tok == t, acc, sol)
            return sol

        w_blk = forward_substitute(beta * v)
        kb_blk = forward_substitute((beta * eg) * k)
        qb = q * eg

        kb_s, qb_s = [], []
        for p in range(rows // 8):
            kb_t, qb_t = kb_blk[8 * p:8 * p + 8], qb[8 * p:8 * p + 8]
            f0 = _dot(jnp.where(first_half, kb_t, pltpu.roll(qb_t, n_tok, axis=0)), s_ref[2 * p, h])
            f1 = _dot(jnp.where(first_half, pltpu.roll(kb_t, n_tok, axis=0), qb_t), s_ref[2 * p + 1, h])
            kb_s.append(jnp.where(first_half, f0, pltpu.roll(f1, n_tok, axis=0)))
            qb_s.append(jnp.where(first_half, pltpu.roll(f0, n_tok, axis=0), f1))
        u = w_blk - jnp.concatenate(kb_s, axis=0)
        o = jnp.concatenate(qb_s, axis=0)
        for d in range(n_tok):
            qk = jnp.where(tok >= d, jnp.sum(q * prev(k, d), axis=-1, keepdims=True) * decay_to(d), 0.0)
            o = o + qk * prev_or_zero(u, d)
        o_heads.append(o)
        kend_heads.append(k * jnp.exp(gam_last - gam))
        u_heads.append(u)
        btot_heads.append(jnp.broadcast_to(jnp.exp(gam_last), (rows, HEAD_DIM)))

    kend_t = jnp.concatenate(kend_heads, axis=0).T
    u_all = jnp.concatenate(u_heads, axis=0).astype(BF16)
    owner = lax.broadcasted_iota(jnp.int32, (1, N_HEADS * rows), 1) // n_tok
    for h in range(N_HEADS):
        for b in range(nb):
            mine = jnp.where(owner == h * nb + b, kend_t, 0.0).astype(BF16)
            last = b * n_tok + n_tok - 1
            so_ref[b, h] = btot_heads[h][last:last + 1] * s_ref[b, h] + jnp.dot(
                mine, u_all, preferred_element_type=F32)

    gate = z_tail[:, 2 * N_HEADS:]
    ob_ref[here, A_WIDTH:] = per_head(lambda a: _rms(a, bog_ref[...]), jnp.concatenate(o_heads, axis=1)) * _silu(gate)

    @pl.when(step == pl.num_programs(0) - 1)
    def _():
        xo_ref[...] = x_ref[...] + _dot(ob_ref[...], wout_ref[...])


def _mix_sample(x, layer, state_s, cpad, s_all, w, n_tok):
    rows_total = x.shape[0]
    nb = SAMPLE_GROUP
    rows = nb * n_tok
    all_rows = lambda width: pl.BlockSpec((rows_total, width), lambda i: (0, 0))
    row_spec = lambda width: pl.BlockSpec((rows, width), lambda i: (i, 0))
    s_spec = pl.BlockSpec((None, nb, N_HEADS, HEAD_DIM, HEAD_DIM), lambda i: (layer, i, 0, 0, 0))
    in_specs = [all_rows(D_MODEL), s_spec, pl.BlockSpec((None, rows, 3 * B_WIDTH), lambda i: (layer, i, 0)),
                pl.BlockSpec(memory_space=pl.ANY)] + _mixer_weight_specs(layer) + [
        _layer_spec((n_tok, rows, A_WIDTH), layer), _layer_spec((rows, A_WIDTH), layer),
        _layer_spec((CONV_W, 3 * B_WIDTH), layer), _layer_spec((2, LANES), layer), _layer_spec((D_MODEL, D_MODEL), 0)]
    out_specs = [all_rows(D_MODEL), s_spec, row_spec(3 * B_WIDTH), row_spec(A_WIDTH)]
    out_shape = [jax.ShapeDtypeStruct(x.shape, F32), jax.ShapeDtypeStruct(state_s.shape, F32),
                 jax.ShapeDtypeStruct((rows_total, 3 * B_WIDTH), F32), jax.ShapeDtypeStruct((rows_total, A_WIDTH), F32)]
    return pl.pallas_call(
        functools.partial(_mix_sample_kernel, n_tok=n_tok),
        grid=(rows_total // rows,),
        in_specs=in_specs,
        out_specs=out_specs,
        out_shape=out_shape,
        input_output_aliases={3: 1},
        scratch_shapes=[pltpu.VMEM((rows_total, Z_FRONT), F32), pltpu.VMEM((rows_total, Z_TAIL), F32),
                        pltpu.VMEM((rows_total, D_MODEL), F32)],
        compiler_params=pltpu.CompilerParams(dimension_semantics=("arbitrary",),
                                             vmem_limit_bytes=VMEM_LIMIT_BYTES),
    )(x, state_s, cpad, s_all, *_mixer_weights(w), w["a_coef"], w["a_bias"], w["b_conv"], w["ab_par"], w["w_out"])


def _prep_tables(n_tok, norm_ffn1, norm_mix, a_v_gain, a_spatial_w, a_spatial_b, a_out_gain, b_conv_w, b_a_log,
                 b_dt_bias, b_out_gain, norm_ffn2, norm_ple):
    par_pad = jnp.zeros((DEPTH, LANES - N_HEADS), F32)
    ab_par = jnp.stack([jnp.concatenate([b_a_log, par_pad], axis=1),
                        jnp.concatenate([b_dt_bias, par_pad], axis=1)], axis=1)

    def sample_rows(a):
        return jnp.tile(jnp.repeat(jnp.transpose(a, (0, 2, 1)), HEAD_DIM, axis=2), (1, SAMPLE_GROUP, 1))

    ws_small = a_spatial_w[:, :, :n_tok, :n_tok]
    a_coef = jnp.stack([sample_rows(jnp.pad(jnp.diagonal(ws_small, offset=-d, axis1=2, axis2=3),
                                            ((0, 0), (0, 0), (d, 0)))) for d in range(n_tok)], axis=1)
    return dict(
        n_f1=norm_ffn1[:, None], n_mix=norm_mix[:, None], ab_par=ab_par,
        a_v_gain=a_v_gain[:, None], a_out_gain=a_out_gain[:, None], b_out_gain=b_out_gain[:, None],
        a_w_s=a_spatial_w, a_b_s_t=jnp.transpose(a_spatial_b, (0, 2, 1)),
        a_coef=a_coef, a_bias=sample_rows(a_spatial_b[:, :, :n_tok]),
        b_conv=b_conv_w, n_f2=norm_ffn2[:, None], n_ple=norm_ple[:, None],
    )


def kernel(x_prompt, x_sample, state_S, state_conv, p_prompt, p_sample, norm_ffn1, w_ffn1_in, w_ffn1_out, norm_mix, w_in, a_v_gain, a_spatial_w, a_spatial_b, a_out_gain, b_conv_w, b_a_log, b_dt_bias, b_out_gain, w_out, norm_ffn2, w_ffn2_in, w_ffn2_out, norm_ple, w_ple_gate, w_ple_proj, final_norm):
    bsz, length, _ = x_prompt.shape
    dec_bsz, n_tok, _ = x_sample.shape
    assert length % MIX_ROWS == 0 and MIX_ROWS % CHUNK_A == 0 and MIX_ROWS % GROUP == 0
    assert dec_bsz % SAMPLE_GROUP == 0
    assert n_tok % CHUNK_A != 0 and n_tok % CHUNK_D != 0
    assert 2 * n_tok == 8 and N_HEADS * SAMPLE_GROUP * n_tok == HEAD_DIM and n_tok >= CONV_W - 1

    w = _prep_tables(n_tok, norm_ffn1, norm_mix, a_v_gain, a_spatial_w, a_spatial_b, a_out_gain, b_conv_w, b_a_log,
                     b_dt_bias, b_out_gain, norm_ffn2, norm_ple)
    ffn1_w = (w_ffn1_in[0:1].astype(BF16), w_ffn1_out[0:1].astype(BF16))
    in_proj_splits = ((0, Z_FRONT), (Z_FRONT, Z_FRONT + Z_TAIL))
    final = final_norm[None, None]
    xp = x_prompt.reshape(bsz * length, D_MODEL)
    xs = x_sample.reshape(dec_bsz * n_tok, D_MODEL)
    pp = p_prompt.reshape(DEPTH, bsz * length, PLE_DIM)
    ps = p_sample.reshape(DEPTH, dec_bsz * n_tok, PLE_DIM)
    keep = CONV_W - 1
    cpad = jnp.pad(state_conv, ((0, 0), (0, 0), (n_tok - keep, 0), (0, 0))).reshape(DEPTH, dec_bsz * n_tok, 3 * B_WIDTH)

    s_prompt, c_prompt, c_sample, v_sample = [], [], [], []
    s_sample = jnp.zeros(state_S.shape, F32)
    for i in range(DEPTH):
        last = dict(final_gain=final) if i == DEPTH - 1 else {}

        xp, (f2_in, f2_out, w_front, w_tail, w_o, ple_gate, ple_proj) = _ffn(
            xp, i, w["n_f1"], *ffn1_w,
            casts=[(w_ffn2_in, i, None), (w_ffn2_out, i, None), (w_in, i, in_proj_splits), (w_out, i, None),
                   (w_ple_gate, i, None), (w_ple_proj, i, None)])
        xs = _ffn(xs, i, w["n_f1"], *ffn1_w)
        mix_w = dict(w, w_front=w_front, w_tail=w_tail, w_out=w_o)
        xp, sp, cp = _mix_prompt(xp.reshape(bsz, length, D_MODEL), i, mix_w)
        xs, s_sample, zq, vs = _mix_sample(xs, i, state_S, cpad, s_sample, mix_w, n_tok)
        xp = xp.reshape(bsz * length, D_MODEL)
        ple = (w["n_ple"], ple_gate, ple_proj)
        if i < DEPTH - 1:
            xp, ffn1_w = _ffn(xp, i, w["n_f2"], f2_in, f2_out, ple=(pp,) + ple,
                              casts=[(w_ffn1_in, i + 1, None), (w_ffn1_out, i + 1, None)])
        else:
            xp = _ffn(xp, i, w["n_f2"], f2_in, f2_out, ple=(pp,) + ple, **last)
        xs = _ffn(xs, i, w["n_f2"], f2_in, f2_out, ple=(ps,) + ple, **last)

        s_prompt.append(sp)
        c_prompt.append(cp)
        c_sample.append(zq.reshape(dec_bsz, n_tok, 3 * B_WIDTH)[:, n_tok - keep:])
        v_sample.append(vs.reshape(dec_bsz, n_tok, N_HEADS, HEAD_DIM))

    return (xp.reshape(bsz, length, D_MODEL), xs.reshape(dec_bsz, n_tok, D_MODEL), jnp.stack(s_prompt),
            jnp.stack(c_prompt), s_sample, jnp.stack(c_sample), jnp.stack(v_sample))
```

```python
import functools

import jax
import jax.numpy as jnp
from jax import lax
from jax.experimental import pallas as pl
from jax.experimental.pallas import tpu as pltpu

F32 = jnp.float32
BF16 = jnp.bfloat16
EPS = 1e-6

D_MODEL = 1024
D_FF = 2816
DEPTH = 4
N_HEADS = 4
HEAD_DIM = 128
A_WIDTH = N_HEADS * HEAD_DIM
B_WIDTH = N_HEADS * HEAD_DIM
CHUNK_A = 128
CHUNK_D = 128
CONV_W = 4
PLE_DIM = 256
Z_FRONT = 2 * A_WIDTH + 3 * B_WIDTH
Z_TAIL = 2 * N_HEADS + B_WIDTH
OFF_QKV = 2 * A_WIDTH
OFF_GATE = OFF_QKV + 3 * B_WIDTH

VMEM_LIMIT_BYTES = 52 * 1024 * 1024
LANES = 128
BF16_SUBLANES = 16
MXU_N = 256
FFN_ROWS = 512
GROUP = CHUNK_D
MIX_ROWS = 512
SAMPLE_GROUP = 8


def _rms(x, gain):
    return x * lax.rsqrt(jnp.mean(x * x, axis=-1, keepdims=True) + EPS) * gain


def _l2(x):
    return x * lax.rsqrt(jnp.sum(x * x, axis=-1, keepdims=True) + EPS)


def _silu(x):
    return x * jax.nn.sigmoid(x)


def _softplus(x):
    return jnp.maximum(x, 0.0) + jnp.log1p(jnp.exp(-jnp.abs(x)))


def _dot(a, b):
    return jnp.dot(a.astype(BF16), b.astype(BF16), preferred_element_type=F32)


def _dot_nt(a, b):
    return lax.dot_general(a.astype(BF16), b.astype(BF16), (((1,), (1,)), ((), ())),
                           preferred_element_type=F32)


def _split3(a):
    p1 = a.astype(BF16)
    r1 = a - p1.astype(F32)
    p2 = r1.astype(BF16)
    p3 = (r1 - p2.astype(F32)).astype(BF16)
    return p1, p2, p3


def _layer_spec(shape, layer, block=None):
    index = (layer,) + (0,) * (len(shape) - 1) + (0 if block is None else block,)
    return pl.BlockSpec((None,) + tuple(shape), lambda *_: index, pipeline_mode=pl.Buffered(1))


def _ffn_kernel(*refs, with_ple, with_final, cast_plan):
    x_ref, gain_ref, wg_ref, wu_ref, wo_ref = refs[:5]
    n_in = 5 + 4 * with_ple + with_final
    cast_in = refs[n_in:n_in + len(cast_plan)]
    o_ref = refs[n_in + len(cast_plan)]
    cast_out = iter(refs[n_in + len(cast_plan) + 1:])
    for src_ref, splits in zip(cast_in, cast_plan):
        for lo, hi in splits:
            next(cast_out)[...] = src_ref[:, lo:hi].astype(BF16)
    x = x_ref[...]
    xn = _rms(x, gain_ref[...]).astype(BF16)
    acc = jnp.zeros_like(x)
    for c in range(D_FF // MXU_N):
        sl = slice(c * MXU_N, (c + 1) * MXU_N)
        gate = jnp.dot(xn, wg_ref[:, sl], preferred_element_type=F32)
        up = jnp.dot(xn, wu_ref[:, sl], preferred_element_type=F32)
        h = (_silu(gate) * up).astype(BF16)
        acc = acc + jnp.dot(h, wo_ref[sl, :], preferred_element_type=F32)
    x = x + 0.5 * acc
    if with_ple:
        p_ref, npl_ref, wpg_ref, wpp_ref = refs[5:9]
        emb = _dot(p_ref[...], wpp_ref[...])
        gate = _dot(_rms(x, npl_ref[...]), wpg_ref[...])
        x = x + emb * jax.nn.sigmoid(gate)
    if with_final:
        x = _rms(x, refs[9][...])
    o_ref[...] = x


def _cast_row_blocks(n_rows, n_steps):
    blocks = n_steps
    while n_rows % blocks or (n_rows // blocks) % BF16_SUBLANES:
        assert blocks % 2 == 0, (n_rows, n_steps)
        blocks //= 2
    return blocks


def _ffn(x, layer, gain, w_in, w_out, ple=None, final_gain=None, casts=()):
    rows = x.shape[0]
    tm = min(FFN_ROWS, rows)
    n_steps = rows // tm
    row_spec = pl.BlockSpec((tm, D_MODEL), lambda i: (i, 0))
    in_specs = [row_spec, _layer_spec((1, D_MODEL), layer),
                _layer_spec((D_MODEL, D_FF), 0, block=0), _layer_spec((D_MODEL, D_FF), 0, block=1),
                _layer_spec((D_FF, D_MODEL), 0)]
    args = [x, gain, w_in, w_in, w_out]
    out_specs = [row_spec]
    out_shape = [jax.ShapeDtypeStruct(x.shape, F32)]
    if ple is not None:
        p, n_ple, w_gate, w_proj = ple
        in_specs += [pl.BlockSpec((None, tm, PLE_DIM), lambda i: (layer, i, 0)), _layer_spec((1, D_MODEL), layer),
                     _layer_spec((D_MODEL, D_MODEL), 0), _layer_spec((PLE_DIM, D_MODEL), 0)]
        args += [p, n_ple, w_gate, w_proj]
    if final_gain is not None:
        in_specs.append(_layer_spec((1, D_MODEL), 0))
        args.append(final_gain)
    cast_plan = []
    for src, layer_c, splits in casts:
        _, n_rows, n_cols = src.shape
        splits = tuple(splits) if splits is not None else ((0, n_cols),)
        blocks = _cast_row_blocks(n_rows, n_steps)
        every = n_steps // blocks
        in_specs.append(pl.BlockSpec((None, n_rows // blocks, n_cols),
                                     lambda i, layer_c=layer_c, every=every: (layer_c, i // every, 0)))
        args.append(src)
        for lo, hi in splits:
            out_specs.append(pl.BlockSpec((None, n_rows // blocks, hi - lo),
                                          lambda i, every=every: (0, i // every, 0)))
            out_shape.append(jax.ShapeDtypeStruct((1, n_rows, hi - lo), BF16))
        cast_plan.append(splits)
    out = pl.pallas_call(
        functools.partial(_ffn_kernel, with_ple=ple is not None, with_final=final_gain is not None,
                          cast_plan=tuple(cast_plan)),
        grid=(n_steps,),
        in_specs=in_specs,
        out_specs=out_specs,
        out_shape=out_shape,
        compiler_params=pltpu.CompilerParams(dimension_semantics=("arbitrary",),
                                             vmem_limit_bytes=VMEM_LIMIT_BYTES),
    )(*args)
    return (out[0], out[1:]) if casts else out[0]


def _split(a):
    hi = a.astype(BF16)
    lo = (a - hi.astype(F32)).astype(BF16)
    return hi, lo


def _dot3(a_hi, a_lo, b_hi, b_lo):
    m = a_hi.shape[0]
    both = jnp.dot(jnp.concatenate([a_hi, a_lo], axis=0), b_hi, preferred_element_type=F32)
    return both[:m] + both[m:] + jnp.dot(a_hi, b_lo, preferred_element_type=F32)


def _expand(packed, diag_ones_ref):
    n = packed.shape[1] // CHUNK_D
    if n == 1:
        return packed
    return jnp.concatenate([packed] * n, axis=0) * diag_ones_ref[...]


def _unit_lower_inverses_packed(l_packed_list, diag_ones_ref):
    c, width = l_packed_list[0].shape
    row = lax.broadcasted_iota(jnp.int32, (c, width), 0)
    col = lax.broadcasted_iota(jnp.int32, (c, width), 1) % c
    zero = jnp.zeros((), BF16)

    def lower_left(bs):
        return (row // (2 * bs) == col // (2 * bs)) & ((row // bs) % 2 == 1) & ((col // bs) % 2 == 0)

    l_bf = [l.astype(BF16) for l in l_packed_list]
    xs = [jnp.where(row == col, 1.0, 0.0) - jnp.where(lower_left(1), l, 0.0) for l in l_packed_list]
    bs = 2
    while bs < c:
        sel = lower_left(bs)
        x_bf = [x.astype(BF16) for x in xs]
        ys = [jnp.dot(jnp.where(sel, l, zero), _expand(x, diag_ones_ref), preferred_element_type=F32)
              for l, x in zip(l_bf, x_bf)]
        xs = [x - jnp.dot(xb, _expand(y.astype(BF16), diag_ones_ref), preferred_element_type=F32)
              for x, xb, y in zip(xs, x_bf, ys)]
        bs *= 2
    return xs


def _mix_prompt_kernel(x_ref, nmix_ref, wmain_ref, wtail_ref, avg_ref, aog_ref, bog_ref, wsp_ref, bsp_ref,
                       cw_ref, abp_ref, wout_ref,
                       xo_ref, s_ref, ct_ref,
                       zlast_ref, ob_ref, bd_ref, wgate_ref):
    tl = x_ref.shape[0]
    step = pl.program_id(1)

    @pl.when((pl.program_id(0) == 0) & (step == 0))
    def _():
        wgate_ref[...] = wtail_ref[:, 2 * N_HEADS:]

    @pl.when(step == 0)
    def _():
        s_ref[...] = jnp.zeros_like(s_ref)
        zlast_ref[...] = jnp.zeros_like(zlast_ref)

    x = x_ref[...]
    xn = _rms(x, nmix_ref[...]).astype(BF16)
    def in_proj(lo, hi):
        return jnp.dot(xn, wmain_ref[:, lo:hi], preferred_element_type=F32)

    zab = jnp.dot(xn, wtail_ref[:, :LANES], preferred_element_type=F32)
    first_tile = lax.broadcasted_iota(jnp.int32, (8, 1), 0)
    qkv_parts = []
    for part in range(3):
        cols = slice(part * B_WIDTH, (part + 1) * B_WIDTH)
        zc = in_proj(OFF_QKV + part * B_WIDTH, OFF_QKV + (part + 1) * B_WIDTH)
        cw = cw_ref[:, cols]
        carried = zlast_ref[:, cols]
        y = zc * cw[CONV_W - 1:CONV_W]
        for d in range(1, CONV_W):
            rolled = pltpu.roll(zc, d, axis=0)
            top = jnp.where(first_tile < d, pltpu.roll(carried, d, axis=0), rolled[0:8])
            y = y + jnp.concatenate([top, rolled[8:]], axis=0) * cw[CONV_W - 1 - d:CONV_W - d]
        zlast_ref[:, cols] = zc[tl - 8:tl]
        ct_ref[:, cols] = zc[tl - (CONV_W - 1):tl]
        qkv_parts.append(_silu(y))

    uv = jax.nn.gelu(in_proj(0, OFF_QKV))
    row = lax.broadcasted_iota(jnp.int32, (CHUNK_A, CHUNK_A), 0)
    col = lax.broadcasted_iota(jnp.int32, (CHUNK_A, CHUNK_A), 1)
    causal = col <= row
    for h in range(N_HEADS):
        hs = slice(h * HEAD_DIM, (h + 1) * HEAD_DIM)
        u_h = uv[:, hs]
        v_h = _rms(uv[:, A_WIDTH + h * HEAD_DIM:A_WIDTH + (h + 1) * HEAD_DIM], avg_ref[...]).astype(BF16)
        w_h = jnp.where(causal, wsp_ref[h], 0.0).astype(BF16)
        bias_h = bsp_ref[:, h:h + 1]
        for c in range(tl // CHUNK_A):
            rs = slice(c * CHUNK_A, (c + 1) * CHUNK_A)
            mixed = jnp.dot(w_h, v_h[rs], preferred_element_type=F32) + bias_h
            ob_ref[rs, hs] = _rms(u_h[rs] * mixed, aog_ref[...]).astype(BF16)

    z_gate = jnp.dot(xn, wgate_ref[...], preferred_element_type=F32)
    abp = abp_ref[...]
    g = -jnp.exp(abp[0:1]) * _softplus(zab + abp[1:2])
    beta = jax.nn.sigmoid(zab)

    r2 = lax.broadcasted_iota(jnp.int32, (GROUP, GROUP), 0)
    c2 = lax.broadcasted_iota(jnp.int32, (GROUP, GROUP), 1)
    same = (r2 // CHUNK_D) == (c2 // CHUNK_D)
    col_ones = jnp.concatenate([jnp.where(same & (c2 <= r2), 1.0, 0.0), jnp.where(same, 1.0, 0.0)],
                               axis=0).astype(BF16)
    upper_ones = jnp.where(same & (r2 <= c2), 1.0, 0.0).astype(BF16)
    bd_ref[...] = jnp.where(same, 1.0, 0.0).astype(BF16)
    n_grp = tl // GROUP
    gam_parts, glast_parts, gam_t = [], [], []
    for gi in range(n_grp):
        g_grp = g[gi * GROUP:(gi + 1) * GROUP]
        by_col = jnp.dot(col_ones, jnp.concatenate(_split3(g_grp), axis=1), preferred_element_type=F32)
        by_col = by_col[:, :LANES] + by_col[:, LANES:2 * LANES] + by_col[:, 2 * LANES:]
        gam_parts.append(by_col[:GROUP])
        glast_parts.append(by_col[GROUP:])
        by_row = jnp.dot(jnp.concatenate(_split3(g_grp.T), axis=0), upper_ones, preferred_element_type=F32)
        gam_t.append(by_row[:LANES] + by_row[LANES:2 * LANES] + by_row[2 * LANES:])
    gam = jnp.concatenate(gam_parts, axis=0)
    glast = jnp.concatenate(glast_parts, axis=0)

    rb = lax.broadcasted_iota(jnp.int32, (CHUNK_D, CHUNK_D), 0)
    cb = lax.broadcasted_iota(jnp.int32, (CHUNK_D, CHUNK_D), 1)
    strict = cb < rb
    eye = jnp.where(cb == rb, 1.0, 0.0)

    heads, a_packed, rhs = [], [], []
    for h in range(N_HEADS):
        q_h = _l2(qkv_parts[0][:, h * HEAD_DIM:(h + 1) * HEAD_DIM]) * (HEAD_DIM ** -0.5)
        k_h = _l2(qkv_parts[1][:, h * HEAD_DIM:(h + 1) * HEAD_DIM])
        v_h = qkv_parts[2][:, h * HEAD_DIM:(h + 1) * HEAD_DIM]
        gc_h = gam[:, h:h + 1]
        gl_h = glast[:, h:h + 1]
        bc_h = beta[:, N_HEADS + h:N_HEADS + h + 1]
        eg_h = jnp.exp(gc_h)
        qk_blocks = []
        for gi in range(n_grp):
            packed = []
            for j in range(GROUP // CHUNK_D):
                rs = slice(gi * GROUP + j * CHUNK_D, gi * GROUP + (j + 1) * CHUNK_D)
                grams = _dot_nt(jnp.concatenate([k_h[rs], q_h[rs]], axis=0), k_h[rs])
                diff = gc_h[rs] - gam_t[gi][h:h + 1, j * CHUNK_D:(j + 1) * CHUNK_D]
                decay = jnp.where(strict, jnp.exp(jnp.where(strict, diff, 0.0)), 0.0)
                packed.append(bc_h[rs] * grams[:CHUNK_D] * decay)
                qk_blocks.append(grams[CHUNK_D:] * (decay + eye))
            a_packed.append(jnp.concatenate(packed, axis=1))
        rhs.append(_split(jnp.concatenate([bc_h * v_h, (bc_h * eg_h) * k_h], axis=1)))
        heads.append((q_h * eg_h, qk_blocks, k_h * jnp.exp(gl_h - gc_h), jnp.exp(gl_h)))
    inv_split = [_split(inv) for inv in _unit_lower_inverses_packed(a_packed, bd_ref)]
    sol = []
    for h in range(N_HEADS):
        sol.append([_dot3(_expand(inv_split[h * n_grp + gi][0], bd_ref), _expand(inv_split[h * n_grp + gi][1], bd_ref),
                          rhs[h][0][gi * GROUP:(gi + 1) * GROUP], rhs[h][1][gi * GROUP:(gi + 1) * GROUP])
                    for gi in range(n_grp)])

    for i in range(tl // CHUNK_D):
        rs = slice(i * CHUNK_D, (i + 1) * CHUNK_D)
        gi, j = divmod(i, GROUP // CHUNK_D)
        ls = slice(j * CHUNK_D, (j + 1) * CHUNK_D)
        for h in range(N_HEADS):
            qb_h, qk_blocks, kend_h, btot_h = heads[h]
            qk = qk_blocks[i]
            s_old = s_ref[h]
            from_s = _dot(jnp.concatenate([sol[h][gi][ls, HEAD_DIM:], qb_h[rs]], axis=0), s_old)
            u = sol[h][gi][ls, :HEAD_DIM] - from_s[:CHUNK_D]
            from_u = _dot(jnp.concatenate([qk, kend_h[rs].T], axis=0), u)
            o = from_s[CHUNK_D:] + from_u[:CHUNK_D]
            s_ref[h] = btot_h[i * CHUNK_D:i * CHUNK_D + 1] * s_old + from_u[CHUNK_D:]
            gate = z_gate[rs, h * HEAD_DIM:(h + 1) * HEAD_DIM]
            ob_ref[rs, A_WIDTH + h * HEAD_DIM:A_WIDTH + (h + 1) * HEAD_DIM] = (
                _rms(o, bog_ref[...]) * _silu(gate)).astype(BF16)

    xo_ref[...] = x + jnp.dot(ob_ref[...], wout_ref[...], preferred_element_type=F32)


def _mixer_weight_specs(layer):
    return [_layer_spec((1, D_MODEL), layer), _layer_spec((D_MODEL, Z_FRONT), 0), _layer_spec((D_MODEL, Z_TAIL), 0),
            _layer_spec((1, HEAD_DIM), layer), _layer_spec((1, HEAD_DIM), layer), _layer_spec((1, HEAD_DIM), layer)]


def _mixer_weights(w):
    return [w["n_mix"], w["w_front"], w["w_tail"], w["a_v_gain"], w["a_out_gain"], w["b_out_gain"]]


def _mix_prompt(x, layer, w):
    bsz, length, _ = x.shape
    tl = MIX_ROWS
    row_spec = pl.BlockSpec((None, tl, D_MODEL), lambda b, t: (b, t, 0))
    in_specs = [row_spec] + _mixer_weight_specs(layer) + [
        _layer_spec((N_HEADS, CHUNK_A, CHUNK_A), layer), _layer_spec((CHUNK_A, N_HEADS), layer),
        _layer_spec((CONV_W, 3 * B_WIDTH), layer), _layer_spec((2, LANES), layer), _layer_spec((D_MODEL, D_MODEL), 0)]
    out_specs = [row_spec,
                 pl.BlockSpec((None, N_HEADS, HEAD_DIM, HEAD_DIM), lambda b, t: (b, 0, 0, 0)),
                 pl.BlockSpec((None, CONV_W - 1, 3 * B_WIDTH), lambda b, t: (b, 0, 0))]
    out_shape = [jax.ShapeDtypeStruct(x.shape, F32),
                 jax.ShapeDtypeStruct((bsz, N_HEADS, HEAD_DIM, HEAD_DIM), F32),
                 jax.ShapeDtypeStruct((bsz, CONV_W - 1, 3 * B_WIDTH), F32)]
    return pl.pallas_call(
        _mix_prompt_kernel,
        grid=(bsz, length // tl),
        in_specs=in_specs,
        out_specs=out_specs,
        out_shape=out_shape,
        scratch_shapes=[pltpu.VMEM((8, 3 * B_WIDTH), F32), pltpu.VMEM((tl, D_MODEL), BF16),
                        pltpu.VMEM((GROUP, GROUP), BF16), pltpu.VMEM((D_MODEL, B_WIDTH), BF16)],
        compiler_params=pltpu.CompilerParams(dimension_semantics=("arbitrary", "arbitrary"),
                                             vmem_limit_bytes=VMEM_LIMIT_BYTES),
    )(x, *_mixer_weights(w), w["a_w_s"], w["a_b_s_t"], w["b_conv"], w["ab_par"], w["w_out"])


def _mix_sample_kernel(x_ref, s_ref, cpad_ref, s_all_ref, nmix_ref, wmain_ref, wtail_ref, avg_ref, aog_ref, bog_ref,
                       coef_ref, bias_ref, cw_ref, abp_ref, wout_ref,
                       xo_ref, so_ref, zq_ref, vo_ref,
                       z_ref, ztail_ref, ob_ref, *, n_tok):
    del s_all_ref
    rows = zq_ref.shape[0]
    nb = rows // n_tok
    step = pl.program_id(0)

    @pl.when(step == 0)
    def _():
        xn = _rms(x_ref[...], nmix_ref[...]).astype(BF16)
        z_ref[...] = jnp.dot(xn, wmain_ref[...], preferred_element_type=F32)
        ztail_ref[...] = jnp.dot(xn, wtail_ref[...], preferred_element_type=F32)

    here = pl.ds(pl.multiple_of(step * rows, rows), rows)
    z = z_ref[here, :]
    z_tail = ztail_ref[here, :]
    zab = z_tail[:, :LANES]
    tok = lax.broadcasted_iota(jnp.int32, (rows, 1), 0) % n_tok

    def prev(a, d):
        return pltpu.roll(a, d, axis=0)

    def prev_or_zero(a, d):
        return a if d == 0 else jnp.where(tok >= d, prev(a, d), 0.0)

    def per_head(fn, a):
        return jnp.concatenate([fn(a[:, h * HEAD_DIM:(h + 1) * HEAD_DIM]) for h in range(N_HEADS)], axis=1)

    uv = jax.nn.gelu(z[:, :2 * A_WIDTH])
    vn = per_head(lambda a: _rms(a, avg_ref[...]), uv[:, A_WIDTH:])
    vo_ref[...] = vn
    mixed = bias_ref[...]
    for d in range(n_tok):
        mixed = mixed + coef_ref[d] * prev_or_zero(vn, d)
    ob_ref[here, :A_WIDTH] = per_head(lambda a: _rms(a, aog_ref[...]), uv[:, :A_WIDTH] * mixed)

    zq = z[:, OFF_QKV:OFF_GATE]
    zq_ref[...] = zq
    cpad = cpad_ref[...]
    cw = cw_ref[...]
    y = zq * cw[CONV_W - 1:CONV_W]
    for d in range(1, CONV_W):
        carried = pltpu.roll(cpad, rows - (n_tok - d), axis=0)
        y = y + jnp.where(tok >= d, prev(zq, d), carried) * cw[CONV_W - 1 - d:CONV_W - d]
    qkv = _silu(y)

    abp = abp_ref[...]
    g_all = -jnp.exp(abp[0:1]) * _softplus(zab + abp[1:2])
    beta_all = jax.nn.sigmoid(zab)

    sub = lax.broadcasted_iota(jnp.int32, (8, 1), 0)
    first_half = sub < n_tok
    o_heads, kend_heads, u_heads, btot_heads = [], [], [], []
    for h in range(N_HEADS):
        q = _l2(qkv[:, h * HEAD_DIM:(h + 1) * HEAD_DIM]) * (HEAD_DIM ** -0.5)
        k = _l2(qkv[:, B_WIDTH + h * HEAD_DIM:B_WIDTH + (h + 1) * HEAD_DIM])
        v = qkv[:, 2 * B_WIDTH + h * HEAD_DIM:2 * B_WIDTH + (h + 1) * HEAD_DIM]
        g = jnp.broadcast_to(g_all[:, h:h + 1], (rows, HEAD_DIM))
        beta = jnp.broadcast_to(beta_all[:, N_HEADS + h:N_HEADS + h + 1], (rows, HEAD_DIM))
        gam = g
        for d in range(1, n_tok):
            gam = gam + prev_or_zero(g, d)
        gam_last = jnp.where(tok == n_tok - 1, gam, 0.0)
        for d in range(1, n_tok):
            gam_last = gam_last + jnp.where(tok == n_tok - 1 - d, pltpu.roll(gam, rows - d, axis=0), 0.0)
        eg = jnp.exp(gam)

        def decay_to(d, gam=gam):
            return jnp.exp(jnp.where(tok >= d, gam - prev(gam, d), 0.0))

        a_sub = [None] + [jnp.where(tok >= d, beta * jnp.sum(k * prev(k, d), axis=-1, keepdims=True) * decay_to(d),
                                    0.0) for d in range(1, n_tok)]
        def forward_substitute(rhs, a_sub=a_sub):
            sol = rhs
            for t in range(1, n_tok):
                acc = rhs
                for d in range(1, t + 1):
                    acc = acc - a_sub[d] * prev(sol, d)
                sol = jnp.where(tok == t, acc, sol)
            return sol

        w_blk = forward_substitute(beta * v)
        kb_blk = forward_substitute((beta * eg) * k)
        qb = q * eg

        kb_s, qb_s = [], []
        for p in range(rows // 8):
            kb_t, qb_t = kb_blk[8 * p:8 * p + 8], qb[8 * p:8 * p + 8]
            f0 = _dot(jnp.where(first_half, kb_t, pltpu.roll(qb_t, n_tok, axis=0)), s_ref[2 * p, h])
            f1 = _dot(jnp.where(first_half, pltpu.roll(kb_t, n_tok, axis=0), qb_t), s_ref[2 * p + 1, h])
            kb_s.append(jnp.where(first_half, f0, pltpu.roll(f1, n_tok, axis=0)))
            qb_s.append(jnp.where(first_half, pltpu.roll(f0, n_tok, axis=0), f1))
        u = w_blk - jnp.concatenate(kb_s, axis=0)
        o = jnp.concatenate(qb_s, axis=0)
        for d in range(n_tok):
            qk = jnp.where(tok >= d, jnp.sum(q * prev(k, d), axis=-1, keepdims=True) * decay_to(d), 0.0)
            o = o + qk * prev_or_zero(u, d)
        o_heads.append(o)
        kend_heads.append(k * jnp.exp(gam_last - gam))
        u_heads.append(u)
        btot_heads.append(jnp.broadcast_to(jnp.exp(gam_last), (rows, HEAD_DIM)))

    kend_t = jnp.concatenate(kend_heads, axis=0).T
    u_all = jnp.concatenate(u_heads, axis=0).astype(BF16)
    owner = lax.broadcasted_iota(jnp.int32, (1, N_HEADS * rows), 1) // n_tok
    for h in range(N_HEADS):
        for b in range(nb):
            mine = jnp.where(owner == h * nb + b, kend_t, 0.0).astype(BF16)
            last = b * n_tok + n_tok - 1
            so_ref[b, h] = btot_heads[h][last:last + 1] * s_ref[b, h] + jnp.dot(
                mine, u_all, preferred_element_type=F32)

    gate = z_tail[:, 2 * N_HEADS:]
    ob_ref[here, A_WIDTH:] = per_head(lambda a: _rms(a, bog_ref[...]), jnp.concatenate(o_heads, axis=1)) * _silu(gate)

    @pl.when(step == pl.num_programs(0) - 1)
    def _():
        xo_ref[...] = x_ref[...] + _dot(ob_ref[...], wout_ref[...])


def _mix_sample(x, layer, state_s, cpad, s_all, w, n_tok):
    rows_total = x.shape[0]
    nb = SAMPLE_GROUP
    rows = nb * n_tok
    all_rows = lambda width: pl.BlockSpec((rows_total, width), lambda i: (0, 0))
    row_spec = lambda width: pl.BlockSpec((rows, width), lambda i: (i, 0))
    s_spec = pl.BlockSpec((None, nb, N_HEADS, HEAD_DIM, HEAD_DIM), lambda i: (layer, i, 0, 0, 0))
    in_specs = [all_rows(D_MODEL), s_spec, pl.BlockSpec((None, rows, 3 * B_WIDTH), lambda i: (layer, i, 0)),
                pl.BlockSpec(memory_space=pl.ANY)] + _mixer_weight_specs(layer) + [
        _layer_spec((n_tok, rows, A_WIDTH), layer), _layer_spec((rows, A_WIDTH), layer),
        _layer_spec((CONV_W, 3 * B_WIDTH), layer), _layer_spec((2, LANES), layer), _layer_spec((D_MODEL, D_MODEL), 0)]
    out_specs = [all_rows(D_MODEL), s_spec, row_spec(3 * B_WIDTH), row_spec(A_WIDTH)]
    out_shape = [jax.ShapeDtypeStruct(x.shape, F32), jax.ShapeDtypeStruct(state_s.shape, F32),
                 jax.ShapeDtypeStruct((rows_total, 3 * B_WIDTH), F32), jax.ShapeDtypeStruct((rows_total, A_WIDTH), F32)]
    return pl.pallas_call(
        functools.partial(_mix_sample_kernel, n_tok=n_tok),
        grid=(rows_total // rows,),
        in_specs=in_specs,
        out_specs=out_specs,
        out_shape=out_shape,
        input_output_aliases={3: 1},
        scratch_shapes=[pltpu.VMEM((rows_total, Z_FRONT), F32), pltpu.VMEM((rows_total, Z_TAIL), F32),
                        pltpu.VMEM((rows_total, D_MODEL), F32)],
        compiler_params=pltpu.CompilerParams(dimension_semantics=("arbitrary",),
                                             vmem_limit_bytes=VMEM_LIMIT_BYTES),
    )(x, state_s, cpad, s_all, *_mixer_weights(w), w["a_coef"], w["a_bias"], w["b_conv"], w["ab_par"], w["w_out"])


def _prep_tables(n_tok, norm_ffn1, norm_mix, a_v_gain, a_spatial_w, a_spatial_b, a_out_gain, b_conv_w, b_a_log,
                 b_dt_bias, b_out_gain, norm_ffn2, norm_ple):
    par_pad = jnp.zeros((DEPTH, LANES - N_HEADS), F32)
    ab_par = jnp.stack([jnp.concatenate([b_a_log, par_pad], axis=1),
                        jnp.concatenate([b_dt_bias, par_pad], axis=1)], axis=1)

    def sample_rows(a):
        return jnp.tile(jnp.repeat(jnp.transpose(a, (0, 2, 1)), HEAD_DIM, axis=2), (1, SAMPLE_GROUP, 1))

    ws_small = a_spatial_w[:, :, :n_tok, :n_tok]
    a_coef = jnp.stack([sample_rows(jnp.pad(jnp.diagonal(ws_small, offset=-d, axis1=2, axis2=3),
                                            ((0, 0), (0, 0), (d, 0)))) for d in range(n_tok)], axis=1)
    return dict(
        n_f1=norm_ffn1[:, None], n_mix=norm_mix[:, None], ab_par=ab_par,
        a_v_gain=a_v_gain[:, None], a_out_gain=a_out_gain[:, None], b_out_gain=b_out_gain[:, None],
        a_w_s=a_spatial_w, a_b_s_t=jnp.transpose(a_spatial_b, (0, 2, 1)),
        a_coef=a_coef, a_bias=sample_rows(a_spatial_b[:, :, :n_tok]),
        b_conv=b_conv_w, n_f2=norm_ffn2[:, None], n_ple=norm_ple[:, None],
    )


def kernel(x_prompt, x_sample, state_S, state_conv, p_prompt, p_sample, norm_ffn1, w_ffn1_in, w_ffn1_out, norm_mix, w_in, a_v_gain, a_spatial_w, a_spatial_b, a_out_gain, b_conv_w, b_a_log, b_dt_bias, b_out_gain, w_out, norm_ffn2, w_ffn2_in, w_ffn2_out, norm_ple, w_ple_gate, w_ple_proj, final_norm):
    bsz, length, _ = x_prompt.shape
    dec_bsz, n_tok, _ = x_sample.shape
    assert length % MIX_ROWS == 0 and MIX_ROWS % CHUNK_A == 0 and MIX_ROWS % GROUP == 0
    assert dec_bsz % SAMPLE_GROUP == 0
    assert n_tok % CHUNK_A != 0 and n_tok % CHUNK_D != 0
    assert 2 * n_tok == 8 and N_HEADS * SAMPLE_GROUP * n_tok == HEAD_DIM and n_tok >= CONV_W - 1

    w = _prep_tables(n_tok, norm_ffn1, norm_mix, a_v_gain, a_spatial_w, a_spatial_b, a_out_gain, b_conv_w, b_a_log,
                     b_dt_bias, b_out_gain, norm_ffn2, norm_ple)
    ffn1_w = (w_ffn1_in[0:1].astype(BF16), w_ffn1_out[0:1].astype(BF16))
    in_proj_splits = ((0, Z_FRONT), (Z_FRONT, Z_FRONT + Z_TAIL))
    final = final_norm[None, None]
    xp = x_prompt.reshape(bsz * length, D_MODEL)
    xs = x_sample.reshape(dec_bsz * n_tok, D_MODEL)
    pp = p_prompt.reshape(DEPTH, bsz * length, PLE_DIM)
    ps = p_sample.reshape(DEPTH, dec_bsz * n_tok, PLE_DIM)
    keep = CONV_W - 1
    cpad = jnp.pad(state_conv, ((0, 0), (0, 0), (n_tok - keep, 0), (0, 0))).reshape(DEPTH, dec_bsz * n_tok, 3 * B_WIDTH)

    s_prompt, c_prompt, c_sample, v_sample = [], [], [], []
    s_sample = jnp.zeros(state_S.shape, F32)
    for i in range(DEPTH):
        last = dict(final_gain=final) if i == DEPTH - 1 else {}

        xp, (f2_in, f2_out, w_front, w_tail, w_o, ple_gate, ple_proj) = _ffn(
            xp, i, w["n_f1"], *ffn1_w,
            casts=[(w_ffn2_in, i, None), (w_ffn2_out, i, None), (w_in, i, in_proj_splits), (w_out, i, None),
                   (w_ple_gate, i, None), (w_ple_proj, i, None)])
        xs = _ffn(xs, i, w["n_f1"], *ffn1_w)
        mix_w = dict(w, w_front=w_front, w_tail=w_tail, w_out=w_o)
        xp, sp, cp = _mix_prompt(xp.reshape(bsz, length, D_MODEL), i, mix_w)
        xs, s_sample, zq, vs = _mix_sample(xs, i, state_S, cpad, s_sample, mix_w, n_tok)
        xp = xp.reshape(bsz * length, D_MODEL)
        ple = (w["n_ple"], ple_gate, ple_proj)
        if i < DEPTH - 1:
            xp, ffn1_w = _ffn(xp, i, w["n_f2"], f2_in, f2_out, ple=(pp,) + ple,
                              casts=[(w_ffn1_in, i + 1, None), (w_ffn1_out, i + 1, None)])
        else:
            xp = _ffn(xp, i, w["n_f2"], f2_in, f2_out, ple=(pp,) + ple, **last)
        xs = _ffn(xs, i, w["n_f2"], f2_in, f2_out, ple=(ps,) + ple, **last)

        s_prompt.append(sp)
        c_prompt.append(cp)
        c_sample.append(zq.reshape(dec_bsz, n_tok, 3 * B_WIDTH)[:, n_tok - keep:])
        v_sample.append(vs.reshape(dec_bsz, n_tok, N_HEADS, HEAD_DIM))

    return (xp.reshape(bsz, length, D_MODEL), xs.reshape(dec_bsz, n_tok, D_MODEL), jnp.stack(s_prompt),
            jnp.stack(c_prompt), s_sample, jnp.stack(c_sample), jnp.stack(v_sample))
```
